```python
import jax, jax.numpy as jnp
from jax import lax
import numpy as np

D_MODEL = 1024
BATCH = 32
SEQ = 256
DEPTH = 1
DEC_BATCH = 4
DEC_SEQ = 1024
PAST_LEN = 256

GRID_W = 64
MLA_HEADS = 8
QK_NOPE = 64
QK_ROPE = 32
V_HEAD = 64
Q_LORA = 256
KV_LORA = 128
MLA_WIDTH = MLA_HEADS * V_HEAD
SGU_HEADS = 8
SGU_WIDTH = D_MODEL - MLA_WIDTH
SGU_HEAD_DIM = SGU_WIDTH // SGU_HEADS
CHUNK = 128
D_FF = 4 * D_MODEL
IN_WIDTH = Q_LORA + KV_LORA + QK_ROPE + 2 * SGU_WIDTH
AXIS_PAIRS = QK_ROPE // 4
ROPE_BASE = 10000.0
EPS = 1e-6
Q_BLOCK = 128
N_MOD = 6
ATTN_SCALE = (QK_NOPE + QK_ROPE) ** -0.5

kernel_name = "hybrid_mla_sgu_prefix_dit_step"


def _rms(x, g):
    xf = x.astype(jnp.float32)
    y = xf * lax.rsqrt(jnp.mean(jnp.square(xf), axis=-1, keepdims=True) + EPS)
    return (y * g.astype(jnp.float32)).astype(x.dtype)


def _layernorm(x, g, b):
    xf = x.astype(jnp.float32)
    mu = jnp.mean(xf, axis=-1, keepdims=True)
    var = jnp.mean(jnp.square(xf - mu), axis=-1, keepdims=True)
    y = (xf - mu) * lax.rsqrt(var + EPS)
    return (y * g.astype(jnp.float32) + b.astype(jnp.float32)).astype(x.dtype)


def _axial_rope(n_tok):
    rows = n_tok // GRID_W
    row = jnp.broadcast_to(jnp.arange(rows)[:, None], (rows, GRID_W)).reshape(-1).astype(jnp.float32)
    col = jnp.broadcast_to(jnp.arange(GRID_W)[None, :], (rows, GRID_W)).reshape(-1).astype(jnp.float32)
    freqs = 1.0 / (ROPE_BASE ** (jnp.arange(AXIS_PAIRS, dtype=jnp.float32) / AXIS_PAIRS))
    ang = jnp.concatenate([row[:, None] * freqs, col[:, None] * freqs], axis=-1)
    return jnp.cos(ang), jnp.sin(ang)


def _apply_rope(x, cos, sin):
    shape = (cos.shape[0],) + (1,) * (x.ndim - 3) + (cos.shape[1],)
    c = cos.reshape(shape)
    s = sin.reshape(shape)
    xf = x.astype(jnp.float32)
    x1, x2 = xf[..., 0::2], xf[..., 1::2]
    out = jnp.stack([x1 * c - x2 * s, x1 * s + x2 * c], axis=-1).reshape(x.shape)
    return out.astype(x.dtype)


def _modulation(cond, w_mod, b_mod):
    m = jax.nn.silu(cond) @ w_mod + b_mod
    return [t[:, None, :] for t in jnp.split(m, N_MOD, axis=-1)]


def _attend(q, k, v):
    B, T, H, Dq = q.shape
    nblk = T // Q_BLOCK
    qb = q.reshape(B, nblk, Q_BLOCK, H, Dq).transpose(1, 0, 2, 3, 4)

    def one(qblk):
        s = jnp.einsum('bqhd,bkhd->bhqk', qblk, k).astype(jnp.float32) * ATTN_SCALE
        p = jax.nn.softmax(s, axis=-1).astype(v.dtype)
        return jnp.einsum('bhqk,bkhd->bqhd', p, v)

    out = lax.map(one, qb)
    return out.transpose(1, 0, 2, 3, 4).reshape(B, T, H * v.shape[-1])


def _expand_kv(ckv, kr, w_ukv):
    B, L, _ = ckv.shape
    kv = (ckv @ w_ukv).reshape(B, L, MLA_HEADS, QK_NOPE + V_HEAD)
    k_nope, val = kv[..., :QK_NOPE], kv[..., QK_NOPE:]
    k_rope = jnp.broadcast_to(kr[:, :, None, :], (B, L, MLA_HEADS, QK_ROPE))
    return jnp.concatenate([k_nope, k_rope], axis=-1), val


def _sgu(u, v, w_sgu, b_sgu, g_sgu, beta_sgu):
    B, T, _ = v.shape
    vn = _layernorm(v, g_sgu, beta_sgu).reshape(B, T // CHUNK, CHUNK, SGU_HEADS, SGU_HEAD_DIM)
    mixed = jnp.einsum('gpq,bnqgc->bnpgc', w_sgu, vn) + b_sgu.T[:, :, None]
    return u * mixed.reshape(B, T, SGU_WIDTH)


def _layer(x, mod, p, ctx_ckv=None, ctx_kr=None, rope=None):
    shift_a, scale_a, gate_a, shift_f, scale_f, gate_f = mod
    B, T, _ = x.shape
    h = _rms(x, p['g_attn_pre']) * (1.0 + scale_a) + shift_a
    proj = h @ p['w_in']
    i0 = Q_LORA
    i1 = i0 + KV_LORA
    i2 = i1 + QK_ROPE
    i3 = i2 + SGU_WIDTH
    cq, ckv, kr = proj[..., :i0], proj[..., i0:i1], proj[..., i1:i2]
    u, v = jax.nn.gelu(proj[..., i2:i3]), jax.nn.gelu(proj[..., i3:])
    q = (_rms(cq, p['g_q']) @ p['w_uq']).reshape(B, T, MLA_HEADS, QK_NOPE + QK_ROPE)
    ckv = _rms(ckv, p['g_kv'])
    if rope is None:
        k, val = _expand_kv(ckv, kr, p['w_ukv'])
    else:
        cos, sin = rope
        q = jnp.concatenate([q[..., :QK_NOPE], _apply_rope(q[..., QK_NOPE:], cos, sin)], axis=-1)
        k_lat, v_lat = _expand_kv(ckv, _apply_rope(kr, cos, sin), p['w_ukv'])
        k_ctx, v_ctx = _expand_kv(ctx_ckv, ctx_kr, p['w_ukv'])
        k = jnp.concatenate([k_ctx, k_lat], axis=1)
        val = jnp.concatenate([v_ctx, v_lat], axis=1)
    attn = _attend(q, k, val)
    sgu = _sgu(u, v, p['w_sgu'], p['b_sgu'], p['g_sgu'], p['beta_sgu'])
    mix = jnp.concatenate([attn, sgu], axis=-1) @ p['w_o']
    x = x + gate_a * _rms(mix, p['g_attn_post'])
    h = _rms(x, p['g_ffn_pre']) * (1.0 + scale_f) + shift_f
    f = jnp.square(jax.nn.relu(h @ p['w_ff1'])) @ p['w_ff2']
    x = x + gate_f * _rms(f, p['g_ffn_post'])
    return x, ckv, kr


def setup_inputs(seed: int = 0) -> dict:
    key = jax.random.key(seed)
    ks = jax.random.split(key, 24)
    n = lambda k, s, sc: jax.random.normal(k, s, jnp.float32) * sc
    L = DEPTH
    return {
        'x_prompt': n(ks[0], (BATCH, SEQ, D_MODEL), 1.0),
        'x_sample': n(ks[1], (DEC_BATCH, DEC_SEQ, D_MODEL), 1.0),
        'cache_ckv': n(ks[2], (DEC_BATCH, DEPTH, PAST_LEN, KV_LORA), 1.0),
        'cache_krope': n(ks[3], (DEC_BATCH, DEPTH, PAST_LEN, QK_ROPE), 1.0),
        'c': n(ks[4], (DEC_BATCH, D_MODEL), 1.0),
        'c_ctx': n(ks[5], (D_MODEL,), 1.0),
        'w_mod': n(ks[6], (L, D_MODEL, N_MOD * D_MODEL), 0.5 * D_MODEL ** -0.5),
        'b_mod': n(ks[7], (L, N_MOD * D_MODEL), 0.02),
        'g_attn_pre': 1.0 + n(ks[8], (L, D_MODEL), 0.05),
        'g_attn_post': 1.0 + n(ks[9], (L, D_MODEL), 0.05),
        'w_in': n(ks[10], (L, D_MODEL, IN_WIDTH), D_MODEL ** -0.5),
        'g_q': 1.0 + n(ks[11], (L, Q_LORA), 0.05),
        'w_uq': n(ks[12], (L, Q_LORA, MLA_HEADS * (QK_NOPE + QK_ROPE)), Q_LORA ** -0.5),
        'g_kv': 1.0 + n(ks[13], (L, KV_LORA), 0.05),
        'w_ukv': n(ks[14], (L, KV_LORA, MLA_HEADS * (QK_NOPE + V_HEAD)), KV_LORA ** -0.5),
        'w_sgu': n(ks[15], (L, SGU_HEADS, CHUNK, CHUNK), CHUNK ** -0.5),
        'b_sgu': 1.0 + n(ks[16], (L, SGU_HEADS, CHUNK), 0.02),
        'g_sgu': 1.0 + n(ks[17], (L, SGU_WIDTH), 0.05),
        'beta_sgu': n(ks[18], (L, SGU_WIDTH), 0.02),
        'w_o': n(ks[19], (L, D_MODEL, D_MODEL), D_MODEL ** -0.5),
        'g_ffn_pre': 1.0 + n(ks[20], (L, D_MODEL), 0.05),
        'g_ffn_post': 1.0 + n(ks[21], (L, D_MODEL), 0.05),
        'w_ff1': n(ks[22], (L, D_MODEL, D_FF), D_MODEL ** -0.5),
        'w_ff2': n(ks[23], (L, D_FF, D_MODEL), D_FF ** -0.5),
    }


def reference(x_prompt, x_sample, cache_ckv, cache_krope, c, c_ctx,
              w_mod, b_mod, g_attn_pre, g_attn_post, w_in, g_q, w_uq, g_kv, w_ukv,
              w_sgu, b_sgu, g_sgu, beta_sgu, w_o, g_ffn_pre, g_ffn_post, w_ff1, w_ff2):
    rope = _axial_rope(x_sample.shape[1])
    xp = x_prompt
    xs = x_sample
    new_ckv = []
    new_kr = []
    for l in range(DEPTH):
        p = {'g_attn_pre': g_attn_pre[l], 'g_attn_post': g_attn_post[l], 'w_in': w_in[l],
             'g_q': g_q[l], 'w_uq': w_uq[l], 'g_kv': g_kv[l], 'w_ukv': w_ukv[l],
             'w_sgu': w_sgu[l], 'b_sgu': b_sgu[l], 'g_sgu': g_sgu[l], 'beta_sgu': beta_sgu[l],
             'w_o': w_o[l], 'g_ffn_pre': g_ffn_pre[l], 'g_ffn_post': g_ffn_post[l],
             'w_ff1': w_ff1[l], 'w_ff2': w_ff2[l]}
        mod_ctx = _modulation(c_ctx[None, :], w_mod[l], b_mod[l])
        xp, ckv_l, kr_l = _layer(xp, mod_ctx, p)
        new_ckv.append(ckv_l)
        new_kr.append(kr_l)
        mod_lat = _modulation(c, w_mod[l], b_mod[l])
        xs, _, _ = _layer(xs, mod_lat, p, ctx_ckv=cache_ckv[:, l], ctx_kr=cache_krope[:, l], rope=rope)
    new_cache_ckv = jnp.stack(new_ckv, axis=1)
    new_cache_krope = jnp.stack(new_kr, axis=1)
    return (xp, xs, new_cache_ckv, new_cache_krope)
```

```python
import functools
import math

import jax
import jax.numpy as jnp
from jax import lax
from jax.experimental import pallas as pl
from jax.experimental.pallas import tpu as pltpu

D_MODEL = 1024
GRID_W = 64
MLA_HEADS = 8
QK_NOPE = 64
QK_ROPE = 32
V_HEAD = 64
Q_LORA = 256
KV_LORA = 128
MLA_WIDTH = MLA_HEADS * V_HEAD
SGU_HEADS = 8
SGU_WIDTH = D_MODEL - MLA_WIDTH
SGU_HEAD_DIM = SGU_WIDTH // SGU_HEADS
CHUNK = 128
D_FF = 4 * D_MODEL
AXIS_PAIRS = QK_ROPE // 4
ROPE_BASE = 10000.0
EPS = 1e-6
N_MOD = 6
ATTN_SCALE = (QK_NOPE + QK_ROPE) ** -0.5

LANES = 128
HEAD_SLOT = LANES
QK_WIDTH = MLA_HEADS * HEAD_SLOT
PROJ_WIDTH = 1536
MOD_ROWS = 8
TOKEN_TILE = 256
FF_CHUNK = 1024
VMEM_LIMIT = 56 * 1024 * 1024

BF16 = jnp.bfloat16
F32 = jnp.float32


def _dot(a, b):
    return jnp.dot(a, b, preferred_element_type=F32)


def _rms(x, g):
    return x * lax.rsqrt(jnp.mean(x * x, axis=-1, keepdims=True) + EPS) * g


def _gelu(x):
    inner = math.sqrt(2.0 / math.pi) * (x + 0.044715 * (x * x * x))
    return x * (0.5 * (1.0 + jnp.tanh(inner)))


def _mod_kernel(cond_ref, w_ref, b_ref, o_ref):
    cnd = cond_ref[...]
    act = cnd * (1.0 / (1.0 + jnp.exp(-cnd)))
    o_ref[...] = _dot(act.astype(BF16), w_ref[...].astype(BF16)) + b_ref[...]


def _modulation(cond8, w_mod, b_mod):
    n = w_mod.shape[1]
    tn = 1536
    return pl.pallas_call(
        _mod_kernel,
        out_shape=jax.ShapeDtypeStruct((MOD_ROWS, n), F32),
        grid=(n // tn,),
        in_specs=[
            pl.BlockSpec((MOD_ROWS, D_MODEL), lambda j: (0, 0)),
            pl.BlockSpec((D_MODEL, tn), lambda j: (0, j)),
            pl.BlockSpec((1, tn), lambda j: (0, j)),
        ],
        out_specs=pl.BlockSpec((MOD_ROWS, tn), lambda j: (0, j)),
        compiler_params=pltpu.CompilerParams(
            dimension_semantics=("arbitrary",), vmem_limit_bytes=VMEM_LIMIT),
        name="modulation",
    )(cond8, w_mod, b_mod)


def _phase1_kernel(emit_cache, x_ref, mod_ref, ktab_ref, qtab_ref, g_pre_ref, w_in_ref, g_q_ref,
                   w_uq_ref, g_kv_ref, w_k_ref, w_uv_ref, g_sgu_ref, beta_sgu_ref, w_sgu_ref,
                   bias_ref, *out_refs):
    if emit_cache:
        q_ref, k_ref, v_ref, sgu_ref, ckv_ref, kr_ref = out_refs
    else:
        q_ref, k_ref, v_ref, sgu_ref = out_refs
    tm = x_ref.shape[0]
    x = x_ref[...]
    shift_a = mod_ref[0:1, :]
    scale_a = mod_ref[1:2, :]
    h = _rms(x, g_pre_ref[...]) * (1.0 + scale_a) + shift_a
    proj = _dot(h.astype(BF16), w_in_ref[...])

    cq = _rms(proj[:, 0:Q_LORA], g_q_ref[...])
    q = _dot(cq.astype(BF16), w_uq_ref[...])
    qtab = qtab_ref[...]
    q_ref[...] = jnp.concatenate(
        [(q[:, s * HEAD_SLOT:(s + 1) * HEAD_SLOT] * qtab).astype(BF16) for s in range(MLA_HEADS)],
        axis=1)

    ckv_n = _rms(proj[:, Q_LORA:Q_LORA + KV_LORA], g_kv_ref[...])
    rope_slab = proj[:, Q_LORA + KV_LORA:Q_LORA + KV_LORA + LANES]
    if emit_cache:
        ckv_ref[...] = ckv_n
        kr_ref[...] = rope_slab[:, 0:QK_ROPE]
    ckv_b = ckv_n.astype(BF16)
    kin = jnp.concatenate([ckv_b, (rope_slab * ktab_ref[...]).astype(BF16)], axis=1)
    k_ref[...] = _dot(kin, w_k_ref[...]).astype(BF16)
    v_ref[...] = _dot(ckv_b, w_uv_ref[...]).astype(BF16)

    u = _gelu(proj[:, 512:512 + SGU_WIDTH])
    vv = _gelu(proj[:, 512 + SGU_WIDTH:512 + 2 * SGU_WIDTH])
    mu = jnp.mean(vv, axis=-1, keepdims=True)
    vc = vv - mu
    var = jnp.mean(vc * vc, axis=-1, keepdims=True)
    vn = (vc * lax.rsqrt(var + EPS) * g_sgu_ref[...] + beta_sgu_ref[...]).astype(BF16)
    n_chunks = tm // CHUNK
    low_half = lax.broadcasted_iota(jnp.int32, (CHUNK, LANES), 1) < SGU_HEAD_DIM
    bias = bias_ref[...]
    mixed_rows = [[] for _ in range(n_chunks)]
    for j in range(SGU_WIDTH // LANES):
        rhs = jnp.concatenate(
            [vn[n * CHUNK:(n + 1) * CHUNK, j * LANES:(j + 1) * LANES] for n in range(n_chunks)],
            axis=1)
        o = _dot(w_sgu_ref[j], rhs)
        for n in range(n_chunks):
            even = o[0:CHUNK, n * LANES:(n + 1) * LANES]
            odd = o[CHUNK:2 * CHUNK, n * LANES:(n + 1) * LANES]
            mixed_rows[n].append(jnp.where(low_half, even, odd))
    for n in range(n_chunks):
        mixed = jnp.concatenate(mixed_rows[n], axis=1) + bias
        sgu_ref[n * CHUNK:(n + 1) * CHUNK, :] = (
            u[n * CHUNK:(n + 1) * CHUNK, :] * mixed).astype(BF16)


def _const_spec(shape):
    nd = len(shape)
    return pl.BlockSpec(shape, lambda *_: (0,) * nd, pipeline_mode=pl.Buffered(1))


def _phase1(x2d, mod3, mod_row_fn, ktab, qtab, tab_fn, wts, emit_cache):
    n_tok = x2d.shape[0]
    tm = TOKEN_TILE
    grid = (n_tok // tm,)
    tab_block = (ktab.shape[0] if ktab.shape[0] == 1 else tm, LANES)
    in_specs = [
        pl.BlockSpec((tm, D_MODEL), lambda i: (i, 0)),
        pl.BlockSpec((None, N_MOD, D_MODEL), lambda i: (mod_row_fn(i), 0, 0)),
        pl.BlockSpec(tab_block, lambda i: (tab_fn(i), 0)),
        pl.BlockSpec(tab_block, lambda i: (tab_fn(i), 0)),
        _const_spec((1, D_MODEL)),
        _const_spec((D_MODEL, PROJ_WIDTH)),
        _const_spec((1, Q_LORA)),
        _const_spec((Q_LORA, QK_WIDTH)),
        _const_spec((1, KV_LORA)),
        _const_spec((2 * LANES, QK_WIDTH)),
        _const_spec((KV_LORA, MLA_WIDTH)),
        _const_spec((1, SGU_WIDTH)),
        _const_spec((1, SGU_WIDTH)),
        _const_spec((SGU_WIDTH // LANES, 2 * CHUNK, CHUNK)),
        _const_spec((CHUNK, SGU_WIDTH)),
    ]
    out_shape = [
        jax.ShapeDtypeStruct((n_tok, QK_WIDTH), BF16),
        jax.ShapeDtypeStruct((n_tok, QK_WIDTH), BF16),
        jax.ShapeDtypeStruct((n_tok, MLA_WIDTH), BF16),
        jax.ShapeDtypeStruct((n_tok, SGU_WIDTH), BF16),
    ]
    out_specs = [
        pl.BlockSpec((tm, QK_WIDTH), lambda i: (i, 0)),
        pl.BlockSpec((tm, QK_WIDTH), lambda i: (i, 0)),
        pl.BlockSpec((tm, MLA_WIDTH), lambda i: (i, 0)),
        pl.BlockSpec((tm, SGU_WIDTH), lambda i: (i, 0)),
    ]
    if emit_cache:
        out_shape += [jax.ShapeDtypeStruct((n_tok, KV_LORA), F32),
                      jax.ShapeDtypeStruct((n_tok, QK_ROPE), F32)]
        out_specs += [pl.BlockSpec((tm, KV_LORA), lambda i: (i, 0)),
                      pl.BlockSpec((tm, QK_ROPE), lambda i: (i, 0))]
    return pl.pallas_call(
        functools.partial(_phase1_kernel, emit_cache),
        out_shape=out_shape,
        grid=grid,
        in_specs=in_specs,
        out_specs=out_specs,
        compiler_params=pltpu.CompilerParams(
            dimension_semantics=("arbitrary",), vmem_limit_bytes=VMEM_LIMIT),
        name="phase1_ctx" if emit_cache else "phase1_lat",
    )(x2d, mod3, ktab, qtab, wts["g_attn_pre"], wts["w_in"], wts["g_q"], wts["w_uq"],
      wts["g_kv"], wts["w_k"], wts["w_uv"], wts["g_sgu"], wts["beta_sgu"], wts["w_sgu"],
      wts["bias_sgu"])


def _cache_kv_kernel(ckv_ref, kr_ref, w_k_ref, w_uv_ref, k_ref, v_ref):
    ckv_b = ckv_ref[...].astype(BF16)
    kr_b = kr_ref[...].astype(BF16)
    lo = KV_LORA
    hi = KV_LORA + 2 * QK_ROPE
    k = (_dot(ckv_b, w_k_ref[0:KV_LORA, :]) + _dot(kr_b, w_k_ref[lo:lo + QK_ROPE, :])
         + _dot(kr_b, w_k_ref[hi:hi + QK_ROPE, :]))
    k_ref[...] = k.astype(BF16)
    v_ref[...] = _dot(ckv_b, w_uv_ref[...]).astype(BF16)


def _cache_kv(ckv2d, kr2d, wts):
    n_tok = ckv2d.shape[0]
    tm = TOKEN_TILE
    return pl.pallas_call(
        _cache_kv_kernel,
        out_shape=[jax.ShapeDtypeStruct((n_tok, QK_WIDTH), BF16),
                   jax.ShapeDtypeStruct((n_tok, MLA_WIDTH), BF16)],
        grid=(n_tok // tm,),
        in_specs=[
            pl.BlockSpec((tm, KV_LORA), lambda i: (i, 0)),
            pl.BlockSpec((tm, QK_ROPE), lambda i: (i, 0)),
            _const_spec((2 * LANES, QK_WIDTH)),
            _const_spec((KV_LORA, MLA_WIDTH)),
        ],
        out_specs=[pl.BlockSpec((tm, QK_WIDTH), lambda i: (i, 0)),
                   pl.BlockSpec((tm, MLA_WIDTH), lambda i: (i, 0))],
        compiler_params=pltpu.CompilerParams(
            dimension_semantics=("arbitrary",), vmem_limit_bytes=VMEM_LIMIT),
        name="cache_kv",
    )(ckv2d, kr2d, wts["w_k"], wts["w_uv"])


def _phase2_kernel(n_kv, x_ref, mod_ref, q_ref, sgu_ref, *refs):
    kv_refs = refs[:2 * n_kv]
    (w_o_ref, g_post_ref, g_fpre_ref, g_fpost_ref, w_ff1_ref, w_ff2_ref, o_ref) = refs[2 * n_kv:]
    tq = x_ref.shape[0]
    low_half = lax.broadcasted_iota(jnp.int32, (tq, LANES), 1) < V_HEAD
    pair_out = []
    head_out = None
    for hd in range(MLA_HEADS):
        qh = q_ref[:, hd * HEAD_SLOT:(hd + 1) * HEAD_SLOT]
        slab = hd // 2
        scores = []
        for t in range(n_kv):
            kh = kv_refs[2 * t][:, hd * HEAD_SLOT:(hd + 1) * HEAD_SLOT]
            scores.append(lax.dot_general(qh, kh, (((1,), (1,)), ((), ())),
                                          preferred_element_type=F32))
        m = scores[0].max(axis=-1, keepdims=True)
        for s in scores[1:]:
            m = jnp.maximum(m, s.max(axis=-1, keepdims=True))
        denom = None
        acc = None
        for t in range(n_kv):
            p = jnp.exp(scores[t] - m)
            ps = p.sum(axis=-1, keepdims=True)
            denom = ps if denom is None else denom + ps
            vh = kv_refs[2 * t + 1][:, slab * LANES:(slab + 1) * LANES]
            pv = _dot(p.astype(BF16), vh)
            acc = pv if acc is None else acc + pv
        o = acc / denom
        if hd % 2 == 0:
            head_out = o
        else:
            pair_out.append(jnp.where(low_half, head_out, o).astype(BF16))
    mix_in = jnp.concatenate(pair_out + [sgu_ref[...]], axis=1)
    mix = _dot(mix_in, w_o_ref[...])

    x = x_ref[...]
    gate_a = mod_ref[2:3, :]
    shift_f = mod_ref[3:4, :]
    scale_f = mod_ref[4:5, :]
    gate_f = mod_ref[5:6, :]
    x1 = x + gate_a * _rms(mix, g_post_ref[...])
    h = (_rms(x1, g_fpre_ref[...]) * (1.0 + scale_f) + shift_f).astype(BF16)
    f = None
    for c in range(D_FF // FF_CHUNK):
        hid = jnp.maximum(_dot(h, w_ff1_ref[:, c * FF_CHUNK:(c + 1) * FF_CHUNK]), 0.0)
        part = _dot((hid * hid).astype(BF16), w_ff2_ref[c * FF_CHUNK:(c + 1) * FF_CHUNK, :])
        f = part if f is None else f + part
    o_ref[...] = x1 + gate_f * _rms(f, g_fpost_ref[...])


def _phase2(x2d, mod3, mod_row_fn, q, sgu, kv_list, kv_lens, n_batch, seq, wts, name):
    tq = TOKEN_TILE
    nq = seq // tq
    n_kv = len(kv_list) // 2
    in_specs = [
        pl.BlockSpec((tq, D_MODEL), lambda b, i: (b * nq + i, 0)),
        pl.BlockSpec((None, N_MOD, D_MODEL), lambda b, i: (mod_row_fn(b), 0, 0)),
        pl.BlockSpec((tq, QK_WIDTH), lambda b, i: (b * nq + i, 0)),
        pl.BlockSpec((tq, SGU_WIDTH), lambda b, i: (b * nq + i, 0)),
    ]
    for t in range(n_kv):
        in_specs.append(pl.BlockSpec((kv_lens[t], QK_WIDTH), lambda b, i: (b, 0)))
        in_specs.append(pl.BlockSpec((kv_lens[t], MLA_WIDTH), lambda b, i: (b, 0)))
    in_specs += [
        _const_spec((D_MODEL, D_MODEL)),
        _const_spec((1, D_MODEL)),
        _const_spec((1, D_MODEL)),
        _const_spec((1, D_MODEL)),
        _const_spec((D_MODEL, D_FF)),
        _const_spec((D_FF, D_MODEL)),
    ]
    return pl.pallas_call(
        functools.partial(_phase2_kernel, n_kv),
        out_shape=jax.ShapeDtypeStruct(x2d.shape, F32),
        grid=(n_batch, nq),
        in_specs=in_specs,
        out_specs=pl.BlockSpec((tq, D_MODEL), lambda b, i: (b * nq + i, 0)),
        compiler_params=pltpu.CompilerParams(
            dimension_semantics=("arbitrary", "arbitrary"), vmem_limit_bytes=VMEM_LIMIT),
        name=name,
    )(x2d, mod3, q, sgu, *kv_list, wts["w_o"], wts["g_attn_post"], wts["g_ffn_pre"],
      wts["g_ffn_post"], wts["w_ff1"], wts["w_ff2"])


def _pair_swap(w):
    shp = w.shape
    return w.reshape(shp[:-1] + (shp[-1] // 2, 2))[..., ::-1].reshape(shp)


def _prepare_weights(g_attn_pre, g_attn_post, w_in, g_q, w_uq, g_kv, w_ukv, w_sgu, b_sgu, g_sgu,
                     beta_sgu, w_o, g_ffn_pre, g_ffn_post, w_ff1, w_ff2):
    i0, i1, i2 = Q_LORA, Q_LORA + KV_LORA, Q_LORA + KV_LORA + QK_ROPE
    w_kr = w_in[:, i1:i2]
    w_kr_sw = _pair_swap(w_kr)
    w_in_ext = jnp.concatenate(
        [w_in[:, :i1], w_kr, w_kr_sw, w_kr, w_kr_sw, w_in[:, i2:]], axis=1).astype(BF16)

    w_uq_h = w_uq.reshape(Q_LORA, MLA_HEADS, QK_NOPE + QK_ROPE)
    w_uq_ext = jnp.concatenate(
        [w_uq_h, _pair_swap(w_uq_h[..., QK_NOPE:])], axis=-1).reshape(Q_LORA, QK_WIDTH).astype(BF16)

    w_ukv_h = w_ukv.reshape(KV_LORA, MLA_HEADS, QK_NOPE + V_HEAD)
    w_uk_slots = jnp.concatenate(
        [w_ukv_h[..., :QK_NOPE], jnp.zeros((KV_LORA, MLA_HEADS, HEAD_SLOT - QK_NOPE), F32)],
        axis=-1).reshape(KV_LORA, QK_WIDTH)
    eye = jnp.eye(QK_ROPE, dtype=F32)
    zeros_rope = jnp.zeros((QK_ROPE, QK_ROPE), F32)
    zeros_nope = jnp.zeros((QK_ROPE, QK_NOPE), F32)
    to_lo = jnp.tile(jnp.concatenate([zeros_nope, eye, zeros_rope], axis=1), (1, MLA_HEADS))
    to_hi = jnp.tile(jnp.concatenate([zeros_nope, zeros_rope, eye], axis=1), (1, MLA_HEADS))
    w_k = jnp.concatenate([w_uk_slots, to_lo, to_lo, to_hi, to_hi], axis=0).astype(BF16)
    w_uv = w_ukv_h[..., QK_NOPE:].reshape(KV_LORA, MLA_WIDTH).astype(BF16)

    w_sgu_pair = w_sgu.reshape(SGU_HEADS // 2, 2 * CHUNK, CHUNK).astype(BF16)
    bias_sgu = jnp.repeat(b_sgu.T, SGU_HEAD_DIM, axis=1)
    row = lambda a: a.reshape(1, -1)
    return {
        "g_attn_pre": row(g_attn_pre), "g_attn_post": row(g_attn_post), "w_in": w_in_ext,
        "g_q": row(g_q), "w_uq": w_uq_ext, "g_kv": row(g_kv), "w_k": w_k, "w_uv": w_uv,
        "g_sgu": row(g_sgu), "beta_sgu": row(beta_sgu), "w_sgu": w_sgu_pair, "bias_sgu": bias_sgu,
        "w_o": w_o.astype(BF16), "g_ffn_pre": row(g_ffn_pre), "g_ffn_post": row(g_ffn_post),
        "w_ff1": w_ff1.astype(BF16), "w_ff2": w_ff2.astype(BF16),
    }


def _rope_tables(n_tok):
    rows = n_tok // GRID_W
    row = jnp.broadcast_to(jnp.arange(rows)[:, None], (rows, GRID_W)).reshape(-1).astype(F32)
    col = jnp.broadcast_to(jnp.arange(GRID_W)[None, :], (rows, GRID_W)).reshape(-1).astype(F32)
    freqs = 1.0 / (ROPE_BASE ** (jnp.arange(AXIS_PAIRS, dtype=F32) / AXIS_PAIRS))
    ang = jnp.concatenate([row[:, None] * freqs, col[:, None] * freqs], axis=-1)
    cos = jnp.repeat(jnp.cos(ang), 2, axis=1)
    sin = jnp.repeat(jnp.sin(ang), 2, axis=1)
    sign = jnp.tile(jnp.array([-1.0, 1.0], F32), QK_ROPE // 2)
    sin = sin * sign
    ktab = jnp.concatenate([cos, sin, cos, sin], axis=1)
    qtab = ATTN_SCALE * jnp.concatenate([jnp.ones((n_tok, QK_NOPE), F32), cos, sin], axis=1)
    return ktab, qtab


def kernel(x_prompt, x_sample, cache_ckv, cache_krope, c, c_ctx, w_mod, b_mod, g_attn_pre,
           g_attn_post, w_in, g_q, w_uq, g_kv, w_ukv, w_sgu, b_sgu, g_sgu, beta_sgu, w_o,
           g_ffn_pre, g_ffn_post, w_ff1, w_ff2):
    batch, seq, _ = x_prompt.shape
    dec_batch, dec_seq, _ = x_sample.shape
    past_len = cache_ckv.shape[2]
    depth = w_mod.shape[0]
    assert depth == 1

    wts = _prepare_weights(g_attn_pre[0], g_attn_post[0], w_in[0], g_q[0], w_uq[0], g_kv[0],
                           w_ukv[0], w_sgu[0], b_sgu[0], g_sgu[0], beta_sgu[0], w_o[0],
                           g_ffn_pre[0], g_ffn_post[0], w_ff1[0], w_ff2[0])

    cond8 = jnp.concatenate(
        [c_ctx[None, :], c, jnp.zeros((MOD_ROWS - 1 - dec_batch, D_MODEL), F32)], axis=0)
    mod3 = _modulation(cond8, w_mod[0], b_mod[0].reshape(1, -1)).reshape(MOD_ROWS, N_MOD, D_MODEL)

    ktab_ctx = jnp.concatenate(
        [jnp.ones((1, QK_ROPE), F32), jnp.zeros((1, LANES - QK_ROPE), F32)], axis=1)
    qtab_ctx = ATTN_SCALE * jnp.concatenate(
        [jnp.ones((1, QK_NOPE + QK_ROPE), F32), jnp.zeros((1, QK_ROPE), F32)], axis=1)
    xp2d = x_prompt.reshape(batch * seq, D_MODEL)
    q_c, k_c, v_c, sgu_c, ckv_c, kr_c = _phase1(
        xp2d, mod3, lambda i: 0, ktab_ctx, qtab_ctx, lambda i: 0, wts, True)
    y_prompt = _phase2(xp2d, mod3, lambda b: 0, q_c, sgu_c, [k_c, v_c], [seq], batch, seq, wts,
                       "phase2_ctx")

    ktab_lat, qtab_lat = _rope_tables(dec_seq)
    tiles_per_seq = dec_seq // TOKEN_TILE
    xs2d = x_sample.reshape(dec_batch * dec_seq, D_MODEL)
    q_l, k_l, v_l, sgu_l = _phase1(
        xs2d, mod3, lambda i: 1 + i // tiles_per_seq, ktab_lat, qtab_lat,
        lambda i: i % tiles_per_seq, wts, False)
    k_p, v_p = _cache_kv(cache_ckv[:, 0].reshape(dec_batch * past_len, KV_LORA),
                         cache_krope[:, 0].reshape(dec_batch * past_len, QK_ROPE), wts)
    y_sample = _phase2(xs2d, mod3, lambda b: 1 + b, q_l, sgu_l, [k_p, v_p, k_l, v_l],
                       [past_len, dec_seq], dec_batch, dec_seq, wts, "phase2_lat")

    return (y_prompt.reshape(batch, seq, D_MODEL),
            y_sample.reshape(dec_batch, dec_seq, D_MODEL),
            ckv_c.reshape(batch, 1, seq, KV_LORA),
            kr_c.reshape(batch, 1, seq, QK_ROPE))
```

```python
import functools
import math

import jax
import jax.numpy as jnp
from jax import lax
from jax.experimental import pallas as pl
from jax.experimental.pallas import tpu as pltpu

D_MODEL = 1024
GRID_W = 64
MLA_HEADS = 8
QK_NOPE = 64
QK_ROPE = 32
V_HEAD = 64
Q_LORA = 256
KV_LORA = 128
MLA_WIDTH = MLA_HEADS * V_HEAD
SGU_HEADS = 8
SGU_WIDTH = D_MODEL - MLA_WIDTH
SGU_HEAD_DIM = SGU_WIDTH // SGU_HEADS
CHUNK = 128
D_FF = 4 * D_MODEL
AXIS_PAIRS = QK_ROPE // 4
ROPE_BASE = 10000.0
EPS = 1e-6
N_MOD = 6
ATTN_SCALE = (QK_NOPE + QK_ROPE) ** -0.5

LANES = 128
HEAD_SLOT = LANES
QK_WIDTH = MLA_HEADS * HEAD_SLOT
PROJ_WIDTH = 1536
MOD_ROWS = 8
TOKEN_TILE = 256
SUB_ROWS = 256
PHASE2_ROWS = 512
FF_CHUNK = 1024
VMEM_LIMIT = 56 * 1024 * 1024

BF16 = jnp.bfloat16
F32 = jnp.float32


def _dot(a, b):
    return jnp.dot(a, b, preferred_element_type=F32)


def _rms(x, g):
    return x * lax.rsqrt(jnp.mean(x * x, axis=-1, keepdims=True) + EPS) * g


def _gelu(x):
    inner = math.sqrt(2.0 / math.pi) * (x + 0.044715 * (x * x * x))
    return x * (0.5 * (1.0 + jnp.tanh(inner)))


def _mod_kernel(cond_ref, w_ref, b_ref, o_ref):
    cnd = cond_ref[...]
    act = cnd * (1.0 / (1.0 + jnp.exp(-cnd)))
    o_ref[...] = _dot(act.astype(BF16), w_ref[...].astype(BF16)) + b_ref[...]


def _modulation(cond8, w_mod, b_mod):
    n = w_mod.shape[1]
    tn = 1536
    return pl.pallas_call(
        _mod_kernel,
        out_shape=jax.ShapeDtypeStruct((MOD_ROWS, n), F32),
        grid=(n // tn,),
        in_specs=[
            pl.BlockSpec((MOD_ROWS, D_MODEL), lambda j: (0, 0)),
            pl.BlockSpec((D_MODEL, tn), lambda j: (0, j)),
            pl.BlockSpec((1, tn), lambda j: (0, j)),
        ],
        out_specs=pl.BlockSpec((MOD_ROWS, tn), lambda j: (0, j)),
        compiler_params=pltpu.CompilerParams(
            dimension_semantics=("arbitrary",), vmem_limit_bytes=VMEM_LIMIT),
        name="modulation",
    )(cond8, w_mod, b_mod)


def _phase1_kernel(emit_cache, x_ref, mod_ref, ktab_ref, qtab_ref, g_pre_ref, w_in_ref, g_q_ref,
                   w_uq_ref, g_kv_ref, w_k_ref, w_uv_ref, g_sgu_ref, beta_sgu_ref, w_sgu_ref,
                   bias_ref, *out_refs):
    if emit_cache:
        q_ref, k_ref, v_ref, sgu_ref, ckv_ref, kr_ref = out_refs
    else:
        q_ref, k_ref, v_ref, sgu_ref = out_refs
    tm = x_ref.shape[0]
    x = x_ref[...]
    shift_a = mod_ref[0:1, :]
    scale_a = mod_ref[1:2, :]
    h = _rms(x, g_pre_ref[...]) * (1.0 + scale_a) + shift_a
    proj = _dot(h.astype(BF16), w_in_ref[...])

    cq = _rms(proj[:, 0:Q_LORA], g_q_ref[...])
    q = _dot(cq.astype(BF16), w_uq_ref[...])
    qtab = qtab_ref[...]
    q_ref[...] = jnp.concatenate(
        [(q[:, s * HEAD_SLOT:(s + 1) * HEAD_SLOT] * qtab).astype(BF16) for s in range(MLA_HEADS)],
        axis=1)

    ckv_n = _rms(proj[:, Q_LORA:Q_LORA + KV_LORA], g_kv_ref[...])
    rope_slab = proj[:, Q_LORA + KV_LORA:Q_LORA + KV_LORA + LANES]
    if emit_cache:
        ckv_ref[...] = ckv_n
        kr_ref[...] = rope_slab[:, 0:QK_ROPE]
    ckv_b = ckv_n.astype(BF16)
    kin = jnp.concatenate([ckv_b, (rope_slab * ktab_ref[...]).astype(BF16)], axis=1)
    k_ref[...] = _dot(kin, w_k_ref[...]).astype(BF16)
    v_ref[...] = _dot(ckv_b, w_uv_ref[...]).astype(BF16)

    u = _gelu(proj[:, 512:512 + SGU_WIDTH])
    vv = _gelu(proj[:, 512 + SGU_WIDTH:512 + 2 * SGU_WIDTH])
    mu = jnp.mean(vv, axis=-1, keepdims=True)
    vc = vv - mu
    var = jnp.mean(vc * vc, axis=-1, keepdims=True)
    vn = (vc * lax.rsqrt(var + EPS) * g_sgu_ref[...] + beta_sgu_ref[...]).astype(BF16)
    n_chunks = tm // CHUNK
    low_half = lax.broadcasted_iota(jnp.int32, (CHUNK, LANES), 1) < SGU_HEAD_DIM
    bias = bias_ref[...]
    mixed_rows = [[] for _ in range(n_chunks)]
    for j in range(SGU_WIDTH // LANES):
        rhs = jnp.concatenate(
            [vn[n * CHUNK:(n + 1) * CHUNK, j * LANES:(j + 1) * LANES] for n in range(n_chunks)],
            axis=1)
        o = _dot(w_sgu_ref[j], rhs)
        for n in range(n_chunks):
            even = o[0:CHUNK, n * LANES:(n + 1) * LANES]
            odd = o[CHUNK:2 * CHUNK, n * LANES:(n + 1) * LANES]
            mixed_rows[n].append(jnp.where(low_half, even, odd))
    for n in range(n_chunks):
        mixed = jnp.concatenate(mixed_rows[n], axis=1) + bias
        sgu_ref[n * CHUNK:(n + 1) * CHUNK, :] = (
            u[n * CHUNK:(n + 1) * CHUNK, :] * mixed).astype(BF16)


def _const_spec(shape):
    nd = len(shape)
    return pl.BlockSpec(shape, lambda *_: (0,) * nd, pipeline_mode=pl.Buffered(1))


def _phase1(x2d, mod3, mod_row_fn, ktab, qtab, tab_fn, wts, emit_cache):
    n_tok = x2d.shape[0]
    tm = TOKEN_TILE
    grid = (n_tok // tm,)
    tab_block = (ktab.shape[0] if ktab.shape[0] == 1 else tm, LANES)
    in_specs = [
        pl.BlockSpec((tm, D_MODEL), lambda i: (i, 0)),
        pl.BlockSpec((None, N_MOD, D_MODEL), lambda i: (mod_row_fn(i), 0, 0)),
        pl.BlockSpec(tab_block, lambda i: (tab_fn(i), 0)),
        pl.BlockSpec(tab_block, lambda i: (tab_fn(i), 0)),
        _const_spec((1, D_MODEL)),
        _const_spec((D_MODEL, PROJ_WIDTH)),
        _const_spec((1, Q_LORA)),
        _const_spec((Q_LORA, QK_WIDTH)),
        _const_spec((1, KV_LORA)),
        _const_spec((2 * LANES, QK_WIDTH)),
        _const_spec((KV_LORA, MLA_WIDTH)),
        _const_spec((1, SGU_WIDTH)),
        _const_spec((1, SGU_WIDTH)),
        _const_spec((SGU_WIDTH // LANES, 2 * CHUNK, CHUNK)),
        _const_spec((CHUNK, SGU_WIDTH)),
    ]
    out_shape = [
        jax.ShapeDtypeStruct((n_tok, QK_WIDTH), BF16),
        jax.ShapeDtypeStruct((n_tok, QK_WIDTH), BF16),
        jax.ShapeDtypeStruct((n_tok, MLA_WIDTH), BF16),
        jax.ShapeDtypeStruct((n_tok, SGU_WIDTH), BF16),
    ]
    out_specs = [
        pl.BlockSpec((tm, QK_WIDTH), lambda i: (i, 0)),
        pl.BlockSpec((tm, QK_WIDTH), lambda i: (i, 0)),
        pl.BlockSpec((tm, MLA_WIDTH), lambda i: (i, 0)),
        pl.BlockSpec((tm, SGU_WIDTH), lambda i: (i, 0)),
    ]
    if emit_cache:
        out_shape += [jax.ShapeDtypeStruct((n_tok, KV_LORA), F32),
                      jax.ShapeDtypeStruct((n_tok, QK_ROPE), F32)]
        out_specs += [pl.BlockSpec((tm, KV_LORA), lambda i: (i, 0)),
                      pl.BlockSpec((tm, QK_ROPE), lambda i: (i, 0))]
    return pl.pallas_call(
        functools.partial(_phase1_kernel, emit_cache),
        out_shape=out_shape,
        grid=grid,
        in_specs=in_specs,
        out_specs=out_specs,
        compiler_params=pltpu.CompilerParams(
            dimension_semantics=("arbitrary",), vmem_limit_bytes=VMEM_LIMIT),
        name="phase1_ctx" if emit_cache else "phase1_lat",
    )(x2d, mod3, ktab, qtab, wts["g_attn_pre"], wts["w_in"], wts["g_q"], wts["w_uq"],
      wts["g_kv"], wts["w_k"], wts["w_uv"], wts["g_sgu"], wts["beta_sgu"], wts["w_sgu"],
      wts["bias_sgu"])


def _cache_kv_kernel(ckv_ref, kr_ref, w_k_ref, w_uv_ref, k_ref, v_ref):
    ckv_b = ckv_ref[...].astype(BF16)
    kr_b = kr_ref[...].astype(BF16)
    lo = KV_LORA
    hi = KV_LORA + 2 * QK_ROPE
    k = (_dot(ckv_b, w_k_ref[0:KV_LORA, :]) + _dot(kr_b, w_k_ref[lo:lo + QK_ROPE, :])
         + _dot(kr_b, w_k_ref[hi:hi + QK_ROPE, :]))
    k_ref[...] = k.astype(BF16)
    v_ref[...] = _dot(ckv_b, w_uv_ref[...]).astype(BF16)


def _cache_kv(ckv2d, kr2d, wts):
    n_tok = ckv2d.shape[0]
    tm = TOKEN_TILE
    return pl.pallas_call(
        _cache_kv_kernel,
        out_shape=[jax.ShapeDtypeStruct((n_tok, QK_WIDTH), BF16),
                   jax.ShapeDtypeStruct((n_tok, MLA_WIDTH), BF16)],
        grid=(n_tok // tm,),
        in_specs=[
            pl.BlockSpec((tm, KV_LORA), lambda i: (i, 0)),
            pl.BlockSpec((tm, QK_ROPE), lambda i: (i, 0)),
            _const_spec((2 * LANES, QK_WIDTH)),
            _const_spec((KV_LORA, MLA_WIDTH)),
        ],
        out_specs=[pl.BlockSpec((tm, QK_WIDTH), lambda i: (i, 0)),
                   pl.BlockSpec((tm, MLA_WIDTH), lambda i: (i, 0))],
        compiler_params=pltpu.CompilerParams(
            dimension_semantics=("arbitrary",), vmem_limit_bytes=VMEM_LIMIT),
        name="cache_kv",
    )(ckv2d, kr2d, wts["w_k"], wts["w_uv"])


def _attend(q_ref, kv_views, r0, low_half):
    pair_out = []
    head_out = None
    for hd in range(MLA_HEADS):
        qh = q_ref[r0:r0 + SUB_ROWS, hd * HEAD_SLOT:(hd + 1) * HEAD_SLOT]
        slab = hd // 2
        scores = []
        for k_view, _ in kv_views:
            kh = k_view(hd * HEAD_SLOT, HEAD_SLOT)
            scores.append(lax.dot_general(qh, kh, (((1,), (1,)), ((), ())),
                                          preferred_element_type=F32))
        m = scores[0].max(axis=-1, keepdims=True)
        for s in scores[1:]:
            m = jnp.maximum(m, s.max(axis=-1, keepdims=True))
        denom = None
        acc = None
        for (_, v_view), s in zip(kv_views, scores):
            p = jnp.exp(s - m)
            ps = p.sum(axis=-1, keepdims=True)
            denom = ps if denom is None else denom + ps
            pv = _dot(p.astype(BF16), v_view(slab * LANES, LANES))
            acc = pv if acc is None else acc + pv
        o = acc / denom
        if hd % 2 == 0:
            head_out = o
        else:
            pair_out.append(jnp.where(low_half, head_out, o).astype(BF16))
    return jnp.concatenate(pair_out, axis=1)


def _phase2_kernel(n_kv, shared_kv, x_ref, mod_ref, q_ref, sgu_ref, *refs):
    kv_refs = refs[:2 * n_kv]
    (w_o_ref, g_post_ref, g_fpre_ref, g_fpost_ref, w_ff1_ref, w_ff2_ref, o_ref) = refs[2 * n_kv:]
    tq = x_ref.shape[0]
    low_half = lax.broadcasted_iota(jnp.int32, (SUB_ROWS, LANES), 1) < V_HEAD
    attn = []
    for a in range(tq // SUB_ROWS):
        r0 = a * SUB_ROWS
        views = []
        for t in range(n_kv):
            k_ref, v_ref = kv_refs[2 * t], kv_refs[2 * t + 1]
            if shared_kv:
                views.append((lambda c, w, k_ref=k_ref: k_ref[:, c:c + w],
                              lambda c, w, v_ref=v_ref: v_ref[:, c:c + w]))
            else:
                views.append((lambda c, w, k_ref=k_ref, r0=r0: k_ref[r0:r0 + SUB_ROWS, c:c + w],
                              lambda c, w, v_ref=v_ref, r0=r0: v_ref[r0:r0 + SUB_ROWS, c:c + w]))
        attn.append(_attend(q_ref, views, r0, low_half))
    mix_in = jnp.concatenate([jnp.concatenate(attn, axis=0), sgu_ref[...]], axis=1)
    mix = _dot(mix_in, w_o_ref[...])

    x = x_ref[...]
    gate_a = mod_ref[2:3, :]
    shift_f = mod_ref[3:4, :]
    scale_f = mod_ref[4:5, :]
    gate_f = mod_ref[5:6, :]
    x1 = x + gate_a * _rms(mix, g_post_ref[...])
    h = (_rms(x1, g_fpre_ref[...]) * (1.0 + scale_f) + shift_f).astype(BF16)
    f = None
    for c in range(D_FF // FF_CHUNK):
        hid = jnp.maximum(_dot(h, w_ff1_ref[:, c * FF_CHUNK:(c + 1) * FF_CHUNK]), 0.0)
        part = _dot((hid * hid).astype(BF16), w_ff2_ref[c * FF_CHUNK:(c + 1) * FF_CHUNK, :])
        f = part if f is None else f + part
    o_ref[...] = x1 + gate_f * _rms(f, g_fpost_ref[...])


def _phase2(x2d, mod3, mod_row_fn, q, sgu, kv_list, kv_rows, shared_kv, n_outer, n_inner, wts, name):
    tq = PHASE2_ROWS
    n_kv = len(kv_list) // 2
    in_specs = [
        pl.BlockSpec((tq, D_MODEL), lambda b, i: (b * n_inner + i, 0)),
        pl.BlockSpec((None, N_MOD, D_MODEL), lambda b, i: (mod_row_fn(b), 0, 0)),
        pl.BlockSpec((tq, QK_WIDTH), lambda b, i: (b * n_inner + i, 0)),
        pl.BlockSpec((tq, SGU_WIDTH), lambda b, i: (b * n_inner + i, 0)),
    ]
    for t in range(n_kv):
        in_specs.append(pl.BlockSpec((kv_rows[t], QK_WIDTH), lambda b, i: (b, 0)))
        in_specs.append(pl.BlockSpec((kv_rows[t], MLA_WIDTH), lambda b, i: (b, 0)))
    in_specs += [
        _const_spec((D_MODEL, D_MODEL)),
        _const_spec((1, D_MODEL)),
        _const_spec((1, D_MODEL)),
        _const_spec((1, D_MODEL)),
        _const_spec((D_MODEL, D_FF)),
        _const_spec((D_FF, D_MODEL)),
    ]
    return pl.pallas_call(
        functools.partial(_phase2_kernel, n_kv, shared_kv),
        out_shape=jax.ShapeDtypeStruct(x2d.shape, F32),
        grid=(n_outer, n_inner),
        in_specs=in_specs,
        out_specs=pl.BlockSpec((tq, D_MODEL), lambda b, i: (b * n_inner + i, 0)),
        compiler_params=pltpu.CompilerParams(
            dimension_semantics=("arbitrary", "arbitrary"), vmem_limit_bytes=VMEM_LIMIT),
        name=name,
    )(x2d, mod3, q, sgu, *kv_list, wts["w_o"], wts["g_attn_post"], wts["g_ffn_pre"],
      wts["g_ffn_post"], wts["w_ff1"], wts["w_ff2"])


def _pair_swap(w):
    shp = w.shape
    return w.reshape(shp[:-1] + (shp[-1] // 2, 2))[..., ::-1].reshape(shp)


def _prepare_weights(g_attn_pre, g_attn_post, w_in, g_q, w_uq, g_kv, w_ukv, w_sgu, b_sgu, g_sgu,
                     beta_sgu, w_o, g_ffn_pre, g_ffn_post, w_ff1, w_ff2):
    i0, i1, i2 = Q_LORA, Q_LORA + KV_LORA, Q_LORA + KV_LORA + QK_ROPE
    w_kr = w_in[:, i1:i2]
    w_kr_sw = _pair_swap(w_kr)
    w_in_ext = jnp.concatenate(
        [w_in[:, :i1], w_kr, w_kr_sw, w_kr, w_kr_sw, w_in[:, i2:]], axis=1).astype(BF16)

    w_uq_h = w_uq.reshape(Q_LORA, MLA_HEADS, QK_NOPE + QK_ROPE)
    w_uq_ext = jnp.concatenate(
        [w_uq_h, _pair_swap(w_uq_h[..., QK_NOPE:])], axis=-1).reshape(Q_LORA, QK_WIDTH).astype(BF16)

    w_ukv_h = w_ukv.reshape(KV_LORA, MLA_HEADS, QK_NOPE + V_HEAD)
    w_uk_slots = jnp.concatenate(
        [w_ukv_h[..., :QK_NOPE], jnp.zeros((KV_LORA, MLA_HEADS, HEAD_SLOT - QK_NOPE), F32)],
        axis=-1).reshape(KV_LORA, QK_WIDTH)
    eye = jnp.eye(QK_ROPE, dtype=F32)
    zeros_rope = jnp.zeros((QK_ROPE, QK_ROPE), F32)
    zeros_nope = jnp.zeros((QK_ROPE, QK_NOPE), F32)
    to_lo = jnp.tile(jnp.concatenate([zeros_nope, eye, zeros_rope], axis=1), (1, MLA_HEADS))
    to_hi = jnp.tile(jnp.concatenate([zeros_nope, zeros_rope, eye], axis=1), (1, MLA_HEADS))
    w_k = jnp.concatenate([w_uk_slots, to_lo, to_lo, to_hi, to_hi], axis=0).astype(BF16)
    w_uv = w_ukv_h[..., QK_NOPE:].reshape(KV_LORA, MLA_WIDTH).astype(BF16)

    w_sgu_pair = w_sgu.reshape(SGU_HEADS // 2, 2 * CHUNK, CHUNK).astype(BF16)
    bias_sgu = jnp.repeat(b_sgu.T, SGU_HEAD_DIM, axis=1)
    row = lambda a: a.reshape(1, -1)
    return {
        "g_attn_pre": row(g_attn_pre), "g_attn_post": row(g_attn_post), "w_in": w_in_ext,
        "g_q": row(g_q), "w_uq": w_uq_ext, "g_kv": row(g_kv), "w_k": w_k, "w_uv": w_uv,
        "g_sgu": row(g_sgu), "beta_sgu": row(beta_sgu), "w_sgu": w_sgu_pair, "bias_sgu": bias_sgu,
        "w_o": w_o.astype(BF16), "g_ffn_pre": row(g_ffn_pre), "g_ffn_post": row(g_ffn_post),
        "w_ff1": w_ff1.astype(BF16), "w_ff2": w_ff2.astype(BF16),
    }


def _rope_tables(n_tok):
    rows = n_tok // GRID_W
    row = jnp.broadcast_to(jnp.arange(rows)[:, None], (rows, GRID_W)).reshape(-1).astype(F32)
    col = jnp.broadcast_to(jnp.arange(GRID_W)[None, :], (rows, GRID_W)).reshape(-1).astype(F32)
    freqs = 1.0 / (ROPE_BASE ** (jnp.arange(AXIS_PAIRS, dtype=F32) / AXIS_PAIRS))
    ang = jnp.concatenate([row[:, None] * freqs, col[:, None] * freqs], axis=-1)
    cos = jnp.repeat(jnp.cos(ang), 2, axis=1)
    sin = jnp.repeat(jnp.sin(ang), 2, axis=1)
    sign = jnp.tile(jnp.array([-1.0, 1.0], F32), QK_ROPE // 2)
    sin = sin * sign
    ktab = jnp.concatenate([cos, sin, cos, sin], axis=1)
    qtab = ATTN_SCALE * jnp.concatenate([jnp.ones((n_tok, QK_NOPE), F32), cos, sin], axis=1)
    return ktab, qtab


def kernel(x_prompt, x_sample, cache_ckv, cache_krope, c, c_ctx, w_mod, b_mod, g_attn_pre,
           g_attn_post, w_in, g_q, w_uq, g_kv, w_ukv, w_sgu, b_sgu, g_sgu, beta_sgu, w_o,
           g_ffn_pre, g_ffn_post, w_ff1, w_ff2):
    batch, seq, _ = x_prompt.shape
    dec_batch, dec_seq, _ = x_sample.shape
    past_len = cache_ckv.shape[2]
    depth = w_mod.shape[0]
    assert depth == 1

    wts = _prepare_weights(g_attn_pre[0], g_attn_post[0], w_in[0], g_q[0], w_uq[0], g_kv[0],
                           w_ukv[0], w_sgu[0], b_sgu[0], g_sgu[0], beta_sgu[0], w_o[0],
                           g_ffn_pre[0], g_ffn_post[0], w_ff1[0], w_ff2[0])

    cond8 = jnp.concatenate(
        [c_ctx[None, :], c, jnp.zeros((MOD_ROWS - 1 - dec_batch, D_MODEL), F32)], axis=0)
    mod3 = _modulation(cond8, w_mod[0], b_mod[0].reshape(1, -1)).reshape(MOD_ROWS, N_MOD, D_MODEL)

    ktab_ctx = jnp.concatenate(
        [jnp.ones((1, QK_ROPE), F32), jnp.zeros((1, LANES - QK_ROPE), F32)], axis=1)
    qtab_ctx = ATTN_SCALE * jnp.concatenate(
        [jnp.ones((1, QK_NOPE + QK_ROPE), F32), jnp.zeros((1, QK_ROPE), F32)], axis=1)
    xp2d = x_prompt.reshape(batch * seq, D_MODEL)
    q_c, k_c, v_c, sgu_c, ckv_c, kr_c = _phase1(
        xp2d, mod3, lambda i: 0, ktab_ctx, qtab_ctx, lambda i: 0, wts, True)
    assert seq == SUB_ROWS and dec_seq % PHASE2_ROWS == 0
    y_prompt = _phase2(xp2d, mod3, lambda b: 0, q_c, sgu_c, [k_c, v_c], [PHASE2_ROWS], False,
                       batch * seq // PHASE2_ROWS, 1, wts, "phase2_ctx")

    ktab_lat, qtab_lat = _rope_tables(dec_seq)
    tiles_per_seq = dec_seq // TOKEN_TILE
    xs2d = x_sample.reshape(dec_batch * dec_seq, D_MODEL)
    q_l, k_l, v_l, sgu_l = _phase1(
        xs2d, mod3, lambda i: 1 + i // tiles_per_seq, ktab_lat, qtab_lat,
        lambda i: i % tiles_per_seq, wts, False)
    k_p, v_p = _cache_kv(cache_ckv[:, 0].reshape(dec_batch * past_len, KV_LORA),
                         cache_krope[:, 0].reshape(dec_batch * past_len, QK_ROPE), wts)
    y_sample = _phase2(xs2d, mod3, lambda b: 1 + b, q_l, sgu_l, [k_p, v_p, k_l, v_l],
                       [past_len, dec_seq], True, dec_batch, dec_seq // PHASE2_ROWS, wts,
                       "phase2_lat")

    return (y_prompt.reshape(batch, seq, D_MODEL),
            y_sample.reshape(dec_batch, dec_seq, D_MODEL),
            ckv_c.reshape(batch, 1, seq, KV_LORA),
            kr_c.reshape(batch, 1, seq, QK_ROPE))
```

```python
import functools
import math

import jax
import jax.numpy as jnp
from jax import lax
from jax.experimental import pallas as pl
from jax.experimental.pallas import tpu as pltpu

D_MODEL = 1024
GRID_W = 64
MLA_HEADS = 8
QK_NOPE = 64
QK_ROPE = 32
V_HEAD = 64
Q_LORA = 256
KV_LORA = 128
MLA_WIDTH = MLA_HEADS * V_HEAD
SGU_HEADS = 8
SGU_WIDTH = D_MODEL - MLA_WIDTH
SGU_HEAD_DIM = SGU_WIDTH // SGU_HEADS
CHUNK = 128
D_FF = 4 * D_MODEL
AXIS_PAIRS = QK_ROPE // 4
ROPE_BASE = 10000.0
EPS = 1e-6
N_MOD = 6
ATTN_SCALE = (QK_NOPE + QK_ROPE) ** -0.5

LANES = 128
HEAD_SLOT = LANES
QK_WIDTH = MLA_HEADS * HEAD_SLOT
ATTN_PROJ = Q_LORA + KV_LORA + LANES
MOD_ROWS = 8
MOD_K_ROWS = 128
TOKEN_TILE = 512
SUB_ROWS = 256
PHASE2_ROWS = 512
FF_CHUNK = 1024
VMEM_LIMIT = 56 * 1024 * 1024

BF16 = jnp.bfloat16
F32 = jnp.float32


def _dot(a, b):
    return jnp.dot(a, b, preferred_element_type=F32)


def _rms(x, g):
    return x * lax.rsqrt(jnp.mean(x * x, axis=-1, keepdims=True) + EPS) * g


def _gelu(x):
    inner = math.sqrt(2.0 / math.pi) * (x + 0.044715 * (x * x * x))
    return x * (0.5 * (1.0 + jnp.tanh(inner)))


def _mod_kernel(cond_ref, w_ref, b_ref, o_ref):
    @pl.when(pl.program_id(0) == 0)
    def _():
        o_ref[...] = jnp.broadcast_to(b_ref[...], o_ref.shape)

    cnd = cond_ref[...]
    act = cnd * (1.0 / (1.0 + jnp.exp(-cnd)))
    o_ref[...] += _dot(act.astype(BF16), w_ref[...].astype(BF16))


def _modulation(cond8, w_mod, b_mod):
    n = w_mod.shape[1]
    tk = MOD_K_ROWS
    return pl.pallas_call(
        _mod_kernel,
        out_shape=jax.ShapeDtypeStruct((MOD_ROWS, n), F32),
        grid=(D_MODEL // tk,),
        in_specs=[
            pl.BlockSpec((MOD_ROWS, tk), lambda k: (0, k)),
            pl.BlockSpec((tk, n), lambda k: (k, 0)),
            pl.BlockSpec((1, n), lambda k: (0, 0)),
        ],
        out_specs=pl.BlockSpec((MOD_ROWS, n), lambda k: (0, 0)),
        compiler_params=pltpu.CompilerParams(
            dimension_semantics=("arbitrary",), vmem_limit_bytes=VMEM_LIMIT),
        name="modulation",
    )(cond8, w_mod, b_mod)


def _phase1_kernel(emit_cache, x_ref, mod_ref, ktab_ref, qtab_ref, g_pre_ref, w_attn_ref, w_gate_ref,
                   g_q_ref, w_uq_ref, g_kv_ref, w_k_ref, w_uv_ref, g_sgu_ref, beta_sgu_ref,
                   w_sgu_ref, bias_ref, *out_refs):
    if emit_cache:
        q_ref, k_ref, v_ref, sgu_ref, ckv_ref, kr_ref = out_refs
    else:
        q_ref, k_ref, v_ref, sgu_ref = out_refs
    tm = x_ref.shape[0]
    x = x_ref[...]
    shift_a = mod_ref[0:1, :]
    scale_a = mod_ref[1:2, :]
    h = (_rms(x, g_pre_ref[...]) * (1.0 + scale_a) + shift_a).astype(BF16)
    proj = _dot(h, w_attn_ref[...])
    gate = _dot(h, w_gate_ref[...])

    cq = _rms(proj[:, 0:Q_LORA], g_q_ref[...])
    q = _dot(cq.astype(BF16), w_uq_ref[...])
    qtab = qtab_ref[...]
    q_ref[...] = jnp.concatenate(
        [(q[:, s * HEAD_SLOT:(s + 1) * HEAD_SLOT] * qtab).astype(BF16) for s in range(MLA_HEADS)],
        axis=1)

    ckv_n = _rms(proj[:, Q_LORA:Q_LORA + KV_LORA], g_kv_ref[...])
    rope_slab = proj[:, Q_LORA + KV_LORA:Q_LORA + KV_LORA + LANES]
    if emit_cache:
        ckv_ref[...] = ckv_n
        kr_ref[...] = rope_slab[:, 0:QK_ROPE]
    ckv_b = ckv_n.astype(BF16)
    kin = jnp.concatenate([ckv_b, (rope_slab * ktab_ref[...]).astype(BF16)], axis=1)
    k_ref[...] = _dot(kin, w_k_ref[...]).astype(BF16)
    v_ref[...] = _dot(ckv_b, w_uv_ref[...]).astype(BF16)

    u = _gelu(gate[:, 0:SGU_WIDTH])
    vv = _gelu(gate[:, SGU_WIDTH:2 * SGU_WIDTH])
    mu = jnp.mean(vv, axis=-1, keepdims=True)
    vc = vv - mu
    var = jnp.mean(vc * vc, axis=-1, keepdims=True)
    vn = (vc * lax.rsqrt(var + EPS) * g_sgu_ref[...] + beta_sgu_ref[...]).astype(BF16)
    n_chunks = tm // CHUNK
    low_half = lax.broadcasted_iota(jnp.int32, (CHUNK, LANES), 1) < SGU_HEAD_DIM
    bias = bias_ref[...]
    mixed_rows = [[] for _ in range(n_chunks)]
    for j in range(SGU_WIDTH // LANES):
        rhs = jnp.concatenate(
            [vn[n * CHUNK:(n + 1) * CHUNK, j * LANES:(j + 1) * LANES] for n in range(n_chunks)],
            axis=1)
        o = _dot(w_sgu_ref[j], rhs)
        for n in range(n_chunks):
            even = o[0:CHUNK, n * LANES:(n + 1) * LANES]
            odd = o[CHUNK:2 * CHUNK, n * LANES:(n + 1) * LANES]
            mixed_rows[n].append(jnp.where(low_half, even, odd))
    for n in range(n_chunks):
        mixed = jnp.concatenate(mixed_rows[n], axis=1) + bias
        sgu_ref[n * CHUNK:(n + 1) * CHUNK, :] = (
            u[n * CHUNK:(n + 1) * CHUNK, :] * mixed).astype(BF16)


def _const_spec(shape):
    nd = len(shape)
    return pl.BlockSpec(shape, lambda *_: (0,) * nd, pipeline_mode=pl.Buffered(1))


def _phase1(x2d, mod3, mod_row_fn, ktab, qtab, tab_fn, wts, emit_cache):
    n_tok = x2d.shape[0]
    tm = TOKEN_TILE
    grid = (n_tok // tm,)
    tab_block = (ktab.shape[0] if ktab.shape[0] == 1 else tm, LANES)
    in_specs = [
        pl.BlockSpec((tm, D_MODEL), lambda i: (i, 0)),
        pl.BlockSpec((None, N_MOD, D_MODEL), lambda i: (mod_row_fn(i), 0, 0)),
        pl.BlockSpec(tab_block, lambda i: (tab_fn(i), 0)),
        pl.BlockSpec(tab_block, lambda i: (tab_fn(i), 0)),
        _const_spec((1, D_MODEL)),
        _const_spec((D_MODEL, ATTN_PROJ)),
        _const_spec((D_MODEL, 2 * SGU_WIDTH)),
        _const_spec((1, Q_LORA)),
        _const_spec((Q_LORA, QK_WIDTH)),
        _const_spec((1, KV_LORA)),
        _const_spec((2 * LANES, QK_WIDTH)),
        _const_spec((KV_LORA, MLA_WIDTH)),
        _const_spec((1, SGU_WIDTH)),
        _const_spec((1, SGU_WIDTH)),
        _const_spec((SGU_WIDTH // LANES, 2 * CHUNK, CHUNK)),
        _const_spec((CHUNK, SGU_WIDTH)),
    ]
    out_shape = [
        jax.ShapeDtypeStruct((n_tok, QK_WIDTH), BF16),
        jax.ShapeDtypeStruct((n_tok, QK_WIDTH), BF16),
        jax.ShapeDtypeStruct((n_tok, MLA_WIDTH), BF16),
        jax.ShapeDtypeStruct((n_tok, SGU_WIDTH), BF16),
    ]
    out_specs = [
        pl.BlockSpec((tm, QK_WIDTH), lambda i: (i, 0)),
        pl.BlockSpec((tm, QK_WIDTH), lambda i: (i, 0)),
        pl.BlockSpec((tm, MLA_WIDTH), lambda i: (i, 0)),
        pl.BlockSpec((tm, SGU_WIDTH), lambda i: (i, 0)),
    ]
    if emit_cache:
        out_shape += [jax.ShapeDtypeStruct((n_tok, KV_LORA), F32),
                      jax.ShapeDtypeStruct((n_tok, QK_ROPE), F32)]
        out_specs += [pl.BlockSpec((tm, KV_LORA), lambda i: (i, 0)),
                      pl.BlockSpec((tm, QK_ROPE), lambda i: (i, 0))]
    return pl.pallas_call(
        functools.partial(_phase1_kernel, emit_cache),
        out_shape=out_shape,
        grid=grid,
        in_specs=in_specs,
        out_specs=out_specs,
        compiler_params=pltpu.CompilerParams(
            dimension_semantics=("arbitrary",), vmem_limit_bytes=VMEM_LIMIT),
        name="phase1_ctx" if emit_cache else "phase1_lat",
    )(x2d, mod3, ktab, qtab, wts["g_attn_pre"], wts["w_attn"], wts["w_gate"], wts["g_q"], wts["w_uq"],
      wts["g_kv"], wts["w_k"], wts["w_uv"], wts["g_sgu"], wts["beta_sgu"], wts["w_sgu"],
      wts["bias_sgu"])


def _cache_kv_kernel(ckv_ref, kr_ref, w_k_ref, w_uv_ref, k_ref, v_ref):
    ckv_b = ckv_ref[...].astype(BF16)
    kr_b = kr_ref[...].astype(BF16)
    lo = KV_LORA
    hi = KV_LORA + 2 * QK_ROPE
    k = (_dot(ckv_b, w_k_ref[0:KV_LORA, :]) + _dot(kr_b, w_k_ref[lo:lo + QK_ROPE, :])
         + _dot(kr_b, w_k_ref[hi:hi + QK_ROPE, :]))
    k_ref[...] = k.astype(BF16)
    v_ref[...] = _dot(ckv_b, w_uv_ref[...]).astype(BF16)


def _cache_kv(ckv2d, kr2d, wts):
    n_tok = ckv2d.shape[0]
    tm = n_tok
    return pl.pallas_call(
        _cache_kv_kernel,
        out_shape=[jax.ShapeDtypeStruct((n_tok, QK_WIDTH), BF16),
                   jax.ShapeDtypeStruct((n_tok, MLA_WIDTH), BF16)],
        grid=(n_tok // tm,),
        in_specs=[
            pl.BlockSpec((tm, KV_LORA), lambda i: (i, 0)),
            pl.BlockSpec((tm, QK_ROPE), lambda i: (i, 0)),
            _const_spec((2 * LANES, QK_WIDTH)),
            _const_spec((KV_LORA, MLA_WIDTH)),
        ],
        out_specs=[pl.BlockSpec((tm, QK_WIDTH), lambda i: (i, 0)),
                   pl.BlockSpec((tm, MLA_WIDTH), lambda i: (i, 0))],
        compiler_params=pltpu.CompilerParams(
            dimension_semantics=("arbitrary",), vmem_limit_bytes=VMEM_LIMIT),
        name="cache_kv",
    )(ckv2d, kr2d, wts["w_k"], wts["w_uv"])


def _attend(q_ref, kv_views, r0, low_half):
    pair_out = []
    head_out = None
    for hd in range(MLA_HEADS):
        qh = q_ref[r0:r0 + SUB_ROWS, hd * HEAD_SLOT:(hd + 1) * HEAD_SLOT]
        slab = hd // 2
        scores = []
        for k_view, _ in kv_views:
            kh = k_view(hd * HEAD_SLOT, HEAD_SLOT)
            scores.append(lax.dot_general(qh, kh, (((1,), (1,)), ((), ())),
                                          preferred_element_type=F32))
        m = scores[0].max(axis=-1, keepdims=True)
        for s in scores[1:]:
            m = jnp.maximum(m, s.max(axis=-1, keepdims=True))
        denom = None
        acc = None
        for (_, v_view), s in zip(kv_views, scores):
            p = jnp.exp(s - m)
            ps = p.sum(axis=-1, keepdims=True)
            denom = ps if denom is None else denom + ps
            pv = _dot(p.astype(BF16), v_view(slab * LANES, LANES))
            acc = pv if acc is None else acc + pv
        o = acc / denom
        if hd % 2 == 0:
            head_out = o
        else:
            pair_out.append(jnp.where(low_half, head_out, o).astype(BF16))
    return jnp.concatenate(pair_out, axis=1)


def _phase2_kernel(n_kv, shared_kv, x_ref, mod_ref, q_ref, sgu_ref, *refs):
    kv_refs = refs[:2 * n_kv]
    (w_o_ref, g_post_ref, g_fpre_ref, g_fpost_ref, w_ff1_ref, w_ff2_ref, o_ref) = refs[2 * n_kv:]
    tq = x_ref.shape[0]
    low_half = lax.broadcasted_iota(jnp.int32, (SUB_ROWS, LANES), 1) < V_HEAD
    attn = []
    for a in range(tq // SUB_ROWS):
        r0 = a * SUB_ROWS
        views = []
        for t in range(n_kv):
            k_ref, v_ref = kv_refs[2 * t], kv_refs[2 * t + 1]
            if shared_kv:
                views.append((lambda c, w, k_ref=k_ref: k_ref[:, c:c + w],
                              lambda c, w, v_ref=v_ref: v_ref[:, c:c + w]))
            else:
                views.append((lambda c, w, k_ref=k_ref, r0=r0: k_ref[r0:r0 + SUB_ROWS, c:c + w],
                              lambda c, w, v_ref=v_ref, r0=r0: v_ref[r0:r0 + SUB_ROWS, c:c + w]))
        attn.append(_attend(q_ref, views, r0, low_half))
    mix_in = jnp.concatenate([jnp.concatenate(attn, axis=0), sgu_ref[...]], axis=1)
    mix = _dot(mix_in, w_o_ref[...])

    x = x_ref[...]
    gate_a = mod_ref[2:3, :]
    shift_f = mod_ref[3:4, :]
    scale_f = mod_ref[4:5, :]
    gate_f = mod_ref[5:6, :]
    x1 = x + gate_a * _rms(mix, g_post_ref[...])
    h = (_rms(x1, g_fpre_ref[...]) * (1.0 + scale_f) + shift_f).astype(BF16)
    f = None
    for c in range(D_FF // FF_CHUNK):
        hid = jnp.maximum(_dot(h, w_ff1_ref[:, c * FF_CHUNK:(c + 1) * FF_CHUNK]), 0.0)
        part = _dot((hid * hid).astype(BF16), w_ff2_ref[c * FF_CHUNK:(c + 1) * FF_CHUNK, :])
        f = part if f is None else f + part
    o_ref[...] = x1 + gate_f * _rms(f, g_fpost_ref[...])


def _phase2(x2d, mod3, mod_row_fn, q, sgu, kv_list, kv_rows, shared_kv, n_outer, n_inner, wts, name):
    tq = PHASE2_ROWS
    n_kv = len(kv_list) // 2
    in_specs = [
        pl.BlockSpec((tq, D_MODEL), lambda b, i: (b * n_inner + i, 0)),
        pl.BlockSpec((None, N_MOD, D_MODEL), lambda b, i: (mod_row_fn(b), 0, 0)),
        pl.BlockSpec((tq, QK_WIDTH), lambda b, i: (b * n_inner + i, 0)),
        pl.BlockSpec((tq, SGU_WIDTH), lambda b, i: (b * n_inner + i, 0)),
    ]
    for t in range(n_kv):
        in_specs.append(pl.BlockSpec((kv_rows[t], QK_WIDTH), lambda b, i: (b, 0)))
        in_specs.append(pl.BlockSpec((kv_rows[t], MLA_WIDTH), lambda b, i: (b, 0)))
    in_specs += [
        _const_spec((D_MODEL, D_MODEL)),
        _const_spec((1, D_MODEL)),
        _const_spec((1, D_MODEL)),
        _const_spec((1, D_MODEL)),
        _const_spec((D_MODEL, D_FF)),
        _const_spec((D_FF, D_MODEL)),
    ]
    return pl.pallas_call(
        functools.partial(_phase2_kernel, n_kv, shared_kv),
        out_shape=jax.ShapeDtypeStruct(x2d.shape, F32),
        grid=(n_outer, n_inner),
        in_specs=in_specs,
        out_specs=pl.BlockSpec((tq, D_MODEL), lambda b, i: (b * n_inner + i, 0)),
        compiler_params=pltpu.CompilerParams(
            dimension_semantics=("arbitrary", "arbitrary"), vmem_limit_bytes=VMEM_LIMIT),
        name=name,
    )(x2d, mod3, q, sgu, *kv_list, wts["w_o"], wts["g_attn_post"], wts["g_ffn_pre"],
      wts["g_ffn_post"], wts["w_ff1"], wts["w_ff2"])


def _pair_swap(w):
    shp = w.shape
    return w.reshape(shp[:-1] + (shp[-1] // 2, 2))[..., ::-1].reshape(shp)


def _prepare_weights(g_attn_pre, g_attn_post, w_in, g_q, w_uq, g_kv, w_ukv, w_sgu, b_sgu, g_sgu,
                     beta_sgu, w_o, g_ffn_pre, g_ffn_post, w_ff1, w_ff2):
    i0, i1, i2 = Q_LORA, Q_LORA + KV_LORA, Q_LORA + KV_LORA + QK_ROPE
    w_kr = w_in[:, i1:i2].astype(BF16)
    w_kr_sw = _pair_swap(w_kr)
    w_attn = jnp.concatenate([w_in[:, :i1].astype(BF16), w_kr, w_kr_sw, w_kr, w_kr_sw], axis=1)
    w_gate = w_in[:, i2:].astype(BF16)

    w_uq_h = w_uq.reshape(Q_LORA, MLA_HEADS, QK_NOPE + QK_ROPE)
    w_uq_ext = jnp.concatenate(
        [w_uq_h, _pair_swap(w_uq_h[..., QK_NOPE:])], axis=-1).reshape(Q_LORA, QK_WIDTH).astype(BF16)

    w_ukv_h = w_ukv.reshape(KV_LORA, MLA_HEADS, QK_NOPE + V_HEAD)
    w_uk_slots = jnp.concatenate(
        [w_ukv_h[..., :QK_NOPE], jnp.zeros((KV_LORA, MLA_HEADS, HEAD_SLOT - QK_NOPE), F32)],
        axis=-1).reshape(KV_LORA, QK_WIDTH)
    eye = jnp.eye(QK_ROPE, dtype=F32)
    zeros_rope = jnp.zeros((QK_ROPE, QK_ROPE), F32)
    zeros_nope = jnp.zeros((QK_ROPE, QK_NOPE), F32)
    to_lo = jnp.tile(jnp.concatenate([zeros_nope, eye, zeros_rope], axis=1), (1, MLA_HEADS))
    to_hi = jnp.tile(jnp.concatenate([zeros_nope, zeros_rope, eye], axis=1), (1, MLA_HEADS))
    w_k = jnp.concatenate([w_uk_slots, to_lo, to_lo, to_hi, to_hi], axis=0).astype(BF16)
    w_uv = w_ukv_h[..., QK_NOPE:].reshape(KV_LORA, MLA_WIDTH).astype(BF16)

    w_sgu_pair = w_sgu.reshape(SGU_HEADS // 2, 2 * CHUNK, CHUNK).astype(BF16)
    bias_sgu = jnp.repeat(b_sgu.T, SGU_HEAD_DIM, axis=1)
    row = lambda a: a.reshape(1, -1)
    return {
        "g_attn_pre": row(g_attn_pre), "g_attn_post": row(g_attn_post), "w_attn": w_attn,
        "w_gate": w_gate,
        "g_q": row(g_q), "w_uq": w_uq_ext, "g_kv": row(g_kv), "w_k": w_k, "w_uv": w_uv,
        "g_sgu": row(g_sgu), "beta_sgu": row(beta_sgu), "w_sgu": w_sgu_pair, "bias_sgu": bias_sgu,
        "w_o": w_o.astype(BF16), "g_ffn_pre": row(g_ffn_pre), "g_ffn_post": row(g_ffn_post),
        "w_ff1": w_ff1.astype(BF16), "w_ff2": w_ff2.astype(BF16),
    }


def _rope_tables(n_tok):
    rows = n_tok // GRID_W
    row = jnp.broadcast_to(jnp.arange(rows)[:, None], (rows, GRID_W)).reshape(-1).astype(F32)
    col = jnp.broadcast_to(jnp.arange(GRID_W)[None, :], (rows, GRID_W)).reshape(-1).astype(F32)
    freqs = 1.0 / (ROPE_BASE ** (jnp.arange(AXIS_PAIRS, dtype=F32) / AXIS_PAIRS))
    ang = jnp.concatenate([row[:, None] * freqs, col[:, None] * freqs], axis=-1)
    cos = jnp.repeat(jnp.cos(ang), 2, axis=1)
    sin = jnp.repeat(jnp.sin(ang), 2, axis=1)
    sign = jnp.tile(jnp.array([-1.0, 1.0], F32), QK_ROPE // 2)
    sin = sin * sign
    ktab = jnp.concatenate([cos, sin, cos, sin], axis=1)
    qtab = ATTN_SCALE * jnp.concatenate([jnp.ones((n_tok, QK_NOPE), F32), cos, sin], axis=1)
    return ktab, qtab


def kernel(x_prompt, x_sample, cache_ckv, cache_krope, c, c_ctx, w_mod, b_mod, g_attn_pre,
           g_attn_post, w_in, g_q, w_uq, g_kv, w_ukv, w_sgu, b_sgu, g_sgu, beta_sgu, w_o,
           g_ffn_pre, g_ffn_post, w_ff1, w_ff2):
    batch, seq, _ = x_prompt.shape
    dec_batch, dec_seq, _ = x_sample.shape
    past_len = cache_ckv.shape[2]
    depth = w_mod.shape[0]
    assert depth == 1

    wts = _prepare_weights(g_attn_pre[0], g_attn_post[0], w_in[0], g_q[0], w_uq[0], g_kv[0],
                           w_ukv[0], w_sgu[0], b_sgu[0], g_sgu[0], beta_sgu[0], w_o[0],
                           g_ffn_pre[0], g_ffn_post[0], w_ff1[0], w_ff2[0])

    cond8 = jnp.concatenate(
        [c_ctx[None, :], c, jnp.zeros((MOD_ROWS - 1 - dec_batch, D_MODEL), F32)], axis=0)
    mod3 = _modulation(cond8, w_mod[0], b_mod[0].reshape(1, -1)).reshape(MOD_ROWS, N_MOD, D_MODEL)

    ktab_ctx = jnp.concatenate(
        [jnp.ones((1, QK_ROPE), F32), jnp.zeros((1, LANES - QK_ROPE), F32)], axis=1)
    qtab_ctx = ATTN_SCALE * jnp.concatenate(
        [jnp.ones((1, QK_NOPE + QK_ROPE), F32), jnp.zeros((1, QK_ROPE), F32)], axis=1)
    xp2d = x_prompt.reshape(batch * seq, D_MODEL)
    q_c, k_c, v_c, sgu_c, ckv_c, kr_c = _phase1(
        xp2d, mod3, lambda i: 0, ktab_ctx, qtab_ctx, lambda i: 0, wts, True)
    assert seq == SUB_ROWS and dec_seq % PHASE2_ROWS == 0
    y_prompt = _phase2(xp2d, mod3, lambda b: 0, q_c, sgu_c, [k_c, v_c], [PHASE2_ROWS], False,
                       batch * seq // PHASE2_ROWS, 1, wts, "phase2_ctx")

    ktab_lat, qtab_lat = _rope_tables(dec_seq)
    tiles_per_seq = dec_seq // TOKEN_TILE
    xs2d = x_sample.reshape(dec_batch * dec_seq, D_MODEL)
    q_l, k_l, v_l, sgu_l = _phase1(
        xs2d, mod3, lambda i: 1 + i // tiles_per_seq, ktab_lat, qtab_lat,
        lambda i: i % tiles_per_seq, wts, False)
    k_p, v_p = _cache_kv(cache_ckv[:, 0].reshape(dec_batch * past_len, KV_LORA),
                         cache_krope[:, 0].reshape(dec_batch * past_len, QK_ROPE), wts)
    y_sample = _phase2(xs2d, mod3, lambda b: 1 + b, q_l, sgu_l, [k_p, v_p, k_l, v_l],
                       [past_len, dec_seq], True, dec_batch, dec_seq // PHASE2_ROWS, wts,
                       "phase2_lat")

    return (y_prompt.reshape(batch, seq, D_MODEL),
            y_sample.reshape(dec_batch, dec_seq, D_MODEL),
            ckv_c.reshape(batch, 1, seq, KV_LORA),
            kr_c.reshape(batch, 1, seq, QK_ROPE))
```

```python
import functools
import math

import jax
import jax.numpy as jnp
from jax import lax
from jax.experimental import pallas as pl
from jax.experimental.pallas import tpu as pltpu

D_MODEL = 1024
GRID_W = 64
MLA_HEADS = 8
QK_NOPE = 64
QK_ROPE = 32
V_HEAD = 64
Q_LORA = 256
KV_LORA = 128
MLA_WIDTH = MLA_HEADS * V_HEAD
SGU_HEADS = 8
SGU_WIDTH = D_MODEL - MLA_WIDTH
SGU_HEAD_DIM = SGU_WIDTH // SGU_HEADS
CHUNK = 128
D_FF = 4 * D_MODEL
AXIS_PAIRS = QK_ROPE // 4
ROPE_BASE = 10000.0
EPS = 1e-6
N_MOD = 6
ATTN_SCALE = (QK_NOPE + QK_ROPE) ** -0.5

LANES = 128
HEAD_SLOT = LANES
QK_WIDTH = MLA_HEADS * HEAD_SLOT
ATTN_PROJ = Q_LORA + KV_LORA + LANES
MOD_ROWS = 8
MOD_K_ROWS = 128
TOKEN_TILE = 512
SUB_ROWS = 256
PHASE2_ROWS = 512
FF_CHUNK = 1024
VMEM_LIMIT = 56 * 1024 * 1024

BF16 = jnp.bfloat16
F32 = jnp.float32


def _dot(a, b):
    return jnp.dot(a, b, preferred_element_type=F32)


def _rms(x, g):
    return x * lax.rsqrt(jnp.mean(x * x, axis=-1, keepdims=True) + EPS) * g


def _gelu(x):
    inner = math.sqrt(2.0 / math.pi) * (x + 0.044715 * (x * x * x))
    return x * (0.5 * (1.0 + jnp.tanh(inner)))


def _mod_kernel(cond_a_ref, cond_b_ref, w_a_ref, w_b_ref, b_ref, o_ref):
    @pl.when(pl.program_id(0) == 0)
    def _():
        o_ref[...] = jnp.broadcast_to(b_ref[...], o_ref.shape)

    def part(cond_ref, w_ref):
        cnd = cond_ref[...]
        act = cnd * (1.0 / (1.0 + jnp.exp(-cnd)))
        return _dot(act.astype(BF16), w_ref[...].astype(BF16))

    o_ref[...] += part(cond_a_ref, w_a_ref) + part(cond_b_ref, w_b_ref)


def _modulation(cond8, w_mod, b_mod):
    n = w_mod.shape[1]
    tk = MOD_K_ROWS
    nk = D_MODEL // tk // 2
    return pl.pallas_call(
        _mod_kernel,
        out_shape=jax.ShapeDtypeStruct((MOD_ROWS, n), F32),
        grid=(nk,),
        in_specs=[
            pl.BlockSpec((MOD_ROWS, tk), lambda k: (0, k)),
            pl.BlockSpec((MOD_ROWS, tk), lambda k: (0, k + nk)),
            pl.BlockSpec((tk, n), lambda k: (k, 0)),
            pl.BlockSpec((tk, n), lambda k: (k + nk, 0)),
            pl.BlockSpec((1, n), lambda k: (0, 0)),
        ],
        out_specs=pl.BlockSpec((MOD_ROWS, n), lambda k: (0, 0)),
        compiler_params=pltpu.CompilerParams(
            dimension_semantics=("arbitrary",), vmem_limit_bytes=VMEM_LIMIT),
        name="modulation",
    )(cond8, cond8, w_mod, w_mod, b_mod)


def _phase1_kernel(emit_cache, x_ref, mod_ref, ktab_ref, qtab_ref, g_pre_ref, w_attn_ref, w_gate_ref,
                   g_q_ref, w_uq_ref, g_kv_ref, w_k_ref, w_uv_ref, g_sgu_ref, beta_sgu_ref,
                   w_sgu_ref, bias_ref, *rest):
    if emit_cache:
        (w_o32_ref, w_ff1_32_ref, w_ff2_32_ref, q_ref, k_ref, v_ref, sgu_ref, ckv_ref, kr_ref,
         w_o16_ref, w_ff1_16_ref, w_ff2_16_ref) = rest
        w_o16_ref[...] = w_o32_ref[...].astype(BF16)
        w_ff1_16_ref[...] = w_ff1_32_ref[...].astype(BF16)
        w_ff2_16_ref[...] = w_ff2_32_ref[...].astype(BF16)
    else:
        q_ref, k_ref, v_ref, sgu_ref = rest
    tm = x_ref.shape[0]
    x = x_ref[...]
    shift_a = mod_ref[0:1, :]
    scale_a = mod_ref[1:2, :]
    h = (_rms(x, g_pre_ref[...]) * (1.0 + scale_a) + shift_a).astype(BF16)
    proj = _dot(h, w_attn_ref[...])
    gate = _dot(h, w_gate_ref[...])

    cq = _rms(proj[:, 0:Q_LORA], g_q_ref[...])
    q = _dot(cq.astype(BF16), w_uq_ref[...])
    qtab = qtab_ref[...]
    q_ref[...] = jnp.concatenate(
        [(q[:, s * HEAD_SLOT:(s + 1) * HEAD_SLOT] * qtab).astype(BF16) for s in range(MLA_HEADS)],
        axis=1)

    ckv_n = _rms(proj[:, Q_LORA:Q_LORA + KV_LORA], g_kv_ref[...])
    rope_slab = proj[:, Q_LORA + KV_LORA:Q_LORA + KV_LORA + LANES]
    if emit_cache:
        ckv_ref[...] = ckv_n
        kr_ref[...] = rope_slab[:, 0:QK_ROPE]
    ckv_b = ckv_n.astype(BF16)
    kin = jnp.concatenate([ckv_b, (rope_slab * ktab_ref[...]).astype(BF16)], axis=1)
    k_ref[...] = _dot(kin, w_k_ref[...]).astype(BF16)
    v_ref[...] = _dot(ckv_b, w_uv_ref[...]).astype(BF16)

    u = _gelu(gate[:, 0:SGU_WIDTH])
    vv = _gelu(gate[:, SGU_WIDTH:2 * SGU_WIDTH])
    mu = jnp.mean(vv, axis=-1, keepdims=True)
    vc = vv - mu
    var = jnp.mean(vc * vc, axis=-1, keepdims=True)
    vn = (vc * lax.rsqrt(var + EPS) * g_sgu_ref[...] + beta_sgu_ref[...]).astype(BF16)
    n_chunks = tm // CHUNK
    low_half = lax.broadcasted_iota(jnp.int32, (CHUNK, LANES), 1) < SGU_HEAD_DIM
    bias = bias_ref[...]
    mixed_rows = [[] for _ in range(n_chunks)]
    for j in range(SGU_WIDTH // LANES):
        rhs = jnp.concatenate(
            [vn[n * CHUNK:(n + 1) * CHUNK, j * LANES:(j + 1) * LANES] for n in range(n_chunks)],
            axis=1)
        o = _dot(w_sgu_ref[j], rhs)
        for n in range(n_chunks):
            even = o[0:CHUNK, n * LANES:(n + 1) * LANES]
            odd = o[CHUNK:2 * CHUNK, n * LANES:(n + 1) * LANES]
            mixed_rows[n].append(jnp.where(low_half, even, odd))
    for n in range(n_chunks):
        mixed = jnp.concatenate(mixed_rows[n], axis=1) + bias
        sgu_ref[n * CHUNK:(n + 1) * CHUNK, :] = (
            u[n * CHUNK:(n + 1) * CHUNK, :] * mixed).astype(BF16)


def _const_spec(shape):
    nd = len(shape)
    return pl.BlockSpec(shape, lambda *_: (0,) * nd, pipeline_mode=pl.Buffered(1))


def _phase1(x2d, mod3, mod_row_fn, ktab, qtab, tab_fn, wts, emit_cache):
    n_tok = x2d.shape[0]
    tm = TOKEN_TILE
    grid = (n_tok // tm,)
    tab_block = (ktab.shape[0] if ktab.shape[0] == 1 else tm, LANES)
    in_specs = [
        pl.BlockSpec((tm, D_MODEL), lambda i: (i, 0)),
        pl.BlockSpec((None, N_MOD, D_MODEL), lambda i: (mod_row_fn(i), 0, 0)),
        pl.BlockSpec(tab_block, lambda i: (tab_fn(i), 0)),
        pl.BlockSpec(tab_block, lambda i: (tab_fn(i), 0)),
        _const_spec((1, D_MODEL)),
        _const_spec((D_MODEL, ATTN_PROJ)),
        _const_spec((D_MODEL, 2 * SGU_WIDTH)),
        _const_spec((1, Q_LORA)),
        _const_spec((Q_LORA, QK_WIDTH)),
        _const_spec((1, KV_LORA)),
        _const_spec((2 * LANES, QK_WIDTH)),
        _const_spec((KV_LORA, MLA_WIDTH)),
        _const_spec((1, SGU_WIDTH)),
        _const_spec((1, SGU_WIDTH)),
        _const_spec((SGU_WIDTH // LANES, 2 * CHUNK, CHUNK)),
        _const_spec((CHUNK, SGU_WIDTH)),
    ]
    out_shape = [
        jax.ShapeDtypeStruct((n_tok, QK_WIDTH), BF16),
        jax.ShapeDtypeStruct((n_tok, QK_WIDTH), BF16),
        jax.ShapeDtypeStruct((n_tok, MLA_WIDTH), BF16),
        jax.ShapeDtypeStruct((n_tok, SGU_WIDTH), BF16),
    ]
    out_specs = [
        pl.BlockSpec((tm, QK_WIDTH), lambda i: (i, 0)),
        pl.BlockSpec((tm, QK_WIDTH), lambda i: (i, 0)),
        pl.BlockSpec((tm, MLA_WIDTH), lambda i: (i, 0)),
        pl.BlockSpec((tm, SGU_WIDTH), lambda i: (i, 0)),
    ]
    extra_inputs = []
    if emit_cache:
        out_shape += [jax.ShapeDtypeStruct((n_tok, KV_LORA), F32),
                      jax.ShapeDtypeStruct((n_tok, QK_ROPE), F32)]
        out_specs += [pl.BlockSpec((tm, KV_LORA), lambda i: (i, 0)),
                      pl.BlockSpec((tm, QK_ROPE), lambda i: (i, 0))]
        for w in (wts["w_o32"], wts["w_ff1_32"], wts["w_ff2_32"]):
            rows, cols = w.shape
            blk = (rows // grid[0], cols)
            extra_inputs.append(w)
            in_specs.append(pl.BlockSpec(blk, lambda i: (i, 0)))
            out_shape.append(jax.ShapeDtypeStruct(w.shape, BF16))
            out_specs.append(pl.BlockSpec(blk, lambda i: (i, 0)))
    return pl.pallas_call(
        functools.partial(_phase1_kernel, emit_cache),
        out_shape=out_shape,
        grid=grid,
        in_specs=in_specs,
        out_specs=out_specs,
        compiler_params=pltpu.CompilerParams(
            dimension_semantics=("arbitrary",), vmem_limit_bytes=VMEM_LIMIT),
        name="phase1_ctx" if emit_cache else "phase1_lat",
    )(x2d, mod3, ktab, qtab, wts["g_attn_pre"], wts["w_attn"], wts["w_gate"], wts["g_q"], wts["w_uq"],
      wts["g_kv"], wts["w_k"], wts["w_uv"], wts["g_sgu"], wts["beta_sgu"], wts["w_sgu"],
      wts["bias_sgu"], *extra_inputs)


def _cache_kv_kernel(ckv_ref, kr_ref, w_k_ref, w_uv_ref, k_ref, v_ref):
    ckv_b = ckv_ref[...].astype(BF16)
    kr_b = kr_ref[...].astype(BF16)
    lo = KV_LORA
    hi = KV_LORA + 2 * QK_ROPE
    k = (_dot(ckv_b, w_k_ref[0:KV_LORA, :]) + _dot(kr_b, w_k_ref[lo:lo + QK_ROPE, :])
         + _dot(kr_b, w_k_ref[hi:hi + QK_ROPE, :]))
    k_ref[...] = k.astype(BF16)
    v_ref[...] = _dot(ckv_b, w_uv_ref[...]).astype(BF16)


def _cache_kv(ckv2d, kr2d, wts):
    n_tok = ckv2d.shape[0]
    tm = n_tok
    return pl.pallas_call(
        _cache_kv_kernel,
        out_shape=[jax.ShapeDtypeStruct((n_tok, QK_WIDTH), BF16),
                   jax.ShapeDtypeStruct((n_tok, MLA_WIDTH), BF16)],
        grid=(n_tok // tm,),
        in_specs=[
            pl.BlockSpec((tm, KV_LORA), lambda i: (i, 0)),
            pl.BlockSpec((tm, QK_ROPE), lambda i: (i, 0)),
            _const_spec((2 * LANES, QK_WIDTH)),
            _const_spec((KV_LORA, MLA_WIDTH)),
        ],
        out_specs=[pl.BlockSpec((tm, QK_WIDTH), lambda i: (i, 0)),
                   pl.BlockSpec((tm, MLA_WIDTH), lambda i: (i, 0))],
        compiler_params=pltpu.CompilerParams(
            dimension_semantics=("arbitrary",), vmem_limit_bytes=VMEM_LIMIT),
        name="cache_kv",
    )(ckv2d, kr2d, wts["w_k"], wts["w_uv"])


def _attend(q_ref, kv_views, r0, low_half):
    pair_out = []
    head_out = None
    for hd in range(MLA_HEADS):
        qh = q_ref[r0:r0 + SUB_ROWS, hd * HEAD_SLOT:(hd + 1) * HEAD_SLOT]
        slab = hd // 2
        scores = []
        for k_view, _ in kv_views:
            kh = k_view(hd * HEAD_SLOT, HEAD_SLOT)
            scores.append(lax.dot_general(qh, kh, (((1,), (1,)), ((), ())),
                                          preferred_element_type=F32))
        m = scores[0].max(axis=-1, keepdims=True)
        for s in scores[1:]:
            m = jnp.maximum(m, s.max(axis=-1, keepdims=True))
        denom = None
        acc = None
        for (_, v_view), s in zip(kv_views, scores):
            p = jnp.exp(s - m)
            ps = p.sum(axis=-1, keepdims=True)
            denom = ps if denom is None else denom + ps
            pv = _dot(p.astype(BF16), v_view(slab * LANES, LANES))
            acc = pv if acc is None else acc + pv
        o = acc / denom
        if hd % 2 == 0:
            head_out = o
        else:
            pair_out.append(jnp.where(low_half, head_out, o).astype(BF16))
    return jnp.concatenate(pair_out, axis=1)


def _phase2_kernel(n_kv, shared_kv, x_ref, mod_ref, q_ref, sgu_ref, *refs):
    kv_refs = refs[:2 * n_kv]
    (w_o_ref, g_post_ref, g_fpre_ref, g_fpost_ref, w_ff1_ref, w_ff2_ref, o_ref) = refs[2 * n_kv:]
    tq = x_ref.shape[0]
    low_half = lax.broadcasted_iota(jnp.int32, (SUB_ROWS, LANES), 1) < V_HEAD
    attn = []
    for a in range(tq // SUB_ROWS):
        r0 = a * SUB_ROWS
        views = []
        for t in range(n_kv):
            k_ref, v_ref = kv_refs[2 * t], kv_refs[2 * t + 1]
            if shared_kv:
                views.append((lambda c, w, k_ref=k_ref: k_ref[:, c:c + w],
                              lambda c, w, v_ref=v_ref: v_ref[:, c:c + w]))
            else:
                views.append((lambda c, w, k_ref=k_ref, r0=r0: k_ref[r0:r0 + SUB_ROWS, c:c + w],
                              lambda c, w, v_ref=v_ref, r0=r0: v_ref[r0:r0 + SUB_ROWS, c:c + w]))
        attn.append(_attend(q_ref, views, r0, low_half))
    mix_in = jnp.concatenate([jnp.concatenate(attn, axis=0), sgu_ref[...]], axis=1)
    mix = _dot(mix_in, w_o_ref[...])

    x = x_ref[...]
    gate_a = mod_ref[2:3, :]
    shift_f = mod_ref[3:4, :]
    scale_f = mod_ref[4:5, :]
    gate_f = mod_ref[5:6, :]
    x1 = x + gate_a * _rms(mix, g_post_ref[...])
    h = (_rms(x1, g_fpre_ref[...]) * (1.0 + scale_f) + shift_f).astype(BF16)
    f = None
    for c in range(D_FF // FF_CHUNK):
        hid = jnp.maximum(_dot(h, w_ff1_ref[:, c * FF_CHUNK:(c + 1) * FF_CHUNK]), 0.0)
        part = _dot((hid * hid).astype(BF16), w_ff2_ref[c * FF_CHUNK:(c + 1) * FF_CHUNK, :])
        f = part if f is None else f + part
    o_ref[...] = x1 + gate_f * _rms(f, g_fpost_ref[...])


def _phase2(x2d, mod3, mod_row_fn, q, sgu, kv_list, kv_rows, shared_kv, n_outer, n_inner, wts, name):
    tq = PHASE2_ROWS
    n_kv = len(kv_list) // 2
    in_specs = [
        pl.BlockSpec((tq, D_MODEL), lambda b, i: (b * n_inner + i, 0)),
        pl.BlockSpec((None, N_MOD, D_MODEL), lambda b, i: (mod_row_fn(b), 0, 0)),
        pl.BlockSpec((tq, QK_WIDTH), lambda b, i: (b * n_inner + i, 0)),
        pl.BlockSpec((tq, SGU_WIDTH), lambda b, i: (b * n_inner + i, 0)),
    ]
    for t in range(n_kv):
        in_specs.append(pl.BlockSpec((kv_rows[t], QK_WIDTH), lambda b, i: (b, 0)))
        in_specs.append(pl.BlockSpec((kv_rows[t], MLA_WIDTH), lambda b, i: (b, 0)))
    in_specs += [
        _const_spec((D_MODEL, D_MODEL)),
        _const_spec((1, D_MODEL)),
        _const_spec((1, D_MODEL)),
        _const_spec((1, D_MODEL)),
        _const_spec((D_MODEL, D_FF)),
        _const_spec((D_FF, D_MODEL)),
    ]
    return pl.pallas_call(
        functools.partial(_phase2_kernel, n_kv, shared_kv),
        out_shape=jax.ShapeDtypeStruct(x2d.shape, F32),
        grid=(n_outer, n_inner),
        in_specs=in_specs,
        out_specs=pl.BlockSpec((tq, D_MODEL), lambda b, i: (b * n_inner + i, 0)),
        compiler_params=pltpu.CompilerParams(
            dimension_semantics=("arbitrary", "arbitrary"), vmem_limit_bytes=VMEM_LIMIT),
        name=name,
    )(x2d, mod3, q, sgu, *kv_list, wts["w_o"], wts["g_attn_post"], wts["g_ffn_pre"],
      wts["g_ffn_post"], wts["w_ff1"], wts["w_ff2"])


def _pair_swap(w):
    shp = w.shape
    return w.reshape(shp[:-1] + (shp[-1] // 2, 2))[..., ::-1].reshape(shp)


def _prepare_weights(g_attn_pre, g_attn_post, w_in, g_q, w_uq, g_kv, w_ukv, w_sgu, b_sgu, g_sgu,
                     beta_sgu, w_o, g_ffn_pre, g_ffn_post, w_ff1, w_ff2):
    i0, i1, i2 = Q_LORA, Q_LORA + KV_LORA, Q_LORA + KV_LORA + QK_ROPE
    w_kr = w_in[:, i1:i2].astype(BF16)
    w_kr_sw = _pair_swap(w_kr)
    w_attn = jnp.concatenate([w_in[:, :i1].astype(BF16), w_kr, w_kr_sw, w_kr, w_kr_sw], axis=1)
    w_gate = w_in[:, i2:].astype(BF16)

    w_uq_h = w_uq.reshape(Q_LORA, MLA_HEADS, QK_NOPE + QK_ROPE)
    w_uq_ext = jnp.concatenate(
        [w_uq_h, _pair_swap(w_uq_h[..., QK_NOPE:])], axis=-1).reshape(Q_LORA, QK_WIDTH).astype(BF16)

    w_ukv_h = w_ukv.reshape(KV_LORA, MLA_HEADS, QK_NOPE + V_HEAD)
    w_uk_slots = jnp.concatenate(
        [w_ukv_h[..., :QK_NOPE], jnp.zeros((KV_LORA, MLA_HEADS, HEAD_SLOT - QK_NOPE), F32)],
        axis=-1).reshape(KV_LORA, QK_WIDTH)
    eye = jnp.eye(QK_ROPE, dtype=F32)
    zeros_rope = jnp.zeros((QK_ROPE, QK_ROPE), F32)
    zeros_nope = jnp.zeros((QK_ROPE, QK_NOPE), F32)
    to_lo = jnp.tile(jnp.concatenate([zeros_nope, eye, zeros_rope], axis=1), (1, MLA_HEADS))
    to_hi = jnp.tile(jnp.concatenate([zeros_nope, zeros_rope, eye], axis=1), (1, MLA_HEADS))
    w_k = jnp.concatenate([w_uk_slots, to_lo, to_lo, to_hi, to_hi], axis=0).astype(BF16)
    w_uv = w_ukv_h[..., QK_NOPE:].reshape(KV_LORA, MLA_WIDTH).astype(BF16)

    w_sgu_pair = w_sgu.reshape(SGU_HEADS // 2, 2 * CHUNK, CHUNK).astype(BF16)
    bias_sgu = jnp.repeat(b_sgu.T, SGU_HEAD_DIM, axis=1)
    row = lambda a: a.reshape(1, -1)
    return {
        "g_attn_pre": row(g_attn_pre), "g_attn_post": row(g_attn_post), "w_attn": w_attn,
        "w_gate": w_gate,
        "g_q": row(g_q), "w_uq": w_uq_ext, "g_kv": row(g_kv), "w_k": w_k, "w_uv": w_uv,
        "g_sgu": row(g_sgu), "beta_sgu": row(beta_sgu), "w_sgu": w_sgu_pair, "bias_sgu": bias_sgu,
        "w_o32": w_o, "g_ffn_pre": row(g_ffn_pre), "g_ffn_post": row(g_ffn_post),
        "w_ff1_32": w_ff1, "w_ff2_32": w_ff2,
    }


def _rope_tables(n_tok):
    rows = n_tok // GRID_W
    row = jnp.broadcast_to(jnp.arange(rows)[:, None], (rows, GRID_W)).reshape(-1).astype(F32)
    col = jnp.broadcast_to(jnp.arange(GRID_W)[None, :], (rows, GRID_W)).reshape(-1).astype(F32)
    freqs = 1.0 / (ROPE_BASE ** (jnp.arange(AXIS_PAIRS, dtype=F32) / AXIS_PAIRS))
    ang = jnp.concatenate([row[:, None] * freqs, col[:, None] * freqs], axis=-1)
    cos = jnp.repeat(jnp.cos(ang), 2, axis=1)
    sin = jnp.repeat(jnp.sin(ang), 2, axis=1)
    sign = jnp.tile(jnp.array([-1.0, 1.0], F32), QK_ROPE // 2)
    sin = sin * sign
    ktab = jnp.concatenate([cos, sin, cos, sin], axis=1)
    qtab = ATTN_SCALE * jnp.concatenate([jnp.ones((n_tok, QK_NOPE), F32), cos, sin], axis=1)
    return ktab, qtab


def kernel(x_prompt, x_sample, cache_ckv, cache_krope, c, c_ctx, w_mod, b_mod, g_attn_pre,
           g_attn_post, w_in, g_q, w_uq, g_kv, w_ukv, w_sgu, b_sgu, g_sgu, beta_sgu, w_o,
           g_ffn_pre, g_ffn_post, w_ff1, w_ff2):
    batch, seq, _ = x_prompt.shape
    dec_batch, dec_seq, _ = x_sample.shape
    past_len = cache_ckv.shape[2]
    depth = w_mod.shape[0]
    assert depth == 1

    wts = _prepare_weights(g_attn_pre[0], g_attn_post[0], w_in[0], g_q[0], w_uq[0], g_kv[0],
                           w_ukv[0], w_sgu[0], b_sgu[0], g_sgu[0], beta_sgu[0], w_o[0],
                           g_ffn_pre[0], g_ffn_post[0], w_ff1[0], w_ff2[0])

    cond8 = jnp.concatenate(
        [c_ctx[None, :], c, jnp.zeros((MOD_ROWS - 1 - dec_batch, D_MODEL), F32)], axis=0)
    mod3 = _modulation(cond8, w_mod[0], b_mod[0].reshape(1, -1)).reshape(MOD_ROWS, N_MOD, D_MODEL)

    ktab_ctx = jnp.concatenate(
        [jnp.ones((1, QK_ROPE), F32), jnp.zeros((1, LANES - QK_ROPE), F32)], axis=1)
    qtab_ctx = ATTN_SCALE * jnp.concatenate(
        [jnp.ones((1, QK_NOPE + QK_ROPE), F32), jnp.zeros((1, QK_ROPE), F32)], axis=1)
    xp2d = x_prompt.reshape(batch * seq, D_MODEL)
    q_c, k_c, v_c, sgu_c, ckv_c, kr_c, w_o16, w_ff1_16, w_ff2_16 = _phase1(
        xp2d, mod3, lambda i: 0, ktab_ctx, qtab_ctx, lambda i: 0, wts, True)
    wts = dict(wts, w_o=w_o16, w_ff1=w_ff1_16, w_ff2=w_ff2_16)
    assert seq == SUB_ROWS and dec_seq % PHASE2_ROWS == 0
    y_prompt = _phase2(xp2d, mod3, lambda b: 0, q_c, sgu_c, [k_c, v_c], [PHASE2_ROWS], False,
                       batch * seq // PHASE2_ROWS, 1, wts, "phase2_ctx")

    ktab_lat, qtab_lat = _rope_tables(dec_seq)
    tiles_per_seq = dec_seq // TOKEN_TILE
    xs2d = x_sample.reshape(dec_batch * dec_seq, D_MODEL)
    q_l, k_l, v_l, sgu_l = _phase1(
        xs2d, mod3, lambda i: 1 + i // tiles_per_seq, ktab_lat, qtab_lat,
        lambda i: i % tiles_per_seq, wts, False)
    k_p, v_p = _cache_kv(cache_ckv[:, 0].reshape(dec_batch * past_len, KV_LORA),
                         cache_krope[:, 0].reshape(dec_batch * past_len, QK_ROPE), wts)
    y_sample = _phase2(xs2d, mod3, lambda b: 1 + b, q_l, sgu_l, [k_p, v_p, k_l, v_l],
                       [past_len, dec_seq], True, dec_batch, dec_seq // PHASE2_ROWS, wts,
                       "phase2_lat")

    return (y_prompt.reshape(batch, seq, D_MODEL),
            y_sample.reshape(dec_batch, dec_seq, D_MODEL),
            ckv_c.reshape(batch, 1, seq, KV_LORA),
            kr_c.reshape(batch, 1, seq, QK_ROPE))
```

```python
import functools
import math

import jax
import jax.numpy as jnp
from jax import lax
from jax.experimental import pallas as pl
from jax.experimental.pallas import tpu as pltpu

D_MODEL = 1024
GRID_W = 64
MLA_HEADS = 8
QK_NOPE = 64
QK_ROPE = 32
V_HEAD = 64
Q_LORA = 256
KV_LORA = 128
MLA_WIDTH = MLA_HEADS * V_HEAD
SGU_HEADS = 8
SGU_WIDTH = D_MODEL - MLA_WIDTH
SGU_HEAD_DIM = SGU_WIDTH // SGU_HEADS
CHUNK = 128
D_FF = 4 * D_MODEL
AXIS_PAIRS = QK_ROPE // 4
ROPE_BASE = 10000.0
EPS = 1e-6
N_MOD = 6
ATTN_SCALE = (QK_NOPE + QK_ROPE) ** -0.5

LANES = 128
HEAD_SLOT = LANES
QK_WIDTH = MLA_HEADS * HEAD_SLOT
ATTN_PROJ = Q_LORA + KV_LORA + LANES
MOD_ROWS = 8
MOD_K_ROWS = 128
TOKEN_TILE = 512
SUB_ROWS = 256
PHASE2_ROWS_CTX = 1024
PHASE2_ROWS_LAT = 512
FF_CHUNK = 1024
VMEM_LIMIT = 56 * 1024 * 1024

BF16 = jnp.bfloat16
F32 = jnp.float32


def _dot(a, b):
    return jnp.dot(a, b, preferred_element_type=F32)


def _rms(x, g):
    return x * lax.rsqrt(jnp.mean(x * x, axis=-1, keepdims=True) + EPS) * g


def _gelu(x):
    inner = math.sqrt(2.0 / math.pi) * (x + 0.044715 * (x * x * x))
    return x * (0.5 * (1.0 + jnp.tanh(inner)))


def _mod_kernel(cond_a_ref, cond_b_ref, w_a_ref, w_b_ref, b_ref, o_ref):
    @pl.when(pl.program_id(0) == 0)
    def _():
        o_ref[...] = jnp.broadcast_to(b_ref[...], o_ref.shape)

    def part(cond_ref, w_ref):
        cnd = cond_ref[...]
        act = cnd * (1.0 / (1.0 + jnp.exp(-cnd)))
        return _dot(act.astype(BF16), w_ref[...].astype(BF16))

    o_ref[...] += part(cond_a_ref, w_a_ref) + part(cond_b_ref, w_b_ref)


def _modulation(cond8, w_mod, b_mod):
    n = w_mod.shape[1]
    tk = MOD_K_ROWS
    nk = D_MODEL // tk // 2
    return pl.pallas_call(
        _mod_kernel,
        out_shape=jax.ShapeDtypeStruct((MOD_ROWS, n), F32),
        grid=(nk,),
        in_specs=[
            pl.BlockSpec((MOD_ROWS, tk), lambda k: (0, k)),
            pl.BlockSpec((MOD_ROWS, tk), lambda k: (0, k + nk)),
            pl.BlockSpec((tk, n), lambda k: (k, 0)),
            pl.BlockSpec((tk, n), lambda k: (k + nk, 0)),
            pl.BlockSpec((1, n), lambda k: (0, 0)),
        ],
        out_specs=pl.BlockSpec((MOD_ROWS, n), lambda k: (0, 0)),
        compiler_params=pltpu.CompilerParams(
            dimension_semantics=("arbitrary",), vmem_limit_bytes=VMEM_LIMIT),
        name="modulation",
    )(cond8, cond8, w_mod, w_mod, b_mod)


def _phase1_kernel(emit_cache, x_ref, mod_ref, ktab_ref, qtab_ref, g_pre_ref, w_attn_ref, w_gate_ref,
                   g_q_ref, w_uq_ref, g_kv_ref, w_k_ref, w_uv_ref, g_sgu_ref, beta_sgu_ref,
                   w_sgu_ref, bias_ref, *rest):
    if emit_cache:
        (w_o32_ref, w_ff1_32_ref, w_ff2_32_ref, q_ref, k_ref, v_ref, sgu_ref, ckv_ref, kr_ref,
         w_o16_ref, w_ff1_16_ref, w_ff2_16_ref) = rest
        w_o16_ref[...] = w_o32_ref[...].astype(BF16)
        w_ff1_16_ref[...] = w_ff1_32_ref[...].astype(BF16)
        w_ff2_16_ref[...] = w_ff2_32_ref[...].astype(BF16)
    else:
        q_ref, k_ref, v_ref, sgu_ref = rest
    tm = x_ref.shape[0]
    x = x_ref[...]
    shift_a = mod_ref[0:1, :]
    scale_a = mod_ref[1:2, :]
    h = (_rms(x, g_pre_ref[...]) * (1.0 + scale_a) + shift_a).astype(BF16)
    proj = _dot(h, w_attn_ref[...])
    gate = _dot(h, w_gate_ref[...])

    cq = _rms(proj[:, 0:Q_LORA], g_q_ref[...])
    q = _dot(cq.astype(BF16), w_uq_ref[...])
    qtab = qtab_ref[...]
    q_ref[...] = jnp.concatenate(
        [(q[:, s * HEAD_SLOT:(s + 1) * HEAD_SLOT] * qtab).astype(BF16) for s in range(MLA_HEADS)],
        axis=1)

    ckv_n = _rms(proj[:, Q_LORA:Q_LORA + KV_LORA], g_kv_ref[...])
    rope_slab = proj[:, Q_LORA + KV_LORA:Q_LORA + KV_LORA + LANES]
    if emit_cache:
        ckv_ref[...] = ckv_n
        kr_ref[...] = rope_slab[:, 0:QK_ROPE]
    ckv_b = ckv_n.astype(BF16)
    kin = jnp.concatenate([ckv_b, (rope_slab * ktab_ref[...]).astype(BF16)], axis=1)
    k_ref[...] = _dot(kin, w_k_ref[...]).astype(BF16)
    v_ref[...] = _dot(ckv_b, w_uv_ref[...]).astype(BF16)

    u = _gelu(gate[:, 0:SGU_WIDTH])
    vv = _gelu(gate[:, SGU_WIDTH:2 * SGU_WIDTH])
    mu = jnp.mean(vv, axis=-1, keepdims=True)
    vc = vv - mu
    var = jnp.mean(vc * vc, axis=-1, keepdims=True)
    vn = (vc * lax.rsqrt(var + EPS) * g_sgu_ref[...] + beta_sgu_ref[...]).astype(BF16)
    n_chunks = tm // CHUNK
    low_half = lax.broadcasted_iota(jnp.int32, (CHUNK, LANES), 1) < SGU_HEAD_DIM
    bias = bias_ref[...]
    mixed_rows = [[] for _ in range(n_chunks)]
    for j in range(SGU_WIDTH // LANES):
        rhs = jnp.concatenate(
            [vn[n * CHUNK:(n + 1) * CHUNK, j * LANES:(j + 1) * LANES] for n in range(n_chunks)],
            axis=1)
        o = _dot(w_sgu_ref[j], rhs)
        for n in range(n_chunks):
            even = o[0:CHUNK, n * LANES:(n + 1) * LANES]
            odd = o[CHUNK:2 * CHUNK, n * LANES:(n + 1) * LANES]
            mixed_rows[n].append(jnp.where(low_half, even, odd))
    for n in range(n_chunks):
        mixed = jnp.concatenate(mixed_rows[n], axis=1) + bias
        sgu_ref[n * CHUNK:(n + 1) * CHUNK, :] = (
            u[n * CHUNK:(n + 1) * CHUNK, :] * mixed).astype(BF16)


def _const_spec(shape):
    nd = len(shape)
    return pl.BlockSpec(shape, lambda *_: (0,) * nd, pipeline_mode=pl.Buffered(1))


def _phase1(x2d, mod3, mod_row_fn, ktab, qtab, tab_fn, wts, emit_cache):
    n_tok = x2d.shape[0]
    tm = TOKEN_TILE
    grid = (n_tok // tm,)
    tab_block = (ktab.shape[0] if ktab.shape[0] == 1 else tm, LANES)
    in_specs = [
        pl.BlockSpec((tm, D_MODEL), lambda i: (i, 0)),
        pl.BlockSpec((None, N_MOD, D_MODEL), lambda i: (mod_row_fn(i), 0, 0)),
        pl.BlockSpec(tab_block, lambda i: (tab_fn(i), 0)),
        pl.BlockSpec(tab_block, lambda i: (tab_fn(i), 0)),
        _const_spec((1, D_MODEL)),
        _const_spec((D_MODEL, ATTN_PROJ)),
        _const_spec((D_MODEL, 2 * SGU_WIDTH)),
        _const_spec((1, Q_LORA)),
        _const_spec((Q_LORA, QK_WIDTH)),
        _const_spec((1, KV_LORA)),
        _const_spec((2 * LANES, QK_WIDTH)),
        _const_spec((KV_LORA, MLA_WIDTH)),
        _const_spec((1, SGU_WIDTH)),
        _const_spec((1, SGU_WIDTH)),
        _const_spec((SGU_WIDTH // LANES, 2 * CHUNK, CHUNK)),
        _const_spec((CHUNK, SGU_WIDTH)),
    ]
    out_shape = [
        jax.ShapeDtypeStruct((n_tok, QK_WIDTH), BF16),
        jax.ShapeDtypeStruct((n_tok, QK_WIDTH), BF16),
        jax.ShapeDtypeStruct((n_tok, MLA_WIDTH), BF16),
        jax.ShapeDtypeStruct((n_tok, SGU_WIDTH), BF16),
    ]
    out_specs = [
        pl.BlockSpec((tm, QK_WIDTH), lambda i: (i, 0)),
        pl.BlockSpec((tm, QK_WIDTH), lambda i: (i, 0)),
        pl.BlockSpec((tm, MLA_WIDTH), lambda i: (i, 0)),
        pl.BlockSpec((tm, SGU_WIDTH), lambda i: (i, 0)),
    ]
    extra_inputs = []
    if emit_cache:
        out_shape += [jax.ShapeDtypeStruct((n_tok, KV_LORA), F32),
                      jax.ShapeDtypeStruct((n_tok, QK_ROPE), F32)]
        out_specs += [pl.BlockSpec((tm, KV_LORA), lambda i: (i, 0)),
                      pl.BlockSpec((tm, QK_ROPE), lambda i: (i, 0))]
        for w in (wts["w_o32"], wts["w_ff1_32"], wts["w_ff2_32"]):
            rows, cols = w.shape
            blk = (rows // grid[0], cols)
            extra_inputs.append(w)
            in_specs.append(pl.BlockSpec(blk, lambda i: (i, 0)))
            out_shape.append(jax.ShapeDtypeStruct(w.shape, BF16))
            out_specs.append(pl.BlockSpec(blk, lambda i: (i, 0)))
    return pl.pallas_call(
        functools.partial(_phase1_kernel, emit_cache),
        out_shape=out_shape,
        grid=grid,
        in_specs=in_specs,
        out_specs=out_specs,
        compiler_params=pltpu.CompilerParams(
            dimension_semantics=("arbitrary",), vmem_limit_bytes=VMEM_LIMIT),
        name="phase1_ctx" if emit_cache else "phase1_lat",
    )(x2d, mod3, ktab, qtab, wts["g_attn_pre"], wts["w_attn"], wts["w_gate"], wts["g_q"], wts["w_uq"],
      wts["g_kv"], wts["w_k"], wts["w_uv"], wts["g_sgu"], wts["beta_sgu"], wts["w_sgu"],
      wts["bias_sgu"], *extra_inputs)


def _cache_kv_kernel(ckv_ref, kr_ref, w_k_ref, w_uv_ref, k_ref, v_ref):
    ckv_b = ckv_ref[...].astype(BF16)
    kr_b = kr_ref[...].astype(BF16)
    lo = KV_LORA
    hi = KV_LORA + 2 * QK_ROPE
    k = (_dot(ckv_b, w_k_ref[0:KV_LORA, :]) + _dot(kr_b, w_k_ref[lo:lo + QK_ROPE, :])
         + _dot(kr_b, w_k_ref[hi:hi + QK_ROPE, :]))
    k_ref[...] = k.astype(BF16)
    v_ref[...] = _dot(ckv_b, w_uv_ref[...]).astype(BF16)


def _cache_kv(ckv2d, kr2d, wts):
    n_tok = ckv2d.shape[0]
    tm = n_tok
    return pl.pallas_call(
        _cache_kv_kernel,
        out_shape=[jax.ShapeDtypeStruct((n_tok, QK_WIDTH), BF16),
                   jax.ShapeDtypeStruct((n_tok, MLA_WIDTH), BF16)],
        grid=(n_tok // tm,),
        in_specs=[
            pl.BlockSpec((tm, KV_LORA), lambda i: (i, 0)),
            pl.BlockSpec((tm, QK_ROPE), lambda i: (i, 0)),
            _const_spec((2 * LANES, QK_WIDTH)),
            _const_spec((KV_LORA, MLA_WIDTH)),
        ],
        out_specs=[pl.BlockSpec((tm, QK_WIDTH), lambda i: (i, 0)),
                   pl.BlockSpec((tm, MLA_WIDTH), lambda i: (i, 0))],
        compiler_params=pltpu.CompilerParams(
            dimension_semantics=("arbitrary",), vmem_limit_bytes=VMEM_LIMIT),
        name="cache_kv",
    )(ckv2d, kr2d, wts["w_k"], wts["w_uv"])


def _attend(q_ref, kv_views, r0, low_half):
    pair_out = []
    head_out = None
    for hd in range(MLA_HEADS):
        qh = q_ref[r0:r0 + SUB_ROWS, hd * HEAD_SLOT:(hd + 1) * HEAD_SLOT]
        slab = hd // 2
        scores = []
        for k_view, _ in kv_views:
            kh = k_view(hd * HEAD_SLOT, HEAD_SLOT)
            scores.append(lax.dot_general(qh, kh, (((1,), (1,)), ((), ())),
                                          preferred_element_type=F32))
        m = scores[0].max(axis=-1, keepdims=True)
        for s in scores[1:]:
            m = jnp.maximum(m, s.max(axis=-1, keepdims=True))
        denom = None
        acc = None
        for (_, v_view), s in zip(kv_views, scores):
            p = jnp.exp(s - m)
            ps = p.sum(axis=-1, keepdims=True)
            denom = ps if denom is None else denom + ps
            pv = _dot(p.astype(BF16), v_view(slab * LANES, LANES))
            acc = pv if acc is None else acc + pv
        o = acc / denom
        if hd % 2 == 0:
            head_out = o
        else:
            pair_out.append(jnp.where(low_half, head_out, o).astype(BF16))
    return jnp.concatenate(pair_out, axis=1)


def _phase2_kernel(n_kv, shared_kv, x_ref, mod_ref, q_ref, sgu_ref, *refs):
    kv_refs = refs[:2 * n_kv]
    (w_o_ref, g_post_ref, g_fpre_ref, g_fpost_ref, w_ff1_ref, w_ff2_ref, o_ref) = refs[2 * n_kv:]
    tq = x_ref.shape[0]
    low_half = lax.broadcasted_iota(jnp.int32, (SUB_ROWS, LANES), 1) < V_HEAD
    gate_a = mod_ref[2:3, :]
    shift_f = mod_ref[3:4, :]
    scale_f = mod_ref[4:5, :]
    gate_f = mod_ref[5:6, :]
    def attention(r0):
        views = []
        for t in range(n_kv):
            k_ref, v_ref = kv_refs[2 * t], kv_refs[2 * t + 1]
            if shared_kv:
                views.append((lambda c, w, k_ref=k_ref: k_ref[:, c:c + w],
                              lambda c, w, v_ref=v_ref: v_ref[:, c:c + w]))
            else:
                views.append((lambda c, w, k_ref=k_ref: k_ref[r0:r0 + SUB_ROWS, c:c + w],
                              lambda c, w, v_ref=v_ref: v_ref[r0:r0 + SUB_ROWS, c:c + w]))
        return _attend(q_ref, views, r0, low_half)

    def mixer_out(r0, attn):
        mix_in = jnp.concatenate([attn, sgu_ref[r0:r0 + SUB_ROWS, :]], axis=1)
        mix = _dot(mix_in, w_o_ref[...])
        x1 = x_ref[r0:r0 + SUB_ROWS, :] + gate_a * _rms(mix, g_post_ref[...])
        h = (_rms(x1, g_fpre_ref[...]) * (1.0 + scale_f) + shift_f).astype(BF16)
        return x1, h

    def ffn(h):
        f = None
        for c in range(D_FF // FF_CHUNK):
            hid = jnp.maximum(_dot(h, w_ff1_ref[:, c * FF_CHUNK:(c + 1) * FF_CHUNK]), 0.0)
            part = _dot((hid * hid).astype(BF16), w_ff2_ref[c * FF_CHUNK:(c + 1) * FF_CHUNK, :])
            f = part if f is None else f + part
        return f

    def finish(r0, x1, f):
        o_ref[r0:r0 + SUB_ROWS, :] = x1 + gate_f * _rms(f, g_fpost_ref[...])

    starts = [a * SUB_ROWS for a in range(tq // SUB_ROWS)]
    x1h = {r0: mixer_out(r0, attention(r0)) for r0 in starts}
    for r0 in starts:
        finish(r0, x1h[r0][0], ffn(x1h[r0][1]))


def _phase2(x2d, mod3, mod_row_fn, q, sgu, kv_list, kv_rows, shared_kv, tq, n_outer, n_inner, wts,
            name):
    n_kv = len(kv_list) // 2
    in_specs = [
        pl.BlockSpec((tq, D_MODEL), lambda b, i: (b * n_inner + i, 0)),
        pl.BlockSpec((None, N_MOD, D_MODEL), lambda b, i: (mod_row_fn(b), 0, 0)),
        pl.BlockSpec((tq, QK_WIDTH), lambda b, i: (b * n_inner + i, 0)),
        pl.BlockSpec((tq, SGU_WIDTH), lambda b, i: (b * n_inner + i, 0)),
    ]
    for t in range(n_kv):
        in_specs.append(pl.BlockSpec((kv_rows[t], QK_WIDTH), lambda b, i: (b, 0)))
        in_specs.append(pl.BlockSpec((kv_rows[t], MLA_WIDTH), lambda b, i: (b, 0)))
    in_specs += [
        _const_spec((D_MODEL, D_MODEL)),
        _const_spec((1, D_MODEL)),
        _const_spec((1, D_MODEL)),
        _const_spec((1, D_MODEL)),
        _const_spec((D_MODEL, D_FF)),
        _const_spec((D_FF, D_MODEL)),
    ]
    return pl.pallas_call(
        functools.partial(_phase2_kernel, n_kv, shared_kv),
        out_shape=jax.ShapeDtypeStruct(x2d.shape, F32),
        grid=(n_outer, n_inner),
        in_specs=in_specs,
        out_specs=pl.BlockSpec((tq, D_MODEL), lambda b, i: (b * n_inner + i, 0)),
        compiler_params=pltpu.CompilerParams(
            dimension_semantics=("arbitrary", "arbitrary"), vmem_limit_bytes=VMEM_LIMIT),
        name=name,
    )(x2d, mod3, q, sgu, *kv_list, wts["w_o"], wts["g_attn_post"], wts["g_ffn_pre"],
      wts["g_ffn_post"], wts["w_ff1"], wts["w_ff2"])


def _pair_swap(w):
    shp = w.shape
    return w.reshape(shp[:-1] + (shp[-1] // 2, 2))[..., ::-1].reshape(shp)


def _prepare_weights(g_attn_pre, g_attn_post, w_in, g_q, w_uq, g_kv, w_ukv, w_sgu, b_sgu, g_sgu,
                     beta_sgu, w_o, g_ffn_pre, g_ffn_post, w_ff1, w_ff2):
    i0, i1, i2 = Q_LORA, Q_LORA + KV_LORA, Q_LORA + KV_LORA + QK_ROPE
    w_kr = w_in[:, i1:i2].astype(BF16)
    w_kr_sw = _pair_swap(w_kr)
    w_attn = jnp.concatenate([w_in[:, :i1].astype(BF16), w_kr, w_kr_sw, w_kr, w_kr_sw], axis=1)
    w_gate = w_in[:, i2:].astype(BF16)

    w_uq_h = w_uq.reshape(Q_LORA, MLA_HEADS, QK_NOPE + QK_ROPE)
    w_uq_ext = jnp.concatenate(
        [w_uq_h, _pair_swap(w_uq_h[..., QK_NOPE:])], axis=-1).reshape(Q_LORA, QK_WIDTH).astype(BF16)

    w_ukv_h = w_ukv.reshape(KV_LORA, MLA_HEADS, QK_NOPE + V_HEAD)
    w_uk_slots = jnp.concatenate(
        [w_ukv_h[..., :QK_NOPE], jnp.zeros((KV_LORA, MLA_HEADS, HEAD_SLOT - QK_NOPE), F32)],
        axis=-1).reshape(KV_LORA, QK_WIDTH)
    eye = jnp.eye(QK_ROPE, dtype=F32)
    zeros_rope = jnp.zeros((QK_ROPE, QK_ROPE), F32)
    zeros_nope = jnp.zeros((QK_ROPE, QK_NOPE), F32)
    to_lo = jnp.tile(jnp.concatenate([zeros_nope, eye, zeros_rope], axis=1), (1, MLA_HEADS))
    to_hi = jnp.tile(jnp.concatenate([zeros_nope, zeros_rope, eye], axis=1), (1, MLA_HEADS))
    w_k = jnp.concatenate([w_uk_slots, to_lo, to_lo, to_hi, to_hi], axis=0).astype(BF16)
    w_uv = w_ukv_h[..., QK_NOPE:].reshape(KV_LORA, MLA_WIDTH).astype(BF16)

    w_sgu_pair = w_sgu.reshape(SGU_HEADS // 2, 2 * CHUNK, CHUNK).astype(BF16)
    bias_sgu = jnp.repeat(b_sgu.T, SGU_HEAD_DIM, axis=1)
    row = lambda a: a.reshape(1, -1)
    return {
        "g_attn_pre": row(g_attn_pre), "g_attn_post": row(g_attn_post), "w_attn": w_attn,
        "w_gate": w_gate,
        "g_q": row(g_q), "w_uq": w_uq_ext, "g_kv": row(g_kv), "w_k": w_k, "w_uv": w_uv,
        "g_sgu": row(g_sgu), "beta_sgu": row(beta_sgu), "w_sgu": w_sgu_pair, "bias_sgu": bias_sgu,
        "w_o32": w_o, "g_ffn_pre": row(g_ffn_pre), "g_ffn_post": row(g_ffn_post),
        "w_ff1_32": w_ff1, "w_ff2_32": w_ff2,
    }


def _rope_tables(n_tok):
    rows = n_tok // GRID_W
    row = jnp.broadcast_to(jnp.arange(rows)[:, None], (rows, GRID_W)).reshape(-1).astype(F32)
    col = jnp.broadcast_to(jnp.arange(GRID_W)[None, :], (rows, GRID_W)).reshape(-1).astype(F32)
    freqs = 1.0 / (ROPE_BASE ** (jnp.arange(AXIS_PAIRS, dtype=F32) / AXIS_PAIRS))
    ang = jnp.concatenate([row[:, None] * freqs, col[:, None] * freqs], axis=-1)
    cos = jnp.repeat(jnp.cos(ang), 2, axis=1)
    sin = jnp.repeat(jnp.sin(ang), 2, axis=1)
    sign = jnp.tile(jnp.array([-1.0, 1.0], F32), QK_ROPE // 2)
    sin = sin * sign
    ktab = jnp.concatenate([cos, sin, cos, sin], axis=1)
    qtab = ATTN_SCALE * jnp.concatenate([jnp.ones((n_tok, QK_NOPE), F32), cos, sin], axis=1)
    return ktab, qtab


def kernel(x_prompt, x_sample, cache_ckv, cache_krope, c, c_ctx, w_mod, b_mod, g_attn_pre,
           g_attn_post, w_in, g_q, w_uq, g_kv, w_ukv, w_sgu, b_sgu, g_sgu, beta_sgu, w_o,
           g_ffn_pre, g_ffn_post, w_ff1, w_ff2):
    batch, seq, _ = x_prompt.shape
    dec_batch, dec_seq, _ = x_sample.shape
    past_len = cache_ckv.shape[2]
    depth = w_mod.shape[0]
    assert depth == 1

    wts = _prepare_weights(g_attn_pre[0], g_attn_post[0], w_in[0], g_q[0], w_uq[0], g_kv[0],
                           w_ukv[0], w_sgu[0], b_sgu[0], g_sgu[0], beta_sgu[0], w_o[0],
                           g_ffn_pre[0], g_ffn_post[0], w_ff1[0], w_ff2[0])

    cond8 = jnp.concatenate(
        [c_ctx[None, :], c, jnp.zeros((MOD_ROWS - 1 - dec_batch, D_MODEL), F32)], axis=0)
    mod3 = _modulation(cond8, w_mod[0], b_mod[0].reshape(1, -1)).reshape(MOD_ROWS, N_MOD, D_MODEL)

    ktab_ctx = jnp.concatenate(
        [jnp.ones((1, QK_ROPE), F32), jnp.zeros((1, LANES - QK_ROPE), F32)], axis=1)
    qtab_ctx = ATTN_SCALE * jnp.concatenate(
        [jnp.ones((1, QK_NOPE + QK_ROPE), F32), jnp.zeros((1, QK_ROPE), F32)], axis=1)
    xp2d = x_prompt.reshape(batch * seq, D_MODEL)
    q_c, k_c, v_c, sgu_c, ckv_c, kr_c, w_o16, w_ff1_16, w_ff2_16 = _phase1(
        xp2d, mod3, lambda i: 0, ktab_ctx, qtab_ctx, lambda i: 0, wts, True)
    wts = dict(wts, w_o=w_o16, w_ff1=w_ff1_16, w_ff2=w_ff2_16)
    assert seq == SUB_ROWS and dec_seq % PHASE2_ROWS_LAT == 0
    y_prompt = _phase2(xp2d, mod3, lambda b: 0, q_c, sgu_c, [k_c, v_c], [PHASE2_ROWS_CTX], False,
                       PHASE2_ROWS_CTX, batch * seq // PHASE2_ROWS_CTX, 1, wts, "phase2_ctx")

    ktab_lat, qtab_lat = _rope_tables(dec_seq)
    tiles_per_seq = dec_seq // TOKEN_TILE
    xs2d = x_sample.reshape(dec_batch * dec_seq, D_MODEL)
    q_l, k_l, v_l, sgu_l = _phase1(
        xs2d, mod3, lambda i: 1 + i // tiles_per_seq, ktab_lat, qtab_lat,
        lambda i: i % tiles_per_seq, wts, False)
    k_p, v_p = _cache_kv(cache_ckv[:, 0].reshape(dec_batch * past_len, KV_LORA),
                         cache_krope[:, 0].reshape(dec_batch * past_len, QK_ROPE), wts)
    y_sample = _phase2(xs2d, mod3, lambda b: 1 + b, q_l, sgu_l, [k_p, v_p, k_l, v_l],
                       [past_len, dec_seq], True, PHASE2_ROWS_LAT, dec_batch,
                       dec_seq // PHASE2_ROWS_LAT, wts, "phase2_lat")

    return (y_prompt.reshape(batch, seq, D_MODEL),
            y_sample.reshape(dec_batch, dec_seq, D_MODEL),
            ckv_c.reshape(batch, 1, seq, KV_LORA),
            kr_c.reshape(batch, 1, seq, QK_ROPE))
```

```python
import functools
import math

import jax
import jax.numpy as jnp
from jax import lax
from jax.experimental import pallas as pl
from jax.experimental.pallas import tpu as pltpu

D_MODEL = 1024
GRID_W = 64
MLA_HEADS = 8
QK_NOPE = 64
QK_ROPE = 32
V_HEAD = 64
Q_LORA = 256
KV_LORA = 128
MLA_WIDTH = MLA_HEADS * V_HEAD
SGU_HEADS = 8
SGU_WIDTH = D_MODEL - MLA_WIDTH
SGU_HEAD_DIM = SGU_WIDTH // SGU_HEADS
CHUNK = 128
D_FF = 4 * D_MODEL
AXIS_PAIRS = QK_ROPE // 4
ROPE_BASE = 10000.0
EPS = 1e-6
N_MOD = 6
ATTN_SCALE = (QK_NOPE + QK_ROPE) ** -0.5

LANES = 128
HEAD_SLOT = LANES
QK_WIDTH = MLA_HEADS * HEAD_SLOT
ATTN_PROJ = Q_LORA + KV_LORA + LANES
MOD_ROWS = 8
MOD_K_ROWS = 128
TOKEN_TILE = 512
SUB_ROWS = 256
PHASE2_ROWS_CTX = 512
PHASE2_ROWS_LAT = 512
FF_CHUNK = 1024
VMEM_LIMIT = 56 * 1024 * 1024

BF16 = jnp.bfloat16
F32 = jnp.float32


def _dot(a, b):
    return jnp.dot(a, b, preferred_element_type=F32)


def _dot_nt(a, b):
    return lax.dot_general(a, b, (((1,), (1,)), ((), ())), preferred_element_type=F32)


def _rms(x, g):
    return x * lax.rsqrt(jnp.mean(x * x, axis=-1, keepdims=True) + EPS) * g


def _gelu(x):
    inner = math.sqrt(2.0 / math.pi) * (x + 0.044715 * (x * x * x))
    return x * (0.5 * (1.0 + jnp.tanh(inner)))


def _mod_kernel(cond_a_ref, cond_b_ref, w_a_ref, w_b_ref, b_ref, o_ref):
    @pl.when(pl.program_id(0) == 0)
    def _():
        o_ref[...] = jnp.broadcast_to(b_ref[...], o_ref.shape)

    def part(cond_ref, w_ref):
        cnd = cond_ref[...]
        act = cnd * (1.0 / (1.0 + jnp.exp(-cnd)))
        return _dot(act.astype(BF16), w_ref[...].astype(BF16))

    o_ref[...] += part(cond_a_ref, w_a_ref) + part(cond_b_ref, w_b_ref)


def _modulation(cond8, w_mod, b_mod):
    n = w_mod.shape[1]
    tk = MOD_K_ROWS
    nk = D_MODEL // tk // 2
    return pl.pallas_call(
        _mod_kernel,
        out_shape=jax.ShapeDtypeStruct((MOD_ROWS, n), F32),
        grid=(nk,),
        in_specs=[
            pl.BlockSpec((MOD_ROWS, tk), lambda k: (0, k)),
            pl.BlockSpec((MOD_ROWS, tk), lambda k: (0, k + nk)),
            pl.BlockSpec((tk, n), lambda k: (k, 0)),
            pl.BlockSpec((tk, n), lambda k: (k + nk, 0)),
            pl.BlockSpec((1, n), lambda k: (0, 0)),
        ],
        out_specs=pl.BlockSpec((MOD_ROWS, n), lambda k: (0, 0)),
        compiler_params=pltpu.CompilerParams(
            dimension_semantics=("arbitrary",), vmem_limit_bytes=VMEM_LIMIT),
        name="modulation",
    )(cond8, cond8, w_mod, w_mod, b_mod)


def _phase1_kernel(emit_cache, x_ref, mod_ref, ktab_ref, qtab_ref, g_pre_ref, w_attn_ref, w_gate_ref,
                   g_q_ref, w_uq_ref, g_kv_ref, w_k_ref, w_uv_ref, g_sgu_ref, beta_sgu_ref,
                   w_sgu_ref, bias_ref, *rest):
    if emit_cache:
        (w_o32_ref, w_ff1_32_ref, w_ff2_32_ref, q_ref, k_ref, v_ref, sgu_ref, ckv_ref, kr_ref,
         w_o16_ref, w_ff1_16_ref, w_ff2_16_ref) = rest
        w_o16_ref[...] = w_o32_ref[...].astype(BF16)
        w_ff1_16_ref[...] = w_ff1_32_ref[...].astype(BF16)
        w_ff2_16_ref[...] = w_ff2_32_ref[...].astype(BF16)
    else:
        q_ref, k_ref, v_ref, sgu_ref = rest
    tm = x_ref.shape[0]
    x = x_ref[...]
    shift_a = mod_ref[0:1, :]
    scale_a = mod_ref[1:2, :]
    h = (_rms(x, g_pre_ref[...]) * (1.0 + scale_a) + shift_a).astype(BF16)
    proj = _dot(h, w_attn_ref[...])
    gate = _dot(h, w_gate_ref[...])

    cq = _rms(proj[:, 0:Q_LORA], g_q_ref[...])
    q = _dot(cq.astype(BF16), w_uq_ref[...])
    qtab = qtab_ref[...]
    q_ref[...] = jnp.concatenate(
        [(q[:, s * HEAD_SLOT:(s + 1) * HEAD_SLOT] * qtab).astype(BF16) for s in range(MLA_HEADS)],
        axis=1)

    ckv_n = _rms(proj[:, Q_LORA:Q_LORA + KV_LORA], g_kv_ref[...])
    rope_slab = proj[:, Q_LORA + KV_LORA:Q_LORA + KV_LORA + LANES]
    if emit_cache:
        ckv_ref[...] = ckv_n
        kr_ref[...] = rope_slab[:, 0:QK_ROPE]
    ckv_b = ckv_n.astype(BF16)
    kin = jnp.concatenate([ckv_b, (rope_slab * ktab_ref[...]).astype(BF16)], axis=1)
    k_ref[...] = _dot_nt(w_k_ref[...], kin).astype(BF16)
    v_ref[...] = _dot(ckv_b, w_uv_ref[...]).astype(BF16)

    u = _gelu(gate[:, 0:SGU_WIDTH])
    vv = _gelu(gate[:, SGU_WIDTH:2 * SGU_WIDTH])
    mu = jnp.mean(vv, axis=-1, keepdims=True)
    vc = vv - mu
    var = jnp.mean(vc * vc, axis=-1, keepdims=True)
    vn = (vc * lax.rsqrt(var + EPS) * g_sgu_ref[...] + beta_sgu_ref[...]).astype(BF16)
    n_chunks = tm // CHUNK
    low_half = lax.broadcasted_iota(jnp.int32, (CHUNK, LANES), 1) < SGU_HEAD_DIM
    bias = bias_ref[...]
    mixed_rows = [[] for _ in range(n_chunks)]
    for j in range(SGU_WIDTH // LANES):
        rhs = jnp.concatenate(
            [vn[n * CHUNK:(n + 1) * CHUNK, j * LANES:(j + 1) * LANES] for n in range(n_chunks)],
            axis=1)
        o = _dot(w_sgu_ref[j], rhs)
        for n in range(n_chunks):
            even = o[0:CHUNK, n * LANES:(n + 1) * LANES]
            odd = o[CHUNK:2 * CHUNK, n * LANES:(n + 1) * LANES]
            mixed_rows[n].append(jnp.where(low_half, even, odd))
    for n in range(n_chunks):
        mixed = jnp.concatenate(mixed_rows[n], axis=1) + bias
        sgu_ref[n * CHUNK:(n + 1) * CHUNK, :] = (
            u[n * CHUNK:(n + 1) * CHUNK, :] * mixed).astype(BF16)


def _const_spec(shape):
    nd = len(shape)
    return pl.BlockSpec(shape, lambda *_: (0,) * nd, pipeline_mode=pl.Buffered(1))


def _phase1(x2d, mod3, mod_row_fn, ktab, qtab, tab_fn, wts, emit_cache):
    n_tok = x2d.shape[0]
    tm = TOKEN_TILE
    grid = (n_tok // tm,)
    tab_block = (ktab.shape[0] if ktab.shape[0] == 1 else tm, LANES)
    in_specs = [
        pl.BlockSpec((tm, D_MODEL), lambda i: (i, 0)),
        pl.BlockSpec((None, N_MOD, D_MODEL), lambda i: (mod_row_fn(i), 0, 0)),
        pl.BlockSpec(tab_block, lambda i: (tab_fn(i), 0)),
        pl.BlockSpec(tab_block, lambda i: (tab_fn(i), 0)),
        _const_spec((1, D_MODEL)),
        _const_spec((D_MODEL, ATTN_PROJ)),
        _const_spec((D_MODEL, 2 * SGU_WIDTH)),
        _const_spec((1, Q_LORA)),
        _const_spec((Q_LORA, QK_WIDTH)),
        _const_spec((1, KV_LORA)),
        _const_spec((QK_WIDTH, 2 * LANES)),
        _const_spec((KV_LORA, MLA_WIDTH)),
        _const_spec((1, SGU_WIDTH)),
        _const_spec((1, SGU_WIDTH)),
        _const_spec((SGU_WIDTH // LANES, 2 * CHUNK, CHUNK)),
        _const_spec((CHUNK, SGU_WIDTH)),
    ]
    out_shape = [
        jax.ShapeDtypeStruct((n_tok, QK_WIDTH), BF16),
        jax.ShapeDtypeStruct((QK_WIDTH, n_tok), BF16),
        jax.ShapeDtypeStruct((n_tok, MLA_WIDTH), BF16),
        jax.ShapeDtypeStruct((n_tok, SGU_WIDTH), BF16),
    ]
    out_specs = [
        pl.BlockSpec((tm, QK_WIDTH), lambda i: (i, 0)),
        pl.BlockSpec((QK_WIDTH, tm), lambda i: (0, i)),
        pl.BlockSpec((tm, MLA_WIDTH), lambda i: (i, 0)),
        pl.BlockSpec((tm, SGU_WIDTH), lambda i: (i, 0)),
    ]
    extra_inputs = []
    if emit_cache:
        out_shape += [jax.ShapeDtypeStruct((n_tok, KV_LORA), F32),
                      jax.ShapeDtypeStruct((n_tok, QK_ROPE), F32)]
        out_specs += [pl.BlockSpec((tm, KV_LORA), lambda i: (i, 0)),
                      pl.BlockSpec((tm, QK_ROPE), lambda i: (i, 0))]
        for w in (wts["w_o32"], wts["w_ff1_32"], wts["w_ff2_32"]):
            rows, cols = w.shape
            blk = (rows // grid[0], cols)
            extra_inputs.append(w)
            in_specs.append(pl.BlockSpec(blk, lambda i: (i, 0)))
            out_shape.append(jax.ShapeDtypeStruct(w.shape, BF16))
            out_specs.append(pl.BlockSpec(blk, lambda i: (i, 0)))
    return pl.pallas_call(
        functools.partial(_phase1_kernel, emit_cache),
        out_shape=out_shape,
        grid=grid,
        in_specs=in_specs,
        out_specs=out_specs,
        compiler_params=pltpu.CompilerParams(
            dimension_semantics=("arbitrary",), vmem_limit_bytes=VMEM_LIMIT),
        name="phase1_ctx" if emit_cache else "phase1_lat",
    )(x2d, mod3, ktab, qtab, wts["g_attn_pre"], wts["w_attn"], wts["w_gate"], wts["g_q"], wts["w_uq"],
      wts["g_kv"], wts["w_k"], wts["w_uv"], wts["g_sgu"], wts["beta_sgu"], wts["w_sgu"],
      wts["bias_sgu"], *extra_inputs)


def _cache_kv_kernel(kin_ref, w_k_ref, w_uv_ref, k_ref, v_ref):
    kin = kin_ref[...].astype(BF16)
    k_ref[...] = _dot_nt(w_k_ref[...], kin).astype(BF16)
    v_ref[...] = _dot(kin[:, 0:KV_LORA], w_uv_ref[...]).astype(BF16)


def _cache_kv(kin2d, wts):
    n_tok = kin2d.shape[0]
    tm = n_tok
    return pl.pallas_call(
        _cache_kv_kernel,
        out_shape=[jax.ShapeDtypeStruct((QK_WIDTH, n_tok), BF16),
                   jax.ShapeDtypeStruct((n_tok, MLA_WIDTH), BF16)],
        grid=(n_tok // tm,),
        in_specs=[
            pl.BlockSpec((tm, 2 * LANES), lambda i: (i, 0)),
            _const_spec((QK_WIDTH, 2 * LANES)),
            _const_spec((KV_LORA, MLA_WIDTH)),
        ],
        out_specs=[pl.BlockSpec((QK_WIDTH, tm), lambda i: (0, i)),
                   pl.BlockSpec((tm, MLA_WIDTH), lambda i: (i, 0))],
        compiler_params=pltpu.CompilerParams(
            dimension_semantics=("arbitrary",), vmem_limit_bytes=VMEM_LIMIT),
        name="cache_kv",
    )(kin2d, wts["w_k"], wts["w_uv"])


def _attend(q_ref, kv_views, r0, low_half):
    pair_out = []
    head_out = None
    for hd in range(MLA_HEADS):
        qh = q_ref[r0:r0 + SUB_ROWS, hd * HEAD_SLOT:(hd + 1) * HEAD_SLOT]
        slab = hd // 2
        scores = []
        for k_view, _ in kv_views:
            scores.append(_dot(qh, k_view(hd * HEAD_SLOT, HEAD_SLOT)))
        m = scores[0].max(axis=-1, keepdims=True)
        for s in scores[1:]:
            m = jnp.maximum(m, s.max(axis=-1, keepdims=True))
        denom = None
        acc = None
        for (_, v_view), s in zip(kv_views, scores):
            p = jnp.exp(s - m)
            ps = p.sum(axis=-1, keepdims=True)
            denom = ps if denom is None else denom + ps
            pv = _dot(p.astype(BF16), v_view(slab * LANES, LANES))
            acc = pv if acc is None else acc + pv
        o = acc / denom
        if hd % 2 == 0:
            head_out = o
        else:
            pair_out.append(jnp.where(low_half, head_out, o).astype(BF16))
    return jnp.concatenate(pair_out, axis=1)


def _phase2_kernel(n_kv, shared_kv, x_ref, mod_ref, q_ref, sgu_ref, *refs):
    kv_refs = refs[:2 * n_kv]
    (w_o_ref, g_post_ref, g_fpre_ref, g_fpost_ref, w_ff1_ref, w_ff2_ref, o_ref) = refs[2 * n_kv:]
    tq = x_ref.shape[0]
    low_half = lax.broadcasted_iota(jnp.int32, (SUB_ROWS, LANES), 1) < V_HEAD
    gate_a = mod_ref[2:3, :]
    shift_f = mod_ref[3:4, :]
    scale_f = mod_ref[4:5, :]
    gate_f = mod_ref[5:6, :]
    def attention(r0):
        views = []
        for t in range(n_kv):
            k_ref, v_ref = kv_refs[2 * t], kv_refs[2 * t + 1]
            if shared_kv:
                views.append((lambda c, w, k_ref=k_ref: k_ref[c:c + w, :],
                              lambda c, w, v_ref=v_ref: v_ref[:, c:c + w]))
            else:
                views.append((lambda c, w, k_ref=k_ref: k_ref[c:c + w, r0:r0 + SUB_ROWS],
                              lambda c, w, v_ref=v_ref: v_ref[r0:r0 + SUB_ROWS, c:c + w]))
        return _attend(q_ref, views, r0, low_half)

    def mixer_out(r0, attn):
        mix_in = jnp.concatenate([attn, sgu_ref[r0:r0 + SUB_ROWS, :]], axis=1)
        mix = _dot(mix_in, w_o_ref[...])
        x1 = x_ref[r0:r0 + SUB_ROWS, :] + gate_a * _rms(mix, g_post_ref[...])
        h = (_rms(x1, g_fpre_ref[...]) * (1.0 + scale_f) + shift_f).astype(BF16)
        return x1, h

    def ffn(h):
        f = None
        for c in range(D_FF // FF_CHUNK):
            hid = jnp.maximum(_dot(h, w_ff1_ref[:, c * FF_CHUNK:(c + 1) * FF_CHUNK]), 0.0)
            part = _dot((hid * hid).astype(BF16), w_ff2_ref[c * FF_CHUNK:(c + 1) * FF_CHUNK, :])
            f = part if f is None else f + part
        return f

    def finish(r0, x1, f):
        o_ref[r0:r0 + SUB_ROWS, :] = x1 + gate_f * _rms(f, g_fpost_ref[...])

    starts = [a * SUB_ROWS for a in range(tq // SUB_ROWS)]
    x1h = {r0: mixer_out(r0, attention(r0)) for r0 in starts}
    for r0 in starts:
        finish(r0, x1h[r0][0], ffn(x1h[r0][1]))


def _phase2(x2d, mod3, mod_row_fn, q, sgu, kv_list, kv_rows, shared_kv, tq, n_outer, n_inner, wts,
            name):
    n_kv = len(kv_list) // 2
    in_specs = [
        pl.BlockSpec((tq, D_MODEL), lambda b, i: (b * n_inner + i, 0)),
        pl.BlockSpec((None, N_MOD, D_MODEL), lambda b, i: (mod_row_fn(b), 0, 0)),
        pl.BlockSpec((tq, QK_WIDTH), lambda b, i: (b * n_inner + i, 0)),
        pl.BlockSpec((tq, SGU_WIDTH), lambda b, i: (b * n_inner + i, 0)),
    ]
    for t in range(n_kv):
        in_specs.append(pl.BlockSpec((QK_WIDTH, kv_rows[t]), lambda b, i: (0, b)))
        in_specs.append(pl.BlockSpec((kv_rows[t], MLA_WIDTH), lambda b, i: (b, 0)))
    in_specs += [
        _const_spec((D_MODEL, D_MODEL)),
        _const_spec((1, D_MODEL)),
        _const_spec((1, D_MODEL)),
        _const_spec((1, D_MODEL)),
        _const_spec((D_MODEL, D_FF)),
        _const_spec((D_FF, D_MODEL)),
    ]
    return pl.pallas_call(
        functools.partial(_phase2_kernel, n_kv, shared_kv),
        out_shape=jax.ShapeDtypeStruct(x2d.shape, F32),
        grid=(n_outer, n_inner),
        in_specs=in_specs,
        out_specs=pl.BlockSpec((tq, D_MODEL), lambda b, i: (b * n_inner + i, 0)),
        compiler_params=pltpu.CompilerParams(
            dimension_semantics=("arbitrary", "arbitrary"), vmem_limit_bytes=VMEM_LIMIT),
        name=name,
    )(x2d, mod3, q, sgu, *kv_list, wts["w_o"], wts["g_attn_post"], wts["g_ffn_pre"],
      wts["g_ffn_post"], wts["w_ff1"], wts["w_ff2"])


def _pair_swap(w):
    shp = w.shape
    return w.reshape(shp[:-1] + (shp[-1] // 2, 2))[..., ::-1].reshape(shp)


def _prepare_weights(g_attn_pre, g_attn_post, w_in, g_q, w_uq, g_kv, w_ukv, w_sgu, b_sgu, g_sgu,
                     beta_sgu, w_o, g_ffn_pre, g_ffn_post, w_ff1, w_ff2):
    i0, i1, i2 = Q_LORA, Q_LORA + KV_LORA, Q_LORA + KV_LORA + QK_ROPE
    w_kr = w_in[:, i1:i2].astype(BF16)
    w_kr_sw = _pair_swap(w_kr)
    w_attn = jnp.concatenate([w_in[:, :i1].astype(BF16), w_kr, w_kr_sw, w_kr, w_kr_sw], axis=1)
    w_gate = w_in[:, i2:].astype(BF16)

    w_uq_h = w_uq.reshape(Q_LORA, MLA_HEADS, QK_NOPE + QK_ROPE)
    w_uq_ext = jnp.concatenate(
        [w_uq_h, _pair_swap(w_uq_h[..., QK_NOPE:])], axis=-1).reshape(Q_LORA, QK_WIDTH).astype(BF16)

    w_ukv_h = w_ukv.reshape(KV_LORA, MLA_HEADS, QK_NOPE + V_HEAD)
    w_uk_slots = jnp.concatenate(
        [w_ukv_h[..., :QK_NOPE], jnp.zeros((KV_LORA, MLA_HEADS, HEAD_SLOT - QK_NOPE), F32)],
        axis=-1).reshape(KV_LORA, QK_WIDTH)
    eye = jnp.eye(QK_ROPE, dtype=F32)
    zeros_rope = jnp.zeros((QK_ROPE, QK_ROPE), F32)
    zeros_nope = jnp.zeros((QK_ROPE, QK_NOPE), F32)
    to_lo = jnp.tile(jnp.concatenate([zeros_nope, eye, zeros_rope], axis=1), (1, MLA_HEADS))
    to_hi = jnp.tile(jnp.concatenate([zeros_nope, zeros_rope, eye], axis=1), (1, MLA_HEADS))
    w_k = jnp.concatenate([w_uk_slots, to_lo, to_lo, to_hi, to_hi], axis=0).T.astype(BF16)
    w_uv = w_ukv_h[..., QK_NOPE:].reshape(KV_LORA, MLA_WIDTH).astype(BF16)

    w_sgu_pair = w_sgu.reshape(SGU_HEADS // 2, 2 * CHUNK, CHUNK).astype(BF16)
    bias_sgu = jnp.repeat(b_sgu.T, SGU_HEAD_DIM, axis=1)
    row = lambda a: a.reshape(1, -1)
    return {
        "g_attn_pre": row(g_attn_pre), "g_attn_post": row(g_attn_post), "w_attn": w_attn,
        "w_gate": w_gate,
        "g_q": row(g_q), "w_uq": w_uq_ext, "g_kv": row(g_kv), "w_k": w_k, "w_uv": w_uv,
        "g_sgu": row(g_sgu), "beta_sgu": row(beta_sgu), "w_sgu": w_sgu_pair, "bias_sgu": bias_sgu,
        "w_o32": w_o, "g_ffn_pre": row(g_ffn_pre), "g_ffn_post": row(g_ffn_post),
        "w_ff1_32": w_ff1, "w_ff2_32": w_ff2,
    }


def _rope_tables(n_tok):
    rows = n_tok // GRID_W
    row = jnp.broadcast_to(jnp.arange(rows)[:, None], (rows, GRID_W)).reshape(-1).astype(F32)
    col = jnp.broadcast_to(jnp.arange(GRID_W)[None, :], (rows, GRID_W)).reshape(-1).astype(F32)
    freqs = 1.0 / (ROPE_BASE ** (jnp.arange(AXIS_PAIRS, dtype=F32) / AXIS_PAIRS))
    ang = jnp.concatenate([row[:, None] * freqs, col[:, None] * freqs], axis=-1)
    cos = jnp.repeat(jnp.cos(ang), 2, axis=1)
    sin = jnp.repeat(jnp.sin(ang), 2, axis=1)
    sign = jnp.tile(jnp.array([-1.0, 1.0], F32), QK_ROPE // 2)
    sin = sin * sign
    ktab = jnp.concatenate([cos, sin, cos, sin], axis=1)
    qtab = ATTN_SCALE * jnp.concatenate([jnp.ones((n_tok, QK_NOPE), F32), cos, sin], axis=1)
    return ktab, qtab


def kernel(x_prompt, x_sample, cache_ckv, cache_krope, c, c_ctx, w_mod, b_mod, g_attn_pre,
           g_attn_post, w_in, g_q, w_uq, g_kv, w_ukv, w_sgu, b_sgu, g_sgu, beta_sgu, w_o,
           g_ffn_pre, g_ffn_post, w_ff1, w_ff2):
    batch, seq, _ = x_prompt.shape
    dec_batch, dec_seq, _ = x_sample.shape
    past_len = cache_ckv.shape[2]
    depth = w_mod.shape[0]
    assert depth == 1

    wts = _prepare_weights(g_attn_pre[0], g_attn_post[0], w_in[0], g_q[0], w_uq[0], g_kv[0],
                           w_ukv[0], w_sgu[0], b_sgu[0], g_sgu[0], beta_sgu[0], w_o[0],
                           g_ffn_pre[0], g_ffn_post[0], w_ff1[0], w_ff2[0])

    cond8 = jnp.concatenate(
        [c_ctx[None, :], c, jnp.zeros((MOD_ROWS - 1 - dec_batch, D_MODEL), F32)], axis=0)
    mod3 = _modulation(cond8, w_mod[0], b_mod[0].reshape(1, -1)).reshape(MOD_ROWS, N_MOD, D_MODEL)

    ktab_ctx = jnp.concatenate(
        [jnp.ones((1, QK_ROPE), F32), jnp.zeros((1, LANES - QK_ROPE), F32)], axis=1)
    qtab_ctx = ATTN_SCALE * jnp.concatenate(
        [jnp.ones((1, QK_NOPE + QK_ROPE), F32), jnp.zeros((1, QK_ROPE), F32)], axis=1)
    xp2d = x_prompt.reshape(batch * seq, D_MODEL)
    q_c, k_c, v_c, sgu_c, ckv_c, kr_c, w_o16, w_ff1_16, w_ff2_16 = _phase1(
        xp2d, mod3, lambda i: 0, ktab_ctx, qtab_ctx, lambda i: 0, wts, True)
    wts = dict(wts, w_o=w_o16, w_ff1=w_ff1_16, w_ff2=w_ff2_16)
    assert seq == SUB_ROWS and dec_seq % PHASE2_ROWS_LAT == 0
    y_prompt = _phase2(xp2d, mod3, lambda b: 0, q_c, sgu_c, [k_c, v_c], [PHASE2_ROWS_CTX], False,
                       PHASE2_ROWS_CTX, batch * seq // PHASE2_ROWS_CTX, 1, wts, "phase2_ctx")

    ktab_lat, qtab_lat = _rope_tables(dec_seq)
    tiles_per_seq = dec_seq // TOKEN_TILE
    xs2d = x_sample.reshape(dec_batch * dec_seq, D_MODEL)
    q_l, k_l, v_l, sgu_l = _phase1(
        xs2d, mod3, lambda i: 1 + i // tiles_per_seq, ktab_lat, qtab_lat,
        lambda i: i % tiles_per_seq, wts, False)
    kr_p = cache_krope[:, 0].reshape(dec_batch * past_len, QK_ROPE)
    pad = jnp.zeros_like(kr_p)
    kin_p = jnp.concatenate(
        [cache_ckv[:, 0].reshape(dec_batch * past_len, KV_LORA), kr_p, pad, kr_p, pad], axis=1)
    k_p, v_p = _cache_kv(kin_p, wts)
    y_sample = _phase2(xs2d, mod3, lambda b: 1 + b, q_l, sgu_l, [k_p, v_p, k_l, v_l],
                       [past_len, dec_seq], True, PHASE2_ROWS_LAT, dec_batch,
                       dec_seq // PHASE2_ROWS_LAT, wts, "phase2_lat")

    return (y_prompt.reshape(batch, seq, D_MODEL),
            y_sample.reshape(dec_batch, dec_seq, D_MODEL),
            ckv_c.reshape(batch, 1, seq, KV_LORA),
            kr_c.reshape(batch, 1, seq, QK_ROPE))
```

```python
import functools
import math

import jax
import jax.numpy as jnp
import numpy as np
from jax import lax
from jax.experimental import pallas as pl
from jax.experimental.pallas import tpu as pltpu

D_MODEL = 1024
GRID_W = 64
MLA_HEADS = 8
QK_NOPE = 64
QK_ROPE = 32
V_HEAD = 64
Q_LORA = 256
KV_LORA = 128
MLA_WIDTH = MLA_HEADS * V_HEAD
SGU_HEADS = 8
SGU_WIDTH = D_MODEL - MLA_WIDTH
SGU_HEAD_DIM = SGU_WIDTH // SGU_HEADS
CHUNK = 128
D_FF = 4 * D_MODEL
AXIS_PAIRS = QK_ROPE // 4
ROPE_BASE = 10000.0
EPS = 1e-6
N_MOD = 6
ATTN_SCALE = (QK_NOPE + QK_ROPE) ** -0.5

LANES = 128
HEAD_SLOT = LANES
QK_WIDTH = MLA_HEADS * HEAD_SLOT
ATTN_PROJ = Q_LORA + KV_LORA + LANES
MOD_ROWS = 8
MOD_K_ROWS = 128
TOKEN_TILE = 512
SUB_ROWS = 256
PHASE2_ROWS_CTX = 512
PHASE2_ROWS_LAT = 512
FF_CHUNK = 1024
VMEM_LIMIT = 56 * 1024 * 1024

BF16 = jnp.bfloat16
F32 = jnp.float32


def _dot(a, b):
    return jnp.dot(a, b, preferred_element_type=F32)


def _dot_nt(a, b):
    return lax.dot_general(a, b, (((1,), (1,)), ((), ())), preferred_element_type=F32)


def _rms(x, g):
    return x * lax.rsqrt(jnp.mean(x * x, axis=-1, keepdims=True) + EPS) * g


def _gelu(x):
    inner = math.sqrt(2.0 / math.pi) * (x + 0.044715 * (x * x * x))
    return x * (0.5 * (1.0 + jnp.tanh(inner)))


def _mod_kernel(cond_a_ref, cond_b_ref, w_a_ref, w_b_ref, b_ref, w_in_ref, o_ref, w_attn_ref,
                w_gate_ref):
    @pl.when(pl.program_id(0) == 0)
    def _():
        o_ref[...] = jnp.broadcast_to(b_ref[...], o_ref.shape)

    def part(cond_ref, w_ref):
        cnd = cond_ref[...]
        act = cnd * (1.0 / (1.0 + jnp.exp(-cnd)))
        return _dot(act.astype(BF16), w_ref[...].astype(BF16))

    o_ref[...] += part(cond_a_ref, w_a_ref) + part(cond_b_ref, w_b_ref)

    w = w_in_ref[...]
    rope0 = Q_LORA + KV_LORA
    slab = w[:, rope0:rope0 + LANES]
    lane = lax.broadcasted_iota(jnp.int32, slab.shape, 1)
    swapped = jnp.where(lane % 2 == 0, pltpu.roll(slab, LANES - 1, 1), pltpu.roll(slab, 1, 1))
    rope = jnp.where(
        lane < QK_ROPE, slab,
        jnp.where(lane < 2 * QK_ROPE, pltpu.roll(swapped, QK_ROPE, 1),
                  jnp.where(lane < 3 * QK_ROPE, pltpu.roll(slab, 2 * QK_ROPE, 1),
                            pltpu.roll(swapped, 3 * QK_ROPE, 1))))
    w_attn_ref[...] = jnp.concatenate([w[:, 0:rope0], rope], axis=1).astype(BF16)
    w_gate_ref[...] = w[:, rope0 + QK_ROPE:].astype(BF16)


def _modulation(cond8, w_mod, b_mod, w_in):
    n = w_mod.shape[1]
    tk = MOD_K_ROWS
    nk = D_MODEL // tk // 2
    in_w = w_in.shape[1]
    rows = D_MODEL // nk
    return pl.pallas_call(
        _mod_kernel,
        out_shape=[jax.ShapeDtypeStruct((MOD_ROWS, n), F32),
                   jax.ShapeDtypeStruct((D_MODEL, ATTN_PROJ), BF16),
                   jax.ShapeDtypeStruct((D_MODEL, 2 * SGU_WIDTH), BF16)],
        grid=(nk,),
        in_specs=[
            pl.BlockSpec((MOD_ROWS, tk), lambda k: (0, k)),
            pl.BlockSpec((MOD_ROWS, tk), lambda k: (0, k + nk)),
            pl.BlockSpec((tk, n), lambda k: (k, 0)),
            pl.BlockSpec((tk, n), lambda k: (k + nk, 0)),
            pl.BlockSpec((1, n), lambda k: (0, 0)),
            pl.BlockSpec((rows, in_w), lambda k: (k, 0)),
        ],
        out_specs=[pl.BlockSpec((MOD_ROWS, n), lambda k: (0, 0)),
                   pl.BlockSpec((rows, ATTN_PROJ), lambda k: (k, 0)),
                   pl.BlockSpec((rows, 2 * SGU_WIDTH), lambda k: (k, 0))],
        compiler_params=pltpu.CompilerParams(
            dimension_semantics=("arbitrary",), vmem_limit_bytes=VMEM_LIMIT),
        name="modulation",
    )(cond8, cond8, w_mod, w_mod, b_mod, w_in)


def _mod_rows(mod_ref, row):
    vec = mod_ref[pl.ds(row, 1), :]
    return [vec[:, j * D_MODEL:(j + 1) * D_MODEL] for j in range(N_MOD)]


def _phase1_kernel(emit_cache, mod_row_fn, x_ref, mod_ref, ktab_ref, qtab_ref, g_pre_ref, w_attn_ref,
                   w_gate_ref,
                   g_q_ref, w_uq_ref, g_kv_ref, w_k_ref, w_uv_ref, g_sgu_ref, beta_sgu_ref,
                   w_sgu_ref, bias_ref, *rest):
    if emit_cache:
        (w_o32_ref, w_ff1_32_ref, w_ff2_32_ref, q_ref, k_ref, v_ref, sgu_ref, ckv_ref, kr_ref,
         w_o16_ref, w_ff1_16_ref, w_ff2_16_ref) = rest
        w_o16_ref[...] = w_o32_ref[...].astype(BF16)
        w_ff1_16_ref[...] = w_ff1_32_ref[...].astype(BF16)
        w_ff2_16_ref[...] = w_ff2_32_ref[...].astype(BF16)
    else:
        q_ref, k_ref, v_ref, sgu_ref = rest
    tm = x_ref.shape[0]
    x = x_ref[...]
    mod = _mod_rows(mod_ref, mod_row_fn(pl.program_id(0)))
    shift_a, scale_a = mod[0], mod[1]
    h = (_rms(x, g_pre_ref[...]) * (1.0 + scale_a) + shift_a).astype(BF16)
    proj = _dot(h, w_attn_ref[...])
    gate = _dot(h, w_gate_ref[...])

    cq = _rms(proj[:, 0:Q_LORA], g_q_ref[...])
    q = _dot(cq.astype(BF16), w_uq_ref[...])
    qtab = qtab_ref[...]
    q_ref[...] = jnp.concatenate(
        [(q[:, s * HEAD_SLOT:(s + 1) * HEAD_SLOT] * qtab).astype(BF16) for s in range(MLA_HEADS)],
        axis=1)

    ckv_n = _rms(proj[:, Q_LORA:Q_LORA + KV_LORA], g_kv_ref[...])
    rope_slab = proj[:, Q_LORA + KV_LORA:Q_LORA + KV_LORA + LANES]
    if emit_cache:
        ckv_ref[...] = ckv_n
        kr_ref[...] = rope_slab[:, 0:QK_ROPE]
    ckv_b = ckv_n.astype(BF16)
    kin = jnp.concatenate([ckv_b, (rope_slab * ktab_ref[...]).astype(BF16)], axis=1)
    k_ref[...] = _dot_nt(w_k_ref[...], kin).astype(BF16)
    v_ref[...] = _dot(ckv_b, w_uv_ref[...]).astype(BF16)

    u = _gelu(gate[:, 0:SGU_WIDTH])
    vv = _gelu(gate[:, SGU_WIDTH:2 * SGU_WIDTH])
    mu = jnp.mean(vv, axis=-1, keepdims=True)
    vc = vv - mu
    var = jnp.mean(vc * vc, axis=-1, keepdims=True)
    vn = (vc * lax.rsqrt(var + EPS) * g_sgu_ref[...] + beta_sgu_ref[...]).astype(BF16)
    n_chunks = tm // CHUNK
    low_half = lax.broadcasted_iota(jnp.int32, (CHUNK, LANES), 1) < SGU_HEAD_DIM
    bias = bias_ref[...]
    mixed_rows = [[] for _ in range(n_chunks)]
    for j in range(SGU_WIDTH // LANES):
        rhs = jnp.concatenate(
            [vn[n * CHUNK:(n + 1) * CHUNK, j * LANES:(j + 1) * LANES] for n in range(n_chunks)],
            axis=1)
        o = _dot(w_sgu_ref[j], rhs)
        for n in range(n_chunks):
            even = o[0:CHUNK, n * LANES:(n + 1) * LANES]
            odd = o[CHUNK:2 * CHUNK, n * LANES:(n + 1) * LANES]
            mixed_rows[n].append(jnp.where(low_half, even, odd))
    for n in range(n_chunks):
        mixed = jnp.concatenate(mixed_rows[n], axis=1) + bias
        sgu_ref[n * CHUNK:(n + 1) * CHUNK, :] = (
            u[n * CHUNK:(n + 1) * CHUNK, :] * mixed).astype(BF16)


def _const_spec(shape):
    nd = len(shape)
    return pl.BlockSpec(shape, lambda *_: (0,) * nd, pipeline_mode=pl.Buffered(1))


def _phase1(x2d, mod3, mod_row_fn, ktab, qtab, tab_fn, wts, emit_cache):
    n_tok = x2d.shape[0]
    tm = TOKEN_TILE
    grid = (n_tok // tm,)
    tab_block = (ktab.shape[0] if ktab.shape[0] == 1 else tm, LANES)
    in_specs = [
        pl.BlockSpec((tm, D_MODEL), lambda i: (i, 0)),
        _const_spec((MOD_ROWS, N_MOD * D_MODEL)),
        pl.BlockSpec(tab_block, lambda i: (tab_fn(i), 0)),
        pl.BlockSpec(tab_block, lambda i: (tab_fn(i), 0)),
        _const_spec((1, D_MODEL)),
        _const_spec((D_MODEL, ATTN_PROJ)),
        _const_spec((D_MODEL, 2 * SGU_WIDTH)),
        _const_spec((1, Q_LORA)),
        _const_spec((Q_LORA, QK_WIDTH)),
        _const_spec((1, KV_LORA)),
        _const_spec((QK_WIDTH, 2 * LANES)),
        _const_spec((KV_LORA, MLA_WIDTH)),
        _const_spec((1, SGU_WIDTH)),
        _const_spec((1, SGU_WIDTH)),
        _const_spec((SGU_WIDTH // LANES, 2 * CHUNK, CHUNK)),
        _const_spec((CHUNK, SGU_WIDTH)),
    ]
    out_shape = [
        jax.ShapeDtypeStruct((n_tok, QK_WIDTH), BF16),
        jax.ShapeDtypeStruct((QK_WIDTH, n_tok), BF16),
        jax.ShapeDtypeStruct((n_tok, MLA_WIDTH), BF16),
        jax.ShapeDtypeStruct((n_tok, SGU_WIDTH), BF16),
    ]
    out_specs = [
        pl.BlockSpec((tm, QK_WIDTH), lambda i: (i, 0)),
        pl.BlockSpec((QK_WIDTH, tm), lambda i: (0, i)),
        pl.BlockSpec((tm, MLA_WIDTH), lambda i: (i, 0)),
        pl.BlockSpec((tm, SGU_WIDTH), lambda i: (i, 0)),
    ]
    extra_inputs = []
    if emit_cache:
        out_shape += [jax.ShapeDtypeStruct((n_tok, KV_LORA), F32),
                      jax.ShapeDtypeStruct((n_tok, QK_ROPE), F32)]
        out_specs += [pl.BlockSpec((tm, KV_LORA), lambda i: (i, 0)),
                      pl.BlockSpec((tm, QK_ROPE), lambda i: (i, 0))]
        for w in (wts["w_o32"], wts["w_ff1_32"], wts["w_ff2_32"]):
            rows, cols = w.shape
            blk = (rows // grid[0], cols)
            extra_inputs.append(w)
            in_specs.append(pl.BlockSpec(blk, lambda i: (i, 0)))
            out_shape.append(jax.ShapeDtypeStruct(w.shape, BF16))
            out_specs.append(pl.BlockSpec(blk, lambda i: (i, 0)))
    return pl.pallas_call(
        functools.partial(_phase1_kernel, emit_cache, mod_row_fn),
        out_shape=out_shape,
        grid=grid,
        in_specs=in_specs,
        out_specs=out_specs,
        compiler_params=pltpu.CompilerParams(
            dimension_semantics=("arbitrary",), vmem_limit_bytes=VMEM_LIMIT),
        name="phase1_ctx" if emit_cache else "phase1_lat",
    )(x2d, mod3, ktab, qtab, wts["g_attn_pre"], wts["w_attn"], wts["w_gate"], wts["g_q"], wts["w_uq"],
      wts["g_kv"], wts["w_k"], wts["w_uv"], wts["g_sgu"], wts["beta_sgu"], wts["w_sgu"],
      wts["bias_sgu"], *extra_inputs)


def _cache_kv_kernel(kin_ref, w_k_ref, w_uv_ref, k_ref, v_ref):
    kin = kin_ref[...].astype(BF16)
    k_ref[...] = _dot_nt(w_k_ref[...], kin).astype(BF16)
    v_ref[...] = _dot(kin[:, 0:KV_LORA], w_uv_ref[...]).astype(BF16)


def _cache_kv(kin2d, wts):
    n_tok = kin2d.shape[0]
    tm = n_tok
    return pl.pallas_call(
        _cache_kv_kernel,
        out_shape=[jax.ShapeDtypeStruct((QK_WIDTH, n_tok), BF16),
                   jax.ShapeDtypeStruct((n_tok, MLA_WIDTH), BF16)],
        grid=(n_tok // tm,),
        in_specs=[
            pl.BlockSpec((tm, 2 * LANES), lambda i: (i, 0)),
            _const_spec((QK_WIDTH, 2 * LANES)),
            _const_spec((KV_LORA, MLA_WIDTH)),
        ],
        out_specs=[pl.BlockSpec((QK_WIDTH, tm), lambda i: (0, i)),
                   pl.BlockSpec((tm, MLA_WIDTH), lambda i: (i, 0))],
        compiler_params=pltpu.CompilerParams(
            dimension_semantics=("arbitrary",), vmem_limit_bytes=VMEM_LIMIT),
        name="cache_kv",
    )(kin2d, wts["w_k"], wts["w_uv"])


def _attend(q_ref, kv_views, r0, low_half):
    pair_out = []
    head_out = None
    for hd in range(MLA_HEADS):
        qh = q_ref[r0:r0 + SUB_ROWS, hd * HEAD_SLOT:(hd + 1) * HEAD_SLOT]
        slab = hd // 2
        scores = []
        for k_view, _ in kv_views:
            scores.append(_dot(qh, k_view(hd * HEAD_SLOT, HEAD_SLOT)))
        m = scores[0].max(axis=-1, keepdims=True)
        for s in scores[1:]:
            m = jnp.maximum(m, s.max(axis=-1, keepdims=True))
        denom = None
        acc = None
        for (_, v_view), s in zip(kv_views, scores):
            p = jnp.exp(s - m)
            ps = p.sum(axis=-1, keepdims=True)
            denom = ps if denom is None else denom + ps
            pv = _dot(p.astype(BF16), v_view(slab * LANES, LANES))
            acc = pv if acc is None else acc + pv
        o = acc / denom
        if hd % 2 == 0:
            head_out = o
        else:
            pair_out.append(jnp.where(low_half, head_out, o).astype(BF16))
    return jnp.concatenate(pair_out, axis=1)


def _phase2_kernel(n_kv, shared_kv, mod_row_fn, x_ref, mod_ref, q_ref, sgu_ref, *refs):
    kv_refs = refs[:2 * n_kv]
    (w_o_ref, g_post_ref, g_fpre_ref, g_fpost_ref, w_ff1_ref, w_ff2_ref, o_ref) = refs[2 * n_kv:]
    tq = x_ref.shape[0]
    low_half = lax.broadcasted_iota(jnp.int32, (SUB_ROWS, LANES), 1) < V_HEAD
    _, _, gate_a, shift_f, scale_f, gate_f = _mod_rows(mod_ref, mod_row_fn(pl.program_id(0)))
    def attention(r0):
        views = []
        for t in range(n_kv):
            k_ref, v_ref = kv_refs[2 * t], kv_refs[2 * t + 1]
            if shared_kv:
                views.append((lambda c, w, k_ref=k_ref: k_ref[c:c + w, :],
                              lambda c, w, v_ref=v_ref: v_ref[:, c:c + w]))
            else:
                views.append((lambda c, w, k_ref=k_ref: k_ref[c:c + w, r0:r0 + SUB_ROWS],
                              lambda c, w, v_ref=v_ref: v_ref[r0:r0 + SUB_ROWS, c:c + w]))
        return _attend(q_ref, views, r0, low_half)

    def mixer_out(r0, attn):
        mix_in = jnp.concatenate([attn, sgu_ref[r0:r0 + SUB_ROWS, :]], axis=1)
        mix = _dot(mix_in, w_o_ref[...])
        x1 = x_ref[r0:r0 + SUB_ROWS, :] + gate_a * _rms(mix, g_post_ref[...])
        h = (_rms(x1, g_fpre_ref[...]) * (1.0 + scale_f) + shift_f).astype(BF16)
        return x1, h

    def ffn(h):
        f = None
        for c in range(D_FF // FF_CHUNK):
            hid = jnp.maximum(_dot(h, w_ff1_ref[:, c * FF_CHUNK:(c + 1) * FF_CHUNK]), 0.0)
            part = _dot((hid * hid).astype(BF16), w_ff2_ref[c * FF_CHUNK:(c + 1) * FF_CHUNK, :])
            f = part if f is None else f + part
        return f

    def finish(r0, x1, f):
        o_ref[r0:r0 + SUB_ROWS, :] = x1 + gate_f * _rms(f, g_fpost_ref[...])

    starts = [a * SUB_ROWS for a in range(tq // SUB_ROWS)]
    x1h = {r0: mixer_out(r0, attention(r0)) for r0 in starts}
    for r0 in starts:
        finish(r0, x1h[r0][0], ffn(x1h[r0][1]))


def _phase2(x2d, mod3, mod_row_fn, q, sgu, kv_list, kv_rows, shared_kv, tq, n_outer, n_inner, wts,
            name):
    n_kv = len(kv_list) // 2
    in_specs = [
        pl.BlockSpec((tq, D_MODEL), lambda b, i: (b * n_inner + i, 0)),
        _const_spec((MOD_ROWS, N_MOD * D_MODEL)),
        pl.BlockSpec((tq, QK_WIDTH), lambda b, i: (b * n_inner + i, 0)),
        pl.BlockSpec((tq, SGU_WIDTH), lambda b, i: (b * n_inner + i, 0)),
    ]
    for t in range(n_kv):
        in_specs.append(pl.BlockSpec((QK_WIDTH, kv_rows[t]), lambda b, i: (0, b)))
        in_specs.append(pl.BlockSpec((kv_rows[t], MLA_WIDTH), lambda b, i: (b, 0)))
    in_specs += [
        _const_spec((D_MODEL, D_MODEL)),
        _const_spec((1, D_MODEL)),
        _const_spec((1, D_MODEL)),
        _const_spec((1, D_MODEL)),
        _const_spec((D_MODEL, D_FF)),
        _const_spec((D_FF, D_MODEL)),
    ]
    return pl.pallas_call(
        functools.partial(_phase2_kernel, n_kv, shared_kv, mod_row_fn),
        out_shape=jax.ShapeDtypeStruct(x2d.shape, F32),
        grid=(n_outer, n_inner),
        in_specs=in_specs,
        out_specs=pl.BlockSpec((tq, D_MODEL), lambda b, i: (b * n_inner + i, 0)),
        compiler_params=pltpu.CompilerParams(
            dimension_semantics=("arbitrary", "arbitrary"), vmem_limit_bytes=VMEM_LIMIT),
        name=name,
    )(x2d, mod3, q, sgu, *kv_list, wts["w_o"], wts["g_attn_post"], wts["g_ffn_pre"],
      wts["g_ffn_post"], wts["w_ff1"], wts["w_ff2"])


def _pair_swap(w):
    shp = w.shape
    return w.reshape(shp[:-1] + (shp[-1] // 2, 2))[..., ::-1].reshape(shp)


def _prepare_weights(g_attn_pre, g_attn_post, g_q, w_uq, g_kv, w_ukv, w_sgu, b_sgu, g_sgu,
                     beta_sgu, w_o, g_ffn_pre, g_ffn_post, w_ff1, w_ff2):
    w_uq_h = w_uq.reshape(Q_LORA, MLA_HEADS, QK_NOPE + QK_ROPE)
    w_uq_ext = jnp.concatenate(
        [w_uq_h, _pair_swap(w_uq_h[..., QK_NOPE:])], axis=-1).reshape(Q_LORA, QK_WIDTH).astype(BF16)

    w_ukv_h = w_ukv.reshape(KV_LORA, MLA_HEADS, QK_NOPE + V_HEAD)
    w_uk_slots = jnp.concatenate(
        [w_ukv_h[..., :QK_NOPE], jnp.zeros((KV_LORA, MLA_HEADS, HEAD_SLOT - QK_NOPE), F32)],
        axis=-1).reshape(KV_LORA, QK_WIDTH)
    eye = jnp.eye(QK_ROPE, dtype=F32)
    zeros_rope = jnp.zeros((QK_ROPE, QK_ROPE), F32)
    zeros_nope = jnp.zeros((QK_ROPE, QK_NOPE), F32)
    to_lo = jnp.tile(jnp.concatenate([zeros_nope, eye, zeros_rope], axis=1), (1, MLA_HEADS))
    to_hi = jnp.tile(jnp.concatenate([zeros_nope, zeros_rope, eye], axis=1), (1, MLA_HEADS))
    w_k = jnp.concatenate([w_uk_slots, to_lo, to_lo, to_hi, to_hi], axis=0).T.astype(BF16)
    w_uv = w_ukv_h[..., QK_NOPE:].reshape(KV_LORA, MLA_WIDTH).astype(BF16)

    w_sgu_pair = w_sgu.reshape(SGU_HEADS // 2, 2 * CHUNK, CHUNK).astype(BF16)
    bias_sgu = jnp.repeat(b_sgu.T, SGU_HEAD_DIM, axis=1)
    row = lambda a: a.reshape(1, -1)
    return {
        "g_attn_pre": row(g_attn_pre), "g_attn_post": row(g_attn_post),
        "g_q": row(g_q), "w_uq": w_uq_ext, "g_kv": row(g_kv), "w_k": w_k, "w_uv": w_uv,
        "g_sgu": row(g_sgu), "beta_sgu": row(beta_sgu), "w_sgu": w_sgu_pair, "bias_sgu": bias_sgu,
        "w_o32": w_o, "g_ffn_pre": row(g_ffn_pre), "g_ffn_post": row(g_ffn_post),
        "w_ff1_32": w_ff1, "w_ff2_32": w_ff2,
    }


def _rope_tables(n_tok):
    rows = n_tok // GRID_W
    row = np.repeat(np.arange(rows), GRID_W).astype(np.float32)
    col = np.tile(np.arange(GRID_W), rows).astype(np.float32)
    freqs = (1.0 / (ROPE_BASE ** (np.arange(AXIS_PAIRS, dtype=np.float32) / AXIS_PAIRS))).astype(
        np.float32)
    ang = np.concatenate([row[:, None] * freqs, col[:, None] * freqs], axis=-1)
    cos = np.repeat(np.cos(ang), 2, axis=1)
    sin = np.repeat(np.sin(ang), 2, axis=1) * np.tile(np.array([-1.0, 1.0], np.float32), QK_ROPE // 2)
    ktab = np.concatenate([cos, sin, cos, sin], axis=1)
    qtab = ATTN_SCALE * np.concatenate([np.ones((n_tok, QK_NOPE), np.float32), cos, sin], axis=1)
    return jnp.asarray(ktab, F32), jnp.asarray(qtab, F32)


def kernel(x_prompt, x_sample, cache_ckv, cache_krope, c, c_ctx, w_mod, b_mod, g_attn_pre,
           g_attn_post, w_in, g_q, w_uq, g_kv, w_ukv, w_sgu, b_sgu, g_sgu, beta_sgu, w_o,
           g_ffn_pre, g_ffn_post, w_ff1, w_ff2):
    batch, seq, _ = x_prompt.shape
    dec_batch, dec_seq, _ = x_sample.shape
    past_len = cache_ckv.shape[2]
    depth = w_mod.shape[0]
    assert depth == 1

    wts = _prepare_weights(g_attn_pre[0], g_attn_post[0], g_q[0], w_uq[0], g_kv[0],
                           w_ukv[0], w_sgu[0], b_sgu[0], g_sgu[0], beta_sgu[0], w_o[0],
                           g_ffn_pre[0], g_ffn_post[0], w_ff1[0], w_ff2[0])

    cond8 = jnp.concatenate(
        [c_ctx[None, :], c, jnp.zeros((MOD_ROWS - 1 - dec_batch, D_MODEL), F32)], axis=0)
    mod3, w_attn, w_gate = _modulation(cond8, w_mod[0], b_mod[0].reshape(1, -1), w_in[0])
    wts = dict(wts, w_attn=w_attn, w_gate=w_gate)

    ktab_ctx = jnp.asarray(np.arange(LANES)[None, :] < QK_ROPE, F32)
    qtab_ctx = jnp.asarray(ATTN_SCALE * (np.arange(LANES)[None, :] < QK_NOPE + QK_ROPE), F32)
    xp2d = x_prompt.reshape(batch * seq, D_MODEL)
    q_c, k_c, v_c, sgu_c, ckv_c, kr_c, w_o16, w_ff1_16, w_ff2_16 = _phase1(
        xp2d, mod3, lambda i: 0, ktab_ctx, qtab_ctx, lambda i: 0, wts, True)
    wts = dict(wts, w_o=w_o16, w_ff1=w_ff1_16, w_ff2=w_ff2_16)
    assert seq == SUB_ROWS and dec_seq % PHASE2_ROWS_LAT == 0
    y_prompt = _phase2(xp2d, mod3, lambda b: 0, q_c, sgu_c, [k_c, v_c], [PHASE2_ROWS_CTX], False,
                       PHASE2_ROWS_CTX, batch * seq // PHASE2_ROWS_CTX, 1, wts, "phase2_ctx")

    ktab_lat, qtab_lat = _rope_tables(dec_seq)
    tiles_per_seq = dec_seq // TOKEN_TILE
    xs2d = x_sample.reshape(dec_batch * dec_seq, D_MODEL)
    q_l, k_l, v_l, sgu_l = _phase1(
        xs2d, mod3, lambda i: 1 + i // tiles_per_seq, ktab_lat, qtab_lat,
        lambda i: i % tiles_per_seq, wts, False)
    kr_p = cache_krope[:, 0].reshape(dec_batch * past_len, QK_ROPE)
    pad = jnp.zeros_like(kr_p)
    kin_p = jnp.concatenate(
        [cache_ckv[:, 0].reshape(dec_batch * past_len, KV_LORA), kr_p, pad, kr_p, pad], axis=1)
    k_p, v_p = _cache_kv(kin_p, wts)
    y_sample = _phase2(xs2d, mod3, lambda b: 1 + b, q_l, sgu_l, [k_p, v_p, k_l, v_l],
                       [past_len, dec_seq], True, PHASE2_ROWS_LAT, dec_batch,
                       dec_seq // PHASE2_ROWS_LAT, wts, "phase2_lat")

    return (y_prompt.reshape(batch, seq, D_MODEL),
            y_sample.reshape(dec_batch, dec_seq, D_MODEL),
            ckv_c.reshape(batch, 1, seq, KV_LORA),
            kr_c.reshape(batch, 1, seq, QK_ROPE))
```

```python
import functools
import math

import jax
import jax.numpy as jnp
import numpy as np
from jax import lax
from jax.experimental import pallas as pl
from jax.experimental.pallas import tpu as pltpu

D_MODEL = 1024
GRID_W = 64
MLA_HEADS = 8
QK_NOPE = 64
QK_ROPE = 32
V_HEAD = 64
Q_LORA = 256
KV_LORA = 128
MLA_WIDTH = MLA_HEADS * V_HEAD
SGU_HEADS = 8
SGU_WIDTH = D_MODEL - MLA_WIDTH
SGU_HEAD_DIM = SGU_WIDTH // SGU_HEADS
CHUNK = 128
D_FF = 4 * D_MODEL
AXIS_PAIRS = QK_ROPE // 4
ROPE_BASE = 10000.0
EPS = 1e-6
N_MOD = 6
ATTN_SCALE = (QK_NOPE + QK_ROPE) ** -0.5

LANES = 128
HEAD_SLOT = LANES
QK_WIDTH = MLA_HEADS * HEAD_SLOT
ATTN_PROJ = Q_LORA + KV_LORA + LANES
MOD_ROWS = 8
MOD_K_ROWS = 128
TOKEN_TILE = 512
SUB_ROWS = 256
PHASE2_ROWS_CTX = 512
PHASE2_ROWS_LAT = 512
FF_CHUNK = 1024
VMEM_LIMIT = 56 * 1024 * 1024

BF16 = jnp.bfloat16
F32 = jnp.float32


def _dot(a, b):
    return jnp.dot(a, b, preferred_element_type=F32)


def _dot_nt(a, b):
    return lax.dot_general(a, b, (((1,), (1,)), ((), ())), preferred_element_type=F32)


def _rms(x, g):
    return x * lax.rsqrt(jnp.mean(x * x, axis=-1, keepdims=True) + EPS) * g


def _gelu(x):
    inner = math.sqrt(2.0 / math.pi) * (x + 0.044715 * (x * x * x))
    return x * (0.5 * (1.0 + jnp.tanh(inner)))


def _mod_kernel(cond_a_ref, cond_b_ref, w_a_ref, w_b_ref, b_ref, w_in_attn_ref, w_in_gate_ref,
                o_ref, w_attn_ref, w_gate_ref):
    @pl.when(pl.program_id(0) == 0)
    def _():
        o_ref[...] = jnp.broadcast_to(b_ref[...], o_ref.shape)

    def part(cond_ref, w_ref):
        cnd = cond_ref[...]
        act = cnd * (1.0 / (1.0 + jnp.exp(-cnd)))
        return _dot(act.astype(BF16), w_ref[...].astype(BF16))

    o_ref[...] += part(cond_a_ref, w_a_ref) + part(cond_b_ref, w_b_ref)

    w_gate_ref[...] = w_in_gate_ref[...].astype(BF16)

    @pl.when(pl.program_id(0) == 0)
    def _():
        rope0 = Q_LORA + KV_LORA
        kr = w_in_attn_ref[rope0:rope0 + QK_ROPE, :]
        row = lax.broadcasted_iota(jnp.int32, kr.shape, 0)
        kr_sw = jnp.where(row % 2 == 0, pltpu.roll(kr, QK_ROPE - 1, 0), pltpu.roll(kr, 1, 0))
        w_attn_ref[0:rope0 + QK_ROPE, :] = w_in_attn_ref[...].astype(BF16)
        w_attn_ref[rope0 + QK_ROPE:ATTN_PROJ, :] = jnp.concatenate(
            [kr_sw, kr, kr_sw], axis=0).astype(BF16)


def _modulation(cond8, w_mod, b_mod, w_in_t):
    n = w_mod.shape[1]
    tk = MOD_K_ROWS
    nk = D_MODEL // tk // 2
    attn_rows = Q_LORA + KV_LORA + QK_ROPE
    gate_rows = 2 * SGU_WIDTH // nk
    return pl.pallas_call(
        _mod_kernel,
        out_shape=[jax.ShapeDtypeStruct((MOD_ROWS, n), F32),
                   jax.ShapeDtypeStruct((ATTN_PROJ, D_MODEL), BF16),
                   jax.ShapeDtypeStruct((2 * SGU_WIDTH, D_MODEL), BF16)],
        grid=(nk,),
        in_specs=[
            pl.BlockSpec((MOD_ROWS, tk), lambda k: (0, k)),
            pl.BlockSpec((MOD_ROWS, tk), lambda k: (0, k + nk)),
            pl.BlockSpec((tk, n), lambda k: (k, 0)),
            pl.BlockSpec((tk, n), lambda k: (k + nk, 0)),
            pl.BlockSpec((1, n), lambda k: (0, 0)),
            pl.BlockSpec((attn_rows, D_MODEL), lambda k: (0, 0)),
            pl.BlockSpec((pl.Element(gate_rows), pl.Element(D_MODEL)),
                         lambda k: (pl.multiple_of(attn_rows + k * gate_rows, QK_ROPE), 0)),
        ],
        out_specs=[pl.BlockSpec((MOD_ROWS, n), lambda k: (0, 0)),
                   pl.BlockSpec((ATTN_PROJ, D_MODEL), lambda k: (0, 0)),
                   pl.BlockSpec((gate_rows, D_MODEL), lambda k: (k, 0))],
        compiler_params=pltpu.CompilerParams(
            dimension_semantics=("arbitrary",), vmem_limit_bytes=VMEM_LIMIT),
        name="modulation",
    )(cond8, cond8, w_mod, w_mod, b_mod, w_in_t, w_in_t)


def _mod_rows(mod_ref, row):
    vec = mod_ref[pl.ds(row, 1), :]
    return [vec[:, j * D_MODEL:(j + 1) * D_MODEL] for j in range(N_MOD)]


def _phase1_kernel(emit_cache, mod_row_fn, x_ref, mod_ref, ktab_ref, qtab_ref, g_pre_ref, w_attn_ref,
                   w_gate_ref,
                   g_q_ref, w_uq_ref, g_kv_ref, w_k_ref, w_uv_ref, g_sgu_ref, beta_sgu_ref,
                   w_sgu_ref, bias_ref, *rest):
    if emit_cache:
        (w_o32_ref, w_ff1_32_ref, w_ff2_32_ref, q_ref, k_ref, v_ref, sgu_ref, ckv_ref, kr_ref,
         w_o16_ref, w_ff1_16_ref, w_ff2_16_ref) = rest
        w_o16_ref[...] = w_o32_ref[...].astype(BF16)
        w_ff1_16_ref[...] = w_ff1_32_ref[...].astype(BF16)
        w_ff2_16_ref[...] = w_ff2_32_ref[...].astype(BF16)
    else:
        q_ref, k_ref, v_ref, sgu_ref = rest
    tm = x_ref.shape[0]
    x = x_ref[...]
    mod = _mod_rows(mod_ref, mod_row_fn(pl.program_id(0)))
    shift_a, scale_a = mod[0], mod[1]
    h = (_rms(x, g_pre_ref[...]) * (1.0 + scale_a) + shift_a).astype(BF16)
    proj = _dot_nt(h, w_attn_ref[...])
    gate = _dot_nt(h, w_gate_ref[...])

    cq = _rms(proj[:, 0:Q_LORA], g_q_ref[...])
    q = _dot(cq.astype(BF16), w_uq_ref[...])
    qtab = qtab_ref[...]
    q_ref[...] = jnp.concatenate(
        [(q[:, s * HEAD_SLOT:(s + 1) * HEAD_SLOT] * qtab).astype(BF16) for s in range(MLA_HEADS)],
        axis=1)

    ckv_n = _rms(proj[:, Q_LORA:Q_LORA + KV_LORA], g_kv_ref[...])
    rope_slab = proj[:, Q_LORA + KV_LORA:Q_LORA + KV_LORA + LANES]
    if emit_cache:
        ckv_ref[...] = ckv_n
        kr_ref[...] = rope_slab[:, 0:QK_ROPE]
    ckv_b = ckv_n.astype(BF16)
    kin = jnp.concatenate([ckv_b, (rope_slab * ktab_ref[...]).astype(BF16)], axis=1)
    k_ref[...] = _dot_nt(w_k_ref[...], kin).astype(BF16)
    v_ref[...] = _dot(ckv_b, w_uv_ref[...]).astype(BF16)

    u = _gelu(gate[:, 0:SGU_WIDTH])
    vv = _gelu(gate[:, SGU_WIDTH:2 * SGU_WIDTH])
    mu = jnp.mean(vv, axis=-1, keepdims=True)
    vc = vv - mu
    var = jnp.mean(vc * vc, axis=-1, keepdims=True)
    vn = (vc * lax.rsqrt(var + EPS) * g_sgu_ref[...] + beta_sgu_ref[...]).astype(BF16)
    n_chunks = tm // CHUNK
    low_half = lax.broadcasted_iota(jnp.int32, (CHUNK, LANES), 1) < SGU_HEAD_DIM
    bias = bias_ref[...]
    mixed_rows = [[] for _ in range(n_chunks)]
    for j in range(SGU_WIDTH // LANES):
        rhs = jnp.concatenate(
            [vn[n * CHUNK:(n + 1) * CHUNK, j * LANES:(j + 1) * LANES] for n in range(n_chunks)],
            axis=1)
        o = _dot(w_sgu_ref[j], rhs)
        for n in range(n_chunks):
            even = o[0:CHUNK, n * LANES:(n + 1) * LANES]
            odd = o[CHUNK:2 * CHUNK, n * LANES:(n + 1) * LANES]
            mixed_rows[n].append(jnp.where(low_half, even, odd))
    for n in range(n_chunks):
        mixed = jnp.concatenate(mixed_rows[n], axis=1) + bias
        sgu_ref[n * CHUNK:(n + 1) * CHUNK, :] = (
            u[n * CHUNK:(n + 1) * CHUNK, :] * mixed).astype(BF16)


def _const_spec(shape):
    nd = len(shape)
    return pl.BlockSpec(shape, lambda *_: (0,) * nd, pipeline_mode=pl.Buffered(1))


def _phase1(x2d, mod3, mod_row_fn, ktab, qtab, tab_fn, wts, emit_cache):
    n_tok = x2d.shape[0]
    tm = TOKEN_TILE
    grid = (n_tok // tm,)
    tab_block = (ktab.shape[0] if ktab.shape[0] == 1 else tm, LANES)
    in_specs = [
        pl.BlockSpec((tm, D_MODEL), lambda i: (i, 0)),
        _const_spec((MOD_ROWS, N_MOD * D_MODEL)),
        pl.BlockSpec(tab_block, lambda i: (tab_fn(i), 0)),
        pl.BlockSpec(tab_block, lambda i: (tab_fn(i), 0)),
        _const_spec((1, D_MODEL)),
        _const_spec((ATTN_PROJ, D_MODEL)),
        _const_spec((2 * SGU_WIDTH, D_MODEL)),
        _const_spec((1, Q_LORA)),
        _const_spec((Q_LORA, QK_WIDTH)),
        _const_spec((1, KV_LORA)),
        _const_spec((QK_WIDTH, 2 * LANES)),
        _const_spec((KV_LORA, MLA_WIDTH)),
        _const_spec((1, SGU_WIDTH)),
        _const_spec((1, SGU_WIDTH)),
        _const_spec((SGU_WIDTH // LANES, 2 * CHUNK, CHUNK)),
        _const_spec((CHUNK, SGU_WIDTH)),
    ]
    out_shape = [
        jax.ShapeDtypeStruct((n_tok, QK_WIDTH), BF16),
        jax.ShapeDtypeStruct((QK_WIDTH, n_tok), BF16),
        jax.ShapeDtypeStruct((n_tok, MLA_WIDTH), BF16),
        jax.ShapeDtypeStruct((n_tok, SGU_WIDTH), BF16),
    ]
    out_specs = [
        pl.BlockSpec((tm, QK_WIDTH), lambda i: (i, 0)),
        pl.BlockSpec((QK_WIDTH, tm), lambda i: (0, i)),
        pl.BlockSpec((tm, MLA_WIDTH), lambda i: (i, 0)),
        pl.BlockSpec((tm, SGU_WIDTH), lambda i: (i, 0)),
    ]
    extra_inputs = []
    if emit_cache:
        out_shape += [jax.ShapeDtypeStruct((n_tok, KV_LORA), F32),
                      jax.ShapeDtypeStruct((n_tok, QK_ROPE), F32)]
        out_specs += [pl.BlockSpec((tm, KV_LORA), lambda i: (i, 0)),
                      pl.BlockSpec((tm, QK_ROPE), lambda i: (i, 0))]
        for w in (wts["w_o32"], wts["w_ff1_32"], wts["w_ff2_32"]):
            rows, cols = w.shape
            blk = (rows // grid[0], cols)
            extra_inputs.append(w)
            in_specs.append(pl.BlockSpec(blk, lambda i: (i, 0)))
            out_shape.append(jax.ShapeDtypeStruct(w.shape, BF16))
            out_specs.append(pl.BlockSpec(blk, lambda i: (i, 0)))
    return pl.pallas_call(
        functools.partial(_phase1_kernel, emit_cache, mod_row_fn),
        out_shape=out_shape,
        grid=grid,
        in_specs=in_specs,
        out_specs=out_specs,
        compiler_params=pltpu.CompilerParams(
            dimension_semantics=("arbitrary",), vmem_limit_bytes=VMEM_LIMIT),
        name="phase1_ctx" if emit_cache else "phase1_lat",
    )(x2d, mod3, ktab, qtab, wts["g_attn_pre"], wts["w_attn"], wts["w_gate"], wts["g_q"], wts["w_uq"],
      wts["g_kv"], wts["w_k"], wts["w_uv"], wts["g_sgu"], wts["beta_sgu"], wts["w_sgu"],
      wts["bias_sgu"], *extra_inputs)


def _cache_kv_kernel(kin_ref, w_k_ref, w_uv_ref, k_ref, v_ref):
    kin = kin_ref[...].astype(BF16)
    k_ref[...] = _dot_nt(w_k_ref[...], kin).astype(BF16)
    v_ref[...] = _dot(kin[:, 0:KV_LORA], w_uv_ref[...]).astype(BF16)


def _cache_kv(kin2d, wts):
    n_tok = kin2d.shape[0]
    tm = n_tok
    return pl.pallas_call(
        _cache_kv_kernel,
        out_shape=[jax.ShapeDtypeStruct((QK_WIDTH, n_tok), BF16),
                   jax.ShapeDtypeStruct((n_tok, MLA_WIDTH), BF16)],
        grid=(n_tok // tm,),
        in_specs=[
            pl.BlockSpec((tm, 2 * LANES), lambda i: (i, 0)),
            _const_spec((QK_WIDTH, 2 * LANES)),
            _const_spec((KV_LORA, MLA_WIDTH)),
        ],
        out_specs=[pl.BlockSpec((QK_WIDTH, tm), lambda i: (0, i)),
                   pl.BlockSpec((tm, MLA_WIDTH), lambda i: (i, 0))],
        compiler_params=pltpu.CompilerParams(
            dimension_semantics=("arbitrary",), vmem_limit_bytes=VMEM_LIMIT),
        name="cache_kv",
    )(kin2d, wts["w_k"], wts["w_uv"])


def _attend(q_ref, kv_views, r0, low_half):
    pair_out = []
    head_out = None
    for hd in range(MLA_HEADS):
        qh = q_ref[r0:r0 + SUB_ROWS, hd * HEAD_SLOT:(hd + 1) * HEAD_SLOT]
        slab = hd // 2
        scores = []
        for k_view, _ in kv_views:
            scores.append(_dot(qh, k_view(hd * HEAD_SLOT, HEAD_SLOT)))
        m = scores[0].max(axis=-1, keepdims=True)
        for s in scores[1:]:
            m = jnp.maximum(m, s.max(axis=-1, keepdims=True))
        denom = None
        acc = None
        for (_, v_view), s in zip(kv_views, scores):
            p = jnp.exp(s - m)
            ps = p.sum(axis=-1, keepdims=True)
            denom = ps if denom is None else denom + ps
            pv = _dot(p.astype(BF16), v_view(slab * LANES, LANES))
            acc = pv if acc is None else acc + pv
        o = acc / denom
        if hd % 2 == 0:
            head_out = o
        else:
            pair_out.append(jnp.where(low_half, head_out, o).astype(BF16))
    return jnp.concatenate(pair_out, axis=1)


def _phase2_kernel(n_kv, shared_kv, mod_row_fn, x_ref, mod_ref, q_ref, sgu_ref, *refs):
    kv_refs = refs[:2 * n_kv]
    (w_o_ref, g_post_ref, g_fpre_ref, g_fpost_ref, w_ff1_ref, w_ff2_ref, o_ref) = refs[2 * n_kv:]
    tq = x_ref.shape[0]
    low_half = lax.broadcasted_iota(jnp.int32, (SUB_ROWS, LANES), 1) < V_HEAD
    _, _, gate_a, shift_f, scale_f, gate_f = _mod_rows(mod_ref, mod_row_fn(pl.program_id(0)))
    def attention(r0):
        views = []
        for t in range(n_kv):
            k_ref, v_ref = kv_refs[2 * t], kv_refs[2 * t + 1]
            if shared_kv:
                views.append((lambda c, w, k_ref=k_ref: k_ref[c:c + w, :],
                              lambda c, w, v_ref=v_ref: v_ref[:, c:c + w]))
            else:
                views.append((lambda c, w, k_ref=k_ref: k_ref[c:c + w, r0:r0 + SUB_ROWS],
                              lambda c, w, v_ref=v_ref: v_ref[r0:r0 + SUB_ROWS, c:c + w]))
        return _attend(q_ref, views, r0, low_half)

    def mixer_out(r0, attn):
        mix_in = jnp.concatenate([attn, sgu_ref[r0:r0 + SUB_ROWS, :]], axis=1)
        mix = _dot(mix_in, w_o_ref[...])
        x1 = x_ref[r0:r0 + SUB_ROWS, :] + gate_a * _rms(mix, g_post_ref[...])
        h = (_rms(x1, g_fpre_ref[...]) * (1.0 + scale_f) + shift_f).astype(BF16)
        return x1, h

    def ffn(h):
        f = None
        for c in range(D_FF // FF_CHUNK):
            hid = jnp.maximum(_dot(h, w_ff1_ref[:, c * FF_CHUNK:(c + 1) * FF_CHUNK]), 0.0)
            part = _dot((hid * hid).astype(BF16), w_ff2_ref[c * FF_CHUNK:(c + 1) * FF_CHUNK, :])
            f = part if f is None else f + part
        return f

    def finish(r0, x1, f):
        o_ref[r0:r0 + SUB_ROWS, :] = x1 + gate_f * _rms(f, g_fpost_ref[...])

    starts = [a * SUB_ROWS for a in range(tq // SUB_ROWS)]
    x1h = {r0: mixer_out(r0, attention(r0)) for r0 in starts}
    for r0 in starts:
        finish(r0, x1h[r0][0], ffn(x1h[r0][1]))


def _phase2(x2d, mod3, mod_row_fn, q, sgu, kv_list, kv_rows, shared_kv, tq, n_outer, n_inner, wts,
            name):
    n_kv = len(kv_list) // 2
    in_specs = [
        pl.BlockSpec((tq, D_MODEL), lambda b, i: (b * n_inner + i, 0)),
        _const_spec((MOD_ROWS, N_MOD * D_MODEL)),
        pl.BlockSpec((tq, QK_WIDTH), lambda b, i: (b * n_inner + i, 0)),
        pl.BlockSpec((tq, SGU_WIDTH), lambda b, i: (b * n_inner + i, 0)),
    ]
    for t in range(n_kv):
        in_specs.append(pl.BlockSpec((QK_WIDTH, kv_rows[t]), lambda b, i: (0, b)))
        in_specs.append(pl.BlockSpec((kv_rows[t], MLA_WIDTH), lambda b, i: (b, 0)))
    in_specs += [
        _const_spec((D_MODEL, D_MODEL)),
        _const_spec((1, D_MODEL)),
        _const_spec((1, D_MODEL)),
        _const_spec((1, D_MODEL)),
        _const_spec((D_MODEL, D_FF)),
        _const_spec((D_FF, D_MODEL)),
    ]
    return pl.pallas_call(
        functools.partial(_phase2_kernel, n_kv, shared_kv, mod_row_fn),
        out_shape=jax.ShapeDtypeStruct(x2d.shape, F32),
        grid=(n_outer, n_inner),
        in_specs=in_specs,
        out_specs=pl.BlockSpec((tq, D_MODEL), lambda b, i: (b * n_inner + i, 0)),
        compiler_params=pltpu.CompilerParams(
            dimension_semantics=("arbitrary", "arbitrary"), vmem_limit_bytes=VMEM_LIMIT),
        name=name,
    )(x2d, mod3, q, sgu, *kv_list, wts["w_o"], wts["g_attn_post"], wts["g_ffn_pre"],
      wts["g_ffn_post"], wts["w_ff1"], wts["w_ff2"])


def _pair_swap(w):
    shp = w.shape
    return w.reshape(shp[:-1] + (shp[-1] // 2, 2))[..., ::-1].reshape(shp)


def _prepare_weights(g_attn_pre, g_attn_post, g_q, w_uq, g_kv, w_ukv, w_sgu, b_sgu, g_sgu,
                     beta_sgu, w_o, g_ffn_pre, g_ffn_post, w_ff1, w_ff2):
    w_uq_h = w_uq.reshape(Q_LORA, MLA_HEADS, QK_NOPE + QK_ROPE)
    w_uq_ext = jnp.concatenate(
        [w_uq_h, _pair_swap(w_uq_h[..., QK_NOPE:])], axis=-1).reshape(Q_LORA, QK_WIDTH).astype(BF16)

    w_ukv_h = w_ukv.reshape(KV_LORA, MLA_HEADS, QK_NOPE + V_HEAD)
    w_uk_slots = jnp.concatenate(
        [w_ukv_h[..., :QK_NOPE], jnp.zeros((KV_LORA, MLA_HEADS, HEAD_SLOT - QK_NOPE), F32)],
        axis=-1).reshape(KV_LORA, QK_WIDTH)
    eye = jnp.eye(QK_ROPE, dtype=F32)
    zeros_rope = jnp.zeros((QK_ROPE, QK_ROPE), F32)
    zeros_nope = jnp.zeros((QK_ROPE, QK_NOPE), F32)
    to_lo = jnp.tile(jnp.concatenate([zeros_nope, eye, zeros_rope], axis=1), (1, MLA_HEADS))
    to_hi = jnp.tile(jnp.concatenate([zeros_nope, zeros_rope, eye], axis=1), (1, MLA_HEADS))
    w_k = jnp.concatenate([w_uk_slots, to_lo, to_lo, to_hi, to_hi], axis=0).T.astype(BF16)
    w_uv = w_ukv_h[..., QK_NOPE:].reshape(KV_LORA, MLA_WIDTH).astype(BF16)

    w_sgu_pair = w_sgu.reshape(SGU_HEADS // 2, 2 * CHUNK, CHUNK).astype(BF16)
    bias_sgu = jnp.repeat(b_sgu.T, SGU_HEAD_DIM, axis=1)
    row = lambda a: a.reshape(1, -1)
    return {
        "g_attn_pre": row(g_attn_pre), "g_attn_post": row(g_attn_post),
        "g_q": row(g_q), "w_uq": w_uq_ext, "g_kv": row(g_kv), "w_k": w_k, "w_uv": w_uv,
        "g_sgu": row(g_sgu), "beta_sgu": row(beta_sgu), "w_sgu": w_sgu_pair, "bias_sgu": bias_sgu,
        "w_o32": w_o, "g_ffn_pre": row(g_ffn_pre), "g_ffn_post": row(g_ffn_post),
        "w_ff1_32": w_ff1, "w_ff2_32": w_ff2,
    }


def _rope_tables(n_tok):
    rows = n_tok // GRID_W
    row = np.repeat(np.arange(rows), GRID_W).astype(np.float32)
    col = np.tile(np.arange(GRID_W), rows).astype(np.float32)
    freqs = (1.0 / (ROPE_BASE ** (np.arange(AXIS_PAIRS, dtype=np.float32) / AXIS_PAIRS))).astype(
        np.float32)
    ang = np.concatenate([row[:, None] * freqs, col[:, None] * freqs], axis=-1)
    cos = np.repeat(np.cos(ang), 2, axis=1)
    sin = np.repeat(np.sin(ang), 2, axis=1) * np.tile(np.array([-1.0, 1.0], np.float32), QK_ROPE // 2)
    ktab = np.concatenate([cos, sin, cos, sin], axis=1)
    qtab = ATTN_SCALE * np.concatenate([np.ones((n_tok, QK_NOPE), np.float32), cos, sin], axis=1)
    return jnp.asarray(ktab, F32), jnp.asarray(qtab, F32)


def kernel(x_prompt, x_sample, cache_ckv, cache_krope, c, c_ctx, w_mod, b_mod, g_attn_pre,
           g_attn_post, w_in, g_q, w_uq, g_kv, w_ukv, w_sgu, b_sgu, g_sgu, beta_sgu, w_o,
           g_ffn_pre, g_ffn_post, w_ff1, w_ff2):
    batch, seq, _ = x_prompt.shape
    dec_batch, dec_seq, _ = x_sample.shape
    past_len = cache_ckv.shape[2]
    depth = w_mod.shape[0]
    assert depth == 1

    wts = _prepare_weights(g_attn_pre[0], g_attn_post[0], g_q[0], w_uq[0], g_kv[0],
                           w_ukv[0], w_sgu[0], b_sgu[0], g_sgu[0], beta_sgu[0], w_o[0],
                           g_ffn_pre[0], g_ffn_post[0], w_ff1[0], w_ff2[0])

    cond8 = jnp.concatenate(
        [c_ctx[None, :], c, jnp.zeros((MOD_ROWS - 1 - dec_batch, D_MODEL), F32)], axis=0)
    mod3, w_attn, w_gate = _modulation(cond8, w_mod[0], b_mod[0].reshape(1, -1), w_in[0].T)
    wts = dict(wts, w_attn=w_attn, w_gate=w_gate)

    ktab_ctx = jnp.asarray(np.arange(LANES)[None, :] < QK_ROPE, F32)
    qtab_ctx = jnp.asarray(ATTN_SCALE * (np.arange(LANES)[None, :] < QK_NOPE + QK_ROPE), F32)
    xp2d = x_prompt.reshape(batch * seq, D_MODEL)
    q_c, k_c, v_c, sgu_c, ckv_c, kr_c, w_o16, w_ff1_16, w_ff2_16 = _phase1(
        xp2d, mod3, lambda i: 0, ktab_ctx, qtab_ctx, lambda i: 0, wts, True)
    wts = dict(wts, w_o=w_o16, w_ff1=w_ff1_16, w_ff2=w_ff2_16)
    assert seq == SUB_ROWS and dec_seq % PHASE2_ROWS_LAT == 0
    y_prompt = _phase2(xp2d, mod3, lambda b: 0, q_c, sgu_c, [k_c, v_c], [PHASE2_ROWS_CTX], False,
                       PHASE2_ROWS_CTX, batch * seq // PHASE2_ROWS_CTX, 1, wts, "phase2_ctx")

    ktab_lat, qtab_lat = _rope_tables(dec_seq)
    tiles_per_seq = dec_seq // TOKEN_TILE
    xs2d = x_sample.reshape(dec_batch * dec_seq, D_MODEL)
    q_l, k_l, v_l, sgu_l = _phase1(
        xs2d, mod3, lambda i: 1 + i // tiles_per_seq, ktab_lat, qtab_lat,
        lambda i: i % tiles_per_seq, wts, False)
    kr_p = cache_krope[:, 0].reshape(dec_batch * past_len, QK_ROPE)
    pad = jnp.zeros_like(kr_p)
    kin_p = jnp.concatenate(
        [cache_ckv[:, 0].reshape(dec_batch * past_len, KV_LORA), kr_p, pad, kr_p, pad], axis=1)
    k_p, v_p = _cache_kv(kin_p, wts)
    y_sample = _phase2(xs2d, mod3, lambda b: 1 + b, q_l, sgu_l, [k_p, v_p, k_l, v_l],
                       [past_len, dec_seq], True, PHASE2_ROWS_LAT, dec_batch,
                       dec_seq // PHASE2_ROWS_LAT, wts, "phase2_lat")

    return (y_prompt.reshape(batch, seq, D_MODEL),
            y_sample.reshape(dec_batch, dec_seq, D_MODEL),
            ckv_c.reshape(batch, 1, seq, KV_LORA),
            kr_c.reshape(batch, 1, seq, QK_ROPE))
```

```python
import functools
import math

import jax
import jax.numpy as jnp
import numpy as np
from jax import lax
from jax.experimental import pallas as pl
from jax.experimental.pallas import tpu as pltpu

D_MODEL = 1024
GRID_W = 64
MLA_HEADS = 8
QK_NOPE = 64
QK_ROPE = 32
V_HEAD = 64
Q_LORA = 256
KV_LORA = 128
MLA_WIDTH = MLA_HEADS * V_HEAD
SGU_HEADS = 8
SGU_WIDTH = D_MODEL - MLA_WIDTH
SGU_HEAD_DIM = SGU_WIDTH // SGU_HEADS
CHUNK = 128
D_FF = 4 * D_MODEL
AXIS_PAIRS = QK_ROPE // 4
ROPE_BASE = 10000.0
EPS = 1e-6
N_MOD = 6
ATTN_SCALE = (QK_NOPE + QK_ROPE) ** -0.5

LANES = 128
HEAD_SLOT = LANES
QK_WIDTH = MLA_HEADS * HEAD_SLOT
ATTN_PROJ = Q_LORA + KV_LORA + LANES
MOD_ROWS = 8
MOD_K_ROWS = 128
TOKEN_TILE = 512
SUB_ROWS = 256
PHASE2_ROWS_CTX = 512
PHASE2_ROWS_LAT = 512
QK_AHEAD = 1
FF_CHUNK = 1024
VMEM_LIMIT = 56 * 1024 * 1024

BF16 = jnp.bfloat16
F32 = jnp.float32


def _dot(a, b):
    return jnp.dot(a, b, preferred_element_type=F32)


def _dot_nt(a, b):
    return lax.dot_general(a, b, (((1,), (1,)), ((), ())), preferred_element_type=F32)


def _rms(x, g):
    return x * lax.rsqrt(jnp.mean(x * x, axis=-1, keepdims=True) + EPS) * g


def _gelu(x):
    inner = math.sqrt(2.0 / math.pi) * (x + 0.044715 * (x * x * x))
    return x * (0.5 * (1.0 + jnp.tanh(inner)))


def _mod_kernel(cond_a_ref, cond_b_ref, w_a_ref, w_b_ref, b_ref, w_in_attn_ref, w_in_gate_ref,
                o_ref, w_attn_ref, w_gate_ref):
    @pl.when(pl.program_id(0) == 0)
    def _():
        o_ref[...] = jnp.broadcast_to(b_ref[...], o_ref.shape)

    def part(cond_ref, w_ref):
        cnd = cond_ref[...]
        act = cnd * (1.0 / (1.0 + jnp.exp(-cnd)))
        return _dot(act.astype(BF16), w_ref[...].astype(BF16))

    o_ref[...] += part(cond_a_ref, w_a_ref) + part(cond_b_ref, w_b_ref)

    w_gate_ref[...] = w_in_gate_ref[...].astype(BF16)

    @pl.when(pl.program_id(0) == 0)
    def _():
        rope0 = Q_LORA + KV_LORA
        kr = w_in_attn_ref[rope0:rope0 + QK_ROPE, :]
        row = lax.broadcasted_iota(jnp.int32, kr.shape, 0)
        kr_sw = jnp.where(row % 2 == 0, pltpu.roll(kr, QK_ROPE - 1, 0), pltpu.roll(kr, 1, 0))
        w_attn_ref[0:rope0 + QK_ROPE, :] = w_in_attn_ref[...].astype(BF16)
        w_attn_ref[rope0 + QK_ROPE:ATTN_PROJ, :] = jnp.concatenate(
            [kr_sw, kr, kr_sw], axis=0).astype(BF16)


def _modulation(cond8, w_mod, b_mod, w_in_t):
    n = w_mod.shape[1]
    tk = MOD_K_ROWS
    nk = D_MODEL // tk // 2
    attn_rows = Q_LORA + KV_LORA + QK_ROPE
    gate_rows = 2 * SGU_WIDTH // nk
    return pl.pallas_call(
        _mod_kernel,
        out_shape=[jax.ShapeDtypeStruct((MOD_ROWS, n), F32),
                   jax.ShapeDtypeStruct((ATTN_PROJ, D_MODEL), BF16),
                   jax.ShapeDtypeStruct((2 * SGU_WIDTH, D_MODEL), BF16)],
        grid=(nk,),
        in_specs=[
            pl.BlockSpec((MOD_ROWS, tk), lambda k: (0, k)),
            pl.BlockSpec((MOD_ROWS, tk), lambda k: (0, k + nk)),
            pl.BlockSpec((tk, n), lambda k: (k, 0)),
            pl.BlockSpec((tk, n), lambda k: (k + nk, 0)),
            pl.BlockSpec((1, n), lambda k: (0, 0)),
            pl.BlockSpec((attn_rows, D_MODEL), lambda k: (0, 0)),
            pl.BlockSpec((pl.Element(gate_rows), pl.Element(D_MODEL)),
                         lambda k: (pl.multiple_of(attn_rows + k * gate_rows, QK_ROPE), 0)),
        ],
        out_specs=[pl.BlockSpec((MOD_ROWS, n), lambda k: (0, 0)),
                   pl.BlockSpec((ATTN_PROJ, D_MODEL), lambda k: (0, 0)),
                   pl.BlockSpec((gate_rows, D_MODEL), lambda k: (k, 0))],
        compiler_params=pltpu.CompilerParams(
            dimension_semantics=("arbitrary",), vmem_limit_bytes=VMEM_LIMIT),
        name="modulation",
    )(cond8, cond8, w_mod, w_mod, b_mod, w_in_t, w_in_t)


def _mod_rows(mod_ref, row):
    vec = mod_ref[pl.ds(row, 1), :]
    return [vec[:, j * D_MODEL:(j + 1) * D_MODEL] for j in range(N_MOD)]


def _phase1_kernel(emit_cache, mod_row_fn, x_ref, mod_ref, ktab_ref, qtab_ref, g_pre_ref, w_attn_ref,
                   w_gate_ref,
                   g_q_ref, w_uq_ref, g_kv_ref, w_k_ref, w_uv_ref, g_sgu_ref, beta_sgu_ref,
                   w_sgu_ref, bias_ref, *rest):
    if emit_cache:
        (w_o32_ref, w_ff1_32_ref, w_ff2_32_ref, q_ref, k_ref, v_ref, sgu_ref, ckv_ref, kr_ref,
         w_o16_ref, w_ff1_16_ref, w_ff2_16_ref) = rest
        w_o16_ref[...] = w_o32_ref[...].astype(BF16)
        w_ff1_16_ref[...] = w_ff1_32_ref[...].astype(BF16)
        w_ff2_16_ref[...] = w_ff2_32_ref[...].astype(BF16)
    else:
        q_ref, k_ref, v_ref, sgu_ref = rest
    tm = x_ref.shape[0]
    x = x_ref[...]
    mod = _mod_rows(mod_ref, mod_row_fn(pl.program_id(0)))
    shift_a, scale_a = mod[0], mod[1]
    h = (_rms(x, g_pre_ref[...]) * (1.0 + scale_a) + shift_a).astype(BF16)
    proj = _dot_nt(h, w_attn_ref[...])
    gate = _dot_nt(h, w_gate_ref[...])

    cq = _rms(proj[:, 0:Q_LORA], g_q_ref[...])
    q = _dot(cq.astype(BF16), w_uq_ref[...])
    qtab = qtab_ref[...]
    q_ref[...] = jnp.concatenate(
        [(q[:, s * HEAD_SLOT:(s + 1) * HEAD_SLOT] * qtab).astype(BF16) for s in range(MLA_HEADS)],
        axis=1)

    ckv_n = _rms(proj[:, Q_LORA:Q_LORA + KV_LORA], g_kv_ref[...])
    rope_slab = proj[:, Q_LORA + KV_LORA:Q_LORA + KV_LORA + LANES]
    if emit_cache:
        ckv_ref[...] = ckv_n
        kr_ref[...] = rope_slab[:, 0:QK_ROPE]
    ckv_b = ckv_n.astype(BF16)
    kin = jnp.concatenate([ckv_b, (rope_slab * ktab_ref[...]).astype(BF16)], axis=1)
    k_ref[...] = _dot_nt(w_k_ref[...], kin).astype(BF16)
    v_ref[...] = _dot(ckv_b, w_uv_ref[...]).astype(BF16)

    u = _gelu(gate[:, 0:SGU_WIDTH])
    vv = _gelu(gate[:, SGU_WIDTH:2 * SGU_WIDTH])
    mu = jnp.mean(vv, axis=-1, keepdims=True)
    vc = vv - mu
    var = jnp.mean(vc * vc, axis=-1, keepdims=True)
    vn = (vc * lax.rsqrt(var + EPS) * g_sgu_ref[...] + beta_sgu_ref[...]).astype(BF16)
    n_chunks = tm // CHUNK
    low_half = lax.broadcasted_iota(jnp.int32, (CHUNK, LANES), 1) < SGU_HEAD_DIM
    bias = bias_ref[...]
    mixed_rows = [[] for _ in range(n_chunks)]
    for j in range(SGU_WIDTH // LANES):
        rhs = jnp.concatenate(
            [vn[n * CHUNK:(n + 1) * CHUNK, j * LANES:(j + 1) * LANES] for n in range(n_chunks)],
            axis=1)
        o = _dot(w_sgu_ref[j], rhs)
        for n in range(n_chunks):
            even = o[0:CHUNK, n * LANES:(n + 1) * LANES]
            odd = o[CHUNK:2 * CHUNK, n * LANES:(n + 1) * LANES]
            mixed_rows[n].append(jnp.where(low_half, even, odd))
    for n in range(n_chunks):
        mixed = jnp.concatenate(mixed_rows[n], axis=1) + bias
        sgu_ref[n * CHUNK:(n + 1) * CHUNK, :] = (
            u[n * CHUNK:(n + 1) * CHUNK, :] * mixed).astype(BF16)


def _const_spec(shape):
    nd = len(shape)
    return pl.BlockSpec(shape, lambda *_: (0,) * nd, pipeline_mode=pl.Buffered(1))


def _phase1(x2d, mod3, mod_row_fn, ktab, qtab, tab_fn, wts, emit_cache):
    n_tok = x2d.shape[0]
    tm = TOKEN_TILE
    grid = (n_tok // tm,)
    tab_block = (ktab.shape[0] if ktab.shape[0] == 1 else tm, LANES)
    in_specs = [
        pl.BlockSpec((tm, D_MODEL), lambda i: (i, 0)),
        _const_spec((MOD_ROWS, N_MOD * D_MODEL)),
        pl.BlockSpec(tab_block, lambda i: (tab_fn(i), 0)),
        pl.BlockSpec(tab_block, lambda i: (tab_fn(i), 0)),
        _const_spec((1, D_MODEL)),
        _const_spec((ATTN_PROJ, D_MODEL)),
        _const_spec((2 * SGU_WIDTH, D_MODEL)),
        _const_spec((1, Q_LORA)),
        _const_spec((Q_LORA, QK_WIDTH)),
        _const_spec((1, KV_LORA)),
        _const_spec((QK_WIDTH, 2 * LANES)),
        _const_spec((KV_LORA, MLA_WIDTH)),
        _const_spec((1, SGU_WIDTH)),
        _const_spec((1, SGU_WIDTH)),
        _const_spec((SGU_WIDTH // LANES, 2 * CHUNK, CHUNK)),
        _const_spec((CHUNK, SGU_WIDTH)),
    ]
    out_shape = [
        jax.ShapeDtypeStruct((n_tok, QK_WIDTH), BF16),
        jax.ShapeDtypeStruct((QK_WIDTH, n_tok), BF16),
        jax.ShapeDtypeStruct((n_tok, MLA_WIDTH), BF16),
        jax.ShapeDtypeStruct((n_tok, SGU_WIDTH), BF16),
    ]
    out_specs = [
        pl.BlockSpec((tm, QK_WIDTH), lambda i: (i, 0)),
        pl.BlockSpec((QK_WIDTH, tm), lambda i: (0, i)),
        pl.BlockSpec((tm, MLA_WIDTH), lambda i: (i, 0)),
        pl.BlockSpec((tm, SGU_WIDTH), lambda i: (i, 0)),
    ]
    extra_inputs = []
    if emit_cache:
        out_shape += [jax.ShapeDtypeStruct((n_tok, KV_LORA), F32),
                      jax.ShapeDtypeStruct((n_tok, QK_ROPE), F32)]
        out_specs += [pl.BlockSpec((tm, KV_LORA), lambda i: (i, 0)),
                      pl.BlockSpec((tm, QK_ROPE), lambda i: (i, 0))]
        for w in (wts["w_o32"], wts["w_ff1_32"], wts["w_ff2_32"]):
            rows, cols = w.shape
            blk = (rows // grid[0], cols)
            extra_inputs.append(w)
            in_specs.append(pl.BlockSpec(blk, lambda i: (i, 0)))
            out_shape.append(jax.ShapeDtypeStruct(w.shape, BF16))
            out_specs.append(pl.BlockSpec(blk, lambda i: (i, 0)))
    return pl.pallas_call(
        functools.partial(_phase1_kernel, emit_cache, mod_row_fn),
        out_shape=out_shape,
        grid=grid,
        in_specs=in_specs,
        out_specs=out_specs,
        compiler_params=pltpu.CompilerParams(
            dimension_semantics=("arbitrary",), vmem_limit_bytes=VMEM_LIMIT),
        name="phase1_ctx" if emit_cache else "phase1_lat",
    )(x2d, mod3, ktab, qtab, wts["g_attn_pre"], wts["w_attn"], wts["w_gate"], wts["g_q"], wts["w_uq"],
      wts["g_kv"], wts["w_k"], wts["w_uv"], wts["g_sgu"], wts["beta_sgu"], wts["w_sgu"],
      wts["bias_sgu"], *extra_inputs)


def _cache_kv_kernel(kin_ref, w_k_ref, w_uv_ref, k_ref, v_ref):
    kin = kin_ref[...].astype(BF16)
    k_ref[...] = _dot_nt(w_k_ref[...], kin).astype(BF16)
    v_ref[...] = _dot(kin[:, 0:KV_LORA], w_uv_ref[...]).astype(BF16)


def _cache_kv(kin2d, wts):
    n_tok = kin2d.shape[0]
    tm = n_tok
    return pl.pallas_call(
        _cache_kv_kernel,
        out_shape=[jax.ShapeDtypeStruct((QK_WIDTH, n_tok), BF16),
                   jax.ShapeDtypeStruct((n_tok, MLA_WIDTH), BF16)],
        grid=(n_tok // tm,),
        in_specs=[
            pl.BlockSpec((tm, 2 * LANES), lambda i: (i, 0)),
            _const_spec((QK_WIDTH, 2 * LANES)),
            _const_spec((KV_LORA, MLA_WIDTH)),
        ],
        out_specs=[pl.BlockSpec((QK_WIDTH, tm), lambda i: (0, i)),
                   pl.BlockSpec((tm, MLA_WIDTH), lambda i: (i, 0))],
        compiler_params=pltpu.CompilerParams(
            dimension_semantics=("arbitrary",), vmem_limit_bytes=VMEM_LIMIT),
        name="cache_kv",
    )(kin2d, wts["w_k"], wts["w_uv"])


def _interleave(*gens):
    results = [None] * len(gens)
    live = list(range(len(gens)))
    while live:
        for i in list(live):
            try:
                next(gens[i])
            except StopIteration as stop:
                results[i] = stop.value
                live.remove(i)
    return results


def _run(gen):
    return _interleave(gen)[0]


def _attend(q_ref, kv_views, r0, low_half):
    def head_scores(hd):
        qh = q_ref[r0:r0 + SUB_ROWS, hd * HEAD_SLOT:(hd + 1) * HEAD_SLOT]
        return [_dot(qh, k_view(hd * HEAD_SLOT, HEAD_SLOT)) for k_view, _ in kv_views]

    def head_probs(scores):
        m = scores[0].max(axis=-1, keepdims=True)
        for s in scores[1:]:
            m = jnp.maximum(m, s.max(axis=-1, keepdims=True))
        probs = []
        denom = None
        for s in scores:
            p = jnp.exp(s - m)
            ps = p.sum(axis=-1, keepdims=True)
            denom = ps if denom is None else denom + ps
            probs.append(p.astype(BF16))
        return probs, denom

    def head_values(hd, probs, denom):
        slab = hd // 2
        acc = None
        for (_, v_view), p in zip(kv_views, probs):
            pv = _dot(p, v_view(slab * LANES, LANES))
            acc = pv if acc is None else acc + pv
        return acc / denom

    pair_out = []
    head_out = None
    ahead = [head_scores(hd) for hd in range(QK_AHEAD)]
    for hd in range(MLA_HEADS):
        if hd + QK_AHEAD < MLA_HEADS:
            ahead.append(head_scores(hd + QK_AHEAD))
        o = head_values(hd, *head_probs(ahead.pop(0)))
        if hd % 2 == 0:
            head_out = o
        else:
            pair_out.append(jnp.where(low_half, head_out, o).astype(BF16))
            yield
    return jnp.concatenate(pair_out, axis=1)


def _phase2_kernel(n_kv, shared_kv, mod_row_fn, x_ref, mod_ref, q_ref, sgu_ref, *refs):
    kv_refs = refs[:2 * n_kv]
    (w_o_ref, g_post_ref, g_fpre_ref, g_fpost_ref, w_ff1_ref, w_ff2_ref, o_ref) = refs[2 * n_kv:]
    tq = x_ref.shape[0]
    low_half = lax.broadcasted_iota(jnp.int32, (SUB_ROWS, LANES), 1) < V_HEAD
    _, _, gate_a, shift_f, scale_f, gate_f = _mod_rows(mod_ref, mod_row_fn(pl.program_id(0)))
    def attention(r0):
        views = []
        for t in range(n_kv):
            k_ref, v_ref = kv_refs[2 * t], kv_refs[2 * t + 1]
            if shared_kv:
                views.append((lambda c, w, k_ref=k_ref: k_ref[c:c + w, :],
                              lambda c, w, v_ref=v_ref: v_ref[:, c:c + w]))
            else:
                views.append((lambda c, w, k_ref=k_ref: k_ref[c:c + w, r0:r0 + SUB_ROWS],
                              lambda c, w, v_ref=v_ref: v_ref[r0:r0 + SUB_ROWS, c:c + w]))
        return _attend(q_ref, views, r0, low_half)

    def mixer_out(r0, attn):
        mix_in = jnp.concatenate([attn, sgu_ref[r0:r0 + SUB_ROWS, :]], axis=1)
        mix = _dot(mix_in, w_o_ref[...])
        x1 = x_ref[r0:r0 + SUB_ROWS, :] + gate_a * _rms(mix, g_post_ref[...])
        h = (_rms(x1, g_fpre_ref[...]) * (1.0 + scale_f) + shift_f).astype(BF16)
        return x1, h

    def ffn(h):
        f = None
        for c in range(D_FF // FF_CHUNK):
            hid = jnp.maximum(_dot(h, w_ff1_ref[:, c * FF_CHUNK:(c + 1) * FF_CHUNK]), 0.0)
            part = _dot((hid * hid).astype(BF16), w_ff2_ref[c * FF_CHUNK:(c + 1) * FF_CHUNK, :])
            f = part if f is None else f + part
            yield
        return f

    def finish(r0, x1, f):
        o_ref[r0:r0 + SUB_ROWS, :] = x1 + gate_f * _rms(f, g_fpost_ref[...])

    starts = [a * SUB_ROWS for a in range(tq // SUB_ROWS)]
    x1h = {r0: mixer_out(r0, _run(attention(r0))) for r0 in starts}
    for r0 in starts:
        finish(r0, x1h[r0][0], _run(ffn(x1h[r0][1])))


def _phase2(x2d, mod3, mod_row_fn, q, sgu, kv_list, kv_rows, shared_kv, tq, n_outer, n_inner, wts,
            name):
    n_kv = len(kv_list) // 2
    in_specs = [
        pl.BlockSpec((tq, D_MODEL), lambda b, i: (b * n_inner + i, 0)),
        _const_spec((MOD_ROWS, N_MOD * D_MODEL)),
        pl.BlockSpec((tq, QK_WIDTH), lambda b, i: (b * n_inner + i, 0)),
        pl.BlockSpec((tq, SGU_WIDTH), lambda b, i: (b * n_inner + i, 0)),
    ]
    for t in range(n_kv):
        in_specs.append(pl.BlockSpec((QK_WIDTH, kv_rows[t]), lambda b, i: (0, b)))
        in_specs.append(pl.BlockSpec((kv_rows[t], MLA_WIDTH), lambda b, i: (b, 0)))
    in_specs += [
        _const_spec((D_MODEL, D_MODEL)),
        _const_spec((1, D_MODEL)),
        _const_spec((1, D_MODEL)),
        _const_spec((1, D_MODEL)),
        _const_spec((D_MODEL, D_FF)),
        _const_spec((D_FF, D_MODEL)),
    ]
    return pl.pallas_call(
        functools.partial(_phase2_kernel, n_kv, shared_kv, mod_row_fn),
        out_shape=jax.ShapeDtypeStruct(x2d.shape, F32),
        grid=(n_outer, n_inner),
        in_specs=in_specs,
        out_specs=pl.BlockSpec((tq, D_MODEL), lambda b, i: (b * n_inner + i, 0)),
        compiler_params=pltpu.CompilerParams(
            dimension_semantics=("arbitrary", "arbitrary"), vmem_limit_bytes=VMEM_LIMIT),
        name=name,
    )(x2d, mod3, q, sgu, *kv_list, wts["w_o"], wts["g_attn_post"], wts["g_ffn_pre"],
      wts["g_ffn_post"], wts["w_ff1"], wts["w_ff2"])


def _pair_swap(w):
    shp = w.shape
    return w.reshape(shp[:-1] + (shp[-1] // 2, 2))[..., ::-1].reshape(shp)


def _prepare_weights(g_attn_pre, g_attn_post, g_q, w_uq, g_kv, w_ukv, w_sgu, b_sgu, g_sgu,
                     beta_sgu, w_o, g_ffn_pre, g_ffn_post, w_ff1, w_ff2):
    w_uq_h = w_uq.reshape(Q_LORA, MLA_HEADS, QK_NOPE + QK_ROPE)
    w_uq_ext = jnp.concatenate(
        [w_uq_h, _pair_swap(w_uq_h[..., QK_NOPE:])], axis=-1).reshape(Q_LORA, QK_WIDTH).astype(BF16)

    w_ukv_h = w_ukv.reshape(KV_LORA, MLA_HEADS, QK_NOPE + V_HEAD)
    w_uk_slots = jnp.concatenate(
        [w_ukv_h[..., :QK_NOPE], jnp.zeros((KV_LORA, MLA_HEADS, HEAD_SLOT - QK_NOPE), F32)],
        axis=-1).reshape(KV_LORA, QK_WIDTH)
    eye = jnp.eye(QK_ROPE, dtype=F32)
    zeros_rope = jnp.zeros((QK_ROPE, QK_ROPE), F32)
    zeros_nope = jnp.zeros((QK_ROPE, QK_NOPE), F32)
    to_lo = jnp.tile(jnp.concatenate([zeros_nope, eye, zeros_rope], axis=1), (1, MLA_HEADS))
    to_hi = jnp.tile(jnp.concatenate([zeros_nope, zeros_rope, eye], axis=1), (1, MLA_HEADS))
    w_k = jnp.concatenate([w_uk_slots, to_lo, to_lo, to_hi, to_hi], axis=0).T.astype(BF16)
    w_uv = w_ukv_h[..., QK_NOPE:].reshape(KV_LORA, MLA_WIDTH).astype(BF16)

    w_sgu_pair = w_sgu.reshape(SGU_HEADS // 2, 2 * CHUNK, CHUNK).astype(BF16)
    bias_sgu = jnp.repeat(b_sgu.T, SGU_HEAD_DIM, axis=1)
    row = lambda a: a.reshape(1, -1)
    return {
        "g_attn_pre": row(g_attn_pre), "g_attn_post": row(g_attn_post),
        "g_q": row(g_q), "w_uq": w_uq_ext, "g_kv": row(g_kv), "w_k": w_k, "w_uv": w_uv,
        "g_sgu": row(g_sgu), "beta_sgu": row(beta_sgu), "w_sgu": w_sgu_pair, "bias_sgu": bias_sgu,
        "w_o32": w_o, "g_ffn_pre": row(g_ffn_pre), "g_ffn_post": row(g_ffn_post),
        "w_ff1_32": w_ff1, "w_ff2_32": w_ff2,
    }


def _rope_tables(n_tok):
    rows = n_tok // GRID_W
    row = np.repeat(np.arange(rows), GRID_W).astype(np.float32)
    col = np.tile(np.arange(GRID_W), rows).astype(np.float32)
    freqs = (1.0 / (ROPE_BASE ** (np.arange(AXIS_PAIRS, dtype=np.float32) / AXIS_PAIRS))).astype(
        np.float32)
    ang = np.concatenate([row[:, None] * freqs, col[:, None] * freqs], axis=-1)
    cos = np.repeat(np.cos(ang), 2, axis=1)
    sin = np.repeat(np.sin(ang), 2, axis=1) * np.tile(np.array([-1.0, 1.0], np.float32), QK_ROPE // 2)
    ktab = np.concatenate([cos, sin, cos, sin], axis=1)
    qtab = ATTN_SCALE * np.concatenate([np.ones((n_tok, QK_NOPE), np.float32), cos, sin], axis=1)
    return jnp.asarray(ktab, F32), jnp.asarray(qtab, F32)


def kernel(x_prompt, x_sample, cache_ckv, cache_krope, c, c_ctx, w_mod, b_mod, g_attn_pre,
           g_attn_post, w_in, g_q, w_uq, g_kv, w_ukv, w_sgu, b_sgu, g_sgu, beta_sgu, w_o,
           g_ffn_pre, g_ffn_post, w_ff1, w_ff2):
    batch, seq, _ = x_prompt.shape
    dec_batch, dec_seq, _ = x_sample.shape
    past_len = cache_ckv.shape[2]
    depth = w_mod.shape[0]
    assert depth == 1

    wts = _prepare_weights(g_attn_pre[0], g_attn_post[0], g_q[0], w_uq[0], g_kv[0],
                           w_ukv[0], w_sgu[0], b_sgu[0], g_sgu[0], beta_sgu[0], w_o[0],
                           g_ffn_pre[0], g_ffn_post[0], w_ff1[0], w_ff2[0])

    cond8 = jnp.concatenate(
        [c_ctx[None, :], c, jnp.zeros((MOD_ROWS - 1 - dec_batch, D_MODEL), F32)], axis=0)
    mod3, w_attn, w_gate = _modulation(cond8, w_mod[0], b_mod[0].reshape(1, -1), w_in[0].T)
    wts = dict(wts, w_attn=w_attn, w_gate=w_gate)

    ktab_ctx = jnp.asarray(np.arange(LANES)[None, :] < QK_ROPE, F32)
    qtab_ctx = jnp.asarray(ATTN_SCALE * (np.arange(LANES)[None, :] < QK_NOPE + QK_ROPE), F32)
    xp2d = x_prompt.reshape(batch * seq, D_MODEL)
    q_c, k_c, v_c, sgu_c, ckv_c, kr_c, w_o16, w_ff1_16, w_ff2_16 = _phase1(
        xp2d, mod3, lambda i: 0, ktab_ctx, qtab_ctx, lambda i: 0, wts, True)
    wts = dict(wts, w_o=w_o16, w_ff1=w_ff1_16, w_ff2=w_ff2_16)
    assert seq == SUB_ROWS and dec_seq % PHASE2_ROWS_LAT == 0
    y_prompt = _phase2(xp2d, mod3, lambda b: 0, q_c, sgu_c, [k_c, v_c], [PHASE2_ROWS_CTX], False,
                       PHASE2_ROWS_CTX, batch * seq // PHASE2_ROWS_CTX, 1, wts, "phase2_ctx")

    ktab_lat, qtab_lat = _rope_tables(dec_seq)
    tiles_per_seq = dec_seq // TOKEN_TILE
    xs2d = x_sample.reshape(dec_batch * dec_seq, D_MODEL)
    q_l, k_l, v_l, sgu_l = _phase1(
        xs2d, mod3, lambda i: 1 + i // tiles_per_seq, ktab_lat, qtab_lat,
        lambda i: i % tiles_per_seq, wts, False)
    kr_p = cache_krope[:, 0].reshape(dec_batch * past_len, QK_ROPE)
    pad = jnp.zeros_like(kr_p)
    kin_p = jnp.concatenate(
        [cache_ckv[:, 0].reshape(dec_batch * past_len, KV_LORA), kr_p, pad, kr_p, pad], axis=1)
    k_p, v_p = _cache_kv(kin_p, wts)
    y_sample = _phase2(xs2d, mod3, lambda b: 1 + b, q_l, sgu_l, [k_p, v_p, k_l, v_l],
                       [past_len, dec_seq], True, PHASE2_ROWS_LAT, dec_batch,
                       dec_seq // PHASE2_ROWS_LAT, wts, "phase2_lat")

    return (y_prompt.reshape(batch, seq, D_MODEL),
            y_sample.reshape(dec_batch, dec_seq, D_MODEL),
            ckv_c.reshape(batch, 1, seq, KV_LORA),
            kr_c.reshape(batch, 1, seq, QK_ROPE))
```

```python
import functools
import math

import jax
import jax.numpy as jnp
import numpy as np
from jax import lax
from jax.experimental import pallas as pl
from jax.experimental.pallas import tpu as pltpu

D_MODEL = 1024
GRID_W = 64
MLA_HEADS = 8
QK_NOPE = 64
QK_ROPE = 32
V_HEAD = 64
Q_LORA = 256
KV_LORA = 128
MLA_WIDTH = MLA_HEADS * V_HEAD
SGU_HEADS = 8
SGU_WIDTH = D_MODEL - MLA_WIDTH
SGU_HEAD_DIM = SGU_WIDTH // SGU_HEADS
CHUNK = 128
D_FF = 4 * D_MODEL
AXIS_PAIRS = QK_ROPE // 4
ROPE_BASE = 10000.0
EPS = 1e-6
N_MOD = 6
ATTN_SCALE = (QK_NOPE + QK_ROPE) ** -0.5

LANES = 128
HEAD_SLOT = LANES
QK_WIDTH = MLA_HEADS * HEAD_SLOT
ATTN_PROJ = Q_LORA + KV_LORA + LANES
MOD_ROWS = 8
MOD_K_ROWS = 128
TOKEN_TILE = 1024
SUB_ROWS = 256
PHASE2_ROWS_CTX = 512
PHASE2_ROWS_LAT = 512
QK_AHEAD = 1
FF_CHUNK = 1024
VMEM_LIMIT = 56 * 1024 * 1024

BF16 = jnp.bfloat16
F32 = jnp.float32


def _dot(a, b):
    return jnp.dot(a, b, preferred_element_type=F32)


def _dot_nt(a, b):
    return lax.dot_general(a, b, (((1,), (1,)), ((), ())), preferred_element_type=F32)


def _rms(x, g):
    return x * lax.rsqrt(jnp.mean(x * x, axis=-1, keepdims=True) + EPS) * g


def _gelu(x):
    inner = math.sqrt(2.0 / math.pi) * (x + 0.044715 * (x * x * x))
    return x * (0.5 * (1.0 + jnp.tanh(inner)))


def _mod_kernel(cond_a_ref, cond_b_ref, w_a_ref, w_b_ref, b_ref, w_in_attn_ref, w_in_gate_ref,
                o_ref, w_attn_ref, w_gate_ref):
    @pl.when(pl.program_id(0) == 0)
    def _():
        o_ref[...] = jnp.broadcast_to(b_ref[...], o_ref.shape)

    def part(cond_ref, w_ref):
        cnd = cond_ref[...]
        act = cnd * (1.0 / (1.0 + jnp.exp(-cnd)))
        return _dot(act.astype(BF16), w_ref[...].astype(BF16))

    o_ref[...] += part(cond_a_ref, w_a_ref) + part(cond_b_ref, w_b_ref)

    w_gate_ref[...] = w_in_gate_ref[...].astype(BF16)

    @pl.when(pl.program_id(0) == 0)
    def _():
        rope0 = Q_LORA + KV_LORA
        kr = w_in_attn_ref[rope0:rope0 + QK_ROPE, :]
        row = lax.broadcasted_iota(jnp.int32, kr.shape, 0)
        kr_sw = jnp.where(row % 2 == 0, pltpu.roll(kr, QK_ROPE - 1, 0), pltpu.roll(kr, 1, 0))
        w_attn_ref[0:rope0 + QK_ROPE, :] = w_in_attn_ref[...].astype(BF16)
        w_attn_ref[rope0 + QK_ROPE:ATTN_PROJ, :] = jnp.concatenate(
            [kr_sw, kr, kr_sw], axis=0).astype(BF16)


def _modulation(cond8, w_mod, b_mod, w_in_t):
    n = w_mod.shape[1]
    tk = MOD_K_ROWS
    nk = D_MODEL // tk // 2
    attn_rows = Q_LORA + KV_LORA + QK_ROPE
    gate_rows = 2 * SGU_WIDTH // nk
    return pl.pallas_call(
        _mod_kernel,
        out_shape=[jax.ShapeDtypeStruct((MOD_ROWS, n), F32),
                   jax.ShapeDtypeStruct((ATTN_PROJ, D_MODEL), BF16),
                   jax.ShapeDtypeStruct((2 * SGU_WIDTH, D_MODEL), BF16)],
        grid=(nk,),
        in_specs=[
            pl.BlockSpec((MOD_ROWS, tk), lambda k: (0, k)),
            pl.BlockSpec((MOD_ROWS, tk), lambda k: (0, k + nk)),
            pl.BlockSpec((tk, n), lambda k: (k, 0)),
            pl.BlockSpec((tk, n), lambda k: (k + nk, 0)),
            pl.BlockSpec((1, n), lambda k: (0, 0)),
            pl.BlockSpec((attn_rows, D_MODEL), lambda k: (0, 0)),
            pl.BlockSpec((pl.Element(gate_rows), pl.Element(D_MODEL)),
                         lambda k: (pl.multiple_of(attn_rows + k * gate_rows, QK_ROPE), 0)),
        ],
        out_specs=[pl.BlockSpec((MOD_ROWS, n), lambda k: (0, 0)),
                   pl.BlockSpec((ATTN_PROJ, D_MODEL), lambda k: (0, 0)),
                   pl.BlockSpec((gate_rows, D_MODEL), lambda k: (k, 0))],
        compiler_params=pltpu.CompilerParams(
            dimension_semantics=("arbitrary",), vmem_limit_bytes=VMEM_LIMIT),
        name="modulation",
    )(cond8, cond8, w_mod, w_mod, b_mod, w_in_t, w_in_t)


def _mod_rows(mod_ref, row):
    vec = mod_ref[pl.ds(row, 1), :]
    return [vec[:, j * D_MODEL:(j + 1) * D_MODEL] for j in range(N_MOD)]


def _phase1_kernel(emit_cache, mod_row_fn, x_ref, mod_ref, ktab_ref, qtab_ref, g_pre_ref, w_attn_ref,
                   w_gate_ref,
                   g_q_ref, w_uq_ref, g_kv_ref, w_k_ref, w_uv_ref, g_sgu_ref, beta_sgu_ref,
                   w_sgu_ref, bias_ref, *rest):
    if emit_cache:
        (w_o32_ref, w_ff1_32_ref, w_ff2_32_ref, q_ref, k_ref, v_ref, sgu_ref, ckv_ref, kr_ref,
         w_o16_ref, w_ff1_16_ref, w_ff2_16_ref) = rest
        w_o16_ref[...] = w_o32_ref[...].astype(BF16)
        w_ff1_16_ref[...] = w_ff1_32_ref[...].astype(BF16)
        w_ff2_16_ref[...] = w_ff2_32_ref[...].astype(BF16)
    else:
        q_ref, k_ref, v_ref, sgu_ref = rest
    tm = x_ref.shape[0]
    mod = _mod_rows(mod_ref, mod_row_fn(pl.program_id(0)))
    shift_a, scale_a = mod[0], mod[1]
    low_half = lax.broadcasted_iota(jnp.int32, (CHUNK, LANES), 1) < SGU_HEAD_DIM
    per_pos_tables = ktab_ref.shape[0] > 1

    def project(r0):
        rows = slice(r0, r0 + SUB_ROWS)
        h = (_rms(x_ref[rows, :], g_pre_ref[...]) * (1.0 + scale_a) + shift_a).astype(BF16)
        proj = _dot_nt(h, w_attn_ref[...])
        gate = _dot_nt(h, w_gate_ref[...])
        return proj, gate

    def expand_qkv(r0, proj):
        rows = slice(r0, r0 + SUB_ROWS)
        tab_rows = rows if per_pos_tables else slice(None)
        cq = _rms(proj[:, 0:Q_LORA], g_q_ref[...])
        q = _dot(cq.astype(BF16), w_uq_ref[...])
        qtab = qtab_ref[tab_rows, :]
        q_ref[rows, :] = jnp.concatenate(
            [(q[:, s * HEAD_SLOT:(s + 1) * HEAD_SLOT] * qtab).astype(BF16)
             for s in range(MLA_HEADS)], axis=1)
        ckv_n = _rms(proj[:, Q_LORA:Q_LORA + KV_LORA], g_kv_ref[...])
        rope_slab = proj[:, Q_LORA + KV_LORA:Q_LORA + KV_LORA + LANES]
        if emit_cache:
            ckv_ref[rows, :] = ckv_n
            kr_ref[rows, :] = rope_slab[:, 0:QK_ROPE]
        ckv_b = ckv_n.astype(BF16)
        kin = jnp.concatenate([ckv_b, (rope_slab * ktab_ref[tab_rows, :]).astype(BF16)], axis=1)
        k_ref[:, rows] = _dot_nt(w_k_ref[...], kin).astype(BF16)
        v_ref[rows, :] = _dot(ckv_b, w_uv_ref[...]).astype(BF16)

    def gating_unit(r0, gate):
        vv = _gelu(gate[:, SGU_WIDTH:2 * SGU_WIDTH])
        mu = jnp.mean(vv, axis=-1, keepdims=True)
        vc = vv - mu
        var = jnp.mean(vc * vc, axis=-1, keepdims=True)
        vn = (vc * lax.rsqrt(var + EPS) * g_sgu_ref[...] + beta_sgu_ref[...]).astype(BF16)
        n_chunks = SUB_ROWS // CHUNK
        for j in range(SGU_WIDTH // LANES):
            lanes = slice(j * LANES, (j + 1) * LANES)
            rhs = jnp.concatenate(
                [vn[n * CHUNK:(n + 1) * CHUNK, lanes] for n in range(n_chunks)], axis=1)
            o = _dot(w_sgu_ref[j], rhs)
            u = _gelu(gate[:, lanes])
            bias = bias_ref[:, lanes]
            for n in range(n_chunks):
                even = o[0:CHUNK, n * LANES:(n + 1) * LANES]
                odd = o[CHUNK:2 * CHUNK, n * LANES:(n + 1) * LANES]
                mixed = jnp.where(low_half, even, odd) + bias
                sgu_ref[r0 + n * CHUNK:r0 + (n + 1) * CHUNK, lanes] = (
                    u[n * CHUNK:(n + 1) * CHUNK, :] * mixed).astype(BF16)

    starts = list(range(0, tm, SUB_ROWS))
    ahead = project(starts[0])
    for i, r0 in enumerate(starts):
        proj, gate = ahead
        if i + 1 < len(starts):
            ahead = project(starts[i + 1])
        expand_qkv(r0, proj)
        gating_unit(r0, gate)


def _const_spec(shape):
    nd = len(shape)
    return pl.BlockSpec(shape, lambda *_: (0,) * nd, pipeline_mode=pl.Buffered(1))


def _phase1(x2d, mod3, mod_row_fn, ktab, qtab, tab_fn, wts, emit_cache):
    n_tok = x2d.shape[0]
    tm = TOKEN_TILE
    grid = (n_tok // tm,)
    tab_block = (ktab.shape[0] if ktab.shape[0] == 1 else tm, LANES)
    in_specs = [
        pl.BlockSpec((tm, D_MODEL), lambda i: (i, 0)),
        _const_spec((MOD_ROWS, N_MOD * D_MODEL)),
        pl.BlockSpec(tab_block, lambda i: (tab_fn(i), 0)),
        pl.BlockSpec(tab_block, lambda i: (tab_fn(i), 0)),
        _const_spec((1, D_MODEL)),
        _const_spec((ATTN_PROJ, D_MODEL)),
        _const_spec((2 * SGU_WIDTH, D_MODEL)),
        _const_spec((1, Q_LORA)),
        _const_spec((Q_LORA, QK_WIDTH)),
        _const_spec((1, KV_LORA)),
        _const_spec((QK_WIDTH, 2 * LANES)),
        _const_spec((KV_LORA, MLA_WIDTH)),
        _const_spec((1, SGU_WIDTH)),
        _const_spec((1, SGU_WIDTH)),
        _const_spec((SGU_WIDTH // LANES, 2 * CHUNK, CHUNK)),
        _const_spec((CHUNK, SGU_WIDTH)),
    ]
    out_shape = [
        jax.ShapeDtypeStruct((n_tok, QK_WIDTH), BF16),
        jax.ShapeDtypeStruct((QK_WIDTH, n_tok), BF16),
        jax.ShapeDtypeStruct((n_tok, MLA_WIDTH), BF16),
        jax.ShapeDtypeStruct((n_tok, SGU_WIDTH), BF16),
    ]
    out_specs = [
        pl.BlockSpec((tm, QK_WIDTH), lambda i: (i, 0)),
        pl.BlockSpec((QK_WIDTH, tm), lambda i: (0, i)),
        pl.BlockSpec((tm, MLA_WIDTH), lambda i: (i, 0)),
        pl.BlockSpec((tm, SGU_WIDTH), lambda i: (i, 0)),
    ]
    extra_inputs = []
    if emit_cache:
        out_shape += [jax.ShapeDtypeStruct((n_tok, KV_LORA), F32),
                      jax.ShapeDtypeStruct((n_tok, QK_ROPE), F32)]
        out_specs += [pl.BlockSpec((tm, KV_LORA), lambda i: (i, 0)),
                      pl.BlockSpec((tm, QK_ROPE), lambda i: (i, 0))]
        for w in (wts["w_o32"], wts["w_ff1_32"], wts["w_ff2_32"]):
            rows, cols = w.shape
            blk = (rows // grid[0], cols)
            extra_inputs.append(w)
            in_specs.append(pl.BlockSpec(blk, lambda i: (i, 0)))
            out_shape.append(jax.ShapeDtypeStruct(w.shape, BF16))
            out_specs.append(pl.BlockSpec(blk, lambda i: (i, 0)))
    return pl.pallas_call(
        functools.partial(_phase1_kernel, emit_cache, mod_row_fn),
        out_shape=out_shape,
        grid=grid,
        in_specs=in_specs,
        out_specs=out_specs,
        compiler_params=pltpu.CompilerParams(
            dimension_semantics=("arbitrary",), vmem_limit_bytes=VMEM_LIMIT),
        name="phase1_ctx" if emit_cache else "phase1_lat",
    )(x2d, mod3, ktab, qtab, wts["g_attn_pre"], wts["w_attn"], wts["w_gate"], wts["g_q"], wts["w_uq"],
      wts["g_kv"], wts["w_k"], wts["w_uv"], wts["g_sgu"], wts["beta_sgu"], wts["w_sgu"],
      wts["bias_sgu"], *extra_inputs)


def _cache_kv_kernel(kin_ref, w_k_ref, w_uv_ref, k_ref, v_ref):
    kin = kin_ref[...].astype(BF16)
    k_ref[...] = _dot_nt(w_k_ref[...], kin).astype(BF16)
    v_ref[...] = _dot(kin[:, 0:KV_LORA], w_uv_ref[...]).astype(BF16)


def _cache_kv(kin2d, wts):
    n_tok = kin2d.shape[0]
    tm = n_tok
    return pl.pallas_call(
        _cache_kv_kernel,
        out_shape=[jax.ShapeDtypeStruct((QK_WIDTH, n_tok), BF16),
                   jax.ShapeDtypeStruct((n_tok, MLA_WIDTH), BF16)],
        grid=(n_tok // tm,),
        in_specs=[
            pl.BlockSpec((tm, 2 * LANES), lambda i: (i, 0)),
            _const_spec((QK_WIDTH, 2 * LANES)),
            _const_spec((KV_LORA, MLA_WIDTH)),
        ],
        out_specs=[pl.BlockSpec((QK_WIDTH, tm), lambda i: (0, i)),
                   pl.BlockSpec((tm, MLA_WIDTH), lambda i: (i, 0))],
        compiler_params=pltpu.CompilerParams(
            dimension_semantics=("arbitrary",), vmem_limit_bytes=VMEM_LIMIT),
        name="cache_kv",
    )(kin2d, wts["w_k"], wts["w_uv"])


def _interleave(*gens):
    results = [None] * len(gens)
    live = list(range(len(gens)))
    while live:
        for i in list(live):
            try:
                next(gens[i])
            except StopIteration as stop:
                results[i] = stop.value
                live.remove(i)
    return results


def _run(gen):
    return _interleave(gen)[0]


def _attend(q_ref, kv_views, r0, low_half):
    def head_scores(hd):
        qh = q_ref[r0:r0 + SUB_ROWS, hd * HEAD_SLOT:(hd + 1) * HEAD_SLOT]
        return [_dot(qh, k_view(hd * HEAD_SLOT, HEAD_SLOT)) for k_view, _ in kv_views]

    def head_probs(scores):
        m = scores[0].max(axis=-1, keepdims=True)
        for s in scores[1:]:
            m = jnp.maximum(m, s.max(axis=-1, keepdims=True))
        probs = []
        denom = None
        for s in scores:
            p = jnp.exp(s - m)
            ps = p.sum(axis=-1, keepdims=True)
            denom = ps if denom is None else denom + ps
            probs.append(p.astype(BF16))
        return probs, denom

    def head_values(hd, probs, denom):
        slab = hd // 2
        acc = None
        for (_, v_view), p in zip(kv_views, probs):
            pv = _dot(p, v_view(slab * LANES, LANES))
            acc = pv if acc is None else acc + pv
        return acc / denom

    pair_out = []
    head_out = None
    ahead = [head_scores(hd) for hd in range(QK_AHEAD)]
    for hd in range(MLA_HEADS):
        if hd + QK_AHEAD < MLA_HEADS:
            ahead.append(head_scores(hd + QK_AHEAD))
        o = head_values(hd, *head_probs(ahead.pop(0)))
        if hd % 2 == 0:
            head_out = o
        else:
            pair_out.append(jnp.where(low_half, head_out, o).astype(BF16))
            yield
    return jnp.concatenate(pair_out, axis=1)


def _phase2_kernel(n_kv, shared_kv, mod_row_fn, x_ref, mod_ref, q_ref, sgu_ref, *refs):
    kv_refs = refs[:2 * n_kv]
    (w_o_ref, g_post_ref, g_fpre_ref, g_fpost_ref, w_ff1_ref, w_ff2_ref, o_ref) = refs[2 * n_kv:]
    tq = x_ref.shape[0]
    low_half = lax.broadcasted_iota(jnp.int32, (SUB_ROWS, LANES), 1) < V_HEAD
    _, _, gate_a, shift_f, scale_f, gate_f = _mod_rows(mod_ref, mod_row_fn(pl.program_id(0)))
    def attention(r0):
        views = []
        for t in range(n_kv):
            k_ref, v_ref = kv_refs[2 * t], kv_refs[2 * t + 1]
            if shared_kv:
                views.append((lambda c, w, k_ref=k_ref: k_ref[c:c + w, :],
                              lambda c, w, v_ref=v_ref: v_ref[:, c:c + w]))
            else:
                views.append((lambda c, w, k_ref=k_ref: k_ref[c:c + w, r0:r0 + SUB_ROWS],
                              lambda c, w, v_ref=v_ref: v_ref[r0:r0 + SUB_ROWS, c:c + w]))
        return _attend(q_ref, views, r0, low_half)

    def mixer_proj(r0, attn):
        mix_in = jnp.concatenate([attn, sgu_ref[r0:r0 + SUB_ROWS, :]], axis=1)
        return _dot(mix_in, w_o_ref[...])

    def mixer_norm(r0, mix):
        x1 = x_ref[r0:r0 + SUB_ROWS, :] + gate_a * _rms(mix, g_post_ref[...])
        h = (_rms(x1, g_fpre_ref[...]) * (1.0 + scale_f) + shift_f).astype(BF16)
        return x1, h

    def ffn(h):
        f = None
        for c in range(D_FF // FF_CHUNK):
            hid = jnp.maximum(_dot(h, w_ff1_ref[:, c * FF_CHUNK:(c + 1) * FF_CHUNK]), 0.0)
            part = _dot((hid * hid).astype(BF16), w_ff2_ref[c * FF_CHUNK:(c + 1) * FF_CHUNK, :])
            f = part if f is None else f + part
            yield
        return f

    def finish(r0, x1, f):
        o_ref[r0:r0 + SUB_ROWS, :] = x1 + gate_f * _rms(f, g_fpost_ref[...])

    starts = [a * SUB_ROWS for a in range(tq // SUB_ROWS)]
    x1h = {r0: mixer_norm(r0, mixer_proj(r0, _run(attention(r0)))) for r0 in starts}
    for r0 in starts:
        finish(r0, x1h[r0][0], _run(ffn(x1h[r0][1])))


def _phase2(x2d, mod3, mod_row_fn, q, sgu, kv_list, kv_rows, shared_kv, tq, n_outer, n_inner, wts,
            name):
    n_kv = len(kv_list) // 2
    in_specs = [
        pl.BlockSpec((tq, D_MODEL), lambda b, i: (b * n_inner + i, 0)),
        _const_spec((MOD_ROWS, N_MOD * D_MODEL)),
        pl.BlockSpec((tq, QK_WIDTH), lambda b, i: (b * n_inner + i, 0)),
        pl.BlockSpec((tq, SGU_WIDTH), lambda b, i: (b * n_inner + i, 0)),
    ]
    for t in range(n_kv):
        in_specs.append(pl.BlockSpec((QK_WIDTH, kv_rows[t]), lambda b, i: (0, b)))
        in_specs.append(pl.BlockSpec((kv_rows[t], MLA_WIDTH), lambda b, i: (b, 0)))
    in_specs += [
        _const_spec((D_MODEL, D_MODEL)),
        _const_spec((1, D_MODEL)),
        _const_spec((1, D_MODEL)),
        _const_spec((1, D_MODEL)),
        _const_spec((D_MODEL, D_FF)),
        _const_spec((D_FF, D_MODEL)),
    ]
    return pl.pallas_call(
        functools.partial(_phase2_kernel, n_kv, shared_kv, mod_row_fn),
        out_shape=jax.ShapeDtypeStruct(x2d.shape, F32),
        grid=(n_outer, n_inner),
        in_specs=in_specs,
        out_specs=pl.BlockSpec((tq, D_MODEL), lambda b, i: (b * n_inner + i, 0)),
        compiler_params=pltpu.CompilerParams(
            dimension_semantics=("arbitrary", "arbitrary"), vmem_limit_bytes=VMEM_LIMIT),
        name=name,
    )(x2d, mod3, q, sgu, *kv_list, wts["w_o"], wts["g_attn_post"], wts["g_ffn_pre"],
      wts["g_ffn_post"], wts["w_ff1"], wts["w_ff2"])


def _pair_swap(w):
    shp = w.shape
    return w.reshape(shp[:-1] + (shp[-1] // 2, 2))[..., ::-1].reshape(shp)


def _prepare_weights(g_attn_pre, g_attn_post, g_q, w_uq, g_kv, w_ukv, w_sgu, b_sgu, g_sgu,
                     beta_sgu, w_o, g_ffn_pre, g_ffn_post, w_ff1, w_ff2):
    w_uq_h = w_uq.reshape(Q_LORA, MLA_HEADS, QK_NOPE + QK_ROPE)
    w_uq_ext = jnp.concatenate(
        [w_uq_h, _pair_swap(w_uq_h[..., QK_NOPE:])], axis=-1).reshape(Q_LORA, QK_WIDTH).astype(BF16)

    w_ukv_h = w_ukv.reshape(KV_LORA, MLA_HEADS, QK_NOPE + V_HEAD)
    w_uk_slots = jnp.concatenate(
        [w_ukv_h[..., :QK_NOPE], jnp.zeros((KV_LORA, MLA_HEADS, HEAD_SLOT - QK_NOPE), F32)],
        axis=-1).reshape(KV_LORA, QK_WIDTH)
    eye = jnp.eye(QK_ROPE, dtype=F32)
    zeros_rope = jnp.zeros((QK_ROPE, QK_ROPE), F32)
    zeros_nope = jnp.zeros((QK_ROPE, QK_NOPE), F32)
    to_lo = jnp.tile(jnp.concatenate([zeros_nope, eye, zeros_rope], axis=1), (1, MLA_HEADS))
    to_hi = jnp.tile(jnp.concatenate([zeros_nope, zeros_rope, eye], axis=1), (1, MLA_HEADS))
    w_k = jnp.concatenate([w_uk_slots, to_lo, to_lo, to_hi, to_hi], axis=0).T.astype(BF16)
    w_uv = w_ukv_h[..., QK_NOPE:].reshape(KV_LORA, MLA_WIDTH).astype(BF16)

    w_sgu_pair = w_sgu.reshape(SGU_HEADS // 2, 2 * CHUNK, CHUNK).astype(BF16)
    bias_sgu = jnp.repeat(b_sgu.T, SGU_HEAD_DIM, axis=1)
    row = lambda a: a.reshape(1, -1)
    return {
        "g_attn_pre": row(g_attn_pre), "g_attn_post": row(g_attn_post),
        "g_q": row(g_q), "w_uq": w_uq_ext, "g_kv": row(g_kv), "w_k": w_k, "w_uv": w_uv,
        "g_sgu": row(g_sgu), "beta_sgu": row(beta_sgu), "w_sgu": w_sgu_pair, "bias_sgu": bias_sgu,
        "w_o32": w_o, "g_ffn_pre": row(g_ffn_pre), "g_ffn_post": row(g_ffn_post),
        "w_ff1_32": w_ff1, "w_ff2_32": w_ff2,
    }


def _rope_tables(n_tok):
    rows = n_tok // GRID_W
    row = np.repeat(np.arange(rows), GRID_W).astype(np.float32)
    col = np.tile(np.arange(GRID_W), rows).astype(np.float32)
    freqs = (1.0 / (ROPE_BASE ** (np.arange(AXIS_PAIRS, dtype=np.float32) / AXIS_PAIRS))).astype(
        np.float32)
    ang = np.concatenate([row[:, None] * freqs, col[:, None] * freqs], axis=-1)
    cos = np.repeat(np.cos(ang), 2, axis=1)
    sin = np.repeat(np.sin(ang), 2, axis=1) * np.tile(np.array([-1.0, 1.0], np.float32), QK_ROPE // 2)
    ktab = np.concatenate([cos, sin, cos, sin], axis=1)
    qtab = ATTN_SCALE * np.concatenate([np.ones((n_tok, QK_NOPE), np.float32), cos, sin], axis=1)
    return jnp.asarray(ktab, F32), jnp.asarray(qtab, F32)


def kernel(x_prompt, x_sample, cache_ckv, cache_krope, c, c_ctx, w_mod, b_mod, g_attn_pre,
           g_attn_post, w_in, g_q, w_uq, g_kv, w_ukv, w_sgu, b_sgu, g_sgu, beta_sgu, w_o,
           g_ffn_pre, g_ffn_post, w_ff1, w_ff2):
    batch, seq, _ = x_prompt.shape
    dec_batch, dec_seq, _ = x_sample.shape
    past_len = cache_ckv.shape[2]
    depth = w_mod.shape[0]
    assert depth == 1

    wts = _prepare_weights(g_attn_pre[0], g_attn_post[0], g_q[0], w_uq[0], g_kv[0],
                           w_ukv[0], w_sgu[0], b_sgu[0], g_sgu[0], beta_sgu[0], w_o[0],
                           g_ffn_pre[0], g_ffn_post[0], w_ff1[0], w_ff2[0])

    cond8 = jnp.concatenate(
        [c_ctx[None, :], c, jnp.zeros((MOD_ROWS - 1 - dec_batch, D_MODEL), F32)], axis=0)
    mod3, w_attn, w_gate = _modulation(cond8, w_mod[0], b_mod[0].reshape(1, -1), w_in[0].T)
    wts = dict(wts, w_attn=w_attn, w_gate=w_gate)

    ktab_ctx = jnp.asarray(np.arange(LANES)[None, :] < QK_ROPE, F32)
    qtab_ctx = jnp.asarray(ATTN_SCALE * (np.arange(LANES)[None, :] < QK_NOPE + QK_ROPE), F32)
    xp2d = x_prompt.reshape(batch * seq, D_MODEL)
    q_c, k_c, v_c, sgu_c, ckv_c, kr_c, w_o16, w_ff1_16, w_ff2_16 = _phase1(
        xp2d, mod3, lambda i: 0, ktab_ctx, qtab_ctx, lambda i: 0, wts, True)
    wts = dict(wts, w_o=w_o16, w_ff1=w_ff1_16, w_ff2=w_ff2_16)
    assert seq == SUB_ROWS and dec_seq % PHASE2_ROWS_LAT == 0
    y_prompt = _phase2(xp2d, mod3, lambda b: 0, q_c, sgu_c, [k_c, v_c], [PHASE2_ROWS_CTX], False,
                       PHASE2_ROWS_CTX, batch * seq // PHASE2_ROWS_CTX, 1, wts, "phase2_ctx")

    ktab_lat, qtab_lat = _rope_tables(dec_seq)
    tiles_per_seq = dec_seq // TOKEN_TILE
    xs2d = x_sample.reshape(dec_batch * dec_seq, D_MODEL)
    q_l, k_l, v_l, sgu_l = _phase1(
        xs2d, mod3, lambda i: 1 + i // tiles_per_seq, ktab_lat, qtab_lat,
        lambda i: i % tiles_per_seq, wts, False)
    kr_p = cache_krope[:, 0].reshape(dec_batch * past_len, QK_ROPE)
    pad = jnp.zeros_like(kr_p)
    kin_p = jnp.concatenate(
        [cache_ckv[:, 0].reshape(dec_batch * past_len, KV_LORA), kr_p, pad, kr_p, pad], axis=1)
    k_p, v_p = _cache_kv(kin_p, wts)
    y_sample = _phase2(xs2d, mod3, lambda b: 1 + b, q_l, sgu_l, [k_p, v_p, k_l, v_l],
                       [past_len, dec_seq], True, PHASE2_ROWS_LAT, dec_batch,
                       dec_seq // PHASE2_ROWS_LAT, wts, "phase2_lat")

    return (y_prompt.reshape(batch, seq, D_MODEL),
            y_sample.reshape(dec_batch, dec_seq, D_MODEL),
            ckv_c.reshape(batch, 1, seq, KV_LORA),
            kr_c.reshape(batch, 1, seq, QK_ROPE))
```

```python
import functools
import math

import jax
import jax.numpy as jnp
import numpy as np
from jax import lax
from jax.experimental import pallas as pl
from jax.experimental.pallas import tpu as pltpu

D_MODEL = 1024
GRID_W = 64
MLA_HEADS = 8
QK_NOPE = 64
QK_ROPE = 32
V_HEAD = 64
Q_LORA = 256
KV_LORA = 128
MLA_WIDTH = MLA_HEADS * V_HEAD
SGU_HEADS = 8
SGU_WIDTH = D_MODEL - MLA_WIDTH
SGU_HEAD_DIM = SGU_WIDTH // SGU_HEADS
CHUNK = 128
D_FF = 4 * D_MODEL
AXIS_PAIRS = QK_ROPE // 4
ROPE_BASE = 10000.0
EPS = 1e-6
N_MOD = 6
ATTN_SCALE = (QK_NOPE + QK_ROPE) ** -0.5

LANES = 128
HEAD_SLOT = LANES
QK_WIDTH = MLA_HEADS * HEAD_SLOT
ATTN_PROJ = Q_LORA + KV_LORA + LANES
MOD_ROWS = 8
MOD_K_ROWS = 128
TOKEN_TILE = 512
SUB_ROWS = 256
PHASE2_ROWS_CTX = 512
PHASE2_ROWS_LAT = 512
QK_AHEAD = 1
FF_CHUNK = 1024
VMEM_LIMIT = 56 * 1024 * 1024

BF16 = jnp.bfloat16
F32 = jnp.float32


def _dot(a, b):
    return jnp.dot(a, b, preferred_element_type=F32)


def _dot_nt(a, b):
    return lax.dot_general(a, b, (((1,), (1,)), ((), ())), preferred_element_type=F32)


def _rms(x, g):
    return x * lax.rsqrt(jnp.mean(x * x, axis=-1, keepdims=True) + EPS) * g


def _gelu(x):
    inner = math.sqrt(2.0 / math.pi) * (x + 0.044715 * (x * x * x))
    return x * (0.5 * (1.0 + jnp.tanh(inner)))


def _mod_kernel(cond_a_ref, cond_b_ref, w_a_ref, w_b_ref, b_ref, w_in_attn_ref, w_in_gate_ref,
                o_ref, w_attn_ref, w_gate_ref):
    @pl.when(pl.program_id(0) == 0)
    def _():
        o_ref[...] = jnp.broadcast_to(b_ref[...], o_ref.shape)

    def part(cond_ref, w_ref):
        cnd = cond_ref[...]
        act = cnd * (1.0 / (1.0 + jnp.exp(-cnd)))
        return _dot(act.astype(BF16), w_ref[...].astype(BF16))

    o_ref[...] += part(cond_a_ref, w_a_ref) + part(cond_b_ref, w_b_ref)

    w_gate_ref[...] = w_in_gate_ref[...].astype(BF16)

    @pl.when(pl.program_id(0) == 0)
    def _():
        rope0 = Q_LORA + KV_LORA
        kr = w_in_attn_ref[rope0:rope0 + QK_ROPE, :]
        row = lax.broadcasted_iota(jnp.int32, kr.shape, 0)
        kr_sw = jnp.where(row % 2 == 0, pltpu.roll(kr, QK_ROPE - 1, 0), pltpu.roll(kr, 1, 0))
        w_attn_ref[0:rope0 + QK_ROPE, :] = w_in_attn_ref[...].astype(BF16)
        w_attn_ref[rope0 + QK_ROPE:ATTN_PROJ, :] = jnp.concatenate(
            [kr_sw, kr, kr_sw], axis=0).astype(BF16)


def _modulation(cond8, w_mod, b_mod, w_in_t):
    n = w_mod.shape[1]
    tk = MOD_K_ROWS
    nk = D_MODEL // tk // 2
    attn_rows = Q_LORA + KV_LORA + QK_ROPE
    gate_rows = 2 * SGU_WIDTH // nk
    return pl.pallas_call(
        _mod_kernel,
        out_shape=[jax.ShapeDtypeStruct((MOD_ROWS, n), F32),
                   jax.ShapeDtypeStruct((ATTN_PROJ, D_MODEL), BF16),
                   jax.ShapeDtypeStruct((2 * SGU_WIDTH, D_MODEL), BF16)],
        grid=(nk,),
        in_specs=[
            pl.BlockSpec((MOD_ROWS, tk), lambda k: (0, k)),
            pl.BlockSpec((MOD_ROWS, tk), lambda k: (0, k + nk)),
            pl.BlockSpec((tk, n), lambda k: (k, 0)),
            pl.BlockSpec((tk, n), lambda k: (k + nk, 0)),
            pl.BlockSpec((1, n), lambda k: (0, 0)),
            pl.BlockSpec((attn_rows, D_MODEL), lambda k: (0, 0)),
            pl.BlockSpec((pl.Element(gate_rows), pl.Element(D_MODEL)),
                         lambda k: (pl.multiple_of(attn_rows + k * gate_rows, QK_ROPE), 0)),
        ],
        out_specs=[pl.BlockSpec((MOD_ROWS, n), lambda k: (0, 0)),
                   pl.BlockSpec((ATTN_PROJ, D_MODEL), lambda k: (0, 0)),
                   pl.BlockSpec((gate_rows, D_MODEL), lambda k: (k, 0))],
        compiler_params=pltpu.CompilerParams(
            dimension_semantics=("arbitrary",), vmem_limit_bytes=VMEM_LIMIT),
        name="modulation",
    )(cond8, cond8, w_mod, w_mod, b_mod, w_in_t, w_in_t)


def _mod_rows(mod_ref, row):
    vec = mod_ref[pl.ds(row, 1), :]
    return [vec[:, j * D_MODEL:(j + 1) * D_MODEL] for j in range(N_MOD)]


def _phase1_kernel(emit_cache, mod_row_fn, x_ref, mod_ref, ktab_ref, qtab_ref, g_pre_ref, w_attn_ref,
                   w_gate_ref,
                   g_q_ref, w_uq_ref, g_kv_ref, w_k_ref, w_uv_ref, g_sgu_ref, beta_sgu_ref,
                   w_sgu_ref, bias_ref, *rest):
    if emit_cache:
        (w_o32_ref, w_ff1_32_ref, w_ff2_32_ref, q_ref, k_ref, v_ref, sgu_ref, ckv_ref, kr_ref,
         w_o16_ref, w_ff1_16_ref, w_ff2_16_ref) = rest
        w_o16_ref[...] = w_o32_ref[...].astype(BF16)
        w_ff1_16_ref[...] = w_ff1_32_ref[...].astype(BF16)
        w_ff2_16_ref[...] = w_ff2_32_ref[...].astype(BF16)
    else:
        q_ref, k_ref, v_ref, sgu_ref = rest
    tm = x_ref.shape[0]
    mod = _mod_rows(mod_ref, mod_row_fn(pl.program_id(0)))
    shift_a, scale_a = mod[0], mod[1]
    low_half = lax.broadcasted_iota(jnp.int32, (CHUNK, LANES), 1) < SGU_HEAD_DIM
    per_pos_tables = ktab_ref.shape[0] > 1

    def project(r0):
        rows = slice(r0, r0 + SUB_ROWS)
        h = (_rms(x_ref[rows, :], g_pre_ref[...]) * (1.0 + scale_a) + shift_a).astype(BF16)
        proj = _dot_nt(h, w_attn_ref[...])
        gate = _dot_nt(h, w_gate_ref[...])
        return proj, gate

    def expand_qkv(r0, proj):
        rows = slice(r0, r0 + SUB_ROWS)
        tab_rows = rows if per_pos_tables else slice(None)
        cq = _rms(proj[:, 0:Q_LORA], g_q_ref[...])
        q = _dot(cq.astype(BF16), w_uq_ref[...])
        qtab = qtab_ref[tab_rows, :]
        q_ref[rows, :] = jnp.concatenate(
            [(q[:, s * HEAD_SLOT:(s + 1) * HEAD_SLOT] * qtab).astype(BF16)
             for s in range(MLA_HEADS)], axis=1)
        ckv_n = _rms(proj[:, Q_LORA:Q_LORA + KV_LORA], g_kv_ref[...])
        rope_slab = proj[:, Q_LORA + KV_LORA:Q_LORA + KV_LORA + LANES]
        if emit_cache:
            ckv_ref[rows, :] = ckv_n
            kr_ref[rows, :] = rope_slab[:, 0:QK_ROPE]
        ckv_b = ckv_n.astype(BF16)
        kin = jnp.concatenate([ckv_b, (rope_slab * ktab_ref[tab_rows, :]).astype(BF16)], axis=1)
        k_ref[:, rows] = _dot_nt(w_k_ref[...], kin).astype(BF16)
        v_ref[rows, :] = _dot(ckv_b, w_uv_ref[...]).astype(BF16)

    def gating_unit(r0, gate):
        vv = _gelu(gate[:, SGU_WIDTH:2 * SGU_WIDTH])
        mu = jnp.mean(vv, axis=-1, keepdims=True)
        vc = vv - mu
        var = jnp.mean(vc * vc, axis=-1, keepdims=True)
        vn = (vc * lax.rsqrt(var + EPS) * g_sgu_ref[...] + beta_sgu_ref[...]).astype(BF16)
        n_chunks = SUB_ROWS // CHUNK
        for j in range(SGU_WIDTH // LANES):
            lanes = slice(j * LANES, (j + 1) * LANES)
            rhs = jnp.concatenate(
                [vn[n * CHUNK:(n + 1) * CHUNK, lanes] for n in range(n_chunks)], axis=1)
            o = _dot(w_sgu_ref[j], rhs)
            u = _gelu(gate[:, lanes])
            bias = bias_ref[:, lanes]
            for n in range(n_chunks):
                even = o[0:CHUNK, n * LANES:(n + 1) * LANES]
                odd = o[CHUNK:2 * CHUNK, n * LANES:(n + 1) * LANES]
                mixed = jnp.where(low_half, even, odd) + bias
                sgu_ref[r0 + n * CHUNK:r0 + (n + 1) * CHUNK, lanes] = (
                    u[n * CHUNK:(n + 1) * CHUNK, :] * mixed).astype(BF16)

    starts = list(range(0, tm, SUB_ROWS))
    ahead = project(starts[0])
    for i, r0 in enumerate(starts):
        proj, gate = ahead
        if i + 1 < len(starts):
            ahead = project(starts[i + 1])
        expand_qkv(r0, proj)
        gating_unit(r0, gate)


def _const_spec(shape):
    nd = len(shape)
    return pl.BlockSpec(shape, lambda *_: (0,) * nd, pipeline_mode=pl.Buffered(1))


def _phase1(x2d, mod3, mod_row_fn, ktab, qtab, tab_fn, wts, emit_cache):
    n_tok = x2d.shape[0]
    tm = TOKEN_TILE
    grid = (n_tok // tm,)
    tab_block = (ktab.shape[0] if ktab.shape[0] == 1 else tm, LANES)
    in_specs = [
        pl.BlockSpec((tm, D_MODEL), lambda i: (i, 0)),
        _const_spec((MOD_ROWS, N_MOD * D_MODEL)),
        pl.BlockSpec(tab_block, lambda i: (tab_fn(i), 0)),
        pl.BlockSpec(tab_block, lambda i: (tab_fn(i), 0)),
        _const_spec((1, D_MODEL)),
        _const_spec((ATTN_PROJ, D_MODEL)),
        _const_spec((2 * SGU_WIDTH, D_MODEL)),
        _const_spec((1, Q_LORA)),
        _const_spec((Q_LORA, QK_WIDTH)),
        _const_spec((1, KV_LORA)),
        _const_spec((QK_WIDTH, 2 * LANES)),
        _const_spec((KV_LORA, MLA_WIDTH)),
        _const_spec((1, SGU_WIDTH)),
        _const_spec((1, SGU_WIDTH)),
        _const_spec((SGU_WIDTH // LANES, 2 * CHUNK, CHUNK)),
        _const_spec((CHUNK, SGU_WIDTH)),
    ]
    out_shape = [
        jax.ShapeDtypeStruct((n_tok, QK_WIDTH), BF16),
        jax.ShapeDtypeStruct((QK_WIDTH, n_tok), BF16),
        jax.ShapeDtypeStruct((n_tok, MLA_WIDTH), BF16),
        jax.ShapeDtypeStruct((n_tok, SGU_WIDTH), BF16),
    ]
    out_specs = [
        pl.BlockSpec((tm, QK_WIDTH), lambda i: (i, 0)),
        pl.BlockSpec((QK_WIDTH, tm), lambda i: (0, i)),
        pl.BlockSpec((tm, MLA_WIDTH), lambda i: (i, 0)),
        pl.BlockSpec((tm, SGU_WIDTH), lambda i: (i, 0)),
    ]
    extra_inputs = []
    if emit_cache:
        out_shape += [jax.ShapeDtypeStruct((n_tok, KV_LORA), F32),
                      jax.ShapeDtypeStruct((n_tok, QK_ROPE), F32)]
        out_specs += [pl.BlockSpec((tm, KV_LORA), lambda i: (i, 0)),
                      pl.BlockSpec((tm, QK_ROPE), lambda i: (i, 0))]
        for w in (wts["w_o32"], wts["w_ff1_32"], wts["w_ff2_32"]):
            rows, cols = w.shape
            blk = (rows // grid[0], cols)
            extra_inputs.append(w)
            in_specs.append(pl.BlockSpec(blk, lambda i: (i, 0)))
            out_shape.append(jax.ShapeDtypeStruct(w.shape, BF16))
            out_specs.append(pl.BlockSpec(blk, lambda i: (i, 0)))
    return pl.pallas_call(
        functools.partial(_phase1_kernel, emit_cache, mod_row_fn),
        out_shape=out_shape,
        grid=grid,
        in_specs=in_specs,
        out_specs=out_specs,
        compiler_params=pltpu.CompilerParams(
            dimension_semantics=("arbitrary",), vmem_limit_bytes=VMEM_LIMIT),
        name="phase1_ctx" if emit_cache else "phase1_lat",
    )(x2d, mod3, ktab, qtab, wts["g_attn_pre"], wts["w_attn"], wts["w_gate"], wts["g_q"], wts["w_uq"],
      wts["g_kv"], wts["w_k"], wts["w_uv"], wts["g_sgu"], wts["beta_sgu"], wts["w_sgu"],
      wts["bias_sgu"], *extra_inputs)


def _cache_kv_kernel(kin_ref, w_k_ref, w_uv_ref, k_ref, v_ref):
    kin = kin_ref[...].astype(BF16)
    k_ref[...] = _dot_nt(w_k_ref[...], kin).astype(BF16)
    v_ref[...] = _dot(kin[:, 0:KV_LORA], w_uv_ref[...]).astype(BF16)


def _cache_kv(kin2d, wts):
    n_tok = kin2d.shape[0]
    tm = n_tok
    return pl.pallas_call(
        _cache_kv_kernel,
        out_shape=[jax.ShapeDtypeStruct((QK_WIDTH, n_tok), BF16),
                   jax.ShapeDtypeStruct((n_tok, MLA_WIDTH), BF16)],
        grid=(n_tok // tm,),
        in_specs=[
            pl.BlockSpec((tm, 2 * LANES), lambda i: (i, 0)),
            _const_spec((QK_WIDTH, 2 * LANES)),
            _const_spec((KV_LORA, MLA_WIDTH)),
        ],
        out_specs=[pl.BlockSpec((QK_WIDTH, tm), lambda i: (0, i)),
                   pl.BlockSpec((tm, MLA_WIDTH), lambda i: (i, 0))],
        compiler_params=pltpu.CompilerParams(
            dimension_semantics=("arbitrary",), vmem_limit_bytes=VMEM_LIMIT),
        name="cache_kv",
    )(kin2d, wts["w_k"], wts["w_uv"])


def _interleave(*gens):
    results = [None] * len(gens)
    live = list(range(len(gens)))
    while live:
        for i in list(live):
            try:
                next(gens[i])
            except StopIteration as stop:
                results[i] = stop.value
                live.remove(i)
    return results


def _run(gen):
    return _interleave(gen)[0]


def _attend(q_ref, kv_views, r0, low_half):
    def head_scores(hd):
        qh = q_ref[r0:r0 + SUB_ROWS, hd * HEAD_SLOT:(hd + 1) * HEAD_SLOT]
        return [_dot(qh, k_view(hd * HEAD_SLOT, HEAD_SLOT)) for k_view, _ in kv_views]

    def head_probs(scores):
        m = scores[0].max(axis=-1, keepdims=True)
        for s in scores[1:]:
            m = jnp.maximum(m, s.max(axis=-1, keepdims=True))
        probs = []
        denom = None
        for s in scores:
            p = jnp.exp(s - m)
            ps = p.sum(axis=-1, keepdims=True)
            denom = ps if denom is None else denom + ps
            probs.append(p.astype(BF16))
        return probs, denom

    def head_values(hd, probs, denom):
        slab = hd // 2
        acc = None
        for (_, v_view), p in zip(kv_views, probs):
            pv = _dot(p, v_view(slab * LANES, LANES))
            acc = pv if acc is None else acc + pv
        return acc / denom

    pair_out = []
    head_out = None
    ahead = [head_scores(hd) for hd in range(QK_AHEAD)]
    for hd in range(MLA_HEADS):
        if hd + QK_AHEAD < MLA_HEADS:
            ahead.append(head_scores(hd + QK_AHEAD))
        o = head_values(hd, *head_probs(ahead.pop(0)))
        if hd % 2 == 0:
            head_out = o
        else:
            pair_out.append(jnp.where(low_half, head_out, o).astype(BF16))
            yield
    return jnp.concatenate(pair_out, axis=1)


def _phase2_kernel(n_kv, shared_kv, mod_row_fn, x_ref, mod_ref, q_ref, sgu_ref, *refs):
    kv_refs = refs[:2 * n_kv]
    (w_o_ref, g_post_ref, g_fpre_ref, g_fpost_ref, w_ff1_ref, w_ff2_ref, o_ref) = refs[2 * n_kv:]
    tq = x_ref.shape[0]
    low_half = lax.broadcasted_iota(jnp.int32, (SUB_ROWS, LANES), 1) < V_HEAD
    _, _, gate_a, shift_f, scale_f, gate_f = _mod_rows(mod_ref, mod_row_fn(pl.program_id(0)))
    def attention(r0):
        views = []
        for t in range(n_kv):
            k_ref, v_ref = kv_refs[2 * t], kv_refs[2 * t + 1]
            if shared_kv:
                views.append((lambda c, w, k_ref=k_ref: k_ref[c:c + w, :],
                              lambda c, w, v_ref=v_ref: v_ref[:, c:c + w]))
            else:
                views.append((lambda c, w, k_ref=k_ref: k_ref[c:c + w, r0:r0 + SUB_ROWS],
                              lambda c, w, v_ref=v_ref: v_ref[r0:r0 + SUB_ROWS, c:c + w]))
        return _attend(q_ref, views, r0, low_half)

    def mixer_proj(r0, attn):
        mix_in = jnp.concatenate([attn, sgu_ref[r0:r0 + SUB_ROWS, :]], axis=1)
        return _dot(mix_in, w_o_ref[...])

    def mixer_norm(r0, mix):
        x1 = x_ref[r0:r0 + SUB_ROWS, :] + gate_a * _rms(mix, g_post_ref[...])
        h = (_rms(x1, g_fpre_ref[...]) * (1.0 + scale_f) + shift_f).astype(BF16)
        return x1, h

    def ffn(h):
        f = None
        for c in range(D_FF // FF_CHUNK):
            hid = jnp.maximum(_dot(h, w_ff1_ref[:, c * FF_CHUNK:(c + 1) * FF_CHUNK]), 0.0)
            part = _dot((hid * hid).astype(BF16), w_ff2_ref[c * FF_CHUNK:(c + 1) * FF_CHUNK, :])
            f = part if f is None else f + part
            yield
        return f

    def finish(r0, x1, f):
        o_ref[r0:r0 + SUB_ROWS, :] = x1 + gate_f * _rms(f, g_fpost_ref[...])

    starts = [a * SUB_ROWS for a in range(tq // SUB_ROWS)]
    x1h = {r0: mixer_norm(r0, mixer_proj(r0, _run(attention(r0)))) for r0 in starts}
    for r0 in starts:
        finish(r0, x1h[r0][0], _run(ffn(x1h[r0][1])))


def _phase2(x2d, mod3, mod_row_fn, q, sgu, kv_list, kv_rows, shared_kv, tq, n_outer, n_inner, wts,
            name):
    n_kv = len(kv_list) // 2
    in_specs = [
        pl.BlockSpec((tq, D_MODEL), lambda b, i: (b * n_inner + i, 0)),
        _const_spec((MOD_ROWS, N_MOD * D_MODEL)),
        pl.BlockSpec((tq, QK_WIDTH), lambda b, i: (b * n_inner + i, 0)),
        pl.BlockSpec((tq, SGU_WIDTH), lambda b, i: (b * n_inner + i, 0)),
    ]
    for t in range(n_kv):
        in_specs.append(pl.BlockSpec((QK_WIDTH, kv_rows[t]), lambda b, i: (0, b)))
        in_specs.append(pl.BlockSpec((kv_rows[t], MLA_WIDTH), lambda b, i: (b, 0)))
    in_specs += [
        _const_spec((D_MODEL, D_MODEL)),
        _const_spec((1, D_MODEL)),
        _const_spec((1, D_MODEL)),
        _const_spec((1, D_MODEL)),
        _const_spec((D_MODEL, D_FF)),
        _const_spec((D_FF, D_MODEL)),
    ]
    return pl.pallas_call(
        functools.partial(_phase2_kernel, n_kv, shared_kv, mod_row_fn),
        out_shape=jax.ShapeDtypeStruct(x2d.shape, F32),
        grid=(n_outer, n_inner),
        in_specs=in_specs,
        out_specs=pl.BlockSpec((tq, D_MODEL), lambda b, i: (b * n_inner + i, 0)),
        compiler_params=pltpu.CompilerParams(
            dimension_semantics=("arbitrary", "arbitrary"), vmem_limit_bytes=VMEM_LIMIT),
        name=name,
    )(x2d, mod3, q, sgu, *kv_list, wts["w_o"], wts["g_attn_post"], wts["g_ffn_pre"],
      wts["g_ffn_post"], wts["w_ff1"], wts["w_ff2"])


def _pair_swap(w):
    shp = w.shape
    return w.reshape(shp[:-1] + (shp[-1] // 2, 2))[..., ::-1].reshape(shp)


def _prepare_weights(g_attn_pre, g_attn_post, g_q, w_uq, g_kv, w_ukv, w_sgu, b_sgu, g_sgu,
                     beta_sgu, w_o, g_ffn_pre, g_ffn_post, w_ff1, w_ff2):
    w_uq_h = w_uq.reshape(Q_LORA, MLA_HEADS, QK_NOPE + QK_ROPE)
    w_uq_ext = jnp.concatenate(
        [w_uq_h, _pair_swap(w_uq_h[..., QK_NOPE:])], axis=-1).reshape(Q_LORA, QK_WIDTH).astype(BF16)

    w_ukv_h = w_ukv.reshape(KV_LORA, MLA_HEADS, QK_NOPE + V_HEAD)
    w_uk_slots = jnp.concatenate(
        [w_ukv_h[..., :QK_NOPE], jnp.zeros((KV_LORA, MLA_HEADS, HEAD_SLOT - QK_NOPE), F32)],
        axis=-1).reshape(KV_LORA, QK_WIDTH)
    eye = jnp.eye(QK_ROPE, dtype=F32)
    zeros_rope = jnp.zeros((QK_ROPE, QK_ROPE), F32)
    zeros_nope = jnp.zeros((QK_ROPE, QK_NOPE), F32)
    to_lo = jnp.tile(jnp.concatenate([zeros_nope, eye, zeros_rope], axis=1), (1, MLA_HEADS))
    to_hi = jnp.tile(jnp.concatenate([zeros_nope, zeros_rope, eye], axis=1), (1, MLA_HEADS))
    w_k = jnp.concatenate([w_uk_slots, to_lo, to_lo, to_hi, to_hi], axis=0).T.astype(BF16)
    w_uv = w_ukv_h[..., QK_NOPE:].reshape(KV_LORA, MLA_WIDTH).astype(BF16)

    w_sgu_pair = w_sgu.reshape(SGU_HEADS // 2, 2 * CHUNK, CHUNK).astype(BF16)
    bias_sgu = jnp.repeat(b_sgu.T, SGU_HEAD_DIM, axis=1)
    row = lambda a: a.reshape(1, -1)
    return {
        "g_attn_pre": row(g_attn_pre), "g_attn_post": row(g_attn_post),
        "g_q": row(g_q), "w_uq": w_uq_ext, "g_kv": row(g_kv), "w_k": w_k, "w_uv": w_uv,
        "g_sgu": row(g_sgu), "beta_sgu": row(beta_sgu), "w_sgu": w_sgu_pair, "bias_sgu": bias_sgu,
        "w_o32": w_o, "g_ffn_pre": row(g_ffn_pre), "g_ffn_post": row(g_ffn_post),
        "w_ff1_32": w_ff1, "w_ff2_32": w_ff2,
    }


def _rope_tables(n_tok):
    rows = n_tok // GRID_W
    row = np.repeat(np.arange(rows), GRID_W).astype(np.float32)
    col = np.tile(np.arange(GRID_W), rows).astype(np.float32)
    freqs = (1.0 / (ROPE_BASE ** (np.arange(AXIS_PAIRS, dtype=np.float32) / AXIS_PAIRS))).astype(
        np.float32)
    ang = np.concatenate([row[:, None] * freqs, col[:, None] * freqs], axis=-1)
    cos = np.repeat(np.cos(ang), 2, axis=1)
    sin = np.repeat(np.sin(ang), 2, axis=1) * np.tile(np.array([-1.0, 1.0], np.float32), QK_ROPE // 2)
    ktab = np.concatenate([cos, sin, cos, sin], axis=1)
    qtab = ATTN_SCALE * np.concatenate([np.ones((n_tok, QK_NOPE), np.float32), cos, sin], axis=1)
    return jnp.asarray(ktab, F32), jnp.asarray(qtab, F32)


def kernel(x_prompt, x_sample, cache_ckv, cache_krope, c, c_ctx, w_mod, b_mod, g_attn_pre,
           g_attn_post, w_in, g_q, w_uq, g_kv, w_ukv, w_sgu, b_sgu, g_sgu, beta_sgu, w_o,
           g_ffn_pre, g_ffn_post, w_ff1, w_ff2):
    batch, seq, _ = x_prompt.shape
    dec_batch, dec_seq, _ = x_sample.shape
    past_len = cache_ckv.shape[2]
    depth = w_mod.shape[0]
    assert depth == 1

    wts = _prepare_weights(g_attn_pre[0], g_attn_post[0], g_q[0], w_uq[0], g_kv[0],
                           w_ukv[0], w_sgu[0], b_sgu[0], g_sgu[0], beta_sgu[0], w_o[0],
                           g_ffn_pre[0], g_ffn_post[0], w_ff1[0], w_ff2[0])

    cond8 = jnp.concatenate(
        [c_ctx[None, :], c, jnp.zeros((MOD_ROWS - 1 - dec_batch, D_MODEL), F32)], axis=0)
    mod3, w_attn, w_gate = _modulation(cond8, w_mod[0], b_mod[0].reshape(1, -1), w_in[0].T)
    wts = dict(wts, w_attn=w_attn, w_gate=w_gate)

    ktab_ctx = jnp.asarray(np.arange(LANES)[None, :] < QK_ROPE, F32)
    qtab_ctx = jnp.asarray(ATTN_SCALE * (np.arange(LANES)[None, :] < QK_NOPE + QK_ROPE), F32)
    xp2d = x_prompt.reshape(batch * seq, D_MODEL)
    q_c, k_c, v_c, sgu_c, ckv_c, kr_c, w_o16, w_ff1_16, w_ff2_16 = _phase1(
        xp2d, mod3, lambda i: 0, ktab_ctx, qtab_ctx, lambda i: 0, wts, True)
    wts = dict(wts, w_o=w_o16, w_ff1=w_ff1_16, w_ff2=w_ff2_16)
    assert seq == SUB_ROWS and dec_seq % PHASE2_ROWS_LAT == 0
    y_prompt = _phase2(xp2d, mod3, lambda b: 0, q_c, sgu_c, [k_c, v_c], [PHASE2_ROWS_CTX], False,
                       PHASE2_ROWS_CTX, batch * seq // PHASE2_ROWS_CTX, 1, wts, "phase2_ctx")

    ktab_lat, qtab_lat = _rope_tables(dec_seq)
    tiles_per_seq = dec_seq // TOKEN_TILE
    xs2d = x_sample.reshape(dec_batch * dec_seq, D_MODEL)
    q_l, k_l, v_l, sgu_l = _phase1(
        xs2d, mod3, lambda i: 1 + i // tiles_per_seq, ktab_lat, qtab_lat,
        lambda i: i % tiles_per_seq, wts, False)
    kr_p = cache_krope[:, 0].reshape(dec_batch * past_len, QK_ROPE)
    pad = jnp.zeros_like(kr_p)
    kin_p = jnp.concatenate(
        [cache_ckv[:, 0].reshape(dec_batch * past_len, KV_LORA), kr_p, pad, kr_p, pad], axis=1)
    k_p, v_p = _cache_kv(kin_p, wts)
    y_sample = _phase2(xs2d, mod3, lambda b: 1 + b, q_l, sgu_l, [k_p, v_p, k_l, v_l],
                       [past_len, dec_seq], True, PHASE2_ROWS_LAT, dec_batch,
                       dec_seq // PHASE2_ROWS_LAT, wts, "phase2_lat")

    return (y_prompt.reshape(batch, seq, D_MODEL),
            y_sample.reshape(dec_batch, dec_seq, D_MODEL),
            ckv_c.reshape(batch, 1, seq, KV_LORA),
            kr_c.reshape(batch, 1, seq, QK_ROPE))
```

```python
import functools
import math

import jax
import jax.numpy as jnp
import numpy as np
from jax import lax
from jax.experimental import pallas as pl
from jax.experimental.pallas import tpu as pltpu

D_MODEL = 1024
GRID_W = 64
MLA_HEADS = 8
QK_NOPE = 64
QK_ROPE = 32
V_HEAD = 64
Q_LORA = 256
KV_LORA = 128
MLA_WIDTH = MLA_HEADS * V_HEAD
SGU_HEADS = 8
SGU_WIDTH = D_MODEL - MLA_WIDTH
SGU_HEAD_DIM = SGU_WIDTH // SGU_HEADS
CHUNK = 128
D_FF = 4 * D_MODEL
AXIS_PAIRS = QK_ROPE // 4
ROPE_BASE = 10000.0
EPS = 1e-6
N_MOD = 6
ATTN_SCALE = (QK_NOPE + QK_ROPE) ** -0.5

LANES = 128
HEAD_SLOT = LANES
QK_WIDTH = MLA_HEADS * HEAD_SLOT
ATTN_PROJ = Q_LORA + KV_LORA + LANES
MOD_ROWS = 8
MOD_K_ROWS = 128
TOKEN_TILE = 1024
PHASE1_CHAIN = 512
SUB_ROWS = 256
PHASE2_ROWS_CTX = 512
PHASE2_ROWS_LAT = 512
QK_AHEAD = 1
FF_CHUNK = 1024
VMEM_LIMIT = 56 * 1024 * 1024

BF16 = jnp.bfloat16
F32 = jnp.float32


def _dot(a, b):
    return jnp.dot(a, b, preferred_element_type=F32)


def _dot_nt(a, b):
    return lax.dot_general(a, b, (((1,), (1,)), ((), ())), preferred_element_type=F32)


def _rms(x, g):
    return x * lax.rsqrt(jnp.mean(x * x, axis=-1, keepdims=True) + EPS) * g


def _gelu(x):
    inner = math.sqrt(2.0 / math.pi) * (x + 0.044715 * (x * x * x))
    return x * (0.5 * (1.0 + jnp.tanh(inner)))


def _mod_kernel(cond_a_ref, cond_b_ref, w_a_ref, w_b_ref, b_ref, w_in_attn_ref, w_in_gate_ref,
                o_ref, w_attn_ref, w_gate_ref):
    @pl.when(pl.program_id(0) == 0)
    def _():
        o_ref[...] = jnp.broadcast_to(b_ref[...], o_ref.shape)

    def part(cond_ref, w_ref):
        cnd = cond_ref[...]
        act = cnd * (1.0 / (1.0 + jnp.exp(-cnd)))
        return _dot(act.astype(BF16), w_ref[...].astype(BF16))

    o_ref[...] += part(cond_a_ref, w_a_ref) + part(cond_b_ref, w_b_ref)

    w_gate_ref[...] = w_in_gate_ref[...].astype(BF16)

    @pl.when(pl.program_id(0) == 0)
    def _():
        rope0 = Q_LORA + KV_LORA
        kr = w_in_attn_ref[rope0:rope0 + QK_ROPE, :]
        row = lax.broadcasted_iota(jnp.int32, kr.shape, 0)
        kr_sw = jnp.where(row % 2 == 0, pltpu.roll(kr, QK_ROPE - 1, 0), pltpu.roll(kr, 1, 0))
        w_attn_ref[0:rope0 + QK_ROPE, :] = w_in_attn_ref[...].astype(BF16)
        w_attn_ref[rope0 + QK_ROPE:ATTN_PROJ, :] = jnp.concatenate(
            [kr_sw, kr, kr_sw], axis=0).astype(BF16)


def _modulation(cond8, w_mod, b_mod, w_in_t):
    n = w_mod.shape[1]
    tk = MOD_K_ROWS
    nk = D_MODEL // tk // 2
    attn_rows = Q_LORA + KV_LORA + QK_ROPE
    gate_rows = 2 * SGU_WIDTH // nk
    return pl.pallas_call(
        _mod_kernel,
        out_shape=[jax.ShapeDtypeStruct((MOD_ROWS, n), F32),
                   jax.ShapeDtypeStruct((ATTN_PROJ, D_MODEL), BF16),
                   jax.ShapeDtypeStruct((2 * SGU_WIDTH, D_MODEL), BF16)],
        grid=(nk,),
        in_specs=[
            pl.BlockSpec((MOD_ROWS, tk), lambda k: (0, k)),
            pl.BlockSpec((MOD_ROWS, tk), lambda k: (0, k + nk)),
            pl.BlockSpec((tk, n), lambda k: (k, 0)),
            pl.BlockSpec((tk, n), lambda k: (k + nk, 0)),
            pl.BlockSpec((1, n), lambda k: (0, 0)),
            pl.BlockSpec((attn_rows, D_MODEL), lambda k: (0, 0)),
            pl.BlockSpec((pl.Element(gate_rows), pl.Element(D_MODEL)),
                         lambda k: (pl.multiple_of(attn_rows + k * gate_rows, QK_ROPE), 0)),
        ],
        out_specs=[pl.BlockSpec((MOD_ROWS, n), lambda k: (0, 0)),
                   pl.BlockSpec((ATTN_PROJ, D_MODEL), lambda k: (0, 0)),
                   pl.BlockSpec((gate_rows, D_MODEL), lambda k: (k, 0))],
        compiler_params=pltpu.CompilerParams(
            dimension_semantics=("arbitrary",), vmem_limit_bytes=VMEM_LIMIT),
        name="modulation",
    )(cond8, cond8, w_mod, w_mod, b_mod, w_in_t, w_in_t)


def _mod_rows(mod_ref, row):
    vec = mod_ref[pl.ds(row, 1), :]
    return [vec[:, j * D_MODEL:(j + 1) * D_MODEL] for j in range(N_MOD)]


def _phase1_kernel(emit_cache, mod_row_fn, x_ref, mod_ref, ktab_ref, qtab_ref, g_pre_ref, w_attn_ref,
                   w_gate_ref,
                   g_q_ref, w_uq_ref, g_kv_ref, w_k_ref, w_uv_ref, g_sgu_ref, beta_sgu_ref,
                   w_sgu_ref, bias_ref, *rest):
    if emit_cache:
        (w_o32_ref, w_ff1_32_ref, w_ff2_32_ref, q_ref, k_ref, v_ref, sgu_ref, ckv_ref, kr_ref,
         w_o16_ref, w_ff1_16_ref, w_ff2_16_ref) = rest
        w_o16_ref[...] = w_o32_ref[...].astype(BF16)
        w_ff1_16_ref[...] = w_ff1_32_ref[...].astype(BF16)
        w_ff2_16_ref[...] = w_ff2_32_ref[...].astype(BF16)
    else:
        q_ref, k_ref, v_ref, sgu_ref = rest
    tm = x_ref.shape[0]
    mod = _mod_rows(mod_ref, mod_row_fn(pl.program_id(0)))
    shift_a, scale_a = mod[0], mod[1]
    low_half = lax.broadcasted_iota(jnp.int32, (CHUNK, LANES), 1) < SGU_HEAD_DIM
    per_pos_tables = ktab_ref.shape[0] > 1

    def project(r0):
        rows = slice(r0, r0 + PHASE1_CHAIN)
        h = (_rms(x_ref[rows, :], g_pre_ref[...]) * (1.0 + scale_a) + shift_a).astype(BF16)
        proj = _dot_nt(h, w_attn_ref[...])
        gate = _dot_nt(h, w_gate_ref[...])
        return proj, gate

    def expand_qkv(r0, proj):
        rows = slice(r0, r0 + PHASE1_CHAIN)
        tab_rows = rows if per_pos_tables else slice(None)
        cq = _rms(proj[:, 0:Q_LORA], g_q_ref[...])
        q = _dot(cq.astype(BF16), w_uq_ref[...])
        qtab = qtab_ref[tab_rows, :]
        q_ref[rows, :] = jnp.concatenate(
            [(q[:, s * HEAD_SLOT:(s + 1) * HEAD_SLOT] * qtab).astype(BF16)
             for s in range(MLA_HEADS)], axis=1)
        ckv_n = _rms(proj[:, Q_LORA:Q_LORA + KV_LORA], g_kv_ref[...])
        rope_slab = proj[:, Q_LORA + KV_LORA:Q_LORA + KV_LORA + LANES]
        if emit_cache:
            ckv_ref[rows, :] = ckv_n
            kr_ref[rows, :] = rope_slab[:, 0:QK_ROPE]
        ckv_b = ckv_n.astype(BF16)
        kin = jnp.concatenate([ckv_b, (rope_slab * ktab_ref[tab_rows, :]).astype(BF16)], axis=1)
        k_ref[:, rows] = _dot_nt(w_k_ref[...], kin).astype(BF16)
        v_ref[rows, :] = _dot(ckv_b, w_uv_ref[...]).astype(BF16)

    def gating_unit(r0, gate):
        vv = _gelu(gate[:, SGU_WIDTH:2 * SGU_WIDTH])
        mu = jnp.mean(vv, axis=-1, keepdims=True)
        vc = vv - mu
        var = jnp.mean(vc * vc, axis=-1, keepdims=True)
        vn = (vc * lax.rsqrt(var + EPS) * g_sgu_ref[...] + beta_sgu_ref[...]).astype(BF16)
        n_chunks = PHASE1_CHAIN // CHUNK
        for j in range(SGU_WIDTH // LANES):
            lanes = slice(j * LANES, (j + 1) * LANES)
            rhs = jnp.concatenate(
                [vn[n * CHUNK:(n + 1) * CHUNK, lanes] for n in range(n_chunks)], axis=1)
            o = _dot(w_sgu_ref[j], rhs)
            u = _gelu(gate[:, lanes])
            bias = bias_ref[:, lanes]
            for n in range(n_chunks):
                even = o[0:CHUNK, n * LANES:(n + 1) * LANES]
                odd = o[CHUNK:2 * CHUNK, n * LANES:(n + 1) * LANES]
                mixed = jnp.where(low_half, even, odd) + bias
                sgu_ref[r0 + n * CHUNK:r0 + (n + 1) * CHUNK, lanes] = (
                    u[n * CHUNK:(n + 1) * CHUNK, :] * mixed).astype(BF16)

    starts = list(range(0, tm, PHASE1_CHAIN))
    ahead = project(starts[0])
    for i, r0 in enumerate(starts):
        proj, gate = ahead
        if i + 1 < len(starts):
            ahead = project(starts[i + 1])
        expand_qkv(r0, proj)
        gating_unit(r0, gate)


def _const_spec(shape):
    nd = len(shape)
    return pl.BlockSpec(shape, lambda *_: (0,) * nd, pipeline_mode=pl.Buffered(1))


def _phase1(x2d, mod3, mod_row_fn, ktab, qtab, tab_fn, wts, emit_cache):
    n_tok = x2d.shape[0]
    tm = TOKEN_TILE
    grid = (n_tok // tm,)
    tab_block = (ktab.shape[0] if ktab.shape[0] == 1 else tm, LANES)
    in_specs = [
        pl.BlockSpec((tm, D_MODEL), lambda i: (i, 0)),
        _const_spec((MOD_ROWS, N_MOD * D_MODEL)),
        pl.BlockSpec(tab_block, lambda i: (tab_fn(i), 0)),
        pl.BlockSpec(tab_block, lambda i: (tab_fn(i), 0)),
        _const_spec((1, D_MODEL)),
        _const_spec((ATTN_PROJ, D_MODEL)),
        _const_spec((2 * SGU_WIDTH, D_MODEL)),
        _const_spec((1, Q_LORA)),
        _const_spec((Q_LORA, QK_WIDTH)),
        _const_spec((1, KV_LORA)),
        _const_spec((QK_WIDTH, 2 * LANES)),
        _const_spec((KV_LORA, MLA_WIDTH)),
        _const_spec((1, SGU_WIDTH)),
        _const_spec((1, SGU_WIDTH)),
        _const_spec((SGU_WIDTH // LANES, 2 * CHUNK, CHUNK)),
        _const_spec((CHUNK, SGU_WIDTH)),
    ]
    out_shape = [
        jax.ShapeDtypeStruct((n_tok, QK_WIDTH), BF16),
        jax.ShapeDtypeStruct((QK_WIDTH, n_tok), BF16),
        jax.ShapeDtypeStruct((n_tok, MLA_WIDTH), BF16),
        jax.ShapeDtypeStruct((n_tok, SGU_WIDTH), BF16),
    ]
    out_specs = [
        pl.BlockSpec((tm, QK_WIDTH), lambda i: (i, 0)),
        pl.BlockSpec((QK_WIDTH, tm), lambda i: (0, i)),
        pl.BlockSpec((tm, MLA_WIDTH), lambda i: (i, 0)),
        pl.BlockSpec((tm, SGU_WIDTH), lambda i: (i, 0)),
    ]
    extra_inputs = []
    if emit_cache:
        out_shape += [jax.ShapeDtypeStruct((n_tok, KV_LORA), F32),
                      jax.ShapeDtypeStruct((n_tok, QK_ROPE), F32)]
        out_specs += [pl.BlockSpec((tm, KV_LORA), lambda i: (i, 0)),
                      pl.BlockSpec((tm, QK_ROPE), lambda i: (i, 0))]
        for w in (wts["w_o32"], wts["w_ff1_32"], wts["w_ff2_32"]):
            rows, cols = w.shape
            blk = (rows // grid[0], cols)
            extra_inputs.append(w)
            in_specs.append(pl.BlockSpec(blk, lambda i: (i, 0)))
            out_shape.append(jax.ShapeDtypeStruct(w.shape, BF16))
            out_specs.append(pl.BlockSpec(blk, lambda i: (i, 0)))
    return pl.pallas_call(
        functools.partial(_phase1_kernel, emit_cache, mod_row_fn),
        out_shape=out_shape,
        grid=grid,
        in_specs=in_specs,
        out_specs=out_specs,
        compiler_params=pltpu.CompilerParams(
            dimension_semantics=("arbitrary",), vmem_limit_bytes=VMEM_LIMIT),
        name="phase1_ctx" if emit_cache else "phase1_lat",
    )(x2d, mod3, ktab, qtab, wts["g_attn_pre"], wts["w_attn"], wts["w_gate"], wts["g_q"], wts["w_uq"],
      wts["g_kv"], wts["w_k"], wts["w_uv"], wts["g_sgu"], wts["beta_sgu"], wts["w_sgu"],
      wts["bias_sgu"], *extra_inputs)


def _cache_kv_kernel(kin_ref, w_k_ref, w_uv_ref, k_ref, v_ref):
    kin = kin_ref[...].astype(BF16)
    k_ref[...] = _dot_nt(w_k_ref[...], kin).astype(BF16)
    v_ref[...] = _dot(kin[:, 0:KV_LORA], w_uv_ref[...]).astype(BF16)


def _cache_kv(kin2d, wts):
    n_tok = kin2d.shape[0]
    tm = n_tok
    return pl.pallas_call(
        _cache_kv_kernel,
        out_shape=[jax.ShapeDtypeStruct((QK_WIDTH, n_tok), BF16),
                   jax.ShapeDtypeStruct((n_tok, MLA_WIDTH), BF16)],
        grid=(n_tok // tm,),
        in_specs=[
            pl.BlockSpec((tm, 2 * LANES), lambda i: (i, 0)),
            _const_spec((QK_WIDTH, 2 * LANES)),
            _const_spec((KV_LORA, MLA_WIDTH)),
        ],
        out_specs=[pl.BlockSpec((QK_WIDTH, tm), lambda i: (0, i)),
                   pl.BlockSpec((tm, MLA_WIDTH), lambda i: (i, 0))],
        compiler_params=pltpu.CompilerParams(
            dimension_semantics=("arbitrary",), vmem_limit_bytes=VMEM_LIMIT),
        name="cache_kv",
    )(kin2d, wts["w_k"], wts["w_uv"])


def _interleave(*gens):
    results = [None] * len(gens)
    live = list(range(len(gens)))
    while live:
        for i in list(live):
            try:
                next(gens[i])
            except StopIteration as stop:
                results[i] = stop.value
                live.remove(i)
    return results


def _run(gen):
    return _interleave(gen)[0]


def _attend(q_ref, kv_views, r0, low_half):
    def head_scores(hd):
        qh = q_ref[r0:r0 + SUB_ROWS, hd * HEAD_SLOT:(hd + 1) * HEAD_SLOT]
        return [_dot(qh, k_view(hd * HEAD_SLOT, HEAD_SLOT)) for k_view, _ in kv_views]

    def head_probs(scores):
        m = scores[0].max(axis=-1, keepdims=True)
        for s in scores[1:]:
            m = jnp.maximum(m, s.max(axis=-1, keepdims=True))
        probs = []
        denom = None
        for s in scores:
            p = jnp.exp(s - m)
            ps = p.sum(axis=-1, keepdims=True)
            denom = ps if denom is None else denom + ps
            probs.append(p.astype(BF16))
        return probs, denom

    def head_values(hd, probs, denom):
        slab = hd // 2
        acc = None
        for (_, v_view), p in zip(kv_views, probs):
            pv = _dot(p, v_view(slab * LANES, LANES))
            acc = pv if acc is None else acc + pv
        return acc / denom

    pair_out = []
    head_out = None
    ahead = [head_scores(hd) for hd in range(QK_AHEAD)]
    for hd in range(MLA_HEADS):
        if hd + QK_AHEAD < MLA_HEADS:
            ahead.append(head_scores(hd + QK_AHEAD))
        o = head_values(hd, *head_probs(ahead.pop(0)))
        if hd % 2 == 0:
            head_out = o
        else:
            pair_out.append(jnp.where(low_half, head_out, o).astype(BF16))
            yield
    return jnp.concatenate(pair_out, axis=1)


def _phase2_kernel(n_kv, shared_kv, mod_row_fn, x_ref, mod_ref, q_ref, sgu_ref, *refs):
    kv_refs = refs[:2 * n_kv]
    (w_o_ref, g_post_ref, g_fpre_ref, g_fpost_ref, w_ff1_ref, w_ff2_ref, o_ref) = refs[2 * n_kv:]
    tq = x_ref.shape[0]
    low_half = lax.broadcasted_iota(jnp.int32, (SUB_ROWS, LANES), 1) < V_HEAD
    _, _, gate_a, shift_f, scale_f, gate_f = _mod_rows(mod_ref, mod_row_fn(pl.program_id(0)))
    def attention(r0):
        views = []
        for t in range(n_kv):
            k_ref, v_ref = kv_refs[2 * t], kv_refs[2 * t + 1]
            if shared_kv:
                views.append((lambda c, w, k_ref=k_ref: k_ref[c:c + w, :],
                              lambda c, w, v_ref=v_ref: v_ref[:, c:c + w]))
            else:
                views.append((lambda c, w, k_ref=k_ref: k_ref[c:c + w, r0:r0 + SUB_ROWS],
                              lambda c, w, v_ref=v_ref: v_ref[r0:r0 + SUB_ROWS, c:c + w]))
        return _attend(q_ref, views, r0, low_half)

    def mixer_proj(r0, attn):
        mix_in = jnp.concatenate([attn, sgu_ref[r0:r0 + SUB_ROWS, :]], axis=1)
        return _dot(mix_in, w_o_ref[...])

    def mixer_norm(r0, mix):
        x1 = x_ref[r0:r0 + SUB_ROWS, :] + gate_a * _rms(mix, g_post_ref[...])
        h = (_rms(x1, g_fpre_ref[...]) * (1.0 + scale_f) + shift_f).astype(BF16)
        return x1, h

    def ffn(h):
        f = None
        for c in range(D_FF // FF_CHUNK):
            hid = jnp.maximum(_dot(h, w_ff1_ref[:, c * FF_CHUNK:(c + 1) * FF_CHUNK]), 0.0)
            part = _dot((hid * hid).astype(BF16), w_ff2_ref[c * FF_CHUNK:(c + 1) * FF_CHUNK, :])
            f = part if f is None else f + part
            yield
        return f

    def finish(r0, x1, f):
        o_ref[r0:r0 + SUB_ROWS, :] = x1 + gate_f * _rms(f, g_fpost_ref[...])

    starts = [a * SUB_ROWS for a in range(tq // SUB_ROWS)]
    x1h = {r0: mixer_norm(r0, mixer_proj(r0, _run(attention(r0)))) for r0 in starts}
    for r0 in starts:
        finish(r0, x1h[r0][0], _run(ffn(x1h[r0][1])))


def _phase2(x2d, mod3, mod_row_fn, q, sgu, kv_list, kv_rows, shared_kv, tq, n_outer, n_inner, wts,
            name):
    n_kv = len(kv_list) // 2
    in_specs = [
        pl.BlockSpec((tq, D_MODEL), lambda b, i: (b * n_inner + i, 0)),
        _const_spec((MOD_ROWS, N_MOD * D_MODEL)),
        pl.BlockSpec((tq, QK_WIDTH), lambda b, i: (b * n_inner + i, 0)),
        pl.BlockSpec((tq, SGU_WIDTH), lambda b, i: (b * n_inner + i, 0)),
    ]
    for t in range(n_kv):
        in_specs.append(pl.BlockSpec((QK_WIDTH, kv_rows[t]), lambda b, i: (0, b)))
        in_specs.append(pl.BlockSpec((kv_rows[t], MLA_WIDTH), lambda b, i: (b, 0)))
    in_specs += [
        _const_spec((D_MODEL, D_MODEL)),
        _const_spec((1, D_MODEL)),
        _const_spec((1, D_MODEL)),
        _const_spec((1, D_MODEL)),
        _const_spec((D_MODEL, D_FF)),
        _const_spec((D_FF, D_MODEL)),
    ]
    return pl.pallas_call(
        functools.partial(_phase2_kernel, n_kv, shared_kv, mod_row_fn),
        out_shape=jax.ShapeDtypeStruct(x2d.shape, F32),
        grid=(n_outer, n_inner),
        in_specs=in_specs,
        out_specs=pl.BlockSpec((tq, D_MODEL), lambda b, i: (b * n_inner + i, 0)),
        compiler_params=pltpu.CompilerParams(
            dimension_semantics=("arbitrary", "arbitrary"), vmem_limit_bytes=VMEM_LIMIT),
        name=name,
    )(x2d, mod3, q, sgu, *kv_list, wts["w_o"], wts["g_attn_post"], wts["g_ffn_pre"],
      wts["g_ffn_post"], wts["w_ff1"], wts["w_ff2"])


def _pair_swap(w):
    shp = w.shape
    return w.reshape(shp[:-1] + (shp[-1] // 2, 2))[..., ::-1].reshape(shp)


def _prepare_weights(g_attn_pre, g_attn_post, g_q, w_uq, g_kv, w_ukv, w_sgu, b_sgu, g_sgu,
                     beta_sgu, w_o, g_ffn_pre, g_ffn_post, w_ff1, w_ff2):
    w_uq_h = w_uq.reshape(Q_LORA, MLA_HEADS, QK_NOPE + QK_ROPE)
    w_uq_ext = jnp.concatenate(
        [w_uq_h, _pair_swap(w_uq_h[..., QK_NOPE:])], axis=-1).reshape(Q_LORA, QK_WIDTH).astype(BF16)

    w_ukv_h = w_ukv.reshape(KV_LORA, MLA_HEADS, QK_NOPE + V_HEAD)
    w_uk_slots = jnp.concatenate(
        [w_ukv_h[..., :QK_NOPE], jnp.zeros((KV_LORA, MLA_HEADS, HEAD_SLOT - QK_NOPE), F32)],
        axis=-1).reshape(KV_LORA, QK_WIDTH)
    eye = jnp.eye(QK_ROPE, dtype=F32)
    zeros_rope = jnp.zeros((QK_ROPE, QK_ROPE), F32)
    zeros_nope = jnp.zeros((QK_ROPE, QK_NOPE), F32)
    to_lo = jnp.tile(jnp.concatenate([zeros_nope, eye, zeros_rope], axis=1), (1, MLA_HEADS))
    to_hi = jnp.tile(jnp.concatenate([zeros_nope, zeros_rope, eye], axis=1), (1, MLA_HEADS))
    w_k = jnp.concatenate([w_uk_slots, to_lo, to_lo, to_hi, to_hi], axis=0).T.astype(BF16)
    w_uv = w_ukv_h[..., QK_NOPE:].reshape(KV_LORA, MLA_WIDTH).astype(BF16)

    w_sgu_pair = w_sgu.reshape(SGU_HEADS // 2, 2 * CHUNK, CHUNK).astype(BF16)
    bias_sgu = jnp.repeat(b_sgu.T, SGU_HEAD_DIM, axis=1)
    row = lambda a: a.reshape(1, -1)
    return {
        "g_attn_pre": row(g_attn_pre), "g_attn_post": row(g_attn_post),
        "g_q": row(g_q), "w_uq": w_uq_ext, "g_kv": row(g_kv), "w_k": w_k, "w_uv": w_uv,
        "g_sgu": row(g_sgu), "beta_sgu": row(beta_sgu), "w_sgu": w_sgu_pair, "bias_sgu": bias_sgu,
        "w_o32": w_o, "g_ffn_pre": row(g_ffn_pre), "g_ffn_post": row(g_ffn_post),
        "w_ff1_32": w_ff1, "w_ff2_32": w_ff2,
    }


def _rope_tables(n_tok):
    rows = n_tok // GRID_W
    row = np.repeat(np.arange(rows), GRID_W).astype(np.float32)
    col = np.tile(np.arange(GRID_W), rows).astype(np.float32)
    freqs = (1.0 / (ROPE_BASE ** (np.arange(AXIS_PAIRS, dtype=np.float32) / AXIS_PAIRS))).astype(
        np.float32)
    ang = np.concatenate([row[:, None] * freqs, col[:, None] * freqs], axis=-1)
    cos = np.repeat(np.cos(ang), 2, axis=1)
    sin = np.repeat(np.sin(ang), 2, axis=1) * np.tile(np.array([-1.0, 1.0], np.float32), QK_ROPE // 2)
    ktab = np.concatenate([cos, sin, cos, sin], axis=1)
    qtab = ATTN_SCALE * np.concatenate([np.ones((n_tok, QK_NOPE), np.float32), cos, sin], axis=1)
    return jnp.asarray(ktab, F32), jnp.asarray(qtab, F32)


def kernel(x_prompt, x_sample, cache_ckv, cache_krope, c, c_ctx, w_mod, b_mod, g_attn_pre,
           g_attn_post, w_in, g_q, w_uq, g_kv, w_ukv, w_sgu, b_sgu, g_sgu, beta_sgu, w_o,
           g_ffn_pre, g_ffn_post, w_ff1, w_ff2):
    batch, seq, _ = x_prompt.shape
    dec_batch, dec_seq, _ = x_sample.shape
    past_len = cache_ckv.shape[2]
    depth = w_mod.shape[0]
    assert depth == 1

    wts = _prepare_weights(g_attn_pre[0], g_attn_post[0], g_q[0], w_uq[0], g_kv[0],
                           w_ukv[0], w_sgu[0], b_sgu[0], g_sgu[0], beta_sgu[0], w_o[0],
                           g_ffn_pre[0], g_ffn_post[0], w_ff1[0], w_ff2[0])

    cond8 = jnp.concatenate(
        [c_ctx[None, :], c, jnp.zeros((MOD_ROWS - 1 - dec_batch, D_MODEL), F32)], axis=0)
    mod3, w_attn, w_gate = _modulation(cond8, w_mod[0], b_mod[0].reshape(1, -1), w_in[0].T)
    wts = dict(wts, w_attn=w_attn, w_gate=w_gate)

    ktab_ctx = jnp.asarray(np.arange(LANES)[None, :] < QK_ROPE, F32)
    qtab_ctx = jnp.asarray(ATTN_SCALE * (np.arange(LANES)[None, :] < QK_NOPE + QK_ROPE), F32)
    xp2d = x_prompt.reshape(batch * seq, D_MODEL)
    q_c, k_c, v_c, sgu_c, ckv_c, kr_c, w_o16, w_ff1_16, w_ff2_16 = _phase1(
        xp2d, mod3, lambda i: 0, ktab_ctx, qtab_ctx, lambda i: 0, wts, True)
    wts = dict(wts, w_o=w_o16, w_ff1=w_ff1_16, w_ff2=w_ff2_16)
    assert seq == SUB_ROWS and dec_seq % PHASE2_ROWS_LAT == 0
    y_prompt = _phase2(xp2d, mod3, lambda b: 0, q_c, sgu_c, [k_c, v_c], [PHASE2_ROWS_CTX], False,
                       PHASE2_ROWS_CTX, batch * seq // PHASE2_ROWS_CTX, 1, wts, "phase2_ctx")

    ktab_lat, qtab_lat = _rope_tables(dec_seq)
    tiles_per_seq = dec_seq // TOKEN_TILE
    xs2d = x_sample.reshape(dec_batch * dec_seq, D_MODEL)
    q_l, k_l, v_l, sgu_l = _phase1(
        xs2d, mod3, lambda i: 1 + i // tiles_per_seq, ktab_lat, qtab_lat,
        lambda i: i % tiles_per_seq, wts, False)
    kr_p = cache_krope[:, 0].reshape(dec_batch * past_len, QK_ROPE)
    pad = jnp.zeros_like(kr_p)
    kin_p = jnp.concatenate(
        [cache_ckv[:, 0].reshape(dec_batch * past_len, KV_LORA), kr_p, pad, kr_p, pad], axis=1)
    k_p, v_p = _cache_kv(kin_p, wts)
    y_sample = _phase2(xs2d, mod3, lambda b: 1 + b, q_l, sgu_l, [k_p, v_p, k_l, v_l],
                       [past_len, dec_seq], True, PHASE2_ROWS_LAT, dec_batch,
                       dec_seq // PHASE2_ROWS_LAT, wts, "phase2_lat")

    return (y_prompt.reshape(batch, seq, D_MODEL),
            y_sample.reshape(dec_batch, dec_seq, D_MODEL),
            ckv_c.reshape(batch, 1, seq, KV_LORA),
            kr_c.reshape(batch, 1, seq, QK_ROPE))
```

```python
import functools
import math

import jax
import jax.numpy as jnp
import numpy as np
from jax import lax
from jax.experimental import pallas as pl
from jax.experimental.pallas import tpu as pltpu

D_MODEL = 1024
GRID_W = 64
MLA_HEADS = 8
QK_NOPE = 64
QK_ROPE = 32
V_HEAD = 64
Q_LORA = 256
KV_LORA = 128
MLA_WIDTH = MLA_HEADS * V_HEAD
SGU_HEADS = 8
SGU_WIDTH = D_MODEL - MLA_WIDTH
SGU_HEAD_DIM = SGU_WIDTH // SGU_HEADS
CHUNK = 128
D_FF = 4 * D_MODEL
AXIS_PAIRS = QK_ROPE // 4
ROPE_BASE = 10000.0
EPS = 1e-6
N_MOD = 6
ATTN_SCALE = (QK_NOPE + QK_ROPE) ** -0.5
QUERY_SCALE = ATTN_SCALE * math.log2(math.e)

LANES = 128
HEAD_SLOT = LANES
QK_WIDTH = MLA_HEADS * HEAD_SLOT
ATTN_PROJ = Q_LORA + KV_LORA + LANES
MOD_ROWS = 8
MOD_K_ROWS = 128
TOKEN_TILE = 1024
PHASE1_CHAIN = 512
SUB_ROWS = 256
PHASE2_ROWS_CTX = 512
PHASE2_ROWS_LAT = 512
QK_AHEAD = 1
FF_CHUNK = 1024
VMEM_LIMIT = 56 * 1024 * 1024

BF16 = jnp.bfloat16
F32 = jnp.float32


def _dot(a, b):
    return jnp.dot(a, b, preferred_element_type=F32)


def _dot_nt(a, b):
    return lax.dot_general(a, b, (((1,), (1,)), ((), ())), preferred_element_type=F32)


def _rms(x, g):
    return x * lax.rsqrt(jnp.mean(x * x, axis=-1, keepdims=True) + EPS) * g


def _gelu(x):
    inner = math.sqrt(2.0 / math.pi) * (x + 0.044715 * (x * x * x))
    return x * (0.5 * (1.0 + jnp.tanh(inner)))


def _mod_kernel(cond_a_ref, cond_b_ref, w_a_ref, w_b_ref, b_ref, w_in_attn_ref, w_in_gate_ref,
                o_ref, w_attn_ref, w_gate_ref):
    @pl.when(pl.program_id(0) == 0)
    def _():
        o_ref[...] = jnp.broadcast_to(b_ref[...], o_ref.shape)

    def part(cond_ref, w_ref):
        cnd = cond_ref[...]
        act = cnd * (1.0 / (1.0 + jnp.exp(-cnd)))
        return _dot(act.astype(BF16), w_ref[...].astype(BF16))

    o_ref[...] += part(cond_a_ref, w_a_ref) + part(cond_b_ref, w_b_ref)

    w_gate_ref[...] = w_in_gate_ref[...].astype(BF16)

    @pl.when(pl.program_id(0) == 0)
    def _():
        rope0 = Q_LORA + KV_LORA
        kr = w_in_attn_ref[rope0:rope0 + QK_ROPE, :]
        row = lax.broadcasted_iota(jnp.int32, kr.shape, 0)
        kr_sw = jnp.where(row % 2 == 0, pltpu.roll(kr, QK_ROPE - 1, 0), pltpu.roll(kr, 1, 0))
        w_attn_ref[0:rope0 + QK_ROPE, :] = w_in_attn_ref[...].astype(BF16)
        w_attn_ref[rope0 + QK_ROPE:ATTN_PROJ, :] = jnp.concatenate(
            [kr_sw, kr, kr_sw], axis=0).astype(BF16)


def _modulation(cond8, w_mod, b_mod, w_in_t):
    n = w_mod.shape[1]
    tk = MOD_K_ROWS
    nk = D_MODEL // tk // 2
    attn_rows = Q_LORA + KV_LORA + QK_ROPE
    gate_rows = 2 * SGU_WIDTH // nk
    return pl.pallas_call(
        _mod_kernel,
        out_shape=[jax.ShapeDtypeStruct((MOD_ROWS, n), F32),
                   jax.ShapeDtypeStruct((ATTN_PROJ, D_MODEL), BF16),
                   jax.ShapeDtypeStruct((2 * SGU_WIDTH, D_MODEL), BF16)],
        grid=(nk,),
        in_specs=[
            pl.BlockSpec((MOD_ROWS, tk), lambda k: (0, k)),
            pl.BlockSpec((MOD_ROWS, tk), lambda k: (0, k + nk)),
            pl.BlockSpec((tk, n), lambda k: (k, 0)),
            pl.BlockSpec((tk, n), lambda k: (k + nk, 0)),
            pl.BlockSpec((1, n), lambda k: (0, 0)),
            pl.BlockSpec((attn_rows, D_MODEL), lambda k: (0, 0)),
            pl.BlockSpec((pl.Element(gate_rows), pl.Element(D_MODEL)),
                         lambda k: (pl.multiple_of(attn_rows + k * gate_rows, QK_ROPE), 0)),
        ],
        out_specs=[pl.BlockSpec((MOD_ROWS, n), lambda k: (0, 0)),
                   pl.BlockSpec((ATTN_PROJ, D_MODEL), lambda k: (0, 0)),
                   pl.BlockSpec((gate_rows, D_MODEL), lambda k: (k, 0))],
        compiler_params=pltpu.CompilerParams(
            dimension_semantics=("arbitrary",), vmem_limit_bytes=VMEM_LIMIT),
        name="modulation",
    )(cond8, cond8, w_mod, w_mod, b_mod, w_in_t, w_in_t)


def _mod_rows(mod_ref, row):
    vec = mod_ref[pl.ds(row, 1), :]
    return [vec[:, j * D_MODEL:(j + 1) * D_MODEL] for j in range(N_MOD)]


def _phase1_kernel(emit_cache, mod_row_fn, x_ref, mod_ref, ktab_ref, qtab_ref, g_pre_ref, w_attn_ref,
                   w_gate_ref,
                   g_q_ref, w_uq_ref, g_kv_ref, w_k_ref, w_uv_ref, g_sgu_ref, beta_sgu_ref,
                   w_sgu_ref, bias_ref, *rest):
    if emit_cache:
        (w_o32_ref, w_ff1_32_ref, w_ff2_32_ref, q_ref, k_ref, v_ref, sgu_ref, ckv_ref, kr_ref,
         w_o16_ref, w_ff1_16_ref, w_ff2_16_ref) = rest
        w_o16_ref[...] = w_o32_ref[...].astype(BF16)
        w_ff1_16_ref[...] = w_ff1_32_ref[...].astype(BF16)
        w_ff2_16_ref[...] = w_ff2_32_ref[...].astype(BF16)
    else:
        q_ref, k_ref, v_ref, sgu_ref = rest
    tm = x_ref.shape[0]
    mod = _mod_rows(mod_ref, mod_row_fn(pl.program_id(0)))
    shift_a, scale_a = mod[0], mod[1]
    low_half = lax.broadcasted_iota(jnp.int32, (CHUNK, LANES), 1) < SGU_HEAD_DIM
    per_pos_tables = ktab_ref.shape[0] > 1

    def project(r0):
        rows = slice(r0, r0 + PHASE1_CHAIN)
        h = (_rms(x_ref[rows, :], g_pre_ref[...]) * (1.0 + scale_a) + shift_a).astype(BF16)
        proj = _dot_nt(h, w_attn_ref[...])
        gate = _dot_nt(h, w_gate_ref[...])
        return proj, gate

    def expand_qkv(r0, proj):
        rows = slice(r0, r0 + PHASE1_CHAIN)
        tab_rows = rows if per_pos_tables else slice(None)
        cq = _rms(proj[:, 0:Q_LORA], g_q_ref[...])
        q = _dot(cq.astype(BF16), w_uq_ref[...])
        qtab = qtab_ref[tab_rows, :]
        q_ref[rows, :] = jnp.concatenate(
            [(q[:, s * HEAD_SLOT:(s + 1) * HEAD_SLOT] * qtab).astype(BF16)
             for s in range(MLA_HEADS)], axis=1)
        ckv_n = _rms(proj[:, Q_LORA:Q_LORA + KV_LORA], g_kv_ref[...])
        rope_slab = proj[:, Q_LORA + KV_LORA:Q_LORA + KV_LORA + LANES]
        if emit_cache:
            ckv_ref[rows, :] = ckv_n
            kr_ref[rows, :] = rope_slab[:, 0:QK_ROPE]
        ckv_b = ckv_n.astype(BF16)
        kin = jnp.concatenate([ckv_b, (rope_slab * ktab_ref[tab_rows, :]).astype(BF16)], axis=1)
        k_ref[:, rows] = _dot_nt(w_k_ref[...], kin).astype(BF16)
        v_ref[rows, :] = _dot(ckv_b, w_uv_ref[...]).astype(BF16)

    def gating_unit(r0, gate):
        vv = _gelu(gate[:, SGU_WIDTH:2 * SGU_WIDTH])
        mu = jnp.mean(vv, axis=-1, keepdims=True)
        vc = vv - mu
        var = jnp.mean(vc * vc, axis=-1, keepdims=True)
        vn = (vc * lax.rsqrt(var + EPS) * g_sgu_ref[...] + beta_sgu_ref[...]).astype(BF16)
        n_chunks = PHASE1_CHAIN // CHUNK
        for j in range(SGU_WIDTH // LANES):
            lanes = slice(j * LANES, (j + 1) * LANES)
            rhs = jnp.concatenate(
                [vn[n * CHUNK:(n + 1) * CHUNK, lanes] for n in range(n_chunks)], axis=1)
            o = _dot(w_sgu_ref[j], rhs)
            u = _gelu(gate[:, lanes])
            bias = bias_ref[:, lanes]
            for n in range(n_chunks):
                even = o[0:CHUNK, n * LANES:(n + 1) * LANES]
                odd = o[CHUNK:2 * CHUNK, n * LANES:(n + 1) * LANES]
                mixed = jnp.where(low_half, even, odd) + bias
                sgu_ref[r0 + n * CHUNK:r0 + (n + 1) * CHUNK, lanes] = (
                    u[n * CHUNK:(n + 1) * CHUNK, :] * mixed).astype(BF16)

    starts = list(range(0, tm, PHASE1_CHAIN))
    ahead = project(starts[0])
    for i, r0 in enumerate(starts):
        proj, gate = ahead
        if i + 1 < len(starts):
            ahead = project(starts[i + 1])
        expand_qkv(r0, proj)
        gating_unit(r0, gate)


def _const_spec(shape):
    nd = len(shape)
    return pl.BlockSpec(shape, lambda *_: (0,) * nd, pipeline_mode=pl.Buffered(1))


def _phase1(x2d, mod3, mod_row_fn, ktab, qtab, tab_fn, wts, emit_cache):
    n_tok = x2d.shape[0]
    tm = TOKEN_TILE
    grid = (n_tok // tm,)
    tab_block = (ktab.shape[0] if ktab.shape[0] == 1 else tm, LANES)
    in_specs = [
        pl.BlockSpec((tm, D_MODEL), lambda i: (i, 0)),
        _const_spec((MOD_ROWS, N_MOD * D_MODEL)),
        pl.BlockSpec(tab_block, lambda i: (tab_fn(i), 0)),
        pl.BlockSpec(tab_block, lambda i: (tab_fn(i), 0)),
        _const_spec((1, D_MODEL)),
        _const_spec((ATTN_PROJ, D_MODEL)),
        _const_spec((2 * SGU_WIDTH, D_MODEL)),
        _const_spec((1, Q_LORA)),
        _const_spec((Q_LORA, QK_WIDTH)),
        _const_spec((1, KV_LORA)),
        _const_spec((QK_WIDTH, 2 * LANES)),
        _const_spec((KV_LORA, MLA_WIDTH)),
        _const_spec((1, SGU_WIDTH)),
        _const_spec((1, SGU_WIDTH)),
        _const_spec((SGU_WIDTH // LANES, 2 * CHUNK, CHUNK)),
        _const_spec((CHUNK, SGU_WIDTH)),
    ]
    out_shape = [
        jax.ShapeDtypeStruct((n_tok, QK_WIDTH), BF16),
        jax.ShapeDtypeStruct((QK_WIDTH, n_tok), BF16),
        jax.ShapeDtypeStruct((n_tok, MLA_WIDTH), BF16),
        jax.ShapeDtypeStruct((n_tok, SGU_WIDTH), BF16),
    ]
    out_specs = [
        pl.BlockSpec((tm, QK_WIDTH), lambda i: (i, 0)),
        pl.BlockSpec((QK_WIDTH, tm), lambda i: (0, i)),
        pl.BlockSpec((tm, MLA_WIDTH), lambda i: (i, 0)),
        pl.BlockSpec((tm, SGU_WIDTH), lambda i: (i, 0)),
    ]
    extra_inputs = []
    if emit_cache:
        out_shape += [jax.ShapeDtypeStruct((n_tok, KV_LORA), F32),
                      jax.ShapeDtypeStruct((n_tok, QK_ROPE), F32)]
        out_specs += [pl.BlockSpec((tm, KV_LORA), lambda i: (i, 0)),
                      pl.BlockSpec((tm, QK_ROPE), lambda i: (i, 0))]
        for w in (wts["w_o32"], wts["w_ff1_32"], wts["w_ff2_32"]):
            rows, cols = w.shape
            blk = (rows // grid[0], cols)
            extra_inputs.append(w)
            in_specs.append(pl.BlockSpec(blk, lambda i: (i, 0)))
            out_shape.append(jax.ShapeDtypeStruct(w.shape, BF16))
            out_specs.append(pl.BlockSpec(blk, lambda i: (i, 0)))
    return pl.pallas_call(
        functools.partial(_phase1_kernel, emit_cache, mod_row_fn),
        out_shape=out_shape,
        grid=grid,
        in_specs=in_specs,
        out_specs=out_specs,
        compiler_params=pltpu.CompilerParams(
            dimension_semantics=("arbitrary",), vmem_limit_bytes=VMEM_LIMIT),
        name="phase1_ctx" if emit_cache else "phase1_lat",
    )(x2d, mod3, ktab, qtab, wts["g_attn_pre"], wts["w_attn"], wts["w_gate"], wts["g_q"], wts["w_uq"],
      wts["g_kv"], wts["w_k"], wts["w_uv"], wts["g_sgu"], wts["beta_sgu"], wts["w_sgu"],
      wts["bias_sgu"], *extra_inputs)


def _cache_kv_kernel(kin_ref, w_k_ref, w_uv_ref, k_ref, v_ref):
    kin = kin_ref[...].astype(BF16)
    k_ref[...] = _dot_nt(w_k_ref[...], kin).astype(BF16)
    v_ref[...] = _dot(kin[:, 0:KV_LORA], w_uv_ref[...]).astype(BF16)


def _cache_kv(kin2d, wts):
    n_tok = kin2d.shape[0]
    tm = n_tok
    return pl.pallas_call(
        _cache_kv_kernel,
        out_shape=[jax.ShapeDtypeStruct((QK_WIDTH, n_tok), BF16),
                   jax.ShapeDtypeStruct((n_tok, MLA_WIDTH), BF16)],
        grid=(n_tok // tm,),
        in_specs=[
            pl.BlockSpec((tm, 2 * LANES), lambda i: (i, 0)),
            _const_spec((QK_WIDTH, 2 * LANES)),
            _const_spec((KV_LORA, MLA_WIDTH)),
        ],
        out_specs=[pl.BlockSpec((QK_WIDTH, tm), lambda i: (0, i)),
                   pl.BlockSpec((tm, MLA_WIDTH), lambda i: (i, 0))],
        compiler_params=pltpu.CompilerParams(
            dimension_semantics=("arbitrary",), vmem_limit_bytes=VMEM_LIMIT),
        name="cache_kv",
    )(kin2d, wts["w_k"], wts["w_uv"])


def _interleave(*gens):
    results = [None] * len(gens)
    live = list(range(len(gens)))
    while live:
        for i in list(live):
            try:
                next(gens[i])
            except StopIteration as stop:
                results[i] = stop.value
                live.remove(i)
    return results


def _run(gen):
    return _interleave(gen)[0]


def _attend(q_ref, kv_views, r0, low_half):
    def head_scores(hd):
        qh = q_ref[r0:r0 + SUB_ROWS, hd * HEAD_SLOT:(hd + 1) * HEAD_SLOT]
        return [_dot(qh, k_view(hd * HEAD_SLOT, HEAD_SLOT)) for k_view, _ in kv_views]

    def head_probs(scores):
        m = scores[0].max(axis=-1, keepdims=True)
        for s in scores[1:]:
            m = jnp.maximum(m, s.max(axis=-1, keepdims=True))
        probs = []
        denom = None
        for s in scores:
            p = jnp.exp2(s - m)
            ps = p.sum(axis=-1, keepdims=True)
            denom = ps if denom is None else denom + ps
            probs.append(p.astype(BF16))
        return probs, denom

    def head_values(hd, probs, denom):
        slab = hd // 2
        acc = None
        for (_, v_view), p in zip(kv_views, probs):
            pv = _dot(p, v_view(slab * LANES, LANES))
            acc = pv if acc is None else acc + pv
        return acc / denom

    pair_out = []
    head_out = None
    ahead = [head_scores(hd) for hd in range(QK_AHEAD)]
    for hd in range(MLA_HEADS):
        if hd + QK_AHEAD < MLA_HEADS:
            ahead.append(head_scores(hd + QK_AHEAD))
        o = head_values(hd, *head_probs(ahead.pop(0)))
        if hd % 2 == 0:
            head_out = o
        else:
            pair_out.append(jnp.where(low_half, head_out, o).astype(BF16))
            yield
    return jnp.concatenate(pair_out, axis=1)


def _phase2_kernel(n_kv, shared_kv, mod_row_fn, x_ref, mod_ref, q_ref, sgu_ref, *refs):
    kv_refs = refs[:2 * n_kv]
    (w_o_ref, g_post_ref, g_fpre_ref, g_fpost_ref, w_ff1_ref, w_ff2_ref, o_ref) = refs[2 * n_kv:]
    tq = x_ref.shape[0]
    low_half = lax.broadcasted_iota(jnp.int32, (SUB_ROWS, LANES), 1) < V_HEAD
    _, _, gate_a, shift_f, scale_f, gate_f = _mod_rows(mod_ref, mod_row_fn(pl.program_id(0)))
    def attention(r0):
        views = []
        for t in range(n_kv):
            k_ref, v_ref = kv_refs[2 * t], kv_refs[2 * t + 1]
            if shared_kv:
                views.append((lambda c, w, k_ref=k_ref: k_ref[c:c + w, :],
                              lambda c, w, v_ref=v_ref: v_ref[:, c:c + w]))
            else:
                views.append((lambda c, w, k_ref=k_ref: k_ref[c:c + w, r0:r0 + SUB_ROWS],
                              lambda c, w, v_ref=v_ref: v_ref[r0:r0 + SUB_ROWS, c:c + w]))
        return _attend(q_ref, views, r0, low_half)

    def mixer_proj(r0, attn):
        mix_in = jnp.concatenate([attn, sgu_ref[r0:r0 + SUB_ROWS, :]], axis=1)
        return _dot(mix_in, w_o_ref[...])

    def mixer_norm(r0, mix):
        x1 = x_ref[r0:r0 + SUB_ROWS, :] + gate_a * _rms(mix, g_post_ref[...])
        h = (_rms(x1, g_fpre_ref[...]) * (1.0 + scale_f) + shift_f).astype(BF16)
        return x1, h

    def ffn(h):
        f = None
        for c in range(D_FF // FF_CHUNK):
            hid = jnp.maximum(_dot(h, w_ff1_ref[:, c * FF_CHUNK:(c + 1) * FF_CHUNK]), 0.0)
            part = _dot((hid * hid).astype(BF16), w_ff2_ref[c * FF_CHUNK:(c + 1) * FF_CHUNK, :])
            f = part if f is None else f + part
            yield
        return f

    def finish(r0, x1, f):
        o_ref[r0:r0 + SUB_ROWS, :] = x1 + gate_f * _rms(f, g_fpost_ref[...])

    starts = [a * SUB_ROWS for a in range(tq // SUB_ROWS)]
    x1h = {r0: mixer_norm(r0, mixer_proj(r0, _run(attention(r0)))) for r0 in starts}
    for r0 in starts:
        finish(r0, x1h[r0][0], _run(ffn(x1h[r0][1])))


def _phase2(x2d, mod3, mod_row_fn, q, sgu, kv_list, kv_rows, shared_kv, tq, n_outer, n_inner, wts,
            name):
    n_kv = len(kv_list) // 2
    in_specs = [
        pl.BlockSpec((tq, D_MODEL), lambda b, i: (b * n_inner + i, 0)),
        _const_spec((MOD_ROWS, N_MOD * D_MODEL)),
        pl.BlockSpec((tq, QK_WIDTH), lambda b, i: (b * n_inner + i, 0)),
        pl.BlockSpec((tq, SGU_WIDTH), lambda b, i: (b * n_inner + i, 0)),
    ]
    for t in range(n_kv):
        in_specs.append(pl.BlockSpec((QK_WIDTH, kv_rows[t]), lambda b, i: (0, b)))
        in_specs.append(pl.BlockSpec((kv_rows[t], MLA_WIDTH), lambda b, i: (b, 0)))
    in_specs += [
        _const_spec((D_MODEL, D_MODEL)),
        _const_spec((1, D_MODEL)),
        _const_spec((1, D_MODEL)),
        _const_spec((1, D_MODEL)),
        _const_spec((D_MODEL, D_FF)),
        _const_spec((D_FF, D_MODEL)),
    ]
    return pl.pallas_call(
        functools.partial(_phase2_kernel, n_kv, shared_kv, mod_row_fn),
        out_shape=jax.ShapeDtypeStruct(x2d.shape, F32),
        grid=(n_outer, n_inner),
        in_specs=in_specs,
        out_specs=pl.BlockSpec((tq, D_MODEL), lambda b, i: (b * n_inner + i, 0)),
        compiler_params=pltpu.CompilerParams(
            dimension_semantics=("arbitrary", "arbitrary"), vmem_limit_bytes=VMEM_LIMIT),
        name=name,
    )(x2d, mod3, q, sgu, *kv_list, wts["w_o"], wts["g_attn_post"], wts["g_ffn_pre"],
      wts["g_ffn_post"], wts["w_ff1"], wts["w_ff2"])


def _pair_swap(w):
    shp = w.shape
    return w.reshape(shp[:-1] + (shp[-1] // 2, 2))[..., ::-1].reshape(shp)


def _prepare_weights(g_attn_pre, g_attn_post, g_q, w_uq, g_kv, w_ukv, w_sgu, b_sgu, g_sgu,
                     beta_sgu, w_o, g_ffn_pre, g_ffn_post, w_ff1, w_ff2):
    w_uq_h = w_uq.reshape(Q_LORA, MLA_HEADS, QK_NOPE + QK_ROPE)
    w_uq_ext = jnp.concatenate(
        [w_uq_h, _pair_swap(w_uq_h[..., QK_NOPE:])], axis=-1).reshape(Q_LORA, QK_WIDTH).astype(BF16)

    w_ukv_h = w_ukv.reshape(KV_LORA, MLA_HEADS, QK_NOPE + V_HEAD)
    w_uk_slots = jnp.concatenate(
        [w_ukv_h[..., :QK_NOPE], jnp.zeros((KV_LORA, MLA_HEADS, HEAD_SLOT - QK_NOPE), F32)],
        axis=-1).reshape(KV_LORA, QK_WIDTH)
    eye = jnp.eye(QK_ROPE, dtype=F32)
    zeros_rope = jnp.zeros((QK_ROPE, QK_ROPE), F32)
    zeros_nope = jnp.zeros((QK_ROPE, QK_NOPE), F32)
    to_lo = jnp.tile(jnp.concatenate([zeros_nope, eye, zeros_rope], axis=1), (1, MLA_HEADS))
    to_hi = jnp.tile(jnp.concatenate([zeros_nope, zeros_rope, eye], axis=1), (1, MLA_HEADS))
    w_k = jnp.concatenate([w_uk_slots, to_lo, to_lo, to_hi, to_hi], axis=0).T.astype(BF16)
    w_uv = w_ukv_h[..., QK_NOPE:].reshape(KV_LORA, MLA_WIDTH).astype(BF16)

    w_sgu_pair = w_sgu.reshape(SGU_HEADS // 2, 2 * CHUNK, CHUNK).astype(BF16)
    bias_sgu = jnp.repeat(b_sgu.T, SGU_HEAD_DIM, axis=1)
    row = lambda a: a.reshape(1, -1)
    return {
        "g_attn_pre": row(g_attn_pre), "g_attn_post": row(g_attn_post),
        "g_q": row(g_q), "w_uq": w_uq_ext, "g_kv": row(g_kv), "w_k": w_k, "w_uv": w_uv,
        "g_sgu": row(g_sgu), "beta_sgu": row(beta_sgu), "w_sgu": w_sgu_pair, "bias_sgu": bias_sgu,
        "w_o32": w_o, "g_ffn_pre": row(g_ffn_pre), "g_ffn_post": row(g_ffn_post),
        "w_ff1_32": w_ff1, "w_ff2_32": w_ff2,
    }


def _rope_tables(n_tok):
    rows = n_tok // GRID_W
    row = np.repeat(np.arange(rows), GRID_W).astype(np.float32)
    col = np.tile(np.arange(GRID_W), rows).astype(np.float32)
    freqs = (1.0 / (ROPE_BASE ** (np.arange(AXIS_PAIRS, dtype=np.float32) / AXIS_PAIRS))).astype(
        np.float32)
    ang = np.concatenate([row[:, None] * freqs, col[:, None] * freqs], axis=-1)
    cos = np.repeat(np.cos(ang), 2, axis=1)
    sin = np.repeat(np.sin(ang), 2, axis=1) * np.tile(np.array([-1.0, 1.0], np.float32), QK_ROPE // 2)
    ktab = np.concatenate([cos, sin, cos, sin], axis=1)
    qtab = QUERY_SCALE * np.concatenate([np.ones((n_tok, QK_NOPE), np.float32), cos, sin], axis=1)
    return jnp.asarray(ktab, F32), jnp.asarray(qtab, F32)


def kernel(x_prompt, x_sample, cache_ckv, cache_krope, c, c_ctx, w_mod, b_mod, g_attn_pre,
           g_attn_post, w_in, g_q, w_uq, g_kv, w_ukv, w_sgu, b_sgu, g_sgu, beta_sgu, w_o,
           g_ffn_pre, g_ffn_post, w_ff1, w_ff2):
    batch, seq, _ = x_prompt.shape
    dec_batch, dec_seq, _ = x_sample.shape
    past_len = cache_ckv.shape[2]
    depth = w_mod.shape[0]
    assert depth == 1

    wts = _prepare_weights(g_attn_pre[0], g_attn_post[0], g_q[0], w_uq[0], g_kv[0],
                           w_ukv[0], w_sgu[0], b_sgu[0], g_sgu[0], beta_sgu[0], w_o[0],
                           g_ffn_pre[0], g_ffn_post[0], w_ff1[0], w_ff2[0])

    cond8 = jnp.concatenate(
        [c_ctx[None, :], c, jnp.zeros((MOD_ROWS - 1 - dec_batch, D_MODEL), F32)], axis=0)
    mod3, w_attn, w_gate = _modulation(cond8, w_mod[0], b_mod[0].reshape(1, -1), w_in[0].T)
    wts = dict(wts, w_attn=w_attn, w_gate=w_gate)

    ktab_ctx = jnp.asarray(np.arange(LANES)[None, :] < QK_ROPE, F32)
    qtab_ctx = jnp.asarray(QUERY_SCALE * (np.arange(LANES)[None, :] < QK_NOPE + QK_ROPE), F32)
    xp2d = x_prompt.reshape(batch * seq, D_MODEL)
    q_c, k_c, v_c, sgu_c, ckv_c, kr_c, w_o16, w_ff1_16, w_ff2_16 = _phase1(
        xp2d, mod3, lambda i: 0, ktab_ctx, qtab_ctx, lambda i: 0, wts, True)
    wts = dict(wts, w_o=w_o16, w_ff1=w_ff1_16, w_ff2=w_ff2_16)
    assert seq == SUB_ROWS and dec_seq % PHASE2_ROWS_LAT == 0
    y_prompt = _phase2(xp2d, mod3, lambda b: 0, q_c, sgu_c, [k_c, v_c], [PHASE2_ROWS_CTX], False,
                       PHASE2_ROWS_CTX, batch * seq // PHASE2_ROWS_CTX, 1, wts, "phase2_ctx")

    ktab_lat, qtab_lat = _rope_tables(dec_seq)
    tiles_per_seq = dec_seq // TOKEN_TILE
    xs2d = x_sample.reshape(dec_batch * dec_seq, D_MODEL)
    q_l, k_l, v_l, sgu_l = _phase1(
        xs2d, mod3, lambda i: 1 + i // tiles_per_seq, ktab_lat, qtab_lat,
        lambda i: i % tiles_per_seq, wts, False)
    kr_p = cache_krope[:, 0].reshape(dec_batch * past_len, QK_ROPE)
    pad = jnp.zeros_like(kr_p)
    kin_p = jnp.concatenate(
        [cache_ckv[:, 0].reshape(dec_batch * past_len, KV_LORA), kr_p, pad, kr_p, pad], axis=1)
    k_p, v_p = _cache_kv(kin_p, wts)
    y_sample = _phase2(xs2d, mod3, lambda b: 1 + b, q_l, sgu_l, [k_p, v_p, k_l, v_l],
                       [past_len, dec_seq], True, PHASE2_ROWS_LAT, dec_batch,
                       dec_seq // PHASE2_ROWS_LAT, wts, "phase2_lat")

    return (y_prompt.reshape(batch, seq, D_MODEL),
            y_sample.reshape(dec_batch, dec_seq, D_MODEL),
            ckv_c.reshape(batch, 1, seq, KV_LORA),
            kr_c.reshape(batch, 1, seq, QK_ROPE))
```

```python
import functools
import math

import jax
import jax.numpy as jnp
import numpy as np
from jax import lax
from jax.experimental import pallas as pl
from jax.experimental.pallas import tpu as pltpu

D_MODEL = 1024
GRID_W = 64
MLA_HEADS = 8
QK_NOPE = 64
QK_ROPE = 32
V_HEAD = 64
Q_LORA = 256
KV_LORA = 128
MLA_WIDTH = MLA_HEADS * V_HEAD
SGU_HEADS = 8
SGU_WIDTH = D_MODEL - MLA_WIDTH
SGU_HEAD_DIM = SGU_WIDTH // SGU_HEADS
CHUNK = 128
D_FF = 4 * D_MODEL
AXIS_PAIRS = QK_ROPE // 4
ROPE_BASE = 10000.0
EPS = 1e-6
N_MOD = 6
ATTN_SCALE = (QK_NOPE + QK_ROPE) ** -0.5
QUERY_SCALE = ATTN_SCALE * math.log2(math.e)

LANES = 128
HEAD_SLOT = LANES
QK_WIDTH = MLA_HEADS * HEAD_SLOT
ATTN_PROJ = Q_LORA + KV_LORA + LANES
MOD_ROWS = 8
MOD_K_ROWS = 128
TOKEN_TILE = 512
PHASE1_CHAIN = 512
CTX_ROWS = 512
SUB_ROWS = 256
PHASE2_ROWS_CTX = 512
PHASE2_ROWS_LAT = 512
QK_AHEAD = 1
FF_CHUNK = 1024
VMEM_LIMIT = 56 * 1024 * 1024

BF16 = jnp.bfloat16
F32 = jnp.float32


def _dot(a, b):
    return jnp.dot(a, b, preferred_element_type=F32)


def _dot_nt(a, b):
    return lax.dot_general(a, b, (((1,), (1,)), ((), ())), preferred_element_type=F32)


def _rms(x, g):
    return x * lax.rsqrt(jnp.mean(x * x, axis=-1, keepdims=True) + EPS) * g


def _gelu(x):
    inner = math.sqrt(2.0 / math.pi) * (x + 0.044715 * (x * x * x))
    return x * (0.5 * (1.0 + jnp.tanh(inner)))


def _mod_kernel(cond_a_ref, cond_b_ref, w_a_ref, w_b_ref, b_ref, w_in_attn_ref, w_in_gate_ref,
                o_ref, w_attn_ref, w_gate_ref):
    @pl.when(pl.program_id(0) == 0)
    def _():
        o_ref[...] = jnp.broadcast_to(b_ref[...], o_ref.shape)

    def part(cond_ref, w_ref):
        cnd = cond_ref[...]
        act = cnd * (1.0 / (1.0 + jnp.exp(-cnd)))
        return _dot(act.astype(BF16), w_ref[...].astype(BF16))

    o_ref[...] += part(cond_a_ref, w_a_ref) + part(cond_b_ref, w_b_ref)

    w_gate_ref[...] = w_in_gate_ref[...].astype(BF16)

    @pl.when(pl.program_id(0) == 0)
    def _():
        rope0 = Q_LORA + KV_LORA
        kr = w_in_attn_ref[rope0:rope0 + QK_ROPE, :]
        row = lax.broadcasted_iota(jnp.int32, kr.shape, 0)
        kr_sw = jnp.where(row % 2 == 0, pltpu.roll(kr, QK_ROPE - 1, 0), pltpu.roll(kr, 1, 0))
        w_attn_ref[0:rope0 + QK_ROPE, :] = w_in_attn_ref[...].astype(BF16)
        w_attn_ref[rope0 + QK_ROPE:ATTN_PROJ, :] = jnp.concatenate(
            [kr_sw, kr, kr_sw], axis=0).astype(BF16)


def _modulation(cond8, w_mod, b_mod, w_in_t):
    n = w_mod.shape[1]
    tk = MOD_K_ROWS
    nk = D_MODEL // tk // 2
    attn_rows = Q_LORA + KV_LORA + QK_ROPE
    gate_rows = 2 * SGU_WIDTH // nk
    return pl.pallas_call(
        _mod_kernel,
        out_shape=[jax.ShapeDtypeStruct((MOD_ROWS, n), F32),
                   jax.ShapeDtypeStruct((ATTN_PROJ, D_MODEL), BF16),
                   jax.ShapeDtypeStruct((2 * SGU_WIDTH, D_MODEL), BF16)],
        grid=(nk,),
        in_specs=[
            pl.BlockSpec((MOD_ROWS, tk), lambda k: (0, k)),
            pl.BlockSpec((MOD_ROWS, tk), lambda k: (0, k + nk)),
            pl.BlockSpec((tk, n), lambda k: (k, 0)),
            pl.BlockSpec((tk, n), lambda k: (k + nk, 0)),
            pl.BlockSpec((1, n), lambda k: (0, 0)),
            pl.BlockSpec((attn_rows, D_MODEL), lambda k: (0, 0)),
            pl.BlockSpec((pl.Element(gate_rows), pl.Element(D_MODEL)),
                         lambda k: (pl.multiple_of(attn_rows + k * gate_rows, QK_ROPE), 0)),
        ],
        out_specs=[pl.BlockSpec((MOD_ROWS, n), lambda k: (0, 0)),
                   pl.BlockSpec((ATTN_PROJ, D_MODEL), lambda k: (0, 0)),
                   pl.BlockSpec((gate_rows, D_MODEL), lambda k: (k, 0))],
        compiler_params=pltpu.CompilerParams(
            dimension_semantics=("arbitrary",), vmem_limit_bytes=VMEM_LIMIT),
        name="modulation",
    )(cond8, cond8, w_mod, w_mod, b_mod, w_in_t, w_in_t)


def _mod_rows(mod_ref, row):
    vec = mod_ref[pl.ds(row, 1), :]
    return [vec[:, j * D_MODEL:(j + 1) * D_MODEL] for j in range(N_MOD)]


def _phase1_kernel(emit_cache, cast_weights, mod_row_fn, x_ref, mod_ref, ktab_ref, qtab_ref,
                   g_pre_ref, w_attn_ref, w_gate_ref,
                   g_q_ref, w_uq_ref, g_kv_ref, w_k_ref, w_uv_ref, g_sgu_ref, beta_sgu_ref,
                   w_sgu_ref, bias_ref, *rest):
    rest = list(rest)
    cast_in = [rest.pop(0) for _ in range(3)] if cast_weights else []
    q_ref, k_ref, v_ref, sgu_ref = rest[:4]
    rest = rest[4:]
    if emit_cache:
        ckv_ref, kr_ref = rest[:2]
        rest = rest[2:]
    for src_ref, dst_ref in zip(cast_in, rest):
        dst_ref[...] = src_ref[...].astype(BF16)
    tm = x_ref.shape[0]
    mod = _mod_rows(mod_ref, mod_row_fn(pl.program_id(0)))
    shift_a, scale_a = mod[0], mod[1]
    low_half = lax.broadcasted_iota(jnp.int32, (CHUNK, LANES), 1) < SGU_HEAD_DIM
    per_pos_tables = ktab_ref.shape[0] > 1

    def project(r0):
        rows = slice(r0, r0 + PHASE1_CHAIN)
        h = (_rms(x_ref[rows, :], g_pre_ref[...]) * (1.0 + scale_a) + shift_a).astype(BF16)
        proj = _dot_nt(h, w_attn_ref[...])
        gate = _dot_nt(h, w_gate_ref[...])
        return proj, gate

    def expand_qkv(r0, proj):
        rows = slice(r0, r0 + PHASE1_CHAIN)
        tab_rows = rows if per_pos_tables else slice(None)
        cq = _rms(proj[:, 0:Q_LORA], g_q_ref[...])
        q = _dot(cq.astype(BF16), w_uq_ref[...])
        qtab = qtab_ref[tab_rows, :]
        q_ref[rows, :] = jnp.concatenate(
            [(q[:, s * HEAD_SLOT:(s + 1) * HEAD_SLOT] * qtab).astype(BF16)
             for s in range(MLA_HEADS)], axis=1)
        ckv_n = _rms(proj[:, Q_LORA:Q_LORA + KV_LORA], g_kv_ref[...])
        rope_slab = proj[:, Q_LORA + KV_LORA:Q_LORA + KV_LORA + LANES]
        if emit_cache:
            ckv_ref[rows, :] = ckv_n
            kr_ref[rows, :] = rope_slab[:, 0:QK_ROPE]
        ckv_b = ckv_n.astype(BF16)
        kin = jnp.concatenate([ckv_b, (rope_slab * ktab_ref[tab_rows, :]).astype(BF16)], axis=1)
        k_ref[:, rows] = _dot_nt(w_k_ref[...], kin).astype(BF16)
        v_ref[rows, :] = _dot(ckv_b, w_uv_ref[...]).astype(BF16)

    def gating_unit(r0, gate):
        vv = _gelu(gate[:, SGU_WIDTH:2 * SGU_WIDTH])
        mu = jnp.mean(vv, axis=-1, keepdims=True)
        vc = vv - mu
        var = jnp.mean(vc * vc, axis=-1, keepdims=True)
        vn = (vc * lax.rsqrt(var + EPS) * g_sgu_ref[...] + beta_sgu_ref[...]).astype(BF16)
        n_chunks = PHASE1_CHAIN // CHUNK
        for j in range(SGU_WIDTH // LANES):
            lanes = slice(j * LANES, (j + 1) * LANES)
            rhs = jnp.concatenate(
                [vn[n * CHUNK:(n + 1) * CHUNK, lanes] for n in range(n_chunks)], axis=1)
            o = _dot(w_sgu_ref[j], rhs)
            u = _gelu(gate[:, lanes])
            bias = bias_ref[:, lanes]
            for n in range(n_chunks):
                even = o[0:CHUNK, n * LANES:(n + 1) * LANES]
                odd = o[CHUNK:2 * CHUNK, n * LANES:(n + 1) * LANES]
                mixed = jnp.where(low_half, even, odd) + bias
                sgu_ref[r0 + n * CHUNK:r0 + (n + 1) * CHUNK, lanes] = (
                    u[n * CHUNK:(n + 1) * CHUNK, :] * mixed).astype(BF16)

    starts = list(range(0, tm, PHASE1_CHAIN))
    ahead = project(starts[0])
    for i, r0 in enumerate(starts):
        proj, gate = ahead
        if i + 1 < len(starts):
            ahead = project(starts[i + 1])
        expand_qkv(r0, proj)
        gating_unit(r0, gate)


def _const_spec(shape):
    nd = len(shape)
    return pl.BlockSpec(shape, lambda *_: (0,) * nd, pipeline_mode=pl.Buffered(1))


def _phase1_weight_specs():
    return [
        _const_spec((1, D_MODEL)),
        _const_spec((ATTN_PROJ, D_MODEL)),
        _const_spec((2 * SGU_WIDTH, D_MODEL)),
        _const_spec((1, Q_LORA)),
        _const_spec((Q_LORA, QK_WIDTH)),
        _const_spec((1, KV_LORA)),
        _const_spec((QK_WIDTH, 2 * LANES)),
        _const_spec((KV_LORA, MLA_WIDTH)),
        _const_spec((1, SGU_WIDTH)),
        _const_spec((1, SGU_WIDTH)),
        _const_spec((SGU_WIDTH // LANES, 2 * CHUNK, CHUNK)),
        _const_spec((CHUNK, SGU_WIDTH)),
    ]


def _phase1_weights(wts):
    return [wts[k] for k in ("g_attn_pre", "w_attn", "w_gate", "g_q", "w_uq", "g_kv", "w_k", "w_uv",
                             "g_sgu", "beta_sgu", "w_sgu", "bias_sgu")]


def _phase2_weight_specs():
    return [
        _const_spec((D_MODEL, D_MODEL)),
        _const_spec((1, D_MODEL)),
        _const_spec((1, D_MODEL)),
        _const_spec((1, D_MODEL)),
        _const_spec((D_MODEL, D_FF)),
        _const_spec((D_FF, D_MODEL)),
    ]


def _phase2_weights(wts):
    return [wts[k] for k in ("w_o", "g_attn_post", "g_ffn_pre", "g_ffn_post", "w_ff1", "w_ff2")]


def _phase1(x2d, mod3, mod_row_fn, ktab, qtab, tab_fn, wts, emit_cache, cast_weights):
    n_tok = x2d.shape[0]
    tm = TOKEN_TILE
    grid = (n_tok // tm,)
    tab_block = (ktab.shape[0] if ktab.shape[0] == 1 else tm, LANES)
    in_specs = [
        pl.BlockSpec((tm, D_MODEL), lambda i: (i, 0)),
        _const_spec((MOD_ROWS, N_MOD * D_MODEL)),
        pl.BlockSpec(tab_block, lambda i: (tab_fn(i), 0)),
        pl.BlockSpec(tab_block, lambda i: (tab_fn(i), 0)),
    ] + _phase1_weight_specs()
    out_shape = [
        jax.ShapeDtypeStruct((n_tok, QK_WIDTH), BF16),
        jax.ShapeDtypeStruct((QK_WIDTH, n_tok), BF16),
        jax.ShapeDtypeStruct((n_tok, MLA_WIDTH), BF16),
        jax.ShapeDtypeStruct((n_tok, SGU_WIDTH), BF16),
    ]
    out_specs = [
        pl.BlockSpec((tm, QK_WIDTH), lambda i: (i, 0)),
        pl.BlockSpec((QK_WIDTH, tm), lambda i: (0, i)),
        pl.BlockSpec((tm, MLA_WIDTH), lambda i: (i, 0)),
        pl.BlockSpec((tm, SGU_WIDTH), lambda i: (i, 0)),
    ]
    extra_inputs = []
    if emit_cache:
        out_shape += [jax.ShapeDtypeStruct((n_tok, KV_LORA), F32),
                      jax.ShapeDtypeStruct((n_tok, QK_ROPE), F32)]
        out_specs += [pl.BlockSpec((tm, KV_LORA), lambda i: (i, 0)),
                      pl.BlockSpec((tm, QK_ROPE), lambda i: (i, 0))]
    if cast_weights:
        for w in (wts["w_o32"], wts["w_ff1_32"], wts["w_ff2_32"]):
            rows, cols = w.shape
            blk = (rows // grid[0], cols)
            extra_inputs.append(w)
            in_specs.append(pl.BlockSpec(blk, lambda i: (i, 0)))
            out_shape.append(jax.ShapeDtypeStruct(w.shape, BF16))
            out_specs.append(pl.BlockSpec(blk, lambda i: (i, 0)))
    return pl.pallas_call(
        functools.partial(_phase1_kernel, emit_cache, cast_weights, mod_row_fn),
        out_shape=out_shape,
        grid=grid,
        in_specs=in_specs,
        out_specs=out_specs,
        compiler_params=pltpu.CompilerParams(
            dimension_semantics=("arbitrary",), vmem_limit_bytes=VMEM_LIMIT),
        name="phase1_ctx" if emit_cache else "phase1_lat",
    )(x2d, mod3, ktab, qtab, *_phase1_weights(wts), *extra_inputs)


def _ctx_layer_kernel(*refs):
    n1, n2 = len(_phase1_weight_specs()), len(_phase2_weight_specs())
    x_ref, mod_ref, ktab_ref, qtab_ref = refs[:4]
    p1_w = refs[4:4 + n1]
    p2_w = refs[4 + n1:4 + n1 + n2]
    y_ref, ckv_ref, kr_ref, q_scr, k_scr, v_scr, sgu_scr = refs[4 + n1 + n2:]
    ctx_row = lambda _: 0
    _phase1_kernel(True, False, ctx_row, x_ref, mod_ref, ktab_ref, qtab_ref, *p1_w,
                   q_scr, k_scr, v_scr, sgu_scr, ckv_ref, kr_ref)
    _phase2_kernel(1, False, ctx_row, x_ref, mod_ref, q_scr, sgu_scr, k_scr, v_scr, *p2_w, y_ref)


def _ctx_layer(x2d, mod3, ktab, qtab, wts):
    n_tok = x2d.shape[0]
    tm = CTX_ROWS
    in_specs = [
        pl.BlockSpec((tm, D_MODEL), lambda i: (i, 0)),
        _const_spec((MOD_ROWS, N_MOD * D_MODEL)),
        _const_spec((1, LANES)),
        _const_spec((1, LANES)),
    ] + _phase1_weight_specs() + _phase2_weight_specs()
    return pl.pallas_call(
        _ctx_layer_kernel,
        out_shape=[jax.ShapeDtypeStruct((n_tok, D_MODEL), F32),
                   jax.ShapeDtypeStruct((n_tok, KV_LORA), F32),
                   jax.ShapeDtypeStruct((n_tok, QK_ROPE), F32)],
        grid=(n_tok // tm,),
        in_specs=in_specs,
        out_specs=[pl.BlockSpec((tm, D_MODEL), lambda i: (i, 0)),
                   pl.BlockSpec((tm, KV_LORA), lambda i: (i, 0)),
                   pl.BlockSpec((tm, QK_ROPE), lambda i: (i, 0))],
        scratch_shapes=[pltpu.VMEM((tm, QK_WIDTH), BF16),
                        pltpu.VMEM((QK_WIDTH, tm), BF16),
                        pltpu.VMEM((tm, MLA_WIDTH), BF16),
                        pltpu.VMEM((tm, SGU_WIDTH), BF16)],
        compiler_params=pltpu.CompilerParams(
            dimension_semantics=("arbitrary",), vmem_limit_bytes=VMEM_LIMIT),
        name="ctx_layer",
    )(x2d, mod3, ktab, qtab, *_phase1_weights(wts), *_phase2_weights(wts))


def _cache_kv_kernel(kin_ref, w_k_ref, w_uv_ref, k_ref, v_ref):
    kin = kin_ref[...].astype(BF16)
    k_ref[...] = _dot_nt(w_k_ref[...], kin).astype(BF16)
    v_ref[...] = _dot(kin[:, 0:KV_LORA], w_uv_ref[...]).astype(BF16)


def _cache_kv(kin2d, wts):
    n_tok = kin2d.shape[0]
    tm = n_tok
    return pl.pallas_call(
        _cache_kv_kernel,
        out_shape=[jax.ShapeDtypeStruct((QK_WIDTH, n_tok), BF16),
                   jax.ShapeDtypeStruct((n_tok, MLA_WIDTH), BF16)],
        grid=(n_tok // tm,),
        in_specs=[
            pl.BlockSpec((tm, 2 * LANES), lambda i: (i, 0)),
            _const_spec((QK_WIDTH, 2 * LANES)),
            _const_spec((KV_LORA, MLA_WIDTH)),
        ],
        out_specs=[pl.BlockSpec((QK_WIDTH, tm), lambda i: (0, i)),
                   pl.BlockSpec((tm, MLA_WIDTH), lambda i: (i, 0))],
        compiler_params=pltpu.CompilerParams(
            dimension_semantics=("arbitrary",), vmem_limit_bytes=VMEM_LIMIT),
        name="cache_kv",
    )(kin2d, wts["w_k"], wts["w_uv"])


def _interleave(*gens):
    results = [None] * len(gens)
    live = list(range(len(gens)))
    while live:
        for i in list(live):
            try:
                next(gens[i])
            except StopIteration as stop:
                results[i] = stop.value
                live.remove(i)
    return results


def _run(gen):
    return _interleave(gen)[0]


def _attend(q_ref, kv_views, r0, low_half):
    def head_scores(hd):
        qh = q_ref[r0:r0 + SUB_ROWS, hd * HEAD_SLOT:(hd + 1) * HEAD_SLOT]
        return [_dot(qh, k_view(hd * HEAD_SLOT, HEAD_SLOT)) for k_view, _ in kv_views]

    def head_probs(scores):
        m = scores[0].max(axis=-1, keepdims=True)
        for s in scores[1:]:
            m = jnp.maximum(m, s.max(axis=-1, keepdims=True))
        probs = []
        denom = None
        for s in scores:
            p = jnp.exp2(s - m)
            ps = p.sum(axis=-1, keepdims=True)
            denom = ps if denom is None else denom + ps
            probs.append(p.astype(BF16))
        return probs, denom

    def head_values(hd, probs, denom):
        slab = hd // 2
        acc = None
        for (_, v_view), p in zip(kv_views, probs):
            pv = _dot(p, v_view(slab * LANES, LANES))
            acc = pv if acc is None else acc + pv
        return acc / denom

    pair_out = []
    head_out = None
    ahead = [head_scores(hd) for hd in range(QK_AHEAD)]
    for hd in range(MLA_HEADS):
        if hd + QK_AHEAD < MLA_HEADS:
            ahead.append(head_scores(hd + QK_AHEAD))
        o = head_values(hd, *head_probs(ahead.pop(0)))
        if hd % 2 == 0:
            head_out = o
        else:
            pair_out.append(jnp.where(low_half, head_out, o).astype(BF16))
            yield
    return jnp.concatenate(pair_out, axis=1)


def _phase2_kernel(n_kv, shared_kv, mod_row_fn, x_ref, mod_ref, q_ref, sgu_ref, *refs):
    kv_refs = refs[:2 * n_kv]
    (w_o_ref, g_post_ref, g_fpre_ref, g_fpost_ref, w_ff1_ref, w_ff2_ref, o_ref) = refs[2 * n_kv:]
    tq = x_ref.shape[0]
    low_half = lax.broadcasted_iota(jnp.int32, (SUB_ROWS, LANES), 1) < V_HEAD
    _, _, gate_a, shift_f, scale_f, gate_f = _mod_rows(mod_ref, mod_row_fn(pl.program_id(0)))
    def attention(r0):
        views = []
        for t in range(n_kv):
            k_ref, v_ref = kv_refs[2 * t], kv_refs[2 * t + 1]
            if shared_kv:
                views.append((lambda c, w, k_ref=k_ref: k_ref[c:c + w, :],
                              lambda c, w, v_ref=v_ref: v_ref[:, c:c + w]))
            else:
                views.append((lambda c, w, k_ref=k_ref: k_ref[c:c + w, r0:r0 + SUB_ROWS],
                              lambda c, w, v_ref=v_ref: v_ref[r0:r0 + SUB_ROWS, c:c + w]))
        return _attend(q_ref, views, r0, low_half)

    def mixer_proj(r0, attn):
        mix_in = jnp.concatenate([attn, sgu_ref[r0:r0 + SUB_ROWS, :]], axis=1)
        return _dot(mix_in, w_o_ref[...])

    def mixer_norm(r0, mix):
        x1 = x_ref[r0:r0 + SUB_ROWS, :] + gate_a * _rms(mix, g_post_ref[...])
        h = (_rms(x1, g_fpre_ref[...]) * (1.0 + scale_f) + shift_f).astype(BF16)
        return x1, h

    def ffn(h):
        f = None
        for c in range(D_FF // FF_CHUNK):
            hid = jnp.maximum(_dot(h, w_ff1_ref[:, c * FF_CHUNK:(c + 1) * FF_CHUNK]), 0.0)
            part = _dot((hid * hid).astype(BF16), w_ff2_ref[c * FF_CHUNK:(c + 1) * FF_CHUNK, :])
            f = part if f is None else f + part
            yield
        return f

    def finish(r0, x1, f):
        o_ref[r0:r0 + SUB_ROWS, :] = x1 + gate_f * _rms(f, g_fpost_ref[...])

    starts = [a * SUB_ROWS for a in range(tq // SUB_ROWS)]
    x1h = {r0: mixer_norm(r0, mixer_proj(r0, _run(attention(r0)))) for r0 in starts}
    for r0 in starts:
        finish(r0, x1h[r0][0], _run(ffn(x1h[r0][1])))


def _phase2(x2d, mod3, mod_row_fn, q, sgu, kv_list, kv_rows, shared_kv, tq, n_outer, n_inner, wts,
            name):
    n_kv = len(kv_list) // 2
    in_specs = [
        pl.BlockSpec((tq, D_MODEL), lambda b, i: (b * n_inner + i, 0)),
        _const_spec((MOD_ROWS, N_MOD * D_MODEL)),
        pl.BlockSpec((tq, QK_WIDTH), lambda b, i: (b * n_inner + i, 0)),
        pl.BlockSpec((tq, SGU_WIDTH), lambda b, i: (b * n_inner + i, 0)),
    ]
    for t in range(n_kv):
        in_specs.append(pl.BlockSpec((QK_WIDTH, kv_rows[t]), lambda b, i: (0, b)))
        in_specs.append(pl.BlockSpec((kv_rows[t], MLA_WIDTH), lambda b, i: (b, 0)))
    in_specs += [
        _const_spec((D_MODEL, D_MODEL)),
        _const_spec((1, D_MODEL)),
        _const_spec((1, D_MODEL)),
        _const_spec((1, D_MODEL)),
        _const_spec((D_MODEL, D_FF)),
        _const_spec((D_FF, D_MODEL)),
    ]
    return pl.pallas_call(
        functools.partial(_phase2_kernel, n_kv, shared_kv, mod_row_fn),
        out_shape=jax.ShapeDtypeStruct(x2d.shape, F32),
        grid=(n_outer, n_inner),
        in_specs=in_specs,
        out_specs=pl.BlockSpec((tq, D_MODEL), lambda b, i: (b * n_inner + i, 0)),
        compiler_params=pltpu.CompilerParams(
            dimension_semantics=("arbitrary", "arbitrary"), vmem_limit_bytes=VMEM_LIMIT),
        name=name,
    )(x2d, mod3, q, sgu, *kv_list, wts["w_o"], wts["g_attn_post"], wts["g_ffn_pre"],
      wts["g_ffn_post"], wts["w_ff1"], wts["w_ff2"])


def _pair_swap(w):
    shp = w.shape
    return w.reshape(shp[:-1] + (shp[-1] // 2, 2))[..., ::-1].reshape(shp)


def _prepare_weights(g_attn_pre, g_attn_post, g_q, w_uq, g_kv, w_ukv, w_sgu, b_sgu, g_sgu,
                     beta_sgu, w_o, g_ffn_pre, g_ffn_post, w_ff1, w_ff2):
    w_uq_h = w_uq.reshape(Q_LORA, MLA_HEADS, QK_NOPE + QK_ROPE)
    w_uq_ext = jnp.concatenate(
        [w_uq_h, _pair_swap(w_uq_h[..., QK_NOPE:])], axis=-1).reshape(Q_LORA, QK_WIDTH).astype(BF16)

    w_ukv_h = w_ukv.reshape(KV_LORA, MLA_HEADS, QK_NOPE + V_HEAD)
    w_uk_slots = jnp.concatenate(
        [w_ukv_h[..., :QK_NOPE], jnp.zeros((KV_LORA, MLA_HEADS, HEAD_SLOT - QK_NOPE), F32)],
        axis=-1).reshape(KV_LORA, QK_WIDTH)
    eye = jnp.eye(QK_ROPE, dtype=F32)
    zeros_rope = jnp.zeros((QK_ROPE, QK_ROPE), F32)
    zeros_nope = jnp.zeros((QK_ROPE, QK_NOPE), F32)
    to_lo = jnp.tile(jnp.concatenate([zeros_nope, eye, zeros_rope], axis=1), (1, MLA_HEADS))
    to_hi = jnp.tile(jnp.concatenate([zeros_nope, zeros_rope, eye], axis=1), (1, MLA_HEADS))
    w_k = jnp.concatenate([w_uk_slots, to_lo, to_lo, to_hi, to_hi], axis=0).T.astype(BF16)
    w_uv = w_ukv_h[..., QK_NOPE:].reshape(KV_LORA, MLA_WIDTH).astype(BF16)

    w_sgu_pair = w_sgu.reshape(SGU_HEADS // 2, 2 * CHUNK, CHUNK).astype(BF16)
    bias_sgu = jnp.repeat(b_sgu.T, SGU_HEAD_DIM, axis=1)
    row = lambda a: a.reshape(1, -1)
    return {
        "g_attn_pre": row(g_attn_pre), "g_attn_post": row(g_attn_post),
        "g_q": row(g_q), "w_uq": w_uq_ext, "g_kv": row(g_kv), "w_k": w_k, "w_uv": w_uv,
        "g_sgu": row(g_sgu), "beta_sgu": row(beta_sgu), "w_sgu": w_sgu_pair, "bias_sgu": bias_sgu,
        "w_o32": w_o, "g_ffn_pre": row(g_ffn_pre), "g_ffn_post": row(g_ffn_post),
        "w_ff1_32": w_ff1, "w_ff2_32": w_ff2,
    }


def _rope_tables(n_tok):
    rows = n_tok // GRID_W
    row = np.repeat(np.arange(rows), GRID_W).astype(np.float32)
    col = np.tile(np.arange(GRID_W), rows).astype(np.float32)
    freqs = (1.0 / (ROPE_BASE ** (np.arange(AXIS_PAIRS, dtype=np.float32) / AXIS_PAIRS))).astype(
        np.float32)
    ang = np.concatenate([row[:, None] * freqs, col[:, None] * freqs], axis=-1)
    cos = np.repeat(np.cos(ang), 2, axis=1)
    sin = np.repeat(np.sin(ang), 2, axis=1) * np.tile(np.array([-1.0, 1.0], np.float32), QK_ROPE // 2)
    ktab = np.concatenate([cos, sin, cos, sin], axis=1)
    qtab = QUERY_SCALE * np.concatenate([np.ones((n_tok, QK_NOPE), np.float32), cos, sin], axis=1)
    return jnp.asarray(ktab, F32), jnp.asarray(qtab, F32)


def kernel(x_prompt, x_sample, cache_ckv, cache_krope, c, c_ctx, w_mod, b_mod, g_attn_pre,
           g_attn_post, w_in, g_q, w_uq, g_kv, w_ukv, w_sgu, b_sgu, g_sgu, beta_sgu, w_o,
           g_ffn_pre, g_ffn_post, w_ff1, w_ff2):
    batch, seq, _ = x_prompt.shape
    dec_batch, dec_seq, _ = x_sample.shape
    past_len = cache_ckv.shape[2]
    depth = w_mod.shape[0]
    assert depth == 1

    wts = _prepare_weights(g_attn_pre[0], g_attn_post[0], g_q[0], w_uq[0], g_kv[0],
                           w_ukv[0], w_sgu[0], b_sgu[0], g_sgu[0], beta_sgu[0], w_o[0],
                           g_ffn_pre[0], g_ffn_post[0], w_ff1[0], w_ff2[0])

    cond8 = jnp.concatenate(
        [c_ctx[None, :], c, jnp.zeros((MOD_ROWS - 1 - dec_batch, D_MODEL), F32)], axis=0)
    mod3, w_attn, w_gate = _modulation(cond8, w_mod[0], b_mod[0].reshape(1, -1), w_in[0].T)
    wts = dict(wts, w_attn=w_attn, w_gate=w_gate)

    assert seq == SUB_ROWS and dec_seq % PHASE2_ROWS_LAT == 0

    ktab_lat, qtab_lat = _rope_tables(dec_seq)
    tiles_per_seq = dec_seq // TOKEN_TILE
    xs2d = x_sample.reshape(dec_batch * dec_seq, D_MODEL)
    q_l, k_l, v_l, sgu_l, w_o16, w_ff1_16, w_ff2_16 = _phase1(
        xs2d, mod3, lambda i: 1 + i // tiles_per_seq, ktab_lat, qtab_lat,
        lambda i: i % tiles_per_seq, wts, False, True)
    wts = dict(wts, w_o=w_o16, w_ff1=w_ff1_16, w_ff2=w_ff2_16)

    ktab_ctx = jnp.asarray(np.arange(LANES)[None, :] < QK_ROPE, F32)
    qtab_ctx = jnp.asarray(QUERY_SCALE * (np.arange(LANES)[None, :] < QK_NOPE + QK_ROPE), F32)
    xp2d = x_prompt.reshape(batch * seq, D_MODEL)
    y_prompt, ckv_c, kr_c = _ctx_layer(xp2d, mod3, ktab_ctx, qtab_ctx, wts)

    kr_p = cache_krope[:, 0].reshape(dec_batch * past_len, QK_ROPE)
    pad = jnp.zeros_like(kr_p)
    kin_p = jnp.concatenate(
        [cache_ckv[:, 0].reshape(dec_batch * past_len, KV_LORA), kr_p, pad, kr_p, pad], axis=1)
    k_p, v_p = _cache_kv(kin_p, wts)
    y_sample = _phase2(xs2d, mod3, lambda b: 1 + b, q_l, sgu_l, [k_p, v_p, k_l, v_l],
                       [past_len, dec_seq], True, PHASE2_ROWS_LAT, dec_batch,
                       dec_seq // PHASE2_ROWS_LAT, wts, "phase2_lat")

    return (y_prompt.reshape(batch, seq, D_MODEL),
            y_sample.reshape(dec_batch, dec_seq, D_MODEL),
            ckv_c.reshape(batch, 1, seq, KV_LORA),
            kr_c.reshape(batch, 1, seq, QK_ROPE))
```

```python
import functools
import math

import jax
import jax.numpy as jnp
import numpy as np
from jax import lax
from jax.experimental import pallas as pl
from jax.experimental.pallas import tpu as pltpu

D_MODEL = 1024
GRID_W = 64
MLA_HEADS = 8
QK_NOPE = 64
QK_ROPE = 32
V_HEAD = 64
Q_LORA = 256
KV_LORA = 128
MLA_WIDTH = MLA_HEADS * V_HEAD
SGU_HEADS = 8
SGU_WIDTH = D_MODEL - MLA_WIDTH
SGU_HEAD_DIM = SGU_WIDTH // SGU_HEADS
CHUNK = 128
D_FF = 4 * D_MODEL
AXIS_PAIRS = QK_ROPE // 4
ROPE_BASE = 10000.0
EPS = 1e-6
N_MOD = 6
ATTN_SCALE = (QK_NOPE + QK_ROPE) ** -0.5
QUERY_SCALE = ATTN_SCALE * math.log2(math.e)

LANES = 128
HEAD_SLOT = LANES
QK_WIDTH = MLA_HEADS * HEAD_SLOT
ATTN_PROJ = Q_LORA + KV_LORA + LANES
MOD_ROWS = 8
MOD_K_ROWS = 128
TOKEN_TILE = 1024
PHASE1_CHAIN = 512
CTX_ROWS = 512
SUB_ROWS = 256
PHASE2_ROWS_CTX = 512
PHASE2_ROWS_LAT = 512
QK_AHEAD = 1
FF_CHUNK = 1024
VMEM_LIMIT = 56 * 1024 * 1024

BF16 = jnp.bfloat16
F32 = jnp.float32


def _dot(a, b):
    return jnp.dot(a, b, preferred_element_type=F32)


def _dot_nt(a, b):
    return lax.dot_general(a, b, (((1,), (1,)), ((), ())), preferred_element_type=F32)


def _rms(x, g):
    return x * lax.rsqrt(jnp.mean(x * x, axis=-1, keepdims=True) + EPS) * g


def _gelu(x):
    inner = math.sqrt(2.0 / math.pi) * (x + 0.044715 * (x * x * x))
    return x * (0.5 * (1.0 + jnp.tanh(inner)))


def _mod_kernel(cond_a_ref, cond_b_ref, w_a_ref, w_b_ref, b_ref, w_in_attn_ref, w_in_gate_ref,
                o_ref, w_attn_ref, w_gate_ref):
    @pl.when(pl.program_id(0) == 0)
    def _():
        o_ref[...] = jnp.broadcast_to(b_ref[...], o_ref.shape)

    def part(cond_ref, w_ref):
        cnd = cond_ref[...]
        act = cnd * (1.0 / (1.0 + jnp.exp(-cnd)))
        return _dot(act.astype(BF16), w_ref[...].astype(BF16))

    o_ref[...] += part(cond_a_ref, w_a_ref) + part(cond_b_ref, w_b_ref)

    w_gate_ref[...] = w_in_gate_ref[...].astype(BF16)

    @pl.when(pl.program_id(0) == 0)
    def _():
        rope0 = Q_LORA + KV_LORA
        kr = w_in_attn_ref[rope0:rope0 + QK_ROPE, :]
        row = lax.broadcasted_iota(jnp.int32, kr.shape, 0)
        kr_sw = jnp.where(row % 2 == 0, pltpu.roll(kr, QK_ROPE - 1, 0), pltpu.roll(kr, 1, 0))
        w_attn_ref[0:rope0 + QK_ROPE, :] = w_in_attn_ref[...].astype(BF16)
        w_attn_ref[rope0 + QK_ROPE:ATTN_PROJ, :] = jnp.concatenate(
            [kr_sw, kr, kr_sw], axis=0).astype(BF16)


def _modulation(cond8, w_mod, b_mod, w_in_t):
    n = w_mod.shape[1]
    tk = MOD_K_ROWS
    nk = D_MODEL // tk // 2
    attn_rows = Q_LORA + KV_LORA + QK_ROPE
    gate_rows = 2 * SGU_WIDTH // nk
    return pl.pallas_call(
        _mod_kernel,
        out_shape=[jax.ShapeDtypeStruct((MOD_ROWS, n), F32),
                   jax.ShapeDtypeStruct((ATTN_PROJ, D_MODEL), BF16),
                   jax.ShapeDtypeStruct((2 * SGU_WIDTH, D_MODEL), BF16)],
        grid=(nk,),
        in_specs=[
            pl.BlockSpec((MOD_ROWS, tk), lambda k: (0, k)),
            pl.BlockSpec((MOD_ROWS, tk), lambda k: (0, k + nk)),
            pl.BlockSpec((tk, n), lambda k: (k, 0)),
            pl.BlockSpec((tk, n), lambda k: (k + nk, 0)),
            pl.BlockSpec((1, n), lambda k: (0, 0)),
            pl.BlockSpec((attn_rows, D_MODEL), lambda k: (0, 0)),
            pl.BlockSpec((pl.Element(gate_rows), pl.Element(D_MODEL)),
                         lambda k: (pl.multiple_of(attn_rows + k * gate_rows, QK_ROPE), 0)),
        ],
        out_specs=[pl.BlockSpec((MOD_ROWS, n), lambda k: (0, 0)),
                   pl.BlockSpec((ATTN_PROJ, D_MODEL), lambda k: (0, 0)),
                   pl.BlockSpec((gate_rows, D_MODEL), lambda k: (k, 0))],
        compiler_params=pltpu.CompilerParams(
            dimension_semantics=("arbitrary",), vmem_limit_bytes=VMEM_LIMIT),
        name="modulation",
    )(cond8, cond8, w_mod, w_mod, b_mod, w_in_t, w_in_t)


def _mod_rows(mod_ref, row):
    vec = mod_ref[pl.ds(row, 1), :]
    return [vec[:, j * D_MODEL:(j + 1) * D_MODEL] for j in range(N_MOD)]


def _phase1_kernel(emit_cache, cast_weights, mod_row_fn, x_ref, mod_ref, ktab_ref, qtab_ref,
                   g_pre_ref, w_attn_ref, w_gate_ref,
                   g_q_ref, w_uq_ref, g_kv_ref, w_k_ref, w_uv_ref, g_sgu_ref, beta_sgu_ref,
                   w_sgu_ref, bias_ref, *rest):
    rest = list(rest)
    cast_in = [rest.pop(0) for _ in range(2)] if cast_weights else []
    q_ref, k_ref, v_ref, sgu_ref = rest[:4]
    rest = rest[4:]
    if emit_cache:
        ckv_ref, kr_ref = rest[:2]
        rest = rest[2:]
    for src_ref, dst_ref in zip(cast_in, rest):
        dst_ref[...] = src_ref[...].astype(BF16)
    tm = x_ref.shape[0]
    mod = _mod_rows(mod_ref, mod_row_fn(pl.program_id(0)))
    shift_a, scale_a = mod[0], mod[1]
    low_half = lax.broadcasted_iota(jnp.int32, (CHUNK, LANES), 1) < SGU_HEAD_DIM
    per_pos_tables = ktab_ref.shape[0] > 1

    def project(r0):
        rows = slice(r0, r0 + PHASE1_CHAIN)
        h = (_rms(x_ref[rows, :], g_pre_ref[...]) * (1.0 + scale_a) + shift_a).astype(BF16)
        proj = _dot_nt(h, w_attn_ref[...])
        gate = _dot_nt(h, w_gate_ref[...])
        return proj, gate

    def expand_qkv(r0, proj):
        rows = slice(r0, r0 + PHASE1_CHAIN)
        tab_rows = rows if per_pos_tables else slice(None)
        cq = _rms(proj[:, 0:Q_LORA], g_q_ref[...])
        q = _dot(cq.astype(BF16), w_uq_ref[...])
        qtab = qtab_ref[tab_rows, :]
        q_ref[rows, :] = jnp.concatenate(
            [(q[:, s * HEAD_SLOT:(s + 1) * HEAD_SLOT] * qtab).astype(BF16)
             for s in range(MLA_HEADS)], axis=1)
        ckv_n = _rms(proj[:, Q_LORA:Q_LORA + KV_LORA], g_kv_ref[...])
        rope_slab = proj[:, Q_LORA + KV_LORA:Q_LORA + KV_LORA + LANES]
        if emit_cache:
            ckv_ref[rows, :] = ckv_n
            kr_ref[rows, :] = rope_slab[:, 0:QK_ROPE]
        ckv_b = ckv_n.astype(BF16)
        kin = jnp.concatenate([ckv_b, (rope_slab * ktab_ref[tab_rows, :]).astype(BF16)], axis=1)
        k_ref[:, rows] = _dot_nt(w_k_ref[...], kin).astype(BF16)
        v_ref[rows, :] = _dot(ckv_b, w_uv_ref[...]).astype(BF16)

    def gating_unit(r0, gate):
        vv = _gelu(gate[:, SGU_WIDTH:2 * SGU_WIDTH])
        mu = jnp.mean(vv, axis=-1, keepdims=True)
        vc = vv - mu
        var = jnp.mean(vc * vc, axis=-1, keepdims=True)
        vn = (vc * lax.rsqrt(var + EPS) * g_sgu_ref[...] + beta_sgu_ref[...]).astype(BF16)
        n_chunks = PHASE1_CHAIN // CHUNK
        for j in range(SGU_WIDTH // LANES):
            lanes = slice(j * LANES, (j + 1) * LANES)
            rhs = jnp.concatenate(
                [vn[n * CHUNK:(n + 1) * CHUNK, lanes] for n in range(n_chunks)], axis=1)
            o = _dot(w_sgu_ref[j], rhs)
            u = _gelu(gate[:, lanes])
            bias = bias_ref[:, lanes]
            for n in range(n_chunks):
                even = o[0:CHUNK, n * LANES:(n + 1) * LANES]
                odd = o[CHUNK:2 * CHUNK, n * LANES:(n + 1) * LANES]
                mixed = jnp.where(low_half, even, odd) + bias
                sgu_ref[r0 + n * CHUNK:r0 + (n + 1) * CHUNK, lanes] = (
                    u[n * CHUNK:(n + 1) * CHUNK, :] * mixed).astype(BF16)

    starts = list(range(0, tm, PHASE1_CHAIN))
    ahead = project(starts[0])
    for i, r0 in enumerate(starts):
        proj, gate = ahead
        if i + 1 < len(starts):
            ahead = project(starts[i + 1])
        expand_qkv(r0, proj)
        gating_unit(r0, gate)


def _const_spec(shape):
    nd = len(shape)
    return pl.BlockSpec(shape, lambda *_: (0,) * nd, pipeline_mode=pl.Buffered(1))


def _phase1_weight_specs():
    return [
        _const_spec((1, D_MODEL)),
        _const_spec((ATTN_PROJ, D_MODEL)),
        _const_spec((2 * SGU_WIDTH, D_MODEL)),
        _const_spec((1, Q_LORA)),
        _const_spec((Q_LORA, QK_WIDTH)),
        _const_spec((1, KV_LORA)),
        _const_spec((QK_WIDTH, 2 * LANES)),
        _const_spec((KV_LORA, MLA_WIDTH)),
        _const_spec((1, SGU_WIDTH)),
        _const_spec((1, SGU_WIDTH)),
        _const_spec((SGU_WIDTH // LANES, 2 * CHUNK, CHUNK)),
        _const_spec((CHUNK, SGU_WIDTH)),
    ]


def _phase1_weights(wts):
    return [wts[k] for k in ("g_attn_pre", "w_attn", "w_gate", "g_q", "w_uq", "g_kv", "w_k", "w_uv",
                             "g_sgu", "beta_sgu", "w_sgu", "bias_sgu")]


def _phase2_weight_specs():
    return [
        _const_spec((D_MODEL, D_MODEL)),
        _const_spec((1, D_MODEL)),
        _const_spec((1, D_MODEL)),
        _const_spec((1, D_MODEL)),
        _const_spec((D_MODEL, D_FF)),
        _const_spec((D_FF, D_MODEL)),
    ]


def _phase2_weights(wts):
    return [wts[k] for k in ("w_o", "g_attn_post", "g_ffn_pre", "g_ffn_post", "w_ff1", "w_ff2")]


def _phase1(x2d, mod3, mod_row_fn, ktab, qtab, tab_fn, wts, emit_cache, cast_weights):
    n_tok = x2d.shape[0]
    tm = TOKEN_TILE
    grid = (n_tok // tm,)
    tab_block = (ktab.shape[0] if ktab.shape[0] == 1 else tm, LANES)
    in_specs = [
        pl.BlockSpec((tm, D_MODEL), lambda i: (i, 0)),
        _const_spec((MOD_ROWS, N_MOD * D_MODEL)),
        pl.BlockSpec(tab_block, lambda i: (tab_fn(i), 0)),
        pl.BlockSpec(tab_block, lambda i: (tab_fn(i), 0)),
    ] + _phase1_weight_specs()
    out_shape = [
        jax.ShapeDtypeStruct((n_tok, QK_WIDTH), BF16),
        jax.ShapeDtypeStruct((QK_WIDTH, n_tok), BF16),
        jax.ShapeDtypeStruct((n_tok, MLA_WIDTH), BF16),
        jax.ShapeDtypeStruct((n_tok, SGU_WIDTH), BF16),
    ]
    out_specs = [
        pl.BlockSpec((tm, QK_WIDTH), lambda i: (i, 0)),
        pl.BlockSpec((QK_WIDTH, tm), lambda i: (0, i)),
        pl.BlockSpec((tm, MLA_WIDTH), lambda i: (i, 0)),
        pl.BlockSpec((tm, SGU_WIDTH), lambda i: (i, 0)),
    ]
    extra_inputs = []
    if emit_cache:
        out_shape += [jax.ShapeDtypeStruct((n_tok, KV_LORA), F32),
                      jax.ShapeDtypeStruct((n_tok, QK_ROPE), F32)]
        out_specs += [pl.BlockSpec((tm, KV_LORA), lambda i: (i, 0)),
                      pl.BlockSpec((tm, QK_ROPE), lambda i: (i, 0))]
    if cast_weights:
        for w in (wts["w_o32"], wts["w_ff1_32"]):
            rows, cols = w.shape
            blk = (rows // grid[0], cols)
            extra_inputs.append(w)
            in_specs.append(pl.BlockSpec(blk, lambda i: (i, 0)))
            out_shape.append(jax.ShapeDtypeStruct(w.shape, BF16))
            out_specs.append(pl.BlockSpec(blk, lambda i: (i, 0)))
    return pl.pallas_call(
        functools.partial(_phase1_kernel, emit_cache, cast_weights, mod_row_fn),
        out_shape=out_shape,
        grid=grid,
        in_specs=in_specs,
        out_specs=out_specs,
        compiler_params=pltpu.CompilerParams(
            dimension_semantics=("arbitrary",), vmem_limit_bytes=VMEM_LIMIT),
        name="phase1_ctx" if emit_cache else "phase1_lat",
    )(x2d, mod3, ktab, qtab, *_phase1_weights(wts), *extra_inputs)


def _ctx_layer_kernel(*refs):
    n1, n2 = len(_phase1_weight_specs()), len(_phase2_weight_specs())
    x_ref, mod_ref, ktab_ref, qtab_ref = refs[:4]
    p1_w = refs[4:4 + n1]
    p2_w = refs[4 + n1:4 + n1 + n2]
    y_ref, ckv_ref, kr_ref, q_scr, k_scr, v_scr, sgu_scr = refs[4 + n1 + n2:]
    ctx_row = lambda _: 0
    _phase1_kernel(True, False, ctx_row, x_ref, mod_ref, ktab_ref, qtab_ref, *p1_w,
                   q_scr, k_scr, v_scr, sgu_scr, ckv_ref, kr_ref)
    _phase2_kernel(1, False, ctx_row, x_ref, mod_ref, q_scr, sgu_scr, k_scr, v_scr, *p2_w, y_ref)


def _ctx_layer(x2d, mod3, ktab, qtab, wts):
    n_tok = x2d.shape[0]
    tm = CTX_ROWS
    in_specs = [
        pl.BlockSpec((tm, D_MODEL), lambda i: (i, 0)),
        _const_spec((MOD_ROWS, N_MOD * D_MODEL)),
        _const_spec((1, LANES)),
        _const_spec((1, LANES)),
    ] + _phase1_weight_specs() + _phase2_weight_specs()
    return pl.pallas_call(
        _ctx_layer_kernel,
        out_shape=[jax.ShapeDtypeStruct((n_tok, D_MODEL), F32),
                   jax.ShapeDtypeStruct((n_tok, KV_LORA), F32),
                   jax.ShapeDtypeStruct((n_tok, QK_ROPE), F32)],
        grid=(n_tok // tm,),
        in_specs=in_specs,
        out_specs=[pl.BlockSpec((tm, D_MODEL), lambda i: (i, 0)),
                   pl.BlockSpec((tm, KV_LORA), lambda i: (i, 0)),
                   pl.BlockSpec((tm, QK_ROPE), lambda i: (i, 0))],
        scratch_shapes=[pltpu.VMEM((tm, QK_WIDTH), BF16),
                        pltpu.VMEM((QK_WIDTH, tm), BF16),
                        pltpu.VMEM((tm, MLA_WIDTH), BF16),
                        pltpu.VMEM((tm, SGU_WIDTH), BF16)],
        compiler_params=pltpu.CompilerParams(
            dimension_semantics=("arbitrary",), vmem_limit_bytes=VMEM_LIMIT),
        name="ctx_layer",
    )(x2d, mod3, ktab, qtab, *_phase1_weights(wts), *_phase2_weights(wts))


def _cache_kv_kernel(kin_ref, w_k_ref, w_uv_ref, k_ref, v_ref):
    kin = kin_ref[...].astype(BF16)
    k_ref[...] = _dot_nt(w_k_ref[...], kin).astype(BF16)
    v_ref[...] = _dot(kin[:, 0:KV_LORA], w_uv_ref[...]).astype(BF16)


def _cache_kv(kin2d, wts):
    n_tok = kin2d.shape[0]
    tm = n_tok
    return pl.pallas_call(
        _cache_kv_kernel,
        out_shape=[jax.ShapeDtypeStruct((QK_WIDTH, n_tok), BF16),
                   jax.ShapeDtypeStruct((n_tok, MLA_WIDTH), BF16)],
        grid=(n_tok // tm,),
        in_specs=[
            pl.BlockSpec((tm, 2 * LANES), lambda i: (i, 0)),
            _const_spec((QK_WIDTH, 2 * LANES)),
            _const_spec((KV_LORA, MLA_WIDTH)),
        ],
        out_specs=[pl.BlockSpec((QK_WIDTH, tm), lambda i: (0, i)),
                   pl.BlockSpec((tm, MLA_WIDTH), lambda i: (i, 0))],
        compiler_params=pltpu.CompilerParams(
            dimension_semantics=("arbitrary",), vmem_limit_bytes=VMEM_LIMIT),
        name="cache_kv",
    )(kin2d, wts["w_k"], wts["w_uv"])


def _interleave(*gens):
    results = [None] * len(gens)
    live = list(range(len(gens)))
    while live:
        for i in list(live):
            try:
                next(gens[i])
            except StopIteration as stop:
                results[i] = stop.value
                live.remove(i)
    return results


def _run(gen):
    return _interleave(gen)[0]


def _attend(q_ref, kv_views, r0, low_half):
    def head_scores(hd):
        qh = q_ref[r0:r0 + SUB_ROWS, hd * HEAD_SLOT:(hd + 1) * HEAD_SLOT]
        return [_dot(qh, k_view(hd * HEAD_SLOT, HEAD_SLOT)) for k_view, _ in kv_views]

    def head_probs(scores):
        m = scores[0].max(axis=-1, keepdims=True)
        for s in scores[1:]:
            m = jnp.maximum(m, s.max(axis=-1, keepdims=True))
        probs = []
        denom = None
        for s in scores:
            p = jnp.exp2(s - m)
            ps = p.sum(axis=-1, keepdims=True)
            denom = ps if denom is None else denom + ps
            probs.append(p.astype(BF16))
        return probs, denom

    def head_values(hd, probs, denom):
        slab = hd // 2
        acc = None
        for (_, v_view), p in zip(kv_views, probs):
            pv = _dot(p, v_view(slab * LANES, LANES))
            acc = pv if acc is None else acc + pv
        return acc / denom

    pair_out = []
    head_out = None
    ahead = [head_scores(hd) for hd in range(QK_AHEAD)]
    for hd in range(MLA_HEADS):
        if hd + QK_AHEAD < MLA_HEADS:
            ahead.append(head_scores(hd + QK_AHEAD))
        o = head_values(hd, *head_probs(ahead.pop(0)))
        if hd % 2 == 0:
            head_out = o
        else:
            pair_out.append(jnp.where(low_half, head_out, o).astype(BF16))
            yield
    return jnp.concatenate(pair_out, axis=1)


def _phase2_kernel(n_kv, shared_kv, mod_row_fn, x_ref, mod_ref, q_ref, sgu_ref, *refs):
    kv_refs = refs[:2 * n_kv]
    (w_o_ref, g_post_ref, g_fpre_ref, g_fpost_ref, w_ff1_ref, w_ff2_ref, o_ref) = refs[2 * n_kv:]
    tq = x_ref.shape[0]
    low_half = lax.broadcasted_iota(jnp.int32, (SUB_ROWS, LANES), 1) < V_HEAD
    _, _, gate_a, shift_f, scale_f, gate_f = _mod_rows(mod_ref, mod_row_fn(pl.program_id(0)))
    def attention(r0):
        views = []
        for t in range(n_kv):
            k_ref, v_ref = kv_refs[2 * t], kv_refs[2 * t + 1]
            if shared_kv:
                views.append((lambda c, w, k_ref=k_ref: k_ref[c:c + w, :],
                              lambda c, w, v_ref=v_ref: v_ref[:, c:c + w]))
            else:
                views.append((lambda c, w, k_ref=k_ref: k_ref[c:c + w, r0:r0 + SUB_ROWS],
                              lambda c, w, v_ref=v_ref: v_ref[r0:r0 + SUB_ROWS, c:c + w]))
        return _attend(q_ref, views, r0, low_half)

    def mixer_proj(r0, attn):
        mix_in = jnp.concatenate([attn, sgu_ref[r0:r0 + SUB_ROWS, :]], axis=1)
        return _dot(mix_in, w_o_ref[...])

    def mixer_norm(r0, mix):
        x1 = x_ref[r0:r0 + SUB_ROWS, :] + gate_a * _rms(mix, g_post_ref[...])
        h = (_rms(x1, g_fpre_ref[...]) * (1.0 + scale_f) + shift_f).astype(BF16)
        return x1, h

    def ffn(h):
        f = None
        for c in range(D_FF // FF_CHUNK):
            hid = jnp.maximum(_dot(h, w_ff1_ref[:, c * FF_CHUNK:(c + 1) * FF_CHUNK]), 0.0)
            part = _dot((hid * hid).astype(BF16), w_ff2_ref[c * FF_CHUNK:(c + 1) * FF_CHUNK, :])
            f = part if f is None else f + part
            yield
        return f

    def finish(r0, x1, f):
        o_ref[r0:r0 + SUB_ROWS, :] = x1 + gate_f * _rms(f, g_fpost_ref[...])

    starts = [a * SUB_ROWS for a in range(tq // SUB_ROWS)]
    x1h = {r0: mixer_norm(r0, mixer_proj(r0, _run(attention(r0)))) for r0 in starts}
    for r0 in starts:
        finish(r0, x1h[r0][0], _run(ffn(x1h[r0][1])))


def _phase2(x2d, mod3, mod_row_fn, q, sgu, kv_list, kv_rows, shared_kv, tq, n_outer, n_inner, wts,
            name):
    n_kv = len(kv_list) // 2
    in_specs = [
        pl.BlockSpec((tq, D_MODEL), lambda b, i: (b * n_inner + i, 0)),
        _const_spec((MOD_ROWS, N_MOD * D_MODEL)),
        pl.BlockSpec((tq, QK_WIDTH), lambda b, i: (b * n_inner + i, 0)),
        pl.BlockSpec((tq, SGU_WIDTH), lambda b, i: (b * n_inner + i, 0)),
    ]
    for t in range(n_kv):
        in_specs.append(pl.BlockSpec((QK_WIDTH, kv_rows[t]), lambda b, i: (0, b)))
        in_specs.append(pl.BlockSpec((kv_rows[t], MLA_WIDTH), lambda b, i: (b, 0)))
    in_specs += [
        _const_spec((D_MODEL, D_MODEL)),
        _const_spec((1, D_MODEL)),
        _const_spec((1, D_MODEL)),
        _const_spec((1, D_MODEL)),
        _const_spec((D_MODEL, D_FF)),
        _const_spec((D_FF, D_MODEL)),
    ]
    return pl.pallas_call(
        functools.partial(_phase2_kernel, n_kv, shared_kv, mod_row_fn),
        out_shape=jax.ShapeDtypeStruct(x2d.shape, F32),
        grid=(n_outer, n_inner),
        in_specs=in_specs,
        out_specs=pl.BlockSpec((tq, D_MODEL), lambda b, i: (b * n_inner + i, 0)),
        compiler_params=pltpu.CompilerParams(
            dimension_semantics=("arbitrary", "arbitrary"), vmem_limit_bytes=VMEM_LIMIT),
        name=name,
    )(x2d, mod3, q, sgu, *kv_list, wts["w_o"], wts["g_attn_post"], wts["g_ffn_pre"],
      wts["g_ffn_post"], wts["w_ff1"], wts["w_ff2"])


def _pair_swap(w):
    shp = w.shape
    return w.reshape(shp[:-1] + (shp[-1] // 2, 2))[..., ::-1].reshape(shp)


def _prepare_weights(g_attn_pre, g_attn_post, g_q, w_uq, g_kv, w_ukv, w_sgu, b_sgu, g_sgu,
                     beta_sgu, w_o, g_ffn_pre, g_ffn_post, w_ff1, w_ff2):
    w_uq_h = w_uq.reshape(Q_LORA, MLA_HEADS, QK_NOPE + QK_ROPE)
    w_uq_ext = jnp.concatenate(
        [w_uq_h, _pair_swap(w_uq_h[..., QK_NOPE:])], axis=-1).reshape(Q_LORA, QK_WIDTH).astype(BF16)

    w_ukv_h = w_ukv.reshape(KV_LORA, MLA_HEADS, QK_NOPE + V_HEAD)
    w_uk_slots = jnp.concatenate(
        [w_ukv_h[..., :QK_NOPE], jnp.zeros((KV_LORA, MLA_HEADS, HEAD_SLOT - QK_NOPE), F32)],
        axis=-1).reshape(KV_LORA, QK_WIDTH)
    eye = jnp.eye(QK_ROPE, dtype=F32)
    zeros_rope = jnp.zeros((QK_ROPE, QK_ROPE), F32)
    zeros_nope = jnp.zeros((QK_ROPE, QK_NOPE), F32)
    to_lo = jnp.tile(jnp.concatenate([zeros_nope, eye, zeros_rope], axis=1), (1, MLA_HEADS))
    to_hi = jnp.tile(jnp.concatenate([zeros_nope, zeros_rope, eye], axis=1), (1, MLA_HEADS))
    w_k = jnp.concatenate([w_uk_slots, to_lo, to_lo, to_hi, to_hi], axis=0).T.astype(BF16)
    w_uv = w_ukv_h[..., QK_NOPE:].reshape(KV_LORA, MLA_WIDTH).astype(BF16)

    w_sgu_pair = w_sgu.reshape(SGU_HEADS // 2, 2 * CHUNK, CHUNK).astype(BF16)
    bias_sgu = jnp.repeat(b_sgu.T, SGU_HEAD_DIM, axis=1)
    row = lambda a: a.reshape(1, -1)
    return {
        "g_attn_pre": row(g_attn_pre), "g_attn_post": row(g_attn_post),
        "g_q": row(g_q), "w_uq": w_uq_ext, "g_kv": row(g_kv), "w_k": w_k, "w_uv": w_uv,
        "g_sgu": row(g_sgu), "beta_sgu": row(beta_sgu), "w_sgu": w_sgu_pair, "bias_sgu": bias_sgu,
        "w_o32": w_o, "g_ffn_pre": row(g_ffn_pre), "g_ffn_post": row(g_ffn_post),
        "w_ff1_32": w_ff1, "w_ff2_32": w_ff2,
    }


def _rope_tables(n_tok):
    rows = n_tok // GRID_W
    row = np.repeat(np.arange(rows), GRID_W).astype(np.float32)
    col = np.tile(np.arange(GRID_W), rows).astype(np.float32)
    freqs = (1.0 / (ROPE_BASE ** (np.arange(AXIS_PAIRS, dtype=np.float32) / AXIS_PAIRS))).astype(
        np.float32)
    ang = np.concatenate([row[:, None] * freqs, col[:, None] * freqs], axis=-1)
    cos = np.repeat(np.cos(ang), 2, axis=1)
    sin = np.repeat(np.sin(ang), 2, axis=1) * np.tile(np.array([-1.0, 1.0], np.float32), QK_ROPE // 2)
    ktab = np.concatenate([cos, sin, cos, sin], axis=1)
    qtab = QUERY_SCALE * np.concatenate([np.ones((n_tok, QK_NOPE), np.float32), cos, sin], axis=1)
    return jnp.asarray(ktab, F32), jnp.asarray(qtab, F32)


def kernel(x_prompt, x_sample, cache_ckv, cache_krope, c, c_ctx, w_mod, b_mod, g_attn_pre,
           g_attn_post, w_in, g_q, w_uq, g_kv, w_ukv, w_sgu, b_sgu, g_sgu, beta_sgu, w_o,
           g_ffn_pre, g_ffn_post, w_ff1, w_ff2):
    batch, seq, _ = x_prompt.shape
    dec_batch, dec_seq, _ = x_sample.shape
    past_len = cache_ckv.shape[2]
    depth = w_mod.shape[0]
    assert depth == 1

    wts = _prepare_weights(g_attn_pre[0], g_attn_post[0], g_q[0], w_uq[0], g_kv[0],
                           w_ukv[0], w_sgu[0], b_sgu[0], g_sgu[0], beta_sgu[0], w_o[0],
                           g_ffn_pre[0], g_ffn_post[0], w_ff1[0], w_ff2[0])

    cond8 = jnp.concatenate(
        [c_ctx[None, :], c, jnp.zeros((MOD_ROWS - 1 - dec_batch, D_MODEL), F32)], axis=0)
    mod3, w_attn, w_gate = _modulation(cond8, w_mod[0], b_mod[0].reshape(1, -1), w_in[0].T)
    wts = dict(wts, w_attn=w_attn, w_gate=w_gate)

    assert seq == SUB_ROWS and dec_seq % PHASE2_ROWS_LAT == 0

    ktab_lat, qtab_lat = _rope_tables(dec_seq)
    tiles_per_seq = dec_seq // TOKEN_TILE
    xs2d = x_sample.reshape(dec_batch * dec_seq, D_MODEL)
    q_l, k_l, v_l, sgu_l, w_o16, w_ff1_16 = _phase1(
        xs2d, mod3, lambda i: 1 + i // tiles_per_seq, ktab_lat, qtab_lat,
        lambda i: i % tiles_per_seq, wts, False, True)
    wts = dict(wts, w_o=w_o16, w_ff1=w_ff1_16, w_ff2=wts["w_ff2_32"].astype(BF16))

    ktab_ctx = jnp.asarray(np.arange(LANES)[None, :] < QK_ROPE, F32)
    qtab_ctx = jnp.asarray(QUERY_SCALE * (np.arange(LANES)[None, :] < QK_NOPE + QK_ROPE), F32)
    xp2d = x_prompt.reshape(batch * seq, D_MODEL)
    y_prompt, ckv_c, kr_c = _ctx_layer(xp2d, mod3, ktab_ctx, qtab_ctx, wts)

    kr_p = cache_krope[:, 0].reshape(dec_batch * past_len, QK_ROPE)
    pad = jnp.zeros_like(kr_p)
    kin_p = jnp.concatenate(
        [cache_ckv[:, 0].reshape(dec_batch * past_len, KV_LORA), kr_p, pad, kr_p, pad], axis=1)
    k_p, v_p = _cache_kv(kin_p, wts)
    y_sample = _phase2(xs2d, mod3, lambda b: 1 + b, q_l, sgu_l, [k_p, v_p, k_l, v_l],
                       [past_len, dec_seq], True, PHASE2_ROWS_LAT, dec_batch,
                       dec_seq // PHASE2_ROWS_LAT, wts, "phase2_lat")

    return (y_prompt.reshape(batch, seq, D_MODEL),
            y_sample.reshape(dec_batch, dec_seq, D_MODEL),
            ckv_c.reshape(batch, 1, seq, KV_LORA),
            kr_c.reshape(batch, 1, seq, QK_ROPE))
```

```python
import functools
import math

import jax
import jax.numpy as jnp
import numpy as np
from jax import lax
from jax.experimental import pallas as pl
from jax.experimental.pallas import tpu as pltpu

D_MODEL = 1024
GRID_W = 64
MLA_HEADS = 8
QK_NOPE = 64
QK_ROPE = 32
V_HEAD = 64
Q_LORA = 256
KV_LORA = 128
MLA_WIDTH = MLA_HEADS * V_HEAD
SGU_HEADS = 8
SGU_WIDTH = D_MODEL - MLA_WIDTH
SGU_HEAD_DIM = SGU_WIDTH // SGU_HEADS
CHUNK = 128
D_FF = 4 * D_MODEL
AXIS_PAIRS = QK_ROPE // 4
ROPE_BASE = 10000.0
EPS = 1e-6
N_MOD = 6
ATTN_SCALE = (QK_NOPE + QK_ROPE) ** -0.5
QUERY_SCALE = ATTN_SCALE * math.log2(math.e)

LANES = 128
HEAD_SLOT = LANES
QK_WIDTH = MLA_HEADS * HEAD_SLOT
ATTN_PROJ = Q_LORA + KV_LORA + LANES
MOD_ROWS = 8
MOD_K_ROWS = 128
TOKEN_TILE = 1024
PHASE1_CHAIN = 512
SUB_ROWS = 256
PHASE2_ROWS_CTX = 512
PHASE2_ROWS_LAT = 512
QK_AHEAD = 1
FF_CHUNK = 1024
VMEM_LIMIT = 56 * 1024 * 1024

BF16 = jnp.bfloat16
F32 = jnp.float32


def _dot(a, b):
    return jnp.dot(a, b, preferred_element_type=F32)


def _dot_nt(a, b):
    return lax.dot_general(a, b, (((1,), (1,)), ((), ())), preferred_element_type=F32)


def _rms(x, g):
    return x * lax.rsqrt(jnp.mean(x * x, axis=-1, keepdims=True) + EPS) * g


def _gelu(x):
    inner = math.sqrt(2.0 / math.pi) * (x + 0.044715 * (x * x * x))
    return x * (0.5 * (1.0 + jnp.tanh(inner)))


def _mod_kernel(cond_a_ref, cond_b_ref, w_a_ref, w_b_ref, b_ref, w_in_attn_ref, w_in_gate_ref,
                o_ref, w_attn_ref, w_gate_ref):
    @pl.when(pl.program_id(0) == 0)
    def _():
        o_ref[...] = jnp.broadcast_to(b_ref[...], o_ref.shape)

    def part(cond_ref, w_ref):
        cnd = cond_ref[...]
        act = cnd * (1.0 / (1.0 + jnp.exp(-cnd)))
        return _dot(act.astype(BF16), w_ref[...].astype(BF16))

    o_ref[...] += part(cond_a_ref, w_a_ref) + part(cond_b_ref, w_b_ref)

    w_gate_ref[...] = w_in_gate_ref[...].astype(BF16)

    @pl.when(pl.program_id(0) == 0)
    def _():
        rope0 = Q_LORA + KV_LORA
        kr = w_in_attn_ref[rope0:rope0 + QK_ROPE, :]
        row = lax.broadcasted_iota(jnp.int32, kr.shape, 0)
        kr_sw = jnp.where(row % 2 == 0, pltpu.roll(kr, QK_ROPE - 1, 0), pltpu.roll(kr, 1, 0))
        w_attn_ref[0:rope0 + QK_ROPE, :] = w_in_attn_ref[...].astype(BF16)
        w_attn_ref[rope0 + QK_ROPE:ATTN_PROJ, :] = jnp.concatenate(
            [kr_sw, kr, kr_sw], axis=0).astype(BF16)


def _modulation(cond8, w_mod, b_mod, w_in_t):
    n = w_mod.shape[1]
    tk = MOD_K_ROWS
    nk = D_MODEL // tk // 2
    attn_rows = Q_LORA + KV_LORA + QK_ROPE
    gate_rows = 2 * SGU_WIDTH // nk
    return pl.pallas_call(
        _mod_kernel,
        out_shape=[jax.ShapeDtypeStruct((MOD_ROWS, n), F32),
                   jax.ShapeDtypeStruct((ATTN_PROJ, D_MODEL), BF16),
                   jax.ShapeDtypeStruct((2 * SGU_WIDTH, D_MODEL), BF16)],
        grid=(nk,),
        in_specs=[
            pl.BlockSpec((MOD_ROWS, tk), lambda k: (0, k)),
            pl.BlockSpec((MOD_ROWS, tk), lambda k: (0, k + nk)),
            pl.BlockSpec((tk, n), lambda k: (k, 0)),
            pl.BlockSpec((tk, n), lambda k: (k + nk, 0)),
            pl.BlockSpec((1, n), lambda k: (0, 0)),
            pl.BlockSpec((attn_rows, D_MODEL), lambda k: (0, 0)),
            pl.BlockSpec((pl.Element(gate_rows), pl.Element(D_MODEL)),
                         lambda k: (pl.multiple_of(attn_rows + k * gate_rows, QK_ROPE), 0)),
        ],
        out_specs=[pl.BlockSpec((MOD_ROWS, n), lambda k: (0, 0)),
                   pl.BlockSpec((ATTN_PROJ, D_MODEL), lambda k: (0, 0)),
                   pl.BlockSpec((gate_rows, D_MODEL), lambda k: (k, 0))],
        compiler_params=pltpu.CompilerParams(
            dimension_semantics=("arbitrary",), vmem_limit_bytes=VMEM_LIMIT),
        name="modulation",
    )(cond8, cond8, w_mod, w_mod, b_mod, w_in_t, w_in_t)


def _mod_rows(mod_ref, row):
    vec = mod_ref[pl.ds(row, 1), :]
    return [vec[:, j * D_MODEL:(j + 1) * D_MODEL] for j in range(N_MOD)]


def _phase1_kernel(emit_cache, mod_row_fn, x_ref, mod_ref, ktab_ref, qtab_ref, g_pre_ref, w_attn_ref,
                   w_gate_ref,
                   g_q_ref, w_uq_ref, g_kv_ref, w_k_ref, w_uv_ref, g_sgu_ref, beta_sgu_ref,
                   w_sgu_ref, bias_ref, *rest):
    if emit_cache:
        (w_o32_ref, w_ff1_32_ref, w_ff2_32_ref, q_ref, k_ref, v_ref, sgu_ref, ckv_ref, kr_ref,
         w_o16_ref, w_ff1_16_ref, w_ff2_16_ref) = rest
        w_o16_ref[...] = w_o32_ref[...].astype(BF16)
        w_ff1_16_ref[...] = w_ff1_32_ref[...].astype(BF16)
        w_ff2_16_ref[...] = w_ff2_32_ref[...].astype(BF16)
    else:
        q_ref, k_ref, v_ref, sgu_ref = rest
    tm = x_ref.shape[0]
    mod = _mod_rows(mod_ref, mod_row_fn(pl.program_id(0)))
    shift_a, scale_a = mod[0], mod[1]
    low_half = lax.broadcasted_iota(jnp.int32, (CHUNK, LANES), 1) < SGU_HEAD_DIM
    per_pos_tables = ktab_ref.shape[0] > 1

    def project(r0):
        rows = slice(r0, r0 + PHASE1_CHAIN)
        h = (_rms(x_ref[rows, :], g_pre_ref[...]) * (1.0 + scale_a) + shift_a).astype(BF16)
        proj = _dot_nt(h, w_attn_ref[...])
        gate = _dot_nt(h, w_gate_ref[...])
        return proj, gate

    def expand_qkv(r0, proj):
        rows = slice(r0, r0 + PHASE1_CHAIN)
        tab_rows = rows if per_pos_tables else slice(None)
        cq = _rms(proj[:, 0:Q_LORA], g_q_ref[...])
        q = _dot(cq.astype(BF16), w_uq_ref[...])
        qtab = qtab_ref[tab_rows, :]
        q_ref[rows, :] = jnp.concatenate(
            [(q[:, s * HEAD_SLOT:(s + 1) * HEAD_SLOT] * qtab).astype(BF16)
             for s in range(MLA_HEADS)], axis=1)
        ckv_n = _rms(proj[:, Q_LORA:Q_LORA + KV_LORA], g_kv_ref[...])
        rope_slab = proj[:, Q_LORA + KV_LORA:Q_LORA + KV_LORA + LANES]
        if emit_cache:
            ckv_ref[rows, :] = ckv_n
            kr_ref[rows, :] = rope_slab[:, 0:QK_ROPE]
        ckv_b = ckv_n.astype(BF16)
        kin = jnp.concatenate([ckv_b, (rope_slab * ktab_ref[tab_rows, :]).astype(BF16)], axis=1)
        k_ref[:, rows] = _dot_nt(w_k_ref[...], kin).astype(BF16)
        v_ref[rows, :] = _dot(ckv_b, w_uv_ref[...]).astype(BF16)

    def gating_unit(r0, gate):
        vv = _gelu(gate[:, SGU_WIDTH:2 * SGU_WIDTH])
        mu = jnp.mean(vv, axis=-1, keepdims=True)
        vc = vv - mu
        var = jnp.mean(vc * vc, axis=-1, keepdims=True)
        vn = (vc * lax.rsqrt(var + EPS) * g_sgu_ref[...] + beta_sgu_ref[...]).astype(BF16)
        n_chunks = PHASE1_CHAIN // CHUNK
        for j in range(SGU_WIDTH // LANES):
            lanes = slice(j * LANES, (j + 1) * LANES)
            rhs = jnp.concatenate(
                [vn[n * CHUNK:(n + 1) * CHUNK, lanes] for n in range(n_chunks)], axis=1)
            o = _dot(w_sgu_ref[j], rhs)
            u = _gelu(gate[:, lanes])
            bias = bias_ref[:, lanes]
            for n in range(n_chunks):
                even = o[0:CHUNK, n * LANES:(n + 1) * LANES]
                odd = o[CHUNK:2 * CHUNK, n * LANES:(n + 1) * LANES]
                mixed = jnp.where(low_half, even, odd) + bias
                sgu_ref[r0 + n * CHUNK:r0 + (n + 1) * CHUNK, lanes] = (
                    u[n * CHUNK:(n + 1) * CHUNK, :] * mixed).astype(BF16)

    starts = list(range(0, tm, PHASE1_CHAIN))
    ahead = project(starts[0])
    for i, r0 in enumerate(starts):
        proj, gate = ahead
        if i + 1 < len(starts):
            ahead = project(starts[i + 1])
        expand_qkv(r0, proj)
        gating_unit(r0, gate)


def _const_spec(shape):
    nd = len(shape)
    return pl.BlockSpec(shape, lambda *_: (0,) * nd, pipeline_mode=pl.Buffered(1))


def _phase1(x2d, mod3, mod_row_fn, ktab, qtab, tab_fn, wts, emit_cache):
    n_tok = x2d.shape[0]
    tm = TOKEN_TILE
    grid = (n_tok // tm,)
    tab_block = (ktab.shape[0] if ktab.shape[0] == 1 else tm, LANES)
    in_specs = [
        pl.BlockSpec((tm, D_MODEL), lambda i: (i, 0)),
        _const_spec((MOD_ROWS, N_MOD * D_MODEL)),
        pl.BlockSpec(tab_block, lambda i: (tab_fn(i), 0)),
        pl.BlockSpec(tab_block, lambda i: (tab_fn(i), 0)),
        _const_spec((1, D_MODEL)),
        _const_spec((ATTN_PROJ, D_MODEL)),
        _const_spec((2 * SGU_WIDTH, D_MODEL)),
        _const_spec((1, Q_LORA)),
        _const_spec((Q_LORA, QK_WIDTH)),
        _const_spec((1, KV_LORA)),
        _const_spec((QK_WIDTH, 2 * LANES)),
        _const_spec((KV_LORA, MLA_WIDTH)),
        _const_spec((1, SGU_WIDTH)),
        _const_spec((1, SGU_WIDTH)),
        _const_spec((SGU_WIDTH // LANES, 2 * CHUNK, CHUNK)),
        _const_spec((CHUNK, SGU_WIDTH)),
    ]
    out_shape = [
        jax.ShapeDtypeStruct((n_tok, QK_WIDTH), BF16),
        jax.ShapeDtypeStruct((QK_WIDTH, n_tok), BF16),
        jax.ShapeDtypeStruct((n_tok, MLA_WIDTH), BF16),
        jax.ShapeDtypeStruct((n_tok, SGU_WIDTH), BF16),
    ]
    out_specs = [
        pl.BlockSpec((tm, QK_WIDTH), lambda i: (i, 0)),
        pl.BlockSpec((QK_WIDTH, tm), lambda i: (0, i)),
        pl.BlockSpec((tm, MLA_WIDTH), lambda i: (i, 0)),
        pl.BlockSpec((tm, SGU_WIDTH), lambda i: (i, 0)),
    ]
    extra_inputs = []
    if emit_cache:
        out_shape += [jax.ShapeDtypeStruct((n_tok, KV_LORA), F32),
                      jax.ShapeDtypeStruct((n_tok, QK_ROPE), F32)]
        out_specs += [pl.BlockSpec((tm, KV_LORA), lambda i: (i, 0)),
                      pl.BlockSpec((tm, QK_ROPE), lambda i: (i, 0))]
        for w in (wts["w_o32"], wts["w_ff1_32"], wts["w_ff2_32"]):
            rows, cols = w.shape
            blk = (rows // grid[0], cols)
            extra_inputs.append(w)
            in_specs.append(pl.BlockSpec(blk, lambda i: (i, 0)))
            out_shape.append(jax.ShapeDtypeStruct(w.shape, BF16))
            out_specs.append(pl.BlockSpec(blk, lambda i: (i, 0)))
    return pl.pallas_call(
        functools.partial(_phase1_kernel, emit_cache, mod_row_fn),
        out_shape=out_shape,
        grid=grid,
        in_specs=in_specs,
        out_specs=out_specs,
        compiler_params=pltpu.CompilerParams(
            dimension_semantics=("arbitrary",), vmem_limit_bytes=VMEM_LIMIT),
        name="phase1_ctx" if emit_cache else "phase1_lat",
    )(x2d, mod3, ktab, qtab, wts["g_attn_pre"], wts["w_attn"], wts["w_gate"], wts["g_q"], wts["w_uq"],
      wts["g_kv"], wts["w_k"], wts["w_uv"], wts["g_sgu"], wts["beta_sgu"], wts["w_sgu"],
      wts["bias_sgu"], *extra_inputs)


def _cache_kv_kernel(kin_ref, w_k_ref, w_uv_ref, k_ref, v_ref):
    kin = kin_ref[...].astype(BF16)
    k_ref[...] = _dot_nt(w_k_ref[...], kin).astype(BF16)
    v_ref[...] = _dot(kin[:, 0:KV_LORA], w_uv_ref[...]).astype(BF16)


def _cache_kv(kin2d, wts):
    n_tok = kin2d.shape[0]
    tm = n_tok
    return pl.pallas_call(
        _cache_kv_kernel,
        out_shape=[jax.ShapeDtypeStruct((QK_WIDTH, n_tok), BF16),
                   jax.ShapeDtypeStruct((n_tok, MLA_WIDTH), BF16)],
        grid=(n_tok // tm,),
        in_specs=[
            pl.BlockSpec((tm, 2 * LANES), lambda i: (i, 0)),
            _const_spec((QK_WIDTH, 2 * LANES)),
            _const_spec((KV_LORA, MLA_WIDTH)),
        ],
        out_specs=[pl.BlockSpec((QK_WIDTH, tm), lambda i: (0, i)),
                   pl.BlockSpec((tm, MLA_WIDTH), lambda i: (i, 0))],
        compiler_params=pltpu.CompilerParams(
            dimension_semantics=("arbitrary",), vmem_limit_bytes=VMEM_LIMIT),
        name="cache_kv",
    )(kin2d, wts["w_k"], wts["w_uv"])


def _interleave(*gens):
    results = [None] * len(gens)
    live = list(range(len(gens)))
    while live:
        for i in list(live):
            try:
                next(gens[i])
            except StopIteration as stop:
                results[i] = stop.value
                live.remove(i)
    return results


def _run(gen):
    return _interleave(gen)[0]


def _attend(q_ref, kv_views, r0, low_half):
    def head_scores(hd):
        qh = q_ref[r0:r0 + SUB_ROWS, hd * HEAD_SLOT:(hd + 1) * HEAD_SLOT]
        return [_dot(qh, k_view(hd * HEAD_SLOT, HEAD_SLOT)) for k_view, _ in kv_views]

    def head_probs(scores):
        m = scores[0].max(axis=-1, keepdims=True)
        for s in scores[1:]:
            m = jnp.maximum(m, s.max(axis=-1, keepdims=True))
        probs = []
        denom = None
        for s in scores:
            p = jnp.exp2(s - m)
            ps = p.sum(axis=-1, keepdims=True)
            denom = ps if denom is None else denom + ps
            probs.append(p.astype(BF16))
        return probs, denom

    def head_values(hd, probs, denom):
        slab = hd // 2
        acc = None
        for (_, v_view), p in zip(kv_views, probs):
            pv = _dot(p, v_view(slab * LANES, LANES))
            acc = pv if acc is None else acc + pv
        return acc / denom

    pair_out = []
    head_out = None
    ahead = [head_scores(hd) for hd in range(QK_AHEAD)]
    for hd in range(MLA_HEADS):
        if hd + QK_AHEAD < MLA_HEADS:
            ahead.append(head_scores(hd + QK_AHEAD))
        o = head_values(hd, *head_probs(ahead.pop(0)))
        if hd % 2 == 0:
            head_out = o
        else:
            pair_out.append(jnp.where(low_half, head_out, o).astype(BF16))
            yield
    return jnp.concatenate(pair_out, axis=1)


def _phase2_kernel(n_kv, shared_kv, mod_row_fn, x_ref, mod_ref, q_ref, sgu_ref, *refs):
    kv_refs = refs[:2 * n_kv]
    (w_o_ref, g_post_ref, g_fpre_ref, g_fpost_ref, w_ff1_ref, w_ff2_ref, o_ref) = refs[2 * n_kv:]
    tq = x_ref.shape[0]
    low_half = lax.broadcasted_iota(jnp.int32, (SUB_ROWS, LANES), 1) < V_HEAD
    _, _, gate_a, shift_f, scale_f, gate_f = _mod_rows(mod_ref, mod_row_fn(pl.program_id(0)))
    def attention(r0):
        views = []
        for t in range(n_kv):
            k_ref, v_ref = kv_refs[2 * t], kv_refs[2 * t + 1]
            if shared_kv:
                views.append((lambda c, w, k_ref=k_ref: k_ref[c:c + w, :],
                              lambda c, w, v_ref=v_ref: v_ref[:, c:c + w]))
            else:
                views.append((lambda c, w, k_ref=k_ref: k_ref[c:c + w, r0:r0 + SUB_ROWS],
                              lambda c, w, v_ref=v_ref: v_ref[r0:r0 + SUB_ROWS, c:c + w]))
        return _attend(q_ref, views, r0, low_half)

    def mixer_proj(r0, attn):
        mix_in = jnp.concatenate([attn, sgu_ref[r0:r0 + SUB_ROWS, :]], axis=1)
        return _dot(mix_in, w_o_ref[...])

    def mixer_norm(r0, mix):
        x1 = x_ref[r0:r0 + SUB_ROWS, :] + gate_a * _rms(mix, g_post_ref[...])
        h = (_rms(x1, g_fpre_ref[...]) * (1.0 + scale_f) + shift_f).astype(BF16)
        return x1, h

    def ffn(h):
        f = None
        for c in range(D_FF // FF_CHUNK):
            hid = jnp.maximum(_dot(h, w_ff1_ref[:, c * FF_CHUNK:(c + 1) * FF_CHUNK]), 0.0)
            part = _dot((hid * hid).astype(BF16), w_ff2_ref[c * FF_CHUNK:(c + 1) * FF_CHUNK, :])
            f = part if f is None else f + part
            yield
        return f

    def finish(r0, x1, f):
        o_ref[r0:r0 + SUB_ROWS, :] = x1 + gate_f * _rms(f, g_fpost_ref[...])

    starts = [a * SUB_ROWS for a in range(tq // SUB_ROWS)]
    x1, h = mixer_norm(starts[0], mixer_proj(starts[0], _run(attention(starts[0]))))
    for r0, nxt in zip(starts, starts[1:] + [None]):
        if nxt is None:
            finish(r0, x1, _run(ffn(h)))
        else:
            f, attn = _interleave(ffn(h), attention(nxt))
            x1_nxt, h_nxt = mixer_norm(nxt, mixer_proj(nxt, attn))
            finish(r0, x1, f)
            x1, h = x1_nxt, h_nxt


def _phase2(x2d, mod3, mod_row_fn, q, sgu, kv_list, kv_rows, shared_kv, tq, n_outer, n_inner, wts,
            name):
    n_kv = len(kv_list) // 2
    in_specs = [
        pl.BlockSpec((tq, D_MODEL), lambda b, i: (b * n_inner + i, 0)),
        _const_spec((MOD_ROWS, N_MOD * D_MODEL)),
        pl.BlockSpec((tq, QK_WIDTH), lambda b, i: (b * n_inner + i, 0)),
        pl.BlockSpec((tq, SGU_WIDTH), lambda b, i: (b * n_inner + i, 0)),
    ]
    for t in range(n_kv):
        in_specs.append(pl.BlockSpec((QK_WIDTH, kv_rows[t]), lambda b, i: (0, b)))
        in_specs.append(pl.BlockSpec((kv_rows[t], MLA_WIDTH), lambda b, i: (b, 0)))
    in_specs += [
        _const_spec((D_MODEL, D_MODEL)),
        _const_spec((1, D_MODEL)),
        _const_spec((1, D_MODEL)),
        _const_spec((1, D_MODEL)),
        _const_spec((D_MODEL, D_FF)),
        _const_spec((D_FF, D_MODEL)),
    ]
    return pl.pallas_call(
        functools.partial(_phase2_kernel, n_kv, shared_kv, mod_row_fn),
        out_shape=jax.ShapeDtypeStruct(x2d.shape, F32),
        grid=(n_outer, n_inner),
        in_specs=in_specs,
        out_specs=pl.BlockSpec((tq, D_MODEL), lambda b, i: (b * n_inner + i, 0)),
        compiler_params=pltpu.CompilerParams(
            dimension_semantics=("arbitrary", "arbitrary"), vmem_limit_bytes=VMEM_LIMIT),
        name=name,
    )(x2d, mod3, q, sgu, *kv_list, wts["w_o"], wts["g_attn_post"], wts["g_ffn_pre"],
      wts["g_ffn_post"], wts["w_ff1"], wts["w_ff2"])


def _pair_swap(w):
    shp = w.shape
    return w.reshape(shp[:-1] + (shp[-1] // 2, 2))[..., ::-1].reshape(shp)


def _prepare_weights(g_attn_pre, g_attn_post, g_q, w_uq, g_kv, w_ukv, w_sgu, b_sgu, g_sgu,
                     beta_sgu, w_o, g_ffn_pre, g_ffn_post, w_ff1, w_ff2):
    w_uq_h = w_uq.reshape(Q_LORA, MLA_HEADS, QK_NOPE + QK_ROPE)
    w_uq_ext = jnp.concatenate(
        [w_uq_h, _pair_swap(w_uq_h[..., QK_NOPE:])], axis=-1).reshape(Q_LORA, QK_WIDTH).astype(BF16)

    w_ukv_h = w_ukv.reshape(KV_LORA, MLA_HEADS, QK_NOPE + V_HEAD)
    w_uk_slots = jnp.concatenate(
        [w_ukv_h[..., :QK_NOPE], jnp.zeros((KV_LORA, MLA_HEADS, HEAD_SLOT - QK_NOPE), F32)],
        axis=-1).reshape(KV_LORA, QK_WIDTH)
    eye = jnp.eye(QK_ROPE, dtype=F32)
    zeros_rope = jnp.zeros((QK_ROPE, QK_ROPE), F32)
    zeros_nope = jnp.zeros((QK_ROPE, QK_NOPE), F32)
    to_lo = jnp.tile(jnp.concatenate([zeros_nope, eye, zeros_rope], axis=1), (1, MLA_HEADS))
    to_hi = jnp.tile(jnp.concatenate([zeros_nope, zeros_rope, eye], axis=1), (1, MLA_HEADS))
    w_k = jnp.concatenate([w_uk_slots, to_lo, to_lo, to_hi, to_hi], axis=0).T.astype(BF16)
    w_uv = w_ukv_h[..., QK_NOPE:].reshape(KV_LORA, MLA_WIDTH).astype(BF16)

    w_sgu_pair = w_sgu.reshape(SGU_HEADS // 2, 2 * CHUNK, CHUNK).astype(BF16)
    bias_sgu = jnp.repeat(b_sgu.T, SGU_HEAD_DIM, axis=1)
    row = lambda a: a.reshape(1, -1)
    return {
        "g_attn_pre": row(g_attn_pre), "g_attn_post": row(g_attn_post),
        "g_q": row(g_q), "w_uq": w_uq_ext, "g_kv": row(g_kv), "w_k": w_k, "w_uv": w_uv,
        "g_sgu": row(g_sgu), "beta_sgu": row(beta_sgu), "w_sgu": w_sgu_pair, "bias_sgu": bias_sgu,
        "w_o32": w_o, "g_ffn_pre": row(g_ffn_pre), "g_ffn_post": row(g_ffn_post),
        "w_ff1_32": w_ff1, "w_ff2_32": w_ff2,
    }


def _rope_tables(n_tok):
    rows = n_tok // GRID_W
    row = np.repeat(np.arange(rows), GRID_W).astype(np.float32)
    col = np.tile(np.arange(GRID_W), rows).astype(np.float32)
    freqs = (1.0 / (ROPE_BASE ** (np.arange(AXIS_PAIRS, dtype=np.float32) / AXIS_PAIRS))).astype(
        np.float32)
    ang = np.concatenate([row[:, None] * freqs, col[:, None] * freqs], axis=-1)
    cos = np.repeat(np.cos(ang), 2, axis=1)
    sin = np.repeat(np.sin(ang), 2, axis=1) * np.tile(np.array([-1.0, 1.0], np.float32), QK_ROPE // 2)
    ktab = np.concatenate([cos, sin, cos, sin], axis=1)
    qtab = QUERY_SCALE * np.concatenate([np.ones((n_tok, QK_NOPE), np.float32), cos, sin], axis=1)
    return jnp.asarray(ktab, F32), jnp.asarray(qtab, F32)


def kernel(x_prompt, x_sample, cache_ckv, cache_krope, c, c_ctx, w_mod, b_mod, g_attn_pre,
           g_attn_post, w_in, g_q, w_uq, g_kv, w_ukv, w_sgu, b_sgu, g_sgu, beta_sgu, w_o,
           g_ffn_pre, g_ffn_post, w_ff1, w_ff2):
    batch, seq, _ = x_prompt.shape
    dec_batch, dec_seq, _ = x_sample.shape
    past_len = cache_ckv.shape[2]
    depth = w_mod.shape[0]
    assert depth == 1

    wts = _prepare_weights(g_attn_pre[0], g_attn_post[0], g_q[0], w_uq[0], g_kv[0],
                           w_ukv[0], w_sgu[0], b_sgu[0], g_sgu[0], beta_sgu[0], w_o[0],
                           g_ffn_pre[0], g_ffn_post[0], w_ff1[0], w_ff2[0])

    cond8 = jnp.concatenate(
        [c_ctx[None, :], c, jnp.zeros((MOD_ROWS - 1 - dec_batch, D_MODEL), F32)], axis=0)
    mod3, w_attn, w_gate = _modulation(cond8, w_mod[0], b_mod[0].reshape(1, -1), w_in[0].T)
    wts = dict(wts, w_attn=w_attn, w_gate=w_gate)

    ktab_ctx = jnp.asarray(np.arange(LANES)[None, :] < QK_ROPE, F32)
    qtab_ctx = jnp.asarray(QUERY_SCALE * (np.arange(LANES)[None, :] < QK_NOPE + QK_ROPE), F32)
    xp2d = x_prompt.reshape(batch * seq, D_MODEL)
    q_c, k_c, v_c, sgu_c, ckv_c, kr_c, w_o16, w_ff1_16, w_ff2_16 = _phase1(
        xp2d, mod3, lambda i: 0, ktab_ctx, qtab_ctx, lambda i: 0, wts, True)
    wts = dict(wts, w_o=w_o16, w_ff1=w_ff1_16, w_ff2=w_ff2_16)
    assert seq == SUB_ROWS and dec_seq % PHASE2_ROWS_LAT == 0
    y_prompt = _phase2(xp2d, mod3, lambda b: 0, q_c, sgu_c, [k_c, v_c], [PHASE2_ROWS_CTX], False,
                       PHASE2_ROWS_CTX, batch * seq // PHASE2_ROWS_CTX, 1, wts, "phase2_ctx")

    ktab_lat, qtab_lat = _rope_tables(dec_seq)
    tiles_per_seq = dec_seq // TOKEN_TILE
    xs2d = x_sample.reshape(dec_batch * dec_seq, D_MODEL)
    q_l, k_l, v_l, sgu_l = _phase1(
        xs2d, mod3, lambda i: 1 + i // tiles_per_seq, ktab_lat, qtab_lat,
        lambda i: i % tiles_per_seq, wts, False)
    kr_p = cache_krope[:, 0].reshape(dec_batch * past_len, QK_ROPE)
    pad = jnp.zeros_like(kr_p)
    kin_p = jnp.concatenate(
        [cache_ckv[:, 0].reshape(dec_batch * past_len, KV_LORA), kr_p, pad, kr_p, pad], axis=1)
    k_p, v_p = _cache_kv(kin_p, wts)
    y_sample = _phase2(xs2d, mod3, lambda b: 1 + b, q_l, sgu_l, [k_p, v_p, k_l, v_l],
                       [past_len, dec_seq], True, PHASE2_ROWS_LAT, dec_batch,
                       dec_seq // PHASE2_ROWS_LAT, wts, "phase2_lat")

    return (y_prompt.reshape(batch, seq, D_MODEL),
            y_sample.reshape(dec_batch, dec_seq, D_MODEL),
            ckv_c.reshape(batch, 1, seq, KV_LORA),
            kr_c.reshape(batch, 1, seq, QK_ROPE))
```

```python
import functools
import math

import jax
import jax.numpy as jnp
import numpy as np
from jax import lax
from jax.experimental import pallas as pl
from jax.experimental.pallas import tpu as pltpu

D_MODEL = 1024
GRID_W = 64
MLA_HEADS = 8
QK_NOPE = 64
QK_ROPE = 32
V_HEAD = 64
Q_LORA = 256
KV_LORA = 128
MLA_WIDTH = MLA_HEADS * V_HEAD
SGU_HEADS = 8
SGU_WIDTH = D_MODEL - MLA_WIDTH
SGU_HEAD_DIM = SGU_WIDTH // SGU_HEADS
CHUNK = 128
D_FF = 4 * D_MODEL
AXIS_PAIRS = QK_ROPE // 4
ROPE_BASE = 10000.0
EPS = 1e-6
N_MOD = 6
ATTN_SCALE = (QK_NOPE + QK_ROPE) ** -0.5
QUERY_SCALE = ATTN_SCALE * math.log2(math.e)

LANES = 128
HEAD_SLOT = LANES
QK_WIDTH = MLA_HEADS * HEAD_SLOT
ATTN_PROJ = Q_LORA + KV_LORA + LANES
MOD_ROWS = 8
MOD_K_ROWS = 128
MOD_STREAMS = 4
TOKEN_TILE = 1024
PHASE1_CHAIN = 512
SUB_ROWS = 256
PHASE2_ROWS_CTX = 512
PHASE2_ROWS_LAT = 512
QK_AHEAD = 1
FF_CHUNK = 1024
VMEM_LIMIT = 56 * 1024 * 1024

BF16 = jnp.bfloat16
F32 = jnp.float32


def _dot(a, b):
    return jnp.dot(a, b, preferred_element_type=F32)


def _dot_nt(a, b):
    return lax.dot_general(a, b, (((1,), (1,)), ((), ())), preferred_element_type=F32)


def _rms(x, g):
    return x * lax.rsqrt(jnp.mean(x * x, axis=-1, keepdims=True) + EPS) * g


def _gelu(x):
    inner = math.sqrt(2.0 / math.pi) * (x + 0.044715 * (x * x * x))
    return x * (0.5 * (1.0 + jnp.tanh(inner)))


def _mod_kernel(*refs):
    cond_refs = refs[:MOD_STREAMS]
    w_refs = refs[MOD_STREAMS:2 * MOD_STREAMS]
    (b_ref, w_in_attn_ref, w_in_gate_ref, o_ref, w_attn_ref, w_gate_ref) = refs[2 * MOD_STREAMS:]

    @pl.when(pl.program_id(0) == 0)
    def _():
        o_ref[...] = jnp.broadcast_to(b_ref[...], o_ref.shape)

    def part(cond_ref, w_ref):
        cnd = cond_ref[...]
        act = cnd * (1.0 / (1.0 + jnp.exp(-cnd)))
        return _dot(act.astype(BF16), w_ref[...].astype(BF16))

    acc = part(cond_refs[0], w_refs[0])
    for cond_ref, w_ref in zip(cond_refs[1:], w_refs[1:]):
        acc = acc + part(cond_ref, w_ref)
    o_ref[...] += acc

    w_gate_ref[...] = w_in_gate_ref[...].astype(BF16)

    @pl.when(pl.program_id(0) == 0)
    def _():
        rope0 = Q_LORA + KV_LORA
        kr = w_in_attn_ref[rope0:rope0 + QK_ROPE, :]
        row = lax.broadcasted_iota(jnp.int32, kr.shape, 0)
        kr_sw = jnp.where(row % 2 == 0, pltpu.roll(kr, QK_ROPE - 1, 0), pltpu.roll(kr, 1, 0))
        w_attn_ref[0:rope0 + QK_ROPE, :] = w_in_attn_ref[...].astype(BF16)
        w_attn_ref[rope0 + QK_ROPE:ATTN_PROJ, :] = jnp.concatenate(
            [kr_sw, kr, kr_sw], axis=0).astype(BF16)


def _modulation(cond8, w_mod, b_mod, w_in_t):
    n = w_mod.shape[1]
    tk = MOD_K_ROWS
    nk = D_MODEL // tk // MOD_STREAMS
    attn_rows = Q_LORA + KV_LORA + QK_ROPE
    gate_rows = 2 * SGU_WIDTH // nk
    cond_specs = [pl.BlockSpec((MOD_ROWS, tk), lambda k, s=s: (0, k + s * nk))
                  for s in range(MOD_STREAMS)]
    w_specs = [pl.BlockSpec((tk, n), lambda k, s=s: (k + s * nk, 0)) for s in range(MOD_STREAMS)]
    return pl.pallas_call(
        _mod_kernel,
        out_shape=[jax.ShapeDtypeStruct((MOD_ROWS, n), F32),
                   jax.ShapeDtypeStruct((ATTN_PROJ, D_MODEL), BF16),
                   jax.ShapeDtypeStruct((2 * SGU_WIDTH, D_MODEL), BF16)],
        grid=(nk,),
        in_specs=cond_specs + w_specs + [
            pl.BlockSpec((1, n), lambda k: (0, 0)),
            pl.BlockSpec((attn_rows, D_MODEL), lambda k: (0, 0)),
            pl.BlockSpec((pl.Element(gate_rows), pl.Element(D_MODEL)),
                         lambda k: (pl.multiple_of(attn_rows + k * gate_rows, QK_ROPE), 0)),
        ],
        out_specs=[pl.BlockSpec((MOD_ROWS, n), lambda k: (0, 0)),
                   pl.BlockSpec((ATTN_PROJ, D_MODEL), lambda k: (0, 0)),
                   pl.BlockSpec((gate_rows, D_MODEL), lambda k: (k, 0))],
        compiler_params=pltpu.CompilerParams(
            dimension_semantics=("arbitrary",), vmem_limit_bytes=VMEM_LIMIT),
        name="modulation",
    )(*([cond8] * MOD_STREAMS), *([w_mod] * MOD_STREAMS), b_mod, w_in_t, w_in_t)


def _mod_rows(mod_ref, row):
    vec = mod_ref[pl.ds(row, 1), :]
    return [vec[:, j * D_MODEL:(j + 1) * D_MODEL] for j in range(N_MOD)]


def _phase1_kernel(emit_cache, mod_row_fn, x_ref, mod_ref, ktab_ref, qtab_ref, g_pre_ref, w_attn_ref,
                   w_gate_ref,
                   g_q_ref, w_uq_ref, g_kv_ref, w_k_ref, w_uv_ref, g_sgu_ref, beta_sgu_ref,
                   w_sgu_ref, bias_ref, *rest):
    if emit_cache:
        (w_o32_ref, w_ff1_32_ref, w_ff2_32_ref, q_ref, k_ref, v_ref, sgu_ref, ckv_ref, kr_ref,
         w_o16_ref, w_ff1_16_ref, w_ff2_16_ref) = rest
        w_o16_ref[...] = w_o32_ref[...].astype(BF16)
        w_ff1_16_ref[...] = w_ff1_32_ref[...].astype(BF16)
        w_ff2_16_ref[...] = w_ff2_32_ref[...].astype(BF16)
    else:
        q_ref, k_ref, v_ref, sgu_ref = rest
    tm = x_ref.shape[0]
    mod = _mod_rows(mod_ref, mod_row_fn(pl.program_id(0)))
    shift_a, scale_a = mod[0], mod[1]
    low_half = lax.broadcasted_iota(jnp.int32, (CHUNK, LANES), 1) < SGU_HEAD_DIM
    per_pos_tables = ktab_ref.shape[0] > 1

    def project(r0):
        rows = slice(r0, r0 + PHASE1_CHAIN)
        h = (_rms(x_ref[rows, :], g_pre_ref[...]) * (1.0 + scale_a) + shift_a).astype(BF16)
        proj = _dot_nt(h, w_attn_ref[...])
        gate = _dot_nt(h, w_gate_ref[...])
        return proj, gate

    def expand_qkv(r0, proj):
        rows = slice(r0, r0 + PHASE1_CHAIN)
        tab_rows = rows if per_pos_tables else slice(None)
        cq = _rms(proj[:, 0:Q_LORA], g_q_ref[...])
        q = _dot(cq.astype(BF16), w_uq_ref[...])
        qtab = qtab_ref[tab_rows, :]
        q_ref[rows, :] = jnp.concatenate(
            [(q[:, s * HEAD_SLOT:(s + 1) * HEAD_SLOT] * qtab).astype(BF16)
             for s in range(MLA_HEADS)], axis=1)
        ckv_n = _rms(proj[:, Q_LORA:Q_LORA + KV_LORA], g_kv_ref[...])
        rope_slab = proj[:, Q_LORA + KV_LORA:Q_LORA + KV_LORA + LANES]
        if emit_cache:
            ckv_ref[rows, :] = ckv_n
            kr_t = rope_slab.T
            for e in range(PHASE1_CHAIN // SUB_ROWS):
                elem = r0 // SUB_ROWS + e
                kr_ref[elem * QK_ROPE:(elem + 1) * QK_ROPE, :] = (
                    kr_t[0:QK_ROPE, e * SUB_ROWS:(e + 1) * SUB_ROWS])
        ckv_b = ckv_n.astype(BF16)
        kin = jnp.concatenate([ckv_b, (rope_slab * ktab_ref[tab_rows, :]).astype(BF16)], axis=1)
        k_ref[:, rows] = _dot_nt(w_k_ref[...], kin).astype(BF16)
        v_ref[rows, :] = _dot(ckv_b, w_uv_ref[...]).astype(BF16)

    def gating_unit(r0, gate):
        vv = _gelu(gate[:, SGU_WIDTH:2 * SGU_WIDTH])
        mu = jnp.mean(vv, axis=-1, keepdims=True)
        vc = vv - mu
        var = jnp.mean(vc * vc, axis=-1, keepdims=True)
        vn = (vc * lax.rsqrt(var + EPS) * g_sgu_ref[...] + beta_sgu_ref[...]).astype(BF16)
        n_chunks = PHASE1_CHAIN // CHUNK
        for j in range(SGU_WIDTH // LANES):
            lanes = slice(j * LANES, (j + 1) * LANES)
            rhs = jnp.concatenate(
                [vn[n * CHUNK:(n + 1) * CHUNK, lanes] for n in range(n_chunks)], axis=1)
            o = _dot(w_sgu_ref[j], rhs)
            u = _gelu(gate[:, lanes])
            bias = bias_ref[:, lanes]
            for n in range(n_chunks):
                even = o[0:CHUNK, n * LANES:(n + 1) * LANES]
                odd = o[CHUNK:2 * CHUNK, n * LANES:(n + 1) * LANES]
                mixed = jnp.where(low_half, even, odd) + bias
                sgu_ref[r0 + n * CHUNK:r0 + (n + 1) * CHUNK, lanes] = (
                    u[n * CHUNK:(n + 1) * CHUNK, :] * mixed).astype(BF16)

    starts = list(range(0, tm, PHASE1_CHAIN))
    ahead = project(starts[0])
    for i, r0 in enumerate(starts):
        proj, gate = ahead
        if i + 1 < len(starts):
            ahead = project(starts[i + 1])
        expand_qkv(r0, proj)
        gating_unit(r0, gate)


def _const_spec(shape):
    nd = len(shape)
    return pl.BlockSpec(shape, lambda *_: (0,) * nd, pipeline_mode=pl.Buffered(1))


def _phase1(x2d, mod3, mod_row_fn, ktab, qtab, tab_fn, wts, emit_cache):
    n_tok = x2d.shape[0]
    tm = TOKEN_TILE
    grid = (n_tok // tm,)
    tab_block = (ktab.shape[0] if ktab.shape[0] == 1 else tm, LANES)
    in_specs = [
        pl.BlockSpec((tm, D_MODEL), lambda i: (i, 0)),
        _const_spec((MOD_ROWS, N_MOD * D_MODEL)),
        pl.BlockSpec(tab_block, lambda i: (tab_fn(i), 0)),
        pl.BlockSpec(tab_block, lambda i: (tab_fn(i), 0)),
        _const_spec((1, D_MODEL)),
        _const_spec((ATTN_PROJ, D_MODEL)),
        _const_spec((2 * SGU_WIDTH, D_MODEL)),
        _const_spec((1, Q_LORA)),
        _const_spec((Q_LORA, QK_WIDTH)),
        _const_spec((1, KV_LORA)),
        _const_spec((QK_WIDTH, 2 * LANES)),
        _const_spec((KV_LORA, MLA_WIDTH)),
        _const_spec((1, SGU_WIDTH)),
        _const_spec((1, SGU_WIDTH)),
        _const_spec((SGU_WIDTH // LANES, 2 * CHUNK, CHUNK)),
        _const_spec((CHUNK, SGU_WIDTH)),
    ]
    out_shape = [
        jax.ShapeDtypeStruct((n_tok, QK_WIDTH), BF16),
        jax.ShapeDtypeStruct((QK_WIDTH, n_tok), BF16),
        jax.ShapeDtypeStruct((n_tok, MLA_WIDTH), BF16),
        jax.ShapeDtypeStruct((n_tok, SGU_WIDTH), BF16),
    ]
    out_specs = [
        pl.BlockSpec((tm, QK_WIDTH), lambda i: (i, 0)),
        pl.BlockSpec((QK_WIDTH, tm), lambda i: (0, i)),
        pl.BlockSpec((tm, MLA_WIDTH), lambda i: (i, 0)),
        pl.BlockSpec((tm, SGU_WIDTH), lambda i: (i, 0)),
    ]
    extra_inputs = []
    if emit_cache:
        kr_rows = tm // SUB_ROWS * QK_ROPE
        out_shape += [jax.ShapeDtypeStruct((n_tok, KV_LORA), F32),
                      jax.ShapeDtypeStruct((n_tok // SUB_ROWS * QK_ROPE, SUB_ROWS), F32)]
        out_specs += [pl.BlockSpec((tm, KV_LORA), lambda i: (i, 0)),
                      pl.BlockSpec((kr_rows, SUB_ROWS), lambda i: (i, 0))]
        for w in (wts["w_o32"], wts["w_ff1_32"], wts["w_ff2_32"]):
            rows, cols = w.shape
            blk = (rows // grid[0], cols)
            extra_inputs.append(w)
            in_specs.append(pl.BlockSpec(blk, lambda i: (i, 0)))
            out_shape.append(jax.ShapeDtypeStruct(w.shape, BF16))
            out_specs.append(pl.BlockSpec(blk, lambda i: (i, 0)))
    return pl.pallas_call(
        functools.partial(_phase1_kernel, emit_cache, mod_row_fn),
        out_shape=out_shape,
        grid=grid,
        in_specs=in_specs,
        out_specs=out_specs,
        compiler_params=pltpu.CompilerParams(
            dimension_semantics=("arbitrary",), vmem_limit_bytes=VMEM_LIMIT),
        name="phase1_ctx" if emit_cache else "phase1_lat",
    )(x2d, mod3, ktab, qtab, wts["g_attn_pre"], wts["w_attn"], wts["w_gate"], wts["g_q"], wts["w_uq"],
      wts["g_kv"], wts["w_k"], wts["w_uv"], wts["g_sgu"], wts["beta_sgu"], wts["w_sgu"],
      wts["bias_sgu"], *extra_inputs)


def _cache_kv_kernel(kin_ref, w_k_ref, w_uv_ref, k_ref, v_ref):
    kin = kin_ref[...].astype(BF16)
    k_ref[...] = _dot_nt(w_k_ref[...], kin).astype(BF16)
    v_ref[...] = _dot(kin[:, 0:KV_LORA], w_uv_ref[...]).astype(BF16)


def _cache_kv(kin2d, wts):
    n_tok = kin2d.shape[0]
    tm = n_tok
    return pl.pallas_call(
        _cache_kv_kernel,
        out_shape=[jax.ShapeDtypeStruct((QK_WIDTH, n_tok), BF16),
                   jax.ShapeDtypeStruct((n_tok, MLA_WIDTH), BF16)],
        grid=(n_tok // tm,),
        in_specs=[
            pl.BlockSpec((tm, 2 * LANES), lambda i: (i, 0)),
            _const_spec((QK_WIDTH, 2 * LANES)),
            _const_spec((KV_LORA, MLA_WIDTH)),
        ],
        out_specs=[pl.BlockSpec((QK_WIDTH, tm), lambda i: (0, i)),
                   pl.BlockSpec((tm, MLA_WIDTH), lambda i: (i, 0))],
        compiler_params=pltpu.CompilerParams(
            dimension_semantics=("arbitrary",), vmem_limit_bytes=VMEM_LIMIT),
        name="cache_kv",
    )(kin2d, wts["w_k"], wts["w_uv"])


def _interleave(*gens):
    results = [None] * len(gens)
    live = list(range(len(gens)))
    while live:
        for i in list(live):
            try:
                next(gens[i])
            except StopIteration as stop:
                results[i] = stop.value
                live.remove(i)
    return results


def _run(gen):
    return _interleave(gen)[0]


def _attend(q_ref, kv_views, r0, low_half):
    def head_scores(hd):
        qh = q_ref[r0:r0 + SUB_ROWS, hd * HEAD_SLOT:(hd + 1) * HEAD_SLOT]
        return [_dot(qh, k_view(hd * HEAD_SLOT, HEAD_SLOT)) for k_view, _ in kv_views]

    def head_probs(scores):
        m = scores[0].max(axis=-1, keepdims=True)
        for s in scores[1:]:
            m = jnp.maximum(m, s.max(axis=-1, keepdims=True))
        probs = []
        denom = None
        for s in scores:
            p = jnp.exp2(s - m)
            ps = p.sum(axis=-1, keepdims=True)
            denom = ps if denom is None else denom + ps
            probs.append(p.astype(BF16))
        return probs, denom

    def head_values(hd, probs, denom):
        slab = hd // 2
        acc = None
        for (_, v_view), p in zip(kv_views, probs):
            pv = _dot(p, v_view(slab * LANES, LANES))
            acc = pv if acc is None else acc + pv
        return acc / denom

    pair_out = []
    head_out = None
    ahead = [head_scores(hd) for hd in range(QK_AHEAD)]
    for hd in range(MLA_HEADS):
        if hd + QK_AHEAD < MLA_HEADS:
            ahead.append(head_scores(hd + QK_AHEAD))
        o = head_values(hd, *head_probs(ahead.pop(0)))
        if hd % 2 == 0:
            head_out = o
        else:
            pair_out.append(jnp.where(low_half, head_out, o).astype(BF16))
            yield
    return jnp.concatenate(pair_out, axis=1)


def _phase2_kernel(n_kv, shared_kv, mod_row_fn, x_ref, mod_ref, q_ref, sgu_ref, *refs):
    kv_refs = refs[:2 * n_kv]
    (w_o_ref, g_post_ref, g_fpre_ref, g_fpost_ref, w_ff1_ref, w_ff2_ref, o_ref) = refs[2 * n_kv:]
    tq = x_ref.shape[0]
    low_half = lax.broadcasted_iota(jnp.int32, (SUB_ROWS, LANES), 1) < V_HEAD
    _, _, gate_a, shift_f, scale_f, gate_f = _mod_rows(mod_ref, mod_row_fn(pl.program_id(0)))
    def attention(r0):
        views = []
        for t in range(n_kv):
            k_ref, v_ref = kv_refs[2 * t], kv_refs[2 * t + 1]
            if shared_kv:
                views.append((lambda c, w, k_ref=k_ref: k_ref[c:c + w, :],
                              lambda c, w, v_ref=v_ref: v_ref[:, c:c + w]))
            else:
                views.append((lambda c, w, k_ref=k_ref: k_ref[c:c + w, r0:r0 + SUB_ROWS],
                              lambda c, w, v_ref=v_ref: v_ref[r0:r0 + SUB_ROWS, c:c + w]))
        return _attend(q_ref, views, r0, low_half)

    def mixer_proj(r0, attn):
        mix_in = jnp.concatenate([attn, sgu_ref[r0:r0 + SUB_ROWS, :]], axis=1)
        return _dot(mix_in, w_o_ref[...])

    def mixer_norm(r0, mix):
        x1 = x_ref[r0:r0 + SUB_ROWS, :] + gate_a * _rms(mix, g_post_ref[...])
        h = (_rms(x1, g_fpre_ref[...]) * (1.0 + scale_f) + shift_f).astype(BF16)
        return x1, h

    def ffn(h):
        f = None
        for c in range(D_FF // FF_CHUNK):
            hid = jnp.maximum(_dot(h, w_ff1_ref[:, c * FF_CHUNK:(c + 1) * FF_CHUNK]), 0.0)
            part = _dot((hid * hid).astype(BF16), w_ff2_ref[c * FF_CHUNK:(c + 1) * FF_CHUNK, :])
            f = part if f is None else f + part
            yield
        return f

    def finish(r0, x1, f):
        o_ref[r0:r0 + SUB_ROWS, :] = x1 + gate_f * _rms(f, g_fpost_ref[...])

    starts = [a * SUB_ROWS for a in range(tq // SUB_ROWS)]
    x1h = {r0: mixer_norm(r0, mixer_proj(r0, _run(attention(r0)))) for r0 in starts}
    for r0 in starts:
        finish(r0, x1h[r0][0], _run(ffn(x1h[r0][1])))


def _phase2(x2d, mod3, mod_row_fn, q, sgu, kv_list, kv_rows, shared_kv, tq, n_outer, n_inner, wts,
            name):
    n_kv = len(kv_list) // 2
    in_specs = [
        pl.BlockSpec((tq, D_MODEL), lambda b, i: (b * n_inner + i, 0)),
        _const_spec((MOD_ROWS, N_MOD * D_MODEL)),
        pl.BlockSpec((tq, QK_WIDTH), lambda b, i: (b * n_inner + i, 0)),
        pl.BlockSpec((tq, SGU_WIDTH), lambda b, i: (b * n_inner + i, 0)),
    ]
    for t in range(n_kv):
        in_specs.append(pl.BlockSpec((QK_WIDTH, kv_rows[t]), lambda b, i: (0, b)))
        in_specs.append(pl.BlockSpec((kv_rows[t], MLA_WIDTH), lambda b, i: (b, 0)))
    in_specs += [
        _const_spec((D_MODEL, D_MODEL)),
        _const_spec((1, D_MODEL)),
        _const_spec((1, D_MODEL)),
        _const_spec((1, D_MODEL)),
        _const_spec((D_MODEL, D_FF)),
        _const_spec((D_FF, D_MODEL)),
    ]
    return pl.pallas_call(
        functools.partial(_phase2_kernel, n_kv, shared_kv, mod_row_fn),
        out_shape=jax.ShapeDtypeStruct(x2d.shape, F32),
        grid=(n_outer, n_inner),
        in_specs=in_specs,
        out_specs=pl.BlockSpec((tq, D_MODEL), lambda b, i: (b * n_inner + i, 0)),
        compiler_params=pltpu.CompilerParams(
            dimension_semantics=("arbitrary", "arbitrary"), vmem_limit_bytes=VMEM_LIMIT),
        name=name,
    )(x2d, mod3, q, sgu, *kv_list, wts["w_o"], wts["g_attn_post"], wts["g_ffn_pre"],
      wts["g_ffn_post"], wts["w_ff1"], wts["w_ff2"])


def _pair_swap(w):
    shp = w.shape
    return w.reshape(shp[:-1] + (shp[-1] // 2, 2))[..., ::-1].reshape(shp)


def _prepare_weights(g_attn_pre, g_attn_post, g_q, w_uq, g_kv, w_ukv, w_sgu, b_sgu, g_sgu,
                     beta_sgu, w_o, g_ffn_pre, g_ffn_post, w_ff1, w_ff2):
    w_uq_h = w_uq.reshape(Q_LORA, MLA_HEADS, QK_NOPE + QK_ROPE)
    w_uq_ext = jnp.concatenate(
        [w_uq_h, _pair_swap(w_uq_h[..., QK_NOPE:])], axis=-1).reshape(Q_LORA, QK_WIDTH).astype(BF16)

    w_ukv_h = w_ukv.reshape(KV_LORA, MLA_HEADS, QK_NOPE + V_HEAD)
    w_uk_slots = jnp.concatenate(
        [w_ukv_h[..., :QK_NOPE], jnp.zeros((KV_LORA, MLA_HEADS, HEAD_SLOT - QK_NOPE), F32)],
        axis=-1).reshape(KV_LORA, QK_WIDTH)
    eye = jnp.eye(QK_ROPE, dtype=F32)
    zeros_rope = jnp.zeros((QK_ROPE, QK_ROPE), F32)
    zeros_nope = jnp.zeros((QK_ROPE, QK_NOPE), F32)
    to_lo = jnp.tile(jnp.concatenate([zeros_nope, eye, zeros_rope], axis=1), (1, MLA_HEADS))
    to_hi = jnp.tile(jnp.concatenate([zeros_nope, zeros_rope, eye], axis=1), (1, MLA_HEADS))
    w_k = jnp.concatenate([w_uk_slots, to_lo, to_lo, to_hi, to_hi], axis=0).T.astype(BF16)
    w_uv = w_ukv_h[..., QK_NOPE:].reshape(KV_LORA, MLA_WIDTH).astype(BF16)

    w_sgu_pair = w_sgu.reshape(SGU_HEADS // 2, 2 * CHUNK, CHUNK).astype(BF16)
    bias_sgu = jnp.repeat(b_sgu.T, SGU_HEAD_DIM, axis=1)
    row = lambda a: a.reshape(1, -1)
    return {
        "g_attn_pre": row(g_attn_pre), "g_attn_post": row(g_attn_post),
        "g_q": row(g_q), "w_uq": w_uq_ext, "g_kv": row(g_kv), "w_k": w_k, "w_uv": w_uv,
        "g_sgu": row(g_sgu), "beta_sgu": row(beta_sgu), "w_sgu": w_sgu_pair, "bias_sgu": bias_sgu,
        "w_o32": w_o, "g_ffn_pre": row(g_ffn_pre), "g_ffn_post": row(g_ffn_post),
        "w_ff1_32": w_ff1, "w_ff2_32": w_ff2,
    }


def _rope_tables(n_tok):
    rows = n_tok // GRID_W
    row = np.repeat(np.arange(rows), GRID_W).astype(np.float32)
    col = np.tile(np.arange(GRID_W), rows).astype(np.float32)
    freqs = (1.0 / (ROPE_BASE ** (np.arange(AXIS_PAIRS, dtype=np.float32) / AXIS_PAIRS))).astype(
        np.float32)
    ang = np.concatenate([row[:, None] * freqs, col[:, None] * freqs], axis=-1)
    cos = np.repeat(np.cos(ang), 2, axis=1)
    sin = np.repeat(np.sin(ang), 2, axis=1) * np.tile(np.array([-1.0, 1.0], np.float32), QK_ROPE // 2)
    ktab = np.concatenate([cos, sin, cos, sin], axis=1)
    qtab = QUERY_SCALE * np.concatenate([np.ones((n_tok, QK_NOPE), np.float32), cos, sin], axis=1)
    return jnp.asarray(ktab, F32), jnp.asarray(qtab, F32)


def kernel(x_prompt, x_sample, cache_ckv, cache_krope, c, c_ctx, w_mod, b_mod, g_attn_pre,
           g_attn_post, w_in, g_q, w_uq, g_kv, w_ukv, w_sgu, b_sgu, g_sgu, beta_sgu, w_o,
           g_ffn_pre, g_ffn_post, w_ff1, w_ff2):
    batch, seq, _ = x_prompt.shape
    dec_batch, dec_seq, _ = x_sample.shape
    past_len = cache_ckv.shape[2]
    depth = w_mod.shape[0]
    assert depth == 1

    wts = _prepare_weights(g_attn_pre[0], g_attn_post[0], g_q[0], w_uq[0], g_kv[0],
                           w_ukv[0], w_sgu[0], b_sgu[0], g_sgu[0], beta_sgu[0], w_o[0],
                           g_ffn_pre[0], g_ffn_post[0], w_ff1[0], w_ff2[0])

    cond8 = jnp.concatenate(
        [c_ctx[None, :], c, jnp.zeros((MOD_ROWS - 1 - dec_batch, D_MODEL), F32)], axis=0)
    mod3, w_attn, w_gate = _modulation(cond8, w_mod[0], b_mod[0].reshape(1, -1), w_in[0].T)
    wts = dict(wts, w_attn=w_attn, w_gate=w_gate)

    ktab_ctx = jnp.asarray(np.arange(LANES)[None, :] < QK_ROPE, F32)
    qtab_ctx = jnp.asarray(QUERY_SCALE * (np.arange(LANES)[None, :] < QK_NOPE + QK_ROPE), F32)
    xp2d = x_prompt.reshape(batch * seq, D_MODEL)
    q_c, k_c, v_c, sgu_c, ckv_c, kr_c, w_o16, w_ff1_16, w_ff2_16 = _phase1(
        xp2d, mod3, lambda i: 0, ktab_ctx, qtab_ctx, lambda i: 0, wts, True)
    wts = dict(wts, w_o=w_o16, w_ff1=w_ff1_16, w_ff2=w_ff2_16)
    assert seq == SUB_ROWS and dec_seq % PHASE2_ROWS_LAT == 0
    y_prompt = _phase2(xp2d, mod3, lambda b: 0, q_c, sgu_c, [k_c, v_c], [PHASE2_ROWS_CTX], False,
                       PHASE2_ROWS_CTX, batch * seq // PHASE2_ROWS_CTX, 1, wts, "phase2_ctx")

    ktab_lat, qtab_lat = _rope_tables(dec_seq)
    tiles_per_seq = dec_seq // TOKEN_TILE
    xs2d = x_sample.reshape(dec_batch * dec_seq, D_MODEL)
    q_l, k_l, v_l, sgu_l = _phase1(
        xs2d, mod3, lambda i: 1 + i // tiles_per_seq, ktab_lat, qtab_lat,
        lambda i: i % tiles_per_seq, wts, False)
    kr_p = cache_krope[:, 0].reshape(dec_batch * past_len, QK_ROPE)
    pad = jnp.zeros_like(kr_p)
    kin_p = jnp.concatenate(
        [cache_ckv[:, 0].reshape(dec_batch * past_len, KV_LORA), kr_p, pad, kr_p, pad], axis=1)
    k_p, v_p = _cache_kv(kin_p, wts)
    y_sample = _phase2(xs2d, mod3, lambda b: 1 + b, q_l, sgu_l, [k_p, v_p, k_l, v_l],
                       [past_len, dec_seq], True, PHASE2_ROWS_LAT, dec_batch,
                       dec_seq // PHASE2_ROWS_LAT, wts, "phase2_lat")

    return (y_prompt.reshape(batch, seq, D_MODEL),
            y_sample.reshape(dec_batch, dec_seq, D_MODEL),
            ckv_c.reshape(batch, 1, seq, KV_LORA),
            kr_c.reshape(batch, 1, QK_ROPE, seq).transpose(0, 1, 3, 2))
```

```python
import functools
import math

import jax
import jax.numpy as jnp
import numpy as np
from jax import lax
from jax.experimental import pallas as pl
from jax.experimental.pallas import tpu as pltpu

D_MODEL = 1024
GRID_W = 64
MLA_HEADS = 8
QK_NOPE = 64
QK_ROPE = 32
V_HEAD = 64
Q_LORA = 256
KV_LORA = 128
MLA_WIDTH = MLA_HEADS * V_HEAD
SGU_HEADS = 8
SGU_WIDTH = D_MODEL - MLA_WIDTH
SGU_HEAD_DIM = SGU_WIDTH // SGU_HEADS
CHUNK = 128
D_FF = 4 * D_MODEL
AXIS_PAIRS = QK_ROPE // 4
ROPE_BASE = 10000.0
EPS = 1e-6
N_MOD = 6
ATTN_SCALE = (QK_NOPE + QK_ROPE) ** -0.5
QUERY_SCALE = ATTN_SCALE * math.log2(math.e)

LANES = 128
HEAD_SLOT = LANES
QK_WIDTH = MLA_HEADS * HEAD_SLOT
ATTN_PROJ = Q_LORA + KV_LORA + LANES
MOD_ROWS = 8
MOD_K_ROWS = 128
MOD_STREAMS = 4
TOKEN_TILE = 1024
PHASE1_CHAIN = 512
SUB_ROWS = 256
PHASE2_ROWS_CTX = 512
PHASE2_ROWS_LAT = 512
QK_AHEAD = 1
FF_CHUNK = 1024
VMEM_LIMIT = 56 * 1024 * 1024

BF16 = jnp.bfloat16
F32 = jnp.float32


def _dot(a, b):
    return jnp.dot(a, b, preferred_element_type=F32)


def _dot_nt(a, b):
    return lax.dot_general(a, b, (((1,), (1,)), ((), ())), preferred_element_type=F32)


def _rms(x, g):
    return x * lax.rsqrt(jnp.mean(x * x, axis=-1, keepdims=True) + EPS) * g


def _gelu(x):
    inner = math.sqrt(2.0 / math.pi) * (x + 0.044715 * (x * x * x))
    return x * (0.5 * (1.0 + jnp.tanh(inner)))


def _mod_kernel(*refs):
    cond_refs = refs[:MOD_STREAMS]
    w_refs = refs[MOD_STREAMS:2 * MOD_STREAMS]
    (b_ref, w_in_attn_ref, w_in_gate_ref, o_ref, w_attn_ref, w_gate_ref) = refs[2 * MOD_STREAMS:]

    @pl.when(pl.program_id(0) == 0)
    def _():
        o_ref[...] = jnp.broadcast_to(b_ref[...], o_ref.shape)

    def part(cond_ref, w_ref):
        cnd = cond_ref[...]
        act = cnd * (1.0 / (1.0 + jnp.exp(-cnd)))
        return _dot(act.astype(BF16), w_ref[...].astype(BF16))

    acc = part(cond_refs[0], w_refs[0])
    for cond_ref, w_ref in zip(cond_refs[1:], w_refs[1:]):
        acc = acc + part(cond_ref, w_ref)
    o_ref[...] += acc

    w_gate_ref[...] = w_in_gate_ref[...].astype(BF16)

    @pl.when(pl.program_id(0) == 0)
    def _():
        rope0 = Q_LORA + KV_LORA
        kr = w_in_attn_ref[rope0:rope0 + QK_ROPE, :]
        row = lax.broadcasted_iota(jnp.int32, kr.shape, 0)
        kr_sw = jnp.where(row % 2 == 0, pltpu.roll(kr, QK_ROPE - 1, 0), pltpu.roll(kr, 1, 0))
        w_attn_ref[0:rope0 + QK_ROPE, :] = w_in_attn_ref[...].astype(BF16)
        w_attn_ref[rope0 + QK_ROPE:ATTN_PROJ, :] = jnp.concatenate(
            [kr_sw, kr, kr_sw], axis=0).astype(BF16)


def _modulation(cond8, w_mod, b_mod, w_in_t):
    n = w_mod.shape[1]
    tk = MOD_K_ROWS
    nk = D_MODEL // tk // MOD_STREAMS
    attn_rows = Q_LORA + KV_LORA + QK_ROPE
    gate_rows = 2 * SGU_WIDTH // nk
    cond_specs = [pl.BlockSpec((MOD_ROWS, tk), lambda k, s=s: (0, k + s * nk))
                  for s in range(MOD_STREAMS)]
    w_specs = [pl.BlockSpec((tk, n), lambda k, s=s: (k + s * nk, 0)) for s in range(MOD_STREAMS)]
    return pl.pallas_call(
        _mod_kernel,
        out_shape=[jax.ShapeDtypeStruct((MOD_ROWS, n), F32),
                   jax.ShapeDtypeStruct((ATTN_PROJ, D_MODEL), BF16),
                   jax.ShapeDtypeStruct((2 * SGU_WIDTH, D_MODEL), BF16)],
        grid=(nk,),
        in_specs=cond_specs + w_specs + [
            pl.BlockSpec((1, n), lambda k: (0, 0)),
            pl.BlockSpec((attn_rows, D_MODEL), lambda k: (0, 0)),
            pl.BlockSpec((pl.Element(gate_rows), pl.Element(D_MODEL)),
                         lambda k: (pl.multiple_of(attn_rows + k * gate_rows, QK_ROPE), 0)),
        ],
        out_specs=[pl.BlockSpec((MOD_ROWS, n), lambda k: (0, 0)),
                   pl.BlockSpec((ATTN_PROJ, D_MODEL), lambda k: (0, 0)),
                   pl.BlockSpec((gate_rows, D_MODEL), lambda k: (k, 0))],
        compiler_params=pltpu.CompilerParams(
            dimension_semantics=("arbitrary",), vmem_limit_bytes=VMEM_LIMIT),
        name="modulation",
    )(*([cond8] * MOD_STREAMS), *([w_mod] * MOD_STREAMS), b_mod, w_in_t, w_in_t)


def _mod_rows(mod_ref, row):
    vec = mod_ref[pl.ds(row, 1), :]
    return [vec[:, j * D_MODEL:(j + 1) * D_MODEL] for j in range(N_MOD)]


def _phase1_kernel(emit_cache, mod_row_fn, x_ref, mod_ref, ktab_ref, qtab_ref, g_pre_ref, w_attn_ref,
                   w_gate_ref,
                   g_q_ref, w_uq_ref, g_kv_ref, w_k_ref, w_uv_ref, g_sgu_ref, beta_sgu_ref,
                   w_sgu_ref, bias_ref, *rest):
    if emit_cache:
        (w_o32_ref, w_ff1_32_ref, w_ff2_32_ref, q_ref, k_ref, v_ref, sgu_ref, ckv_ref, kr_ref,
         w_o16_ref, w_ff1_16_ref, w_ff2_16_ref) = rest
        w_o16_ref[...] = w_o32_ref[...].astype(BF16)
        w_ff1_16_ref[...] = w_ff1_32_ref[...].astype(BF16)
        w_ff2_16_ref[...] = w_ff2_32_ref[...].astype(BF16)
    else:
        kin_past_ref, q_ref, k_ref, v_ref, sgu_ref = rest
        kin_past = kin_past_ref[...].astype(BF16)
        past = kin_past.shape[0]
        k_ref[:, 0:past] = _dot_nt(w_k_ref[...], kin_past).astype(BF16)
        v_ref[0:past, :] = _dot(kin_past[:, 0:KV_LORA], w_uv_ref[...]).astype(BF16)
    kv0 = 0 if emit_cache else past
    tm = x_ref.shape[0]
    mod = _mod_rows(mod_ref, mod_row_fn(pl.program_id(0)))
    shift_a, scale_a = mod[0], mod[1]
    low_half = lax.broadcasted_iota(jnp.int32, (CHUNK, LANES), 1) < SGU_HEAD_DIM
    per_pos_tables = ktab_ref.shape[0] > 1

    def project(r0):
        rows = slice(r0, r0 + PHASE1_CHAIN)
        h = (_rms(x_ref[rows, :], g_pre_ref[...]) * (1.0 + scale_a) + shift_a).astype(BF16)
        proj = _dot_nt(h, w_attn_ref[...])
        gate = _dot_nt(h, w_gate_ref[...])
        return proj, gate

    def expand_qkv(r0, proj):
        rows = slice(r0, r0 + PHASE1_CHAIN)
        tab_rows = rows if per_pos_tables else slice(None)
        cq = _rms(proj[:, 0:Q_LORA], g_q_ref[...])
        q = _dot(cq.astype(BF16), w_uq_ref[...])
        qtab = qtab_ref[tab_rows, :]
        q_ref[rows, :] = jnp.concatenate(
            [(q[:, s * HEAD_SLOT:(s + 1) * HEAD_SLOT] * qtab).astype(BF16)
             for s in range(MLA_HEADS)], axis=1)
        ckv_n = _rms(proj[:, Q_LORA:Q_LORA + KV_LORA], g_kv_ref[...])
        rope_slab = proj[:, Q_LORA + KV_LORA:Q_LORA + KV_LORA + LANES]
        if emit_cache:
            ckv_ref[rows, :] = ckv_n
            kr_t = rope_slab.T
            for e in range(PHASE1_CHAIN // SUB_ROWS):
                elem = r0 // SUB_ROWS + e
                kr_ref[elem * QK_ROPE:(elem + 1) * QK_ROPE, :] = (
                    kr_t[0:QK_ROPE, e * SUB_ROWS:(e + 1) * SUB_ROWS])
        ckv_b = ckv_n.astype(BF16)
        kin = jnp.concatenate([ckv_b, (rope_slab * ktab_ref[tab_rows, :]).astype(BF16)], axis=1)
        kv_rows = slice(kv0 + r0, kv0 + r0 + PHASE1_CHAIN)
        k_ref[:, kv_rows] = _dot_nt(w_k_ref[...], kin).astype(BF16)
        v_ref[kv_rows, :] = _dot(ckv_b, w_uv_ref[...]).astype(BF16)

    def gating_unit(r0, gate):
        vv = _gelu(gate[:, SGU_WIDTH:2 * SGU_WIDTH])
        mu = jnp.mean(vv, axis=-1, keepdims=True)
        vc = vv - mu
        var = jnp.mean(vc * vc, axis=-1, keepdims=True)
        vn = (vc * lax.rsqrt(var + EPS) * g_sgu_ref[...] + beta_sgu_ref[...]).astype(BF16)
        n_chunks = PHASE1_CHAIN // CHUNK
        for j in range(SGU_WIDTH // LANES):
            lanes = slice(j * LANES, (j + 1) * LANES)
            rhs = jnp.concatenate(
                [vn[n * CHUNK:(n + 1) * CHUNK, lanes] for n in range(n_chunks)], axis=1)
            o = _dot(w_sgu_ref[j], rhs)
            u = _gelu(gate[:, lanes])
            bias = bias_ref[:, lanes]
            for n in range(n_chunks):
                even = o[0:CHUNK, n * LANES:(n + 1) * LANES]
                odd = o[CHUNK:2 * CHUNK, n * LANES:(n + 1) * LANES]
                mixed = jnp.where(low_half, even, odd) + bias
                sgu_ref[r0 + n * CHUNK:r0 + (n + 1) * CHUNK, lanes] = (
                    u[n * CHUNK:(n + 1) * CHUNK, :] * mixed).astype(BF16)

    starts = list(range(0, tm, PHASE1_CHAIN))
    ahead = project(starts[0])
    for i, r0 in enumerate(starts):
        proj, gate = ahead
        if i + 1 < len(starts):
            ahead = project(starts[i + 1])
        expand_qkv(r0, proj)
        gating_unit(r0, gate)


def _const_spec(shape):
    nd = len(shape)
    return pl.BlockSpec(shape, lambda *_: (0,) * nd, pipeline_mode=pl.Buffered(1))


def _phase1(x2d, mod3, mod_row_fn, ktab, qtab, tab_fn, wts, emit_cache, kin_past=None):
    n_tok = x2d.shape[0]
    tm = TOKEN_TILE
    grid = (n_tok // tm,)
    tab_block = (ktab.shape[0] if ktab.shape[0] == 1 else tm, LANES)
    in_specs = [
        pl.BlockSpec((tm, D_MODEL), lambda i: (i, 0)),
        _const_spec((MOD_ROWS, N_MOD * D_MODEL)),
        pl.BlockSpec(tab_block, lambda i: (tab_fn(i), 0)),
        pl.BlockSpec(tab_block, lambda i: (tab_fn(i), 0)),
        _const_spec((1, D_MODEL)),
        _const_spec((ATTN_PROJ, D_MODEL)),
        _const_spec((2 * SGU_WIDTH, D_MODEL)),
        _const_spec((1, Q_LORA)),
        _const_spec((Q_LORA, QK_WIDTH)),
        _const_spec((1, KV_LORA)),
        _const_spec((QK_WIDTH, 2 * LANES)),
        _const_spec((KV_LORA, MLA_WIDTH)),
        _const_spec((1, SGU_WIDTH)),
        _const_spec((1, SGU_WIDTH)),
        _const_spec((SGU_WIDTH // LANES, 2 * CHUNK, CHUNK)),
        _const_spec((CHUNK, SGU_WIDTH)),
    ]
    out_shape = [
        jax.ShapeDtypeStruct((n_tok, QK_WIDTH), BF16),
        jax.ShapeDtypeStruct((QK_WIDTH, n_tok), BF16),
        jax.ShapeDtypeStruct((n_tok, MLA_WIDTH), BF16),
        jax.ShapeDtypeStruct((n_tok, SGU_WIDTH), BF16),
    ]
    out_specs = [
        pl.BlockSpec((tm, QK_WIDTH), lambda i: (i, 0)),
        pl.BlockSpec((QK_WIDTH, tm), lambda i: (0, i)),
        pl.BlockSpec((tm, MLA_WIDTH), lambda i: (i, 0)),
        pl.BlockSpec((tm, SGU_WIDTH), lambda i: (i, 0)),
    ]
    extra_inputs = []
    if kin_past is not None:
        past = kin_past.shape[0] // grid[0]
        extra_inputs.append(kin_past)
        in_specs.append(pl.BlockSpec((past, 2 * LANES), lambda i: (i, 0)))
        kv_len = past + tm
        out_shape[1] = jax.ShapeDtypeStruct((QK_WIDTH, grid[0] * kv_len), BF16)
        out_shape[2] = jax.ShapeDtypeStruct((grid[0] * kv_len, MLA_WIDTH), BF16)
        out_specs[1] = pl.BlockSpec((QK_WIDTH, kv_len), lambda i: (0, i))
        out_specs[2] = pl.BlockSpec((kv_len, MLA_WIDTH), lambda i: (i, 0))
    if emit_cache:
        kr_rows = tm // SUB_ROWS * QK_ROPE
        out_shape += [jax.ShapeDtypeStruct((n_tok, KV_LORA), F32),
                      jax.ShapeDtypeStruct((n_tok // SUB_ROWS * QK_ROPE, SUB_ROWS), F32)]
        out_specs += [pl.BlockSpec((tm, KV_LORA), lambda i: (i, 0)),
                      pl.BlockSpec((kr_rows, SUB_ROWS), lambda i: (i, 0))]
        for w in (wts["w_o32"], wts["w_ff1_32"], wts["w_ff2_32"]):
            rows, cols = w.shape
            blk = (rows // grid[0], cols)
            extra_inputs.append(w)
            in_specs.append(pl.BlockSpec(blk, lambda i: (i, 0)))
            out_shape.append(jax.ShapeDtypeStruct(w.shape, BF16))
            out_specs.append(pl.BlockSpec(blk, lambda i: (i, 0)))
    return pl.pallas_call(
        functools.partial(_phase1_kernel, emit_cache, mod_row_fn),
        out_shape=out_shape,
        grid=grid,
        in_specs=in_specs,
        out_specs=out_specs,
        compiler_params=pltpu.CompilerParams(
            dimension_semantics=("arbitrary",), vmem_limit_bytes=VMEM_LIMIT),
        name="phase1_ctx" if emit_cache else "phase1_lat",
    )(x2d, mod3, ktab, qtab, wts["g_attn_pre"], wts["w_attn"], wts["w_gate"], wts["g_q"], wts["w_uq"],
      wts["g_kv"], wts["w_k"], wts["w_uv"], wts["g_sgu"], wts["beta_sgu"], wts["w_sgu"],
      wts["bias_sgu"], *extra_inputs)


def _interleave(*gens):
    results = [None] * len(gens)
    live = list(range(len(gens)))
    while live:
        for i in list(live):
            try:
                next(gens[i])
            except StopIteration as stop:
                results[i] = stop.value
                live.remove(i)
    return results


def _run(gen):
    return _interleave(gen)[0]


def _attend(q_ref, kv_views, r0, low_half):
    def head_scores(hd):
        qh = q_ref[r0:r0 + SUB_ROWS, hd * HEAD_SLOT:(hd + 1) * HEAD_SLOT]
        return [_dot(qh, k_view(hd * HEAD_SLOT, HEAD_SLOT)) for k_view, _ in kv_views]

    def head_probs(scores):
        m = scores[0].max(axis=-1, keepdims=True)
        for s in scores[1:]:
            m = jnp.maximum(m, s.max(axis=-1, keepdims=True))
        probs = []
        denom = None
        for s in scores:
            p = jnp.exp2(s - m)
            ps = p.sum(axis=-1, keepdims=True)
            denom = ps if denom is None else denom + ps
            probs.append(p.astype(BF16))
        return probs, denom

    def head_values(hd, probs, denom):
        slab = hd // 2
        acc = None
        for (_, v_view), p in zip(kv_views, probs):
            pv = _dot(p, v_view(slab * LANES, LANES))
            acc = pv if acc is None else acc + pv
        return acc / denom

    pair_out = []
    head_out = None
    ahead = [head_scores(hd) for hd in range(QK_AHEAD)]
    for hd in range(MLA_HEADS):
        if hd + QK_AHEAD < MLA_HEADS:
            ahead.append(head_scores(hd + QK_AHEAD))
        o = head_values(hd, *head_probs(ahead.pop(0)))
        if hd % 2 == 0:
            head_out = o
        else:
            pair_out.append(jnp.where(low_half, head_out, o).astype(BF16))
            yield
    return jnp.concatenate(pair_out, axis=1)


def _phase2_kernel(n_kv, shared_kv, mod_row_fn, x_ref, mod_ref, q_ref, sgu_ref, *refs):
    kv_refs = refs[:2 * n_kv]
    (w_o_ref, g_post_ref, g_fpre_ref, g_fpost_ref, w_ff1_ref, w_ff2_ref, o_ref) = refs[2 * n_kv:]
    tq = x_ref.shape[0]
    low_half = lax.broadcasted_iota(jnp.int32, (SUB_ROWS, LANES), 1) < V_HEAD
    _, _, gate_a, shift_f, scale_f, gate_f = _mod_rows(mod_ref, mod_row_fn(pl.program_id(0)))
    def attention(r0):
        views = []
        for t in range(n_kv):
            k_ref, v_ref = kv_refs[2 * t], kv_refs[2 * t + 1]
            if shared_kv:
                views.append((lambda c, w, k_ref=k_ref: k_ref[c:c + w, :],
                              lambda c, w, v_ref=v_ref: v_ref[:, c:c + w]))
            else:
                views.append((lambda c, w, k_ref=k_ref: k_ref[c:c + w, r0:r0 + SUB_ROWS],
                              lambda c, w, v_ref=v_ref: v_ref[r0:r0 + SUB_ROWS, c:c + w]))
        return _attend(q_ref, views, r0, low_half)

    def mixer_proj(r0, attn):
        mix_in = jnp.concatenate([attn, sgu_ref[r0:r0 + SUB_ROWS, :]], axis=1)
        return _dot(mix_in, w_o_ref[...])

    def mixer_norm(r0, mix):
        x1 = x_ref[r0:r0 + SUB_ROWS, :] + gate_a * _rms(mix, g_post_ref[...])
        h = (_rms(x1, g_fpre_ref[...]) * (1.0 + scale_f) + shift_f).astype(BF16)
        return x1, h

    def ffn(h):
        f = None
        for c in range(D_FF // FF_CHUNK):
            hid = jnp.maximum(_dot(h, w_ff1_ref[:, c * FF_CHUNK:(c + 1) * FF_CHUNK]), 0.0)
            part = _dot((hid * hid).astype(BF16), w_ff2_ref[c * FF_CHUNK:(c + 1) * FF_CHUNK, :])
            f = part if f is None else f + part
            yield
        return f

    def finish(r0, x1, f):
        o_ref[r0:r0 + SUB_ROWS, :] = x1 + gate_f * _rms(f, g_fpost_ref[...])

    starts = [a * SUB_ROWS for a in range(tq // SUB_ROWS)]
    x1h = {r0: mixer_norm(r0, mixer_proj(r0, _run(attention(r0)))) for r0 in starts}
    for r0 in starts:
        finish(r0, x1h[r0][0], _run(ffn(x1h[r0][1])))


def _phase2(x2d, mod3, mod_row_fn, q, sgu, kv_list, kv_rows, shared_kv, tq, n_outer, n_inner, wts,
            name):
    n_kv = len(kv_list) // 2
    in_specs = [
        pl.BlockSpec((tq, D_MODEL), lambda b, i: (b * n_inner + i, 0)),
        _const_spec((MOD_ROWS, N_MOD * D_MODEL)),
        pl.BlockSpec((tq, QK_WIDTH), lambda b, i: (b * n_inner + i, 0)),
        pl.BlockSpec((tq, SGU_WIDTH), lambda b, i: (b * n_inner + i, 0)),
    ]
    for t in range(n_kv):
        in_specs.append(pl.BlockSpec((QK_WIDTH, kv_rows[t]), lambda b, i: (0, b)))
        in_specs.append(pl.BlockSpec((kv_rows[t], MLA_WIDTH), lambda b, i: (b, 0)))
    in_specs += [
        _const_spec((D_MODEL, D_MODEL)),
        _const_spec((1, D_MODEL)),
        _const_spec((1, D_MODEL)),
        _const_spec((1, D_MODEL)),
        _const_spec((D_MODEL, D_FF)),
        _const_spec((D_FF, D_MODEL)),
    ]
    return pl.pallas_call(
        functools.partial(_phase2_kernel, n_kv, shared_kv, mod_row_fn),
        out_shape=jax.ShapeDtypeStruct(x2d.shape, F32),
        grid=(n_outer, n_inner),
        in_specs=in_specs,
        out_specs=pl.BlockSpec((tq, D_MODEL), lambda b, i: (b * n_inner + i, 0)),
        compiler_params=pltpu.CompilerParams(
            dimension_semantics=("arbitrary", "arbitrary"), vmem_limit_bytes=VMEM_LIMIT),
        name=name,
    )(x2d, mod3, q, sgu, *kv_list, wts["w_o"], wts["g_attn_post"], wts["g_ffn_pre"],
      wts["g_ffn_post"], wts["w_ff1"], wts["w_ff2"])


def _pair_swap(w):
    shp = w.shape
    return w.reshape(shp[:-1] + (shp[-1] // 2, 2))[..., ::-1].reshape(shp)


def _prepare_weights(g_attn_pre, g_attn_post, g_q, w_uq, g_kv, w_ukv, w_sgu, b_sgu, g_sgu,
                     beta_sgu, w_o, g_ffn_pre, g_ffn_post, w_ff1, w_ff2):
    w_uq_h = w_uq.reshape(Q_LORA, MLA_HEADS, QK_NOPE + QK_ROPE)
    w_uq_ext = jnp.concatenate(
        [w_uq_h, _pair_swap(w_uq_h[..., QK_NOPE:])], axis=-1).reshape(Q_LORA, QK_WIDTH).astype(BF16)

    w_ukv_h = w_ukv.reshape(KV_LORA, MLA_HEADS, QK_NOPE + V_HEAD)
    w_uk_slots = jnp.concatenate(
        [w_ukv_h[..., :QK_NOPE], jnp.zeros((KV_LORA, MLA_HEADS, HEAD_SLOT - QK_NOPE), F32)],
        axis=-1).reshape(KV_LORA, QK_WIDTH)
    eye = jnp.eye(QK_ROPE, dtype=F32)
    zeros_rope = jnp.zeros((QK_ROPE, QK_ROPE), F32)
    zeros_nope = jnp.zeros((QK_ROPE, QK_NOPE), F32)
    to_lo = jnp.tile(jnp.concatenate([zeros_nope, eye, zeros_rope], axis=1), (1, MLA_HEADS))
    to_hi = jnp.tile(jnp.concatenate([zeros_nope, zeros_rope, eye], axis=1), (1, MLA_HEADS))
    w_k = jnp.concatenate([w_uk_slots, to_lo, to_lo, to_hi, to_hi], axis=0).T.astype(BF16)
    w_uv = w_ukv_h[..., QK_NOPE:].reshape(KV_LORA, MLA_WIDTH).astype(BF16)

    w_sgu_pair = w_sgu.reshape(SGU_HEADS // 2, 2 * CHUNK, CHUNK).astype(BF16)
    bias_sgu = jnp.repeat(b_sgu.T, SGU_HEAD_DIM, axis=1)
    row = lambda a: a.reshape(1, -1)
    return {
        "g_attn_pre": row(g_attn_pre), "g_attn_post": row(g_attn_post),
        "g_q": row(g_q), "w_uq": w_uq_ext, "g_kv": row(g_kv), "w_k": w_k, "w_uv": w_uv,
        "g_sgu": row(g_sgu), "beta_sgu": row(beta_sgu), "w_sgu": w_sgu_pair, "bias_sgu": bias_sgu,
        "w_o32": w_o, "g_ffn_pre": row(g_ffn_pre), "g_ffn_post": row(g_ffn_post),
        "w_ff1_32": w_ff1, "w_ff2_32": w_ff2,
    }


def _rope_tables(n_tok):
    rows = n_tok // GRID_W
    row = np.repeat(np.arange(rows), GRID_W).astype(np.float32)
    col = np.tile(np.arange(GRID_W), rows).astype(np.float32)
    freqs = (1.0 / (ROPE_BASE ** (np.arange(AXIS_PAIRS, dtype=np.float32) / AXIS_PAIRS))).astype(
        np.float32)
    ang = np.concatenate([row[:, None] * freqs, col[:, None] * freqs], axis=-1)
    cos = np.repeat(np.cos(ang), 2, axis=1)
    sin = np.repeat(np.sin(ang), 2, axis=1) * np.tile(np.array([-1.0, 1.0], np.float32), QK_ROPE // 2)
    ktab = np.concatenate([cos, sin, cos, sin], axis=1)
    qtab = QUERY_SCALE * np.concatenate([np.ones((n_tok, QK_NOPE), np.float32), cos, sin], axis=1)
    return jnp.asarray(ktab, F32), jnp.asarray(qtab, F32)


def kernel(x_prompt, x_sample, cache_ckv, cache_krope, c, c_ctx, w_mod, b_mod, g_attn_pre,
           g_attn_post, w_in, g_q, w_uq, g_kv, w_ukv, w_sgu, b_sgu, g_sgu, beta_sgu, w_o,
           g_ffn_pre, g_ffn_post, w_ff1, w_ff2):
    batch, seq, _ = x_prompt.shape
    dec_batch, dec_seq, _ = x_sample.shape
    past_len = cache_ckv.shape[2]
    depth = w_mod.shape[0]
    assert depth == 1

    wts = _prepare_weights(g_attn_pre[0], g_attn_post[0], g_q[0], w_uq[0], g_kv[0],
                           w_ukv[0], w_sgu[0], b_sgu[0], g_sgu[0], beta_sgu[0], w_o[0],
                           g_ffn_pre[0], g_ffn_post[0], w_ff1[0], w_ff2[0])

    cond8 = jnp.concatenate(
        [c_ctx[None, :], c, jnp.zeros((MOD_ROWS - 1 - dec_batch, D_MODEL), F32)], axis=0)
    mod3, w_attn, w_gate = _modulation(cond8, w_mod[0], b_mod[0].reshape(1, -1), w_in[0].T)
    wts = dict(wts, w_attn=w_attn, w_gate=w_gate)

    ktab_ctx = jnp.asarray(np.arange(LANES)[None, :] < QK_ROPE, F32)
    qtab_ctx = jnp.asarray(QUERY_SCALE * (np.arange(LANES)[None, :] < QK_NOPE + QK_ROPE), F32)
    xp2d = x_prompt.reshape(batch * seq, D_MODEL)
    q_c, k_c, v_c, sgu_c, ckv_c, kr_c, w_o16, w_ff1_16, w_ff2_16 = _phase1(
        xp2d, mod3, lambda i: 0, ktab_ctx, qtab_ctx, lambda i: 0, wts, True)
    wts = dict(wts, w_o=w_o16, w_ff1=w_ff1_16, w_ff2=w_ff2_16)
    assert seq == SUB_ROWS and dec_seq % PHASE2_ROWS_LAT == 0
    y_prompt = _phase2(xp2d, mod3, lambda b: 0, q_c, sgu_c, [k_c, v_c], [PHASE2_ROWS_CTX], False,
                       PHASE2_ROWS_CTX, batch * seq // PHASE2_ROWS_CTX, 1, wts, "phase2_ctx")

    ktab_lat, qtab_lat = _rope_tables(dec_seq)
    tiles_per_seq = dec_seq // TOKEN_TILE
    xs2d = x_sample.reshape(dec_batch * dec_seq, D_MODEL)
    assert tiles_per_seq == 1
    kr_p = cache_krope[:, 0].reshape(dec_batch * past_len, QK_ROPE)
    pad = jnp.zeros_like(kr_p)
    kin_p = jnp.concatenate(
        [cache_ckv[:, 0].reshape(dec_batch * past_len, KV_LORA), kr_p, pad, kr_p, pad], axis=1)
    q_l, k_l, v_l, sgu_l = _phase1(
        xs2d, mod3, lambda i: 1 + i, ktab_lat, qtab_lat, lambda i: 0, wts, False, kin_p)
    y_sample = _phase2(xs2d, mod3, lambda b: 1 + b, q_l, sgu_l, [k_l, v_l],
                       [past_len + dec_seq], True, PHASE2_ROWS_LAT, dec_batch,
                       dec_seq // PHASE2_ROWS_LAT, wts, "phase2_lat")

    return (y_prompt.reshape(batch, seq, D_MODEL),
            y_sample.reshape(dec_batch, dec_seq, D_MODEL),
            ckv_c.reshape(batch, 1, seq, KV_LORA),
            kr_c.reshape(batch, 1, QK_ROPE, seq).transpose(0, 1, 3, 2))
```

```python
import functools
import math

import jax
import jax.numpy as jnp
import numpy as np
from jax import lax
from jax.experimental import pallas as pl
from jax.experimental.pallas import tpu as pltpu

D_MODEL = 1024
GRID_W = 64
MLA_HEADS = 8
QK_NOPE = 64
QK_ROPE = 32
V_HEAD = 64
Q_LORA = 256
KV_LORA = 128
MLA_WIDTH = MLA_HEADS * V_HEAD
SGU_HEADS = 8
SGU_WIDTH = D_MODEL - MLA_WIDTH
SGU_HEAD_DIM = SGU_WIDTH // SGU_HEADS
CHUNK = 128
D_FF = 4 * D_MODEL
AXIS_PAIRS = QK_ROPE // 4
ROPE_BASE = 10000.0
EPS = 1e-6
N_MOD = 6
ATTN_SCALE = (QK_NOPE + QK_ROPE) ** -0.5
QUERY_SCALE = ATTN_SCALE * math.log2(math.e)

LANES = 128
HEAD_SLOT = LANES
QK_WIDTH = MLA_HEADS * HEAD_SLOT
ATTN_PROJ = Q_LORA + KV_LORA + LANES
MOD_ROWS = 8
MOD_K_ROWS = 128
MOD_STREAMS = 4
TOKEN_TILE = 1024
PHASE1_CHAIN = 1024
SUB_ROWS = 256
PHASE2_ROWS_CTX = 512
PHASE2_ROWS_LAT = 512
QK_AHEAD = 1
FF_CHUNK = 1024
VMEM_LIMIT = 56 * 1024 * 1024

BF16 = jnp.bfloat16
F32 = jnp.float32


def _dot(a, b):
    return jnp.dot(a, b, preferred_element_type=F32)


def _dot_nt(a, b):
    return lax.dot_general(a, b, (((1,), (1,)), ((), ())), preferred_element_type=F32)


def _rms(x, g):
    return x * lax.rsqrt(jnp.mean(x * x, axis=-1, keepdims=True) + EPS) * g


def _gelu(x):
    inner = math.sqrt(2.0 / math.pi) * (x + 0.044715 * (x * x * x))
    return x * (0.5 * (1.0 + jnp.tanh(inner)))


def _mod_kernel(*refs):
    cond_refs = refs[:MOD_STREAMS]
    w_refs = refs[MOD_STREAMS:2 * MOD_STREAMS]
    (b_ref, w_in_attn_ref, w_in_gate_ref, o_ref, w_attn_ref, w_gate_ref) = refs[2 * MOD_STREAMS:]

    @pl.when(pl.program_id(0) == 0)
    def _():
        o_ref[...] = jnp.broadcast_to(b_ref[...], o_ref.shape)

    def part(cond_ref, w_ref):
        cnd = cond_ref[...]
        act = cnd * (1.0 / (1.0 + jnp.exp(-cnd)))
        return _dot(act.astype(BF16), w_ref[...].astype(BF16))

    acc = part(cond_refs[0], w_refs[0])
    for cond_ref, w_ref in zip(cond_refs[1:], w_refs[1:]):
        acc = acc + part(cond_ref, w_ref)
    o_ref[...] += acc

    w_gate_ref[...] = w_in_gate_ref[...].astype(BF16)

    @pl.when(pl.program_id(0) == 0)
    def _():
        rope0 = Q_LORA + KV_LORA
        kr = w_in_attn_ref[rope0:rope0 + QK_ROPE, :]
        row = lax.broadcasted_iota(jnp.int32, kr.shape, 0)
        kr_sw = jnp.where(row % 2 == 0, pltpu.roll(kr, QK_ROPE - 1, 0), pltpu.roll(kr, 1, 0))
        w_attn_ref[0:rope0 + QK_ROPE, :] = w_in_attn_ref[...].astype(BF16)
        w_attn_ref[rope0 + QK_ROPE:ATTN_PROJ, :] = jnp.concatenate(
            [kr_sw, kr, kr_sw], axis=0).astype(BF16)


def _modulation(cond8, w_mod, b_mod, w_in_t):
    n = w_mod.shape[1]
    tk = MOD_K_ROWS
    nk = D_MODEL // tk // MOD_STREAMS
    attn_rows = Q_LORA + KV_LORA + QK_ROPE
    gate_rows = 2 * SGU_WIDTH // nk
    cond_specs = [pl.BlockSpec((MOD_ROWS, tk), lambda k, s=s: (0, k + s * nk))
                  for s in range(MOD_STREAMS)]
    w_specs = [pl.BlockSpec((tk, n), lambda k, s=s: (k + s * nk, 0)) for s in range(MOD_STREAMS)]
    return pl.pallas_call(
        _mod_kernel,
        out_shape=[jax.ShapeDtypeStruct((MOD_ROWS, n), F32),
                   jax.ShapeDtypeStruct((ATTN_PROJ, D_MODEL), BF16),
                   jax.ShapeDtypeStruct((2 * SGU_WIDTH, D_MODEL), BF16)],
        grid=(nk,),
        in_specs=cond_specs + w_specs + [
            pl.BlockSpec((1, n), lambda k: (0, 0)),
            pl.BlockSpec((attn_rows, D_MODEL), lambda k: (0, 0)),
            pl.BlockSpec((pl.Element(gate_rows), pl.Element(D_MODEL)),
                         lambda k: (pl.multiple_of(attn_rows + k * gate_rows, QK_ROPE), 0)),
        ],
        out_specs=[pl.BlockSpec((MOD_ROWS, n), lambda k: (0, 0)),
                   pl.BlockSpec((ATTN_PROJ, D_MODEL), lambda k: (0, 0)),
                   pl.BlockSpec((gate_rows, D_MODEL), lambda k: (k, 0))],
        compiler_params=pltpu.CompilerParams(
            dimension_semantics=("arbitrary",), vmem_limit_bytes=VMEM_LIMIT),
        name="modulation",
    )(*([cond8] * MOD_STREAMS), *([w_mod] * MOD_STREAMS), b_mod, w_in_t, w_in_t)


def _mod_rows(mod_ref, row):
    vec = mod_ref[pl.ds(row, 1), :]
    return [vec[:, j * D_MODEL:(j + 1) * D_MODEL] for j in range(N_MOD)]


def _phase1_kernel(emit_cache, mod_row_fn, x_ref, mod_ref, ktab_ref, qtab_ref, g_pre_ref, w_attn_ref,
                   w_gate_ref,
                   g_q_ref, w_uq_ref, g_kv_ref, w_k_ref, w_uv_ref, g_sgu_ref, beta_sgu_ref,
                   w_sgu_ref, bias_ref, *rest):
    if emit_cache:
        (w_o32_ref, w_ff1_32_ref, w_ff2_32_ref, q_ref, k_ref, v_ref, sgu_ref, ckv_ref, kr_ref,
         w_o16_ref, w_ff1_16_ref, w_ff2_16_ref) = rest
        w_o16_ref[...] = w_o32_ref[...].astype(BF16)
        w_ff1_16_ref[...] = w_ff1_32_ref[...].astype(BF16)
        w_ff2_16_ref[...] = w_ff2_32_ref[...].astype(BF16)
    else:
        q_ref, k_ref, v_ref, sgu_ref = rest
    tm = x_ref.shape[0]
    mod = _mod_rows(mod_ref, mod_row_fn(pl.program_id(0)))
    shift_a, scale_a = mod[0], mod[1]
    low_half = lax.broadcasted_iota(jnp.int32, (CHUNK, LANES), 1) < SGU_HEAD_DIM
    per_pos_tables = ktab_ref.shape[0] > 1

    def project(r0):
        rows = slice(r0, r0 + PHASE1_CHAIN)
        h = (_rms(x_ref[rows, :], g_pre_ref[...]) * (1.0 + scale_a) + shift_a).astype(BF16)
        proj = _dot_nt(h, w_attn_ref[...])
        gate = _dot_nt(h, w_gate_ref[...])
        return proj, gate

    def expand_qkv(r0, proj):
        rows = slice(r0, r0 + PHASE1_CHAIN)
        tab_rows = rows if per_pos_tables else slice(None)
        cq = _rms(proj[:, 0:Q_LORA], g_q_ref[...])
        q = _dot(cq.astype(BF16), w_uq_ref[...])
        qtab = qtab_ref[tab_rows, :]
        q_ref[rows, :] = jnp.concatenate(
            [(q[:, s * HEAD_SLOT:(s + 1) * HEAD_SLOT] * qtab).astype(BF16)
             for s in range(MLA_HEADS)], axis=1)
        ckv_n = _rms(proj[:, Q_LORA:Q_LORA + KV_LORA], g_kv_ref[...])
        rope_slab = proj[:, Q_LORA + KV_LORA:Q_LORA + KV_LORA + LANES]
        if emit_cache:
            ckv_ref[rows, :] = ckv_n
            kr_t = rope_slab.T
            for e in range(PHASE1_CHAIN // SUB_ROWS):
                elem = r0 // SUB_ROWS + e
                kr_ref[elem * QK_ROPE:(elem + 1) * QK_ROPE, :] = (
                    kr_t[0:QK_ROPE, e * SUB_ROWS:(e + 1) * SUB_ROWS])
        ckv_b = ckv_n.astype(BF16)
        kin = jnp.concatenate([ckv_b, (rope_slab * ktab_ref[tab_rows, :]).astype(BF16)], axis=1)
        k_ref[:, rows] = _dot_nt(w_k_ref[...], kin).astype(BF16)
        v_ref[rows, :] = _dot(ckv_b, w_uv_ref[...]).astype(BF16)

    def gating_unit(r0, gate):
        vv = _gelu(gate[:, SGU_WIDTH:2 * SGU_WIDTH])
        mu = jnp.mean(vv, axis=-1, keepdims=True)
        vc = vv - mu
        var = jnp.mean(vc * vc, axis=-1, keepdims=True)
        vn = (vc * lax.rsqrt(var + EPS) * g_sgu_ref[...] + beta_sgu_ref[...]).astype(BF16)
        n_chunks = PHASE1_CHAIN // CHUNK
        for j in range(SGU_WIDTH // LANES):
            lanes = slice(j * LANES, (j + 1) * LANES)
            rhs = jnp.concatenate(
                [vn[n * CHUNK:(n + 1) * CHUNK, lanes] for n in range(n_chunks)], axis=1)
            o = _dot(w_sgu_ref[j], rhs)
            u = _gelu(gate[:, lanes])
            bias = bias_ref[:, lanes]
            for n in range(n_chunks):
                even = o[0:CHUNK, n * LANES:(n + 1) * LANES]
                odd = o[CHUNK:2 * CHUNK, n * LANES:(n + 1) * LANES]
                mixed = jnp.where(low_half, even, odd) + bias
                sgu_ref[r0 + n * CHUNK:r0 + (n + 1) * CHUNK, lanes] = (
                    u[n * CHUNK:(n + 1) * CHUNK, :] * mixed).astype(BF16)

    starts = list(range(0, tm, PHASE1_CHAIN))
    ahead = project(starts[0])
    for i, r0 in enumerate(starts):
        proj, gate = ahead
        if i + 1 < len(starts):
            ahead = project(starts[i + 1])
        expand_qkv(r0, proj)
        gating_unit(r0, gate)


def _const_spec(shape):
    nd = len(shape)
    return pl.BlockSpec(shape, lambda *_: (0,) * nd, pipeline_mode=pl.Buffered(1))


def _phase1(x2d, mod3, mod_row_fn, ktab, qtab, tab_fn, wts, emit_cache):
    n_tok = x2d.shape[0]
    tm = TOKEN_TILE
    grid = (n_tok // tm,)
    tab_block = (ktab.shape[0] if ktab.shape[0] == 1 else tm, LANES)
    in_specs = [
        pl.BlockSpec((tm, D_MODEL), lambda i: (i, 0)),
        _const_spec((MOD_ROWS, N_MOD * D_MODEL)),
        pl.BlockSpec(tab_block, lambda i: (tab_fn(i), 0)),
        pl.BlockSpec(tab_block, lambda i: (tab_fn(i), 0)),
        _const_spec((1, D_MODEL)),
        _const_spec((ATTN_PROJ, D_MODEL)),
        _const_spec((2 * SGU_WIDTH, D_MODEL)),
        _const_spec((1, Q_LORA)),
        _const_spec((Q_LORA, QK_WIDTH)),
        _const_spec((1, KV_LORA)),
        _const_spec((QK_WIDTH, 2 * LANES)),
        _const_spec((KV_LORA, MLA_WIDTH)),
        _const_spec((1, SGU_WIDTH)),
        _const_spec((1, SGU_WIDTH)),
        _const_spec((SGU_WIDTH // LANES, 2 * CHUNK, CHUNK)),
        _const_spec((CHUNK, SGU_WIDTH)),
    ]
    out_shape = [
        jax.ShapeDtypeStruct((n_tok, QK_WIDTH), BF16),
        jax.ShapeDtypeStruct((QK_WIDTH, n_tok), BF16),
        jax.ShapeDtypeStruct((n_tok, MLA_WIDTH), BF16),
        jax.ShapeDtypeStruct((n_tok, SGU_WIDTH), BF16),
    ]
    out_specs = [
        pl.BlockSpec((tm, QK_WIDTH), lambda i: (i, 0)),
        pl.BlockSpec((QK_WIDTH, tm), lambda i: (0, i)),
        pl.BlockSpec((tm, MLA_WIDTH), lambda i: (i, 0)),
        pl.BlockSpec((tm, SGU_WIDTH), lambda i: (i, 0)),
    ]
    extra_inputs = []
    if emit_cache:
        kr_rows = tm // SUB_ROWS * QK_ROPE
        out_shape += [jax.ShapeDtypeStruct((n_tok, KV_LORA), F32),
                      jax.ShapeDtypeStruct((n_tok // SUB_ROWS * QK_ROPE, SUB_ROWS), F32)]
        out_specs += [pl.BlockSpec((tm, KV_LORA), lambda i: (i, 0)),
                      pl.BlockSpec((kr_rows, SUB_ROWS), lambda i: (i, 0))]
        for w in (wts["w_o32"], wts["w_ff1_32"], wts["w_ff2_32"]):
            rows, cols = w.shape
            blk = (rows // grid[0], cols)
            extra_inputs.append(w)
            in_specs.append(pl.BlockSpec(blk, lambda i: (i, 0)))
            out_shape.append(jax.ShapeDtypeStruct(w.shape, BF16))
            out_specs.append(pl.BlockSpec(blk, lambda i: (i, 0)))
    return pl.pallas_call(
        functools.partial(_phase1_kernel, emit_cache, mod_row_fn),
        out_shape=out_shape,
        grid=grid,
        in_specs=in_specs,
        out_specs=out_specs,
        compiler_params=pltpu.CompilerParams(
            dimension_semantics=("arbitrary",), vmem_limit_bytes=VMEM_LIMIT),
        name="phase1_ctx" if emit_cache else "phase1_lat",
    )(x2d, mod3, ktab, qtab, wts["g_attn_pre"], wts["w_attn"], wts["w_gate"], wts["g_q"], wts["w_uq"],
      wts["g_kv"], wts["w_k"], wts["w_uv"], wts["g_sgu"], wts["beta_sgu"], wts["w_sgu"],
      wts["bias_sgu"], *extra_inputs)


def _cache_kv_kernel(kin_ref, w_k_ref, w_uv_ref, k_ref, v_ref):
    kin = kin_ref[...].astype(BF16)
    k_ref[...] = _dot_nt(w_k_ref[...], kin).astype(BF16)
    v_ref[...] = _dot(kin[:, 0:KV_LORA], w_uv_ref[...]).astype(BF16)


def _cache_kv(kin2d, wts):
    n_tok = kin2d.shape[0]
    tm = n_tok
    return pl.pallas_call(
        _cache_kv_kernel,
        out_shape=[jax.ShapeDtypeStruct((QK_WIDTH, n_tok), BF16),
                   jax.ShapeDtypeStruct((n_tok, MLA_WIDTH), BF16)],
        grid=(n_tok // tm,),
        in_specs=[
            pl.BlockSpec((tm, 2 * LANES), lambda i: (i, 0)),
            _const_spec((QK_WIDTH, 2 * LANES)),
            _const_spec((KV_LORA, MLA_WIDTH)),
        ],
        out_specs=[pl.BlockSpec((QK_WIDTH, tm), lambda i: (0, i)),
                   pl.BlockSpec((tm, MLA_WIDTH), lambda i: (i, 0))],
        compiler_params=pltpu.CompilerParams(
            dimension_semantics=("arbitrary",), vmem_limit_bytes=VMEM_LIMIT),
        name="cache_kv",
    )(kin2d, wts["w_k"], wts["w_uv"])


def _interleave(*gens):
    results = [None] * len(gens)
    live = list(range(len(gens)))
    while live:
        for i in list(live):
            try:
                next(gens[i])
            except StopIteration as stop:
                results[i] = stop.value
                live.remove(i)
    return results


def _run(gen):
    return _interleave(gen)[0]


def _attend(q_ref, kv_views, r0, low_half):
    def head_scores(hd):
        qh = q_ref[r0:r0 + SUB_ROWS, hd * HEAD_SLOT:(hd + 1) * HEAD_SLOT]
        return [_dot(qh, k_view(hd * HEAD_SLOT, HEAD_SLOT)) for k_view, _ in kv_views]

    def head_probs(scores):
        m = scores[0].max(axis=-1, keepdims=True)
        for s in scores[1:]:
            m = jnp.maximum(m, s.max(axis=-1, keepdims=True))
        probs = []
        denom = None
        for s in scores:
            p = jnp.exp2(s - m)
            ps = p.sum(axis=-1, keepdims=True)
            denom = ps if denom is None else denom + ps
            probs.append(p.astype(BF16))
        return probs, denom

    def head_values(hd, probs, denom):
        slab = hd // 2
        acc = None
        for (_, v_view), p in zip(kv_views, probs):
            pv = _dot(p, v_view(slab * LANES, LANES))
            acc = pv if acc is None else acc + pv
        return acc / denom

    pair_out = []
    head_out = None
    ahead = [head_scores(hd) for hd in range(QK_AHEAD)]
    for hd in range(MLA_HEADS):
        if hd + QK_AHEAD < MLA_HEADS:
            ahead.append(head_scores(hd + QK_AHEAD))
        o = head_values(hd, *head_probs(ahead.pop(0)))
        if hd % 2 == 0:
            head_out = o
        else:
            pair_out.append(jnp.where(low_half, head_out, o).astype(BF16))
            yield
    return jnp.concatenate(pair_out, axis=1)


def _phase2_kernel(n_kv, shared_kv, mod_row_fn, x_ref, mod_ref, q_ref, sgu_ref, *refs):
    kv_refs = refs[:2 * n_kv]
    (w_o_ref, g_post_ref, g_fpre_ref, g_fpost_ref, w_ff1_ref, w_ff2_ref, o_ref) = refs[2 * n_kv:]
    tq = x_ref.shape[0]
    low_half = lax.broadcasted_iota(jnp.int32, (SUB_ROWS, LANES), 1) < V_HEAD
    _, _, gate_a, shift_f, scale_f, gate_f = _mod_rows(mod_ref, mod_row_fn(pl.program_id(0)))
    def attention(r0):
        views = []
        for t in range(n_kv):
            k_ref, v_ref = kv_refs[2 * t], kv_refs[2 * t + 1]
            if shared_kv:
                views.append((lambda c, w, k_ref=k_ref: k_ref[c:c + w, :],
                              lambda c, w, v_ref=v_ref: v_ref[:, c:c + w]))
            else:
                views.append((lambda c, w, k_ref=k_ref: k_ref[c:c + w, r0:r0 + SUB_ROWS],
                              lambda c, w, v_ref=v_ref: v_ref[r0:r0 + SUB_ROWS, c:c + w]))
        return _attend(q_ref, views, r0, low_half)

    def mixer_proj(r0, attn):
        mix_in = jnp.concatenate([attn, sgu_ref[r0:r0 + SUB_ROWS, :]], axis=1)
        return _dot(mix_in, w_o_ref[...])

    def mixer_norm(r0, mix):
        x1 = x_ref[r0:r0 + SUB_ROWS, :] + gate_a * _rms(mix, g_post_ref[...])
        h = (_rms(x1, g_fpre_ref[...]) * (1.0 + scale_f) + shift_f).astype(BF16)
        return x1, h

    def ffn(h):
        f = None
        for c in range(D_FF // FF_CHUNK):
            hid = jnp.maximum(_dot(h, w_ff1_ref[:, c * FF_CHUNK:(c + 1) * FF_CHUNK]), 0.0)
            part = _dot((hid * hid).astype(BF16), w_ff2_ref[c * FF_CHUNK:(c + 1) * FF_CHUNK, :])
            f = part if f is None else f + part
            yield
        return f

    def finish(r0, x1, f):
        o_ref[r0:r0 + SUB_ROWS, :] = x1 + gate_f * _rms(f, g_fpost_ref[...])

    starts = [a * SUB_ROWS for a in range(tq // SUB_ROWS)]
    x1h = {r0: mixer_norm(r0, mixer_proj(r0, _run(attention(r0)))) for r0 in starts}
    for r0 in starts:
        finish(r0, x1h[r0][0], _run(ffn(x1h[r0][1])))


def _phase2(x2d, mod3, mod_row_fn, q, sgu, kv_list, kv_rows, shared_kv, tq, n_outer, n_inner, wts,
            name):
    n_kv = len(kv_list) // 2
    in_specs = [
        pl.BlockSpec((tq, D_MODEL), lambda b, i: (b * n_inner + i, 0)),
        _const_spec((MOD_ROWS, N_MOD * D_MODEL)),
        pl.BlockSpec((tq, QK_WIDTH), lambda b, i: (b * n_inner + i, 0)),
        pl.BlockSpec((tq, SGU_WIDTH), lambda b, i: (b * n_inner + i, 0)),
    ]
    for t in range(n_kv):
        in_specs.append(pl.BlockSpec((QK_WIDTH, kv_rows[t]), lambda b, i: (0, b)))
        in_specs.append(pl.BlockSpec((kv_rows[t], MLA_WIDTH), lambda b, i: (b, 0)))
    in_specs += [
        _const_spec((D_MODEL, D_MODEL)),
        _const_spec((1, D_MODEL)),
        _const_spec((1, D_MODEL)),
        _const_spec((1, D_MODEL)),
        _const_spec((D_MODEL, D_FF)),
        _const_spec((D_FF, D_MODEL)),
    ]
    return pl.pallas_call(
        functools.partial(_phase2_kernel, n_kv, shared_kv, mod_row_fn),
        out_shape=jax.ShapeDtypeStruct(x2d.shape, F32),
        grid=(n_outer, n_inner),
        in_specs=in_specs,
        out_specs=pl.BlockSpec((tq, D_MODEL), lambda b, i: (b * n_inner + i, 0)),
        compiler_params=pltpu.CompilerParams(
            dimension_semantics=("arbitrary", "arbitrary"), vmem_limit_bytes=VMEM_LIMIT),
        name=name,
    )(x2d, mod3, q, sgu, *kv_list, wts["w_o"], wts["g_attn_post"], wts["g_ffn_pre"],
      wts["g_ffn_post"], wts["w_ff1"], wts["w_ff2"])


def _pair_swap(w):
    shp = w.shape
    return w.reshape(shp[:-1] + (shp[-1] // 2, 2))[..., ::-1].reshape(shp)


def _prepare_weights(g_attn_pre, g_attn_post, g_q, w_uq, g_kv, w_ukv, w_sgu, b_sgu, g_sgu,
                     beta_sgu, w_o, g_ffn_pre, g_ffn_post, w_ff1, w_ff2):
    w_uq_h = w_uq.reshape(Q_LORA, MLA_HEADS, QK_NOPE + QK_ROPE)
    w_uq_ext = jnp.concatenate(
        [w_uq_h, _pair_swap(w_uq_h[..., QK_NOPE:])], axis=-1).reshape(Q_LORA, QK_WIDTH).astype(BF16)

    w_ukv_h = w_ukv.reshape(KV_LORA, MLA_HEADS, QK_NOPE + V_HEAD)
    w_uk_slots = jnp.concatenate(
        [w_ukv_h[..., :QK_NOPE], jnp.zeros((KV_LORA, MLA_HEADS, HEAD_SLOT - QK_NOPE), F32)],
        axis=-1).reshape(KV_LORA, QK_WIDTH)
    eye = jnp.eye(QK_ROPE, dtype=F32)
    zeros_rope = jnp.zeros((QK_ROPE, QK_ROPE), F32)
    zeros_nope = jnp.zeros((QK_ROPE, QK_NOPE), F32)
    to_lo = jnp.tile(jnp.concatenate([zeros_nope, eye, zeros_rope], axis=1), (1, MLA_HEADS))
    to_hi = jnp.tile(jnp.concatenate([zeros_nope, zeros_rope, eye], axis=1), (1, MLA_HEADS))
    w_k = jnp.concatenate([w_uk_slots, to_lo, to_lo, to_hi, to_hi], axis=0).T.astype(BF16)
    w_uv = w_ukv_h[..., QK_NOPE:].reshape(KV_LORA, MLA_WIDTH).astype(BF16)

    w_sgu_pair = w_sgu.reshape(SGU_HEADS // 2, 2 * CHUNK, CHUNK).astype(BF16)
    bias_sgu = jnp.repeat(b_sgu.T, SGU_HEAD_DIM, axis=1)
    row = lambda a: a.reshape(1, -1)
    return {
        "g_attn_pre": row(g_attn_pre), "g_attn_post": row(g_attn_post),
        "g_q": row(g_q), "w_uq": w_uq_ext, "g_kv": row(g_kv), "w_k": w_k, "w_uv": w_uv,
        "g_sgu": row(g_sgu), "beta_sgu": row(beta_sgu), "w_sgu": w_sgu_pair, "bias_sgu": bias_sgu,
        "w_o32": w_o, "g_ffn_pre": row(g_ffn_pre), "g_ffn_post": row(g_ffn_post),
        "w_ff1_32": w_ff1, "w_ff2_32": w_ff2,
    }


def _rope_tables(n_tok):
    rows = n_tok // GRID_W
    row = np.repeat(np.arange(rows), GRID_W).astype(np.float32)
    col = np.tile(np.arange(GRID_W), rows).astype(np.float32)
    freqs = (1.0 / (ROPE_BASE ** (np.arange(AXIS_PAIRS, dtype=np.float32) / AXIS_PAIRS))).astype(
        np.float32)
    ang = np.concatenate([row[:, None] * freqs, col[:, None] * freqs], axis=-1)
    cos = np.repeat(np.cos(ang), 2, axis=1)
    sin = np.repeat(np.sin(ang), 2, axis=1) * np.tile(np.array([-1.0, 1.0], np.float32), QK_ROPE // 2)
    ktab = np.concatenate([cos, sin, cos, sin], axis=1)
    qtab = QUERY_SCALE * np.concatenate([np.ones((n_tok, QK_NOPE), np.float32), cos, sin], axis=1)
    return jnp.asarray(ktab, F32), jnp.asarray(qtab, F32)


def kernel(x_prompt, x_sample, cache_ckv, cache_krope, c, c_ctx, w_mod, b_mod, g_attn_pre,
           g_attn_post, w_in, g_q, w_uq, g_kv, w_ukv, w_sgu, b_sgu, g_sgu, beta_sgu, w_o,
           g_ffn_pre, g_ffn_post, w_ff1, w_ff2):
    batch, seq, _ = x_prompt.shape
    dec_batch, dec_seq, _ = x_sample.shape
    past_len = cache_ckv.shape[2]
    depth = w_mod.shape[0]
    assert depth == 1

    wts = _prepare_weights(g_attn_pre[0], g_attn_post[0], g_q[0], w_uq[0], g_kv[0],
                           w_ukv[0], w_sgu[0], b_sgu[0], g_sgu[0], beta_sgu[0], w_o[0],
                           g_ffn_pre[0], g_ffn_post[0], w_ff1[0], w_ff2[0])

    cond8 = jnp.concatenate(
        [c_ctx[None, :], c, jnp.zeros((MOD_ROWS - 1 - dec_batch, D_MODEL), F32)], axis=0)
    mod3, w_attn, w_gate = _modulation(cond8, w_mod[0], b_mod[0].reshape(1, -1), w_in[0].T)
    wts = dict(wts, w_attn=w_attn, w_gate=w_gate)

    ktab_ctx = jnp.asarray(np.arange(LANES)[None, :] < QK_ROPE, F32)
    qtab_ctx = jnp.asarray(QUERY_SCALE * (np.arange(LANES)[None, :] < QK_NOPE + QK_ROPE), F32)
    xp2d = x_prompt.reshape(batch * seq, D_MODEL)
    q_c, k_c, v_c, sgu_c, ckv_c, kr_c, w_o16, w_ff1_16, w_ff2_16 = _phase1(
        xp2d, mod3, lambda i: 0, ktab_ctx, qtab_ctx, lambda i: 0, wts, True)
    wts = dict(wts, w_o=w_o16, w_ff1=w_ff1_16, w_ff2=w_ff2_16)
    assert seq == SUB_ROWS and dec_seq % PHASE2_ROWS_LAT == 0
    y_prompt = _phase2(xp2d, mod3, lambda b: 0, q_c, sgu_c, [k_c, v_c], [PHASE2_ROWS_CTX], False,
                       PHASE2_ROWS_CTX, batch * seq // PHASE2_ROWS_CTX, 1, wts, "phase2_ctx")

    ktab_lat, qtab_lat = _rope_tables(dec_seq)
    tiles_per_seq = dec_seq // TOKEN_TILE
    xs2d = x_sample.reshape(dec_batch * dec_seq, D_MODEL)
    q_l, k_l, v_l, sgu_l = _phase1(
        xs2d, mod3, lambda i: 1 + i // tiles_per_seq, ktab_lat, qtab_lat,
        lambda i: i % tiles_per_seq, wts, False)
    kr_p = cache_krope[:, 0].reshape(dec_batch * past_len, QK_ROPE)
    pad = jnp.zeros_like(kr_p)
    kin_p = jnp.concatenate(
        [cache_ckv[:, 0].reshape(dec_batch * past_len, KV_LORA), kr_p, pad, kr_p, pad], axis=1)
    k_p, v_p = _cache_kv(kin_p, wts)
    y_sample = _phase2(xs2d, mod3, lambda b: 1 + b, q_l, sgu_l, [k_p, v_p, k_l, v_l],
                       [past_len, dec_seq], True, PHASE2_ROWS_LAT, dec_batch,
                       dec_seq // PHASE2_ROWS_LAT, wts, "phase2_lat")

    return (y_prompt.reshape(batch, seq, D_MODEL),
            y_sample.reshape(dec_batch, dec_seq, D_MODEL),
            ckv_c.reshape(batch, 1, seq, KV_LORA),
            kr_c.reshape(batch, 1, QK_ROPE, seq).transpose(0, 1, 3, 2))
```

```python
import functools
import math

import jax
import jax.numpy as jnp
import numpy as np
from jax import lax
from jax.experimental import pallas as pl
from jax.experimental.pallas import tpu as pltpu

D_MODEL = 1024
GRID_W = 64
MLA_HEADS = 8
QK_NOPE = 64
QK_ROPE = 32
V_HEAD = 64
Q_LORA = 256
KV_LORA = 128
MLA_WIDTH = MLA_HEADS * V_HEAD
SGU_HEADS = 8
SGU_WIDTH = D_MODEL - MLA_WIDTH
SGU_HEAD_DIM = SGU_WIDTH // SGU_HEADS
CHUNK = 128
D_FF = 4 * D_MODEL
AXIS_PAIRS = QK_ROPE // 4
ROPE_BASE = 10000.0
EPS = 1e-6
N_MOD = 6
ATTN_SCALE = (QK_NOPE + QK_ROPE) ** -0.5
QUERY_SCALE = ATTN_SCALE * math.log2(math.e)

LANES = 128
HEAD_SLOT = LANES
QK_WIDTH = MLA_HEADS * HEAD_SLOT
ATTN_PROJ = Q_LORA + KV_LORA + LANES
MOD_ROWS = 8
MOD_K_ROWS = 128
MOD_STREAMS = 2
TOKEN_TILE = 1024
PHASE1_CHAIN = 1024
SUB_ROWS = 256
PHASE2_ROWS_CTX = 512
PHASE2_ROWS_LAT = 512
QK_AHEAD = 1
FF_CHUNK = 1024
VMEM_LIMIT = 56 * 1024 * 1024

BF16 = jnp.bfloat16
F32 = jnp.float32


def _dot(a, b):
    return jnp.dot(a, b, preferred_element_type=F32)


def _dot_nt(a, b):
    return lax.dot_general(a, b, (((1,), (1,)), ((), ())), preferred_element_type=F32)


def _rms(x, g):
    return x * lax.rsqrt(jnp.mean(x * x, axis=-1, keepdims=True) + EPS) * g


def _gelu(x):
    inner = math.sqrt(2.0 / math.pi) * (x + 0.044715 * (x * x * x))
    return x * (0.5 * (1.0 + jnp.tanh(inner)))


def _mod_kernel(*refs):
    cond_refs = refs[:MOD_STREAMS]
    w_refs = refs[MOD_STREAMS:2 * MOD_STREAMS]
    (b_ref, w_in_attn_ref, w_in_gate_ref, o_ref, w_attn_ref, w_gate_ref) = refs[2 * MOD_STREAMS:]

    @pl.when(pl.program_id(0) == 0)
    def _():
        o_ref[...] = jnp.broadcast_to(b_ref[...], o_ref.shape)

    def part(cond_ref, w_ref):
        cnd = cond_ref[...]
        act = cnd * (1.0 / (1.0 + jnp.exp(-cnd)))
        return _dot(act.astype(BF16), w_ref[...].astype(BF16))

    acc = part(cond_refs[0], w_refs[0])
    for cond_ref, w_ref in zip(cond_refs[1:], w_refs[1:]):
        acc = acc + part(cond_ref, w_ref)
    o_ref[...] += acc

    w_gate_ref[...] = w_in_gate_ref[...].astype(BF16)

    @pl.when(pl.program_id(0) == 0)
    def _():
        rope0 = Q_LORA + KV_LORA
        kr = w_in_attn_ref[rope0:rope0 + QK_ROPE, :]
        row = lax.broadcasted_iota(jnp.int32, kr.shape, 0)
        kr_sw = jnp.where(row % 2 == 0, pltpu.roll(kr, QK_ROPE - 1, 0), pltpu.roll(kr, 1, 0))
        w_attn_ref[0:rope0 + QK_ROPE, :] = w_in_attn_ref[...].astype(BF16)
        w_attn_ref[rope0 + QK_ROPE:ATTN_PROJ, :] = jnp.concatenate(
            [kr_sw, kr, kr_sw], axis=0).astype(BF16)


def _modulation(cond8, w_mod, b_mod, w_in_t):
    n = w_mod.shape[1]
    tk = MOD_K_ROWS
    nk = D_MODEL // tk // MOD_STREAMS
    attn_rows = Q_LORA + KV_LORA + QK_ROPE
    gate_rows = 2 * SGU_WIDTH // nk
    cond_specs = [pl.BlockSpec((MOD_ROWS, tk), lambda k, s=s: (0, k + s * nk))
                  for s in range(MOD_STREAMS)]
    w_specs = [pl.BlockSpec((tk, n), lambda k, s=s: (k + s * nk, 0)) for s in range(MOD_STREAMS)]
    return pl.pallas_call(
        _mod_kernel,
        out_shape=[jax.ShapeDtypeStruct((MOD_ROWS, n), F32),
                   jax.ShapeDtypeStruct((ATTN_PROJ, D_MODEL), BF16),
                   jax.ShapeDtypeStruct((2 * SGU_WIDTH, D_MODEL), BF16)],
        grid=(nk,),
        in_specs=cond_specs + w_specs + [
            pl.BlockSpec((1, n), lambda k: (0, 0)),
            pl.BlockSpec((attn_rows, D_MODEL), lambda k: (0, 0)),
            pl.BlockSpec((pl.Element(gate_rows), pl.Element(D_MODEL)),
                         lambda k: (pl.multiple_of(attn_rows + k * gate_rows, QK_ROPE), 0)),
        ],
        out_specs=[pl.BlockSpec((MOD_ROWS, n), lambda k: (0, 0)),
                   pl.BlockSpec((ATTN_PROJ, D_MODEL), lambda k: (0, 0)),
                   pl.BlockSpec((gate_rows, D_MODEL), lambda k: (k, 0))],
        compiler_params=pltpu.CompilerParams(
            dimension_semantics=("arbitrary",), vmem_limit_bytes=VMEM_LIMIT),
        name="modulation",
    )(*([cond8] * MOD_STREAMS), *([w_mod] * MOD_STREAMS), b_mod, w_in_t, w_in_t)


def _mod_rows(mod_ref, row):
    vec = mod_ref[pl.ds(row, 1), :]
    return [vec[:, j * D_MODEL:(j + 1) * D_MODEL] for j in range(N_MOD)]


def _phase1_kernel(emit_cache, mod_row_fn, x_ref, mod_ref, ktab_ref, qtab_ref, g_pre_ref, w_attn_ref,
                   w_gate_ref,
                   g_q_ref, w_uq_ref, g_kv_ref, w_k_ref, w_uv_ref, g_sgu_ref, beta_sgu_ref,
                   w_sgu_ref, bias_ref, *rest):
    if emit_cache:
        (w_o32_ref, w_ff1_32_ref, w_ff2_32_ref, q_ref, k_ref, v_ref, sgu_ref, ckv_ref, kr_ref,
         w_o16_ref, w_ff1_16_ref, w_ff2_16_ref) = rest
        w_o16_ref[...] = w_o32_ref[...].astype(BF16)
        w_ff1_16_ref[...] = w_ff1_32_ref[...].astype(BF16)
        w_ff2_16_ref[...] = w_ff2_32_ref[...].astype(BF16)
    else:
        q_ref, k_ref, v_ref, sgu_ref = rest
    tm = x_ref.shape[0]
    mod = _mod_rows(mod_ref, mod_row_fn(pl.program_id(0)))
    shift_a, scale_a = mod[0], mod[1]
    low_half = lax.broadcasted_iota(jnp.int32, (CHUNK, LANES), 1) < SGU_HEAD_DIM
    per_pos_tables = ktab_ref.shape[0] > 1

    def project(r0):
        rows = slice(r0, r0 + PHASE1_CHAIN)
        h = (_rms(x_ref[rows, :], g_pre_ref[...]) * (1.0 + scale_a) + shift_a).astype(BF16)
        proj = _dot_nt(h, w_attn_ref[...])
        gate = _dot_nt(h, w_gate_ref[...])
        return proj, gate

    def expand_qkv(r0, proj):
        rows = slice(r0, r0 + PHASE1_CHAIN)
        tab_rows = rows if per_pos_tables else slice(None)
        cq = _rms(proj[:, 0:Q_LORA], g_q_ref[...])
        q = _dot(cq.astype(BF16), w_uq_ref[...])
        qtab = qtab_ref[tab_rows, :]
        q_ref[rows, :] = jnp.concatenate(
            [(q[:, s * HEAD_SLOT:(s + 1) * HEAD_SLOT] * qtab).astype(BF16)
             for s in range(MLA_HEADS)], axis=1)
        ckv_n = _rms(proj[:, Q_LORA:Q_LORA + KV_LORA], g_kv_ref[...])
        rope_slab = proj[:, Q_LORA + KV_LORA:Q_LORA + KV_LORA + LANES]
        if emit_cache:
            ckv_ref[rows, :] = ckv_n
            kr_t = rope_slab.T
            for e in range(PHASE1_CHAIN // SUB_ROWS):
                elem = r0 // SUB_ROWS + e
                kr_ref[elem * QK_ROPE:(elem + 1) * QK_ROPE, :] = (
                    kr_t[0:QK_ROPE, e * SUB_ROWS:(e + 1) * SUB_ROWS])
        ckv_b = ckv_n.astype(BF16)
        kin = jnp.concatenate([ckv_b, (rope_slab * ktab_ref[tab_rows, :]).astype(BF16)], axis=1)
        k_ref[:, rows] = _dot_nt(w_k_ref[...], kin).astype(BF16)
        v_ref[rows, :] = _dot(ckv_b, w_uv_ref[...]).astype(BF16)

    def gating_unit(r0, gate):
        vv = _gelu(gate[:, SGU_WIDTH:2 * SGU_WIDTH])
        mu = jnp.mean(vv, axis=-1, keepdims=True)
        vc = vv - mu
        var = jnp.mean(vc * vc, axis=-1, keepdims=True)
        vn = (vc * lax.rsqrt(var + EPS) * g_sgu_ref[...] + beta_sgu_ref[...]).astype(BF16)
        n_chunks = PHASE1_CHAIN // CHUNK
        for j in range(SGU_WIDTH // LANES):
            lanes = slice(j * LANES, (j + 1) * LANES)
            rhs = jnp.concatenate(
                [vn[n * CHUNK:(n + 1) * CHUNK, lanes] for n in range(n_chunks)], axis=1)
            o = _dot(w_sgu_ref[j], rhs)
            u = _gelu(gate[:, lanes])
            bias = bias_ref[:, lanes]
            for n in range(n_chunks):
                even = o[0:CHUNK, n * LANES:(n + 1) * LANES]
                odd = o[CHUNK:2 * CHUNK, n * LANES:(n + 1) * LANES]
                mixed = jnp.where(low_half, even, odd) + bias
                sgu_ref[r0 + n * CHUNK:r0 + (n + 1) * CHUNK, lanes] = (
                    u[n * CHUNK:(n + 1) * CHUNK, :] * mixed).astype(BF16)

    starts = list(range(0, tm, PHASE1_CHAIN))
    ahead = project(starts[0])
    for i, r0 in enumerate(starts):
        proj, gate = ahead
        if i + 1 < len(starts):
            ahead = project(starts[i + 1])
        expand_qkv(r0, proj)
        gating_unit(r0, gate)


def _const_spec(shape):
    nd = len(shape)
    return pl.BlockSpec(shape, lambda *_: (0,) * nd, pipeline_mode=pl.Buffered(1))


def _phase1(x2d, mod3, mod_row_fn, ktab, qtab, tab_fn, wts, emit_cache):
    n_tok = x2d.shape[0]
    tm = TOKEN_TILE
    grid = (n_tok // tm,)
    tab_block = (ktab.shape[0] if ktab.shape[0] == 1 else tm, LANES)
    in_specs = [
        pl.BlockSpec((tm, D_MODEL), lambda i: (i, 0)),
        _const_spec((MOD_ROWS, N_MOD * D_MODEL)),
        pl.BlockSpec(tab_block, lambda i: (tab_fn(i), 0)),
        pl.BlockSpec(tab_block, lambda i: (tab_fn(i), 0)),
        _const_spec((1, D_MODEL)),
        _const_spec((ATTN_PROJ, D_MODEL)),
        _const_spec((2 * SGU_WIDTH, D_MODEL)),
        _const_spec((1, Q_LORA)),
        _const_spec((Q_LORA, QK_WIDTH)),
        _const_spec((1, KV_LORA)),
        _const_spec((QK_WIDTH, 2 * LANES)),
        _const_spec((KV_LORA, MLA_WIDTH)),
        _const_spec((1, SGU_WIDTH)),
        _const_spec((1, SGU_WIDTH)),
        _const_spec((SGU_WIDTH // LANES, 2 * CHUNK, CHUNK)),
        _const_spec((CHUNK, SGU_WIDTH)),
    ]
    out_shape = [
        jax.ShapeDtypeStruct((n_tok, QK_WIDTH), BF16),
        jax.ShapeDtypeStruct((QK_WIDTH, n_tok), BF16),
        jax.ShapeDtypeStruct((n_tok, MLA_WIDTH), BF16),
        jax.ShapeDtypeStruct((n_tok, SGU_WIDTH), BF16),
    ]
    out_specs = [
        pl.BlockSpec((tm, QK_WIDTH), lambda i: (i, 0)),
        pl.BlockSpec((QK_WIDTH, tm), lambda i: (0, i)),
        pl.BlockSpec((tm, MLA_WIDTH), lambda i: (i, 0)),
        pl.BlockSpec((tm, SGU_WIDTH), lambda i: (i, 0)),
    ]
    extra_inputs = []
    if emit_cache:
        kr_rows = tm // SUB_ROWS * QK_ROPE
        out_shape += [jax.ShapeDtypeStruct((n_tok, KV_LORA), F32),
                      jax.ShapeDtypeStruct((n_tok // SUB_ROWS * QK_ROPE, SUB_ROWS), F32)]
        out_specs += [pl.BlockSpec((tm, KV_LORA), lambda i: (i, 0)),
                      pl.BlockSpec((kr_rows, SUB_ROWS), lambda i: (i, 0))]
        for w in (wts["w_o32"], wts["w_ff1_32"], wts["w_ff2_32"]):
            rows, cols = w.shape
            blk = (rows // grid[0], cols)
            extra_inputs.append(w)
            in_specs.append(pl.BlockSpec(blk, lambda i: (i, 0)))
            out_shape.append(jax.ShapeDtypeStruct(w.shape, BF16))
            out_specs.append(pl.BlockSpec(blk, lambda i: (i, 0)))
    return pl.pallas_call(
        functools.partial(_phase1_kernel, emit_cache, mod_row_fn),
        out_shape=out_shape,
        grid=grid,
        in_specs=in_specs,
        out_specs=out_specs,
        compiler_params=pltpu.CompilerParams(
            dimension_semantics=("arbitrary",), vmem_limit_bytes=VMEM_LIMIT),
        name="phase1_ctx" if emit_cache else "phase1_lat",
    )(x2d, mod3, ktab, qtab, wts["g_attn_pre"], wts["w_attn"], wts["w_gate"], wts["g_q"], wts["w_uq"],
      wts["g_kv"], wts["w_k"], wts["w_uv"], wts["g_sgu"], wts["beta_sgu"], wts["w_sgu"],
      wts["bias_sgu"], *extra_inputs)


def _cache_kv_kernel(kin_ref, w_k_ref, w_uv_ref, k_ref, v_ref):
    kin = kin_ref[...].astype(BF16)
    k_ref[...] = _dot_nt(w_k_ref[...], kin).astype(BF16)
    v_ref[...] = _dot(kin[:, 0:KV_LORA], w_uv_ref[...]).astype(BF16)


def _cache_kv(kin2d, wts):
    n_tok = kin2d.shape[0]
    tm = n_tok
    return pl.pallas_call(
        _cache_kv_kernel,
        out_shape=[jax.ShapeDtypeStruct((QK_WIDTH, n_tok), BF16),
                   jax.ShapeDtypeStruct((n_tok, MLA_WIDTH), BF16)],
        grid=(n_tok // tm,),
        in_specs=[
            pl.BlockSpec((tm, 2 * LANES), lambda i: (i, 0)),
            _const_spec((QK_WIDTH, 2 * LANES)),
            _const_spec((KV_LORA, MLA_WIDTH)),
        ],
        out_specs=[pl.BlockSpec((QK_WIDTH, tm), lambda i: (0, i)),
                   pl.BlockSpec((tm, MLA_WIDTH), lambda i: (i, 0))],
        compiler_params=pltpu.CompilerParams(
            dimension_semantics=("arbitrary",), vmem_limit_bytes=VMEM_LIMIT),
        name="cache_kv",
    )(kin2d, wts["w_k"], wts["w_uv"])


def _interleave(*gens):
    results = [None] * len(gens)
    live = list(range(len(gens)))
    while live:
        for i in list(live):
            try:
                next(gens[i])
            except StopIteration as stop:
                results[i] = stop.value
                live.remove(i)
    return results


def _run(gen):
    return _interleave(gen)[0]


def _attend(q_ref, kv_views, r0, low_half):
    def head_scores(hd):
        qh = q_ref[r0:r0 + SUB_ROWS, hd * HEAD_SLOT:(hd + 1) * HEAD_SLOT]
        return [_dot(qh, k_view(hd * HEAD_SLOT, HEAD_SLOT)) for k_view, _ in kv_views]

    def head_probs(scores):
        m = scores[0].max(axis=-1, keepdims=True)
        for s in scores[1:]:
            m = jnp.maximum(m, s.max(axis=-1, keepdims=True))
        probs = []
        denom = None
        for s in scores:
            p = jnp.exp2(s - m)
            ps = p.sum(axis=-1, keepdims=True)
            denom = ps if denom is None else denom + ps
            probs.append(p.astype(BF16))
        return probs, denom

    def head_values(hd, probs, denom):
        slab = hd // 2
        acc = None
        for (_, v_view), p in zip(kv_views, probs):
            pv = _dot(p, v_view(slab * LANES, LANES))
            acc = pv if acc is None else acc + pv
        return acc / denom

    pair_out = []
    head_out = None
    ahead = [head_scores(hd) for hd in range(QK_AHEAD)]
    for hd in range(MLA_HEADS):
        if hd + QK_AHEAD < MLA_HEADS:
            ahead.append(head_scores(hd + QK_AHEAD))
        o = head_values(hd, *head_probs(ahead.pop(0)))
        if hd % 2 == 0:
            head_out = o
        else:
            pair_out.append(jnp.where(low_half, head_out, o).astype(BF16))
            yield
    return jnp.concatenate(pair_out, axis=1)


def _phase2_kernel(n_kv, shared_kv, mod_row_fn, x_ref, mod_ref, q_ref, sgu_ref, *refs):
    kv_refs = refs[:2 * n_kv]
    (w_o_ref, g_post_ref, g_fpre_ref, g_fpost_ref, w_ff1_ref, w_ff2_ref, o_ref) = refs[2 * n_kv:]
    tq = x_ref.shape[0]
    low_half = lax.broadcasted_iota(jnp.int32, (SUB_ROWS, LANES), 1) < V_HEAD
    _, _, gate_a, shift_f, scale_f, gate_f = _mod_rows(mod_ref, mod_row_fn(pl.program_id(0)))
    def attention(r0):
        views = []
        for t in range(n_kv):
            k_ref, v_ref = kv_refs[2 * t], kv_refs[2 * t + 1]
            if shared_kv:
                views.append((lambda c, w, k_ref=k_ref: k_ref[c:c + w, :],
                              lambda c, w, v_ref=v_ref: v_ref[:, c:c + w]))
            else:
                views.append((lambda c, w, k_ref=k_ref: k_ref[c:c + w, r0:r0 + SUB_ROWS],
                              lambda c, w, v_ref=v_ref: v_ref[r0:r0 + SUB_ROWS, c:c + w]))
        return _attend(q_ref, views, r0, low_half)

    def mixer_proj(r0, attn):
        mix_in = jnp.concatenate([attn, sgu_ref[r0:r0 + SUB_ROWS, :]], axis=1)
        return _dot(mix_in, w_o_ref[...])

    def mixer_norm(r0, mix):
        x1 = x_ref[r0:r0 + SUB_ROWS, :] + gate_a * _rms(mix, g_post_ref[...])
        h = (_rms(x1, g_fpre_ref[...]) * (1.0 + scale_f) + shift_f).astype(BF16)
        return x1, h

    def ffn(h):
        f = None
        for c in range(D_FF // FF_CHUNK):
            hid = jnp.maximum(_dot(h, w_ff1_ref[:, c * FF_CHUNK:(c + 1) * FF_CHUNK]), 0.0)
            part = _dot((hid * hid).astype(BF16), w_ff2_ref[c * FF_CHUNK:(c + 1) * FF_CHUNK, :])
            f = part if f is None else f + part
            yield
        return f

    def finish(r0, x1, f):
        o_ref[r0:r0 + SUB_ROWS, :] = x1 + gate_f * _rms(f, g_fpost_ref[...])

    starts = [a * SUB_ROWS for a in range(tq // SUB_ROWS)]
    x1h = {r0: mixer_norm(r0, mixer_proj(r0, _run(attention(r0)))) for r0 in starts}
    for r0 in starts:
        finish(r0, x1h[r0][0], _run(ffn(x1h[r0][1])))


def _phase2(x2d, mod3, mod_row_fn, q, sgu, kv_list, kv_rows, shared_kv, tq, n_outer, n_inner, wts,
            name):
    n_kv = len(kv_list) // 2
    in_specs = [
        pl.BlockSpec((tq, D_MODEL), lambda b, i: (b * n_inner + i, 0)),
        _const_spec((MOD_ROWS, N_MOD * D_MODEL)),
        pl.BlockSpec((tq, QK_WIDTH), lambda b, i: (b * n_inner + i, 0)),
        pl.BlockSpec((tq, SGU_WIDTH), lambda b, i: (b * n_inner + i, 0)),
    ]
    for t in range(n_kv):
        in_specs.append(pl.BlockSpec((QK_WIDTH, kv_rows[t]), lambda b, i: (0, b)))
        in_specs.append(pl.BlockSpec((kv_rows[t], MLA_WIDTH), lambda b, i: (b, 0)))
    in_specs += [
        _const_spec((D_MODEL, D_MODEL)),
        _const_spec((1, D_MODEL)),
        _const_spec((1, D_MODEL)),
        _const_spec((1, D_MODEL)),
        _const_spec((D_MODEL, D_FF)),
        _const_spec((D_FF, D_MODEL)),
    ]
    return pl.pallas_call(
        functools.partial(_phase2_kernel, n_kv, shared_kv, mod_row_fn),
        out_shape=jax.ShapeDtypeStruct(x2d.shape, F32),
        grid=(n_outer, n_inner),
        in_specs=in_specs,
        out_specs=pl.BlockSpec((tq, D_MODEL), lambda b, i: (b * n_inner + i, 0)),
        compiler_params=pltpu.CompilerParams(
            dimension_semantics=("arbitrary", "arbitrary"), vmem_limit_bytes=VMEM_LIMIT),
        name=name,
    )(x2d, mod3, q, sgu, *kv_list, wts["w_o"], wts["g_attn_post"], wts["g_ffn_pre"],
      wts["g_ffn_post"], wts["w_ff1"], wts["w_ff2"])


def _pair_swap(w):
    shp = w.shape
    return w.reshape(shp[:-1] + (shp[-1] // 2, 2))[..., ::-1].reshape(shp)


def _prepare_weights(g_attn_pre, g_attn_post, g_q, w_uq, g_kv, w_ukv, w_sgu, b_sgu, g_sgu,
                     beta_sgu, w_o, g_ffn_pre, g_ffn_post, w_ff1, w_ff2):
    w_uq_h = w_uq.reshape(Q_LORA, MLA_HEADS, QK_NOPE + QK_ROPE)
    w_uq_ext = jnp.concatenate(
        [w_uq_h, _pair_swap(w_uq_h[..., QK_NOPE:])], axis=-1).reshape(Q_LORA, QK_WIDTH).astype(BF16)

    w_ukv_h = w_ukv.reshape(KV_LORA, MLA_HEADS, QK_NOPE + V_HEAD)
    w_uk_slots = jnp.concatenate(
        [w_ukv_h[..., :QK_NOPE], jnp.zeros((KV_LORA, MLA_HEADS, HEAD_SLOT - QK_NOPE), F32)],
        axis=-1).reshape(KV_LORA, QK_WIDTH)
    eye = jnp.eye(QK_ROPE, dtype=F32)
    zeros_rope = jnp.zeros((QK_ROPE, QK_ROPE), F32)
    zeros_nope = jnp.zeros((QK_ROPE, QK_NOPE), F32)
    to_lo = jnp.tile(jnp.concatenate([zeros_nope, eye, zeros_rope], axis=1), (1, MLA_HEADS))
    to_hi = jnp.tile(jnp.concatenate([zeros_nope, zeros_rope, eye], axis=1), (1, MLA_HEADS))
    w_k = jnp.concatenate([w_uk_slots, to_lo, to_lo, to_hi, to_hi], axis=0).T.astype(BF16)
    w_uv = w_ukv_h[..., QK_NOPE:].reshape(KV_LORA, MLA_WIDTH).astype(BF16)

    w_sgu_pair = w_sgu.reshape(SGU_HEADS // 2, 2 * CHUNK, CHUNK).astype(BF16)
    bias_sgu = jnp.repeat(b_sgu.T, SGU_HEAD_DIM, axis=1)
    row = lambda a: a.reshape(1, -1)
    return {
        "g_attn_pre": row(g_attn_pre), "g_attn_post": row(g_attn_post),
        "g_q": row(g_q), "w_uq": w_uq_ext, "g_kv": row(g_kv), "w_k": w_k, "w_uv": w_uv,
        "g_sgu": row(g_sgu), "beta_sgu": row(beta_sgu), "w_sgu": w_sgu_pair, "bias_sgu": bias_sgu,
        "w_o32": w_o, "g_ffn_pre": row(g_ffn_pre), "g_ffn_post": row(g_ffn_post),
        "w_ff1_32": w_ff1, "w_ff2_32": w_ff2,
    }


def _rope_tables(n_tok):
    rows = n_tok // GRID_W
    row = np.repeat(np.arange(rows), GRID_W).astype(np.float32)
    col = np.tile(np.arange(GRID_W), rows).astype(np.float32)
    freqs = (1.0 / (ROPE_BASE ** (np.arange(AXIS_PAIRS, dtype=np.float32) / AXIS_PAIRS))).astype(
        np.float32)
    ang = np.concatenate([row[:, None] * freqs, col[:, None] * freqs], axis=-1)
    cos = np.repeat(np.cos(ang), 2, axis=1)
    sin = np.repeat(np.sin(ang), 2, axis=1) * np.tile(np.array([-1.0, 1.0], np.float32), QK_ROPE // 2)
    ktab = np.concatenate([cos, sin, cos, sin], axis=1)
    qtab = QUERY_SCALE * np.concatenate([np.ones((n_tok, QK_NOPE), np.float32), cos, sin], axis=1)
    return jnp.asarray(ktab, F32), jnp.asarray(qtab, F32)


def kernel(x_prompt, x_sample, cache_ckv, cache_krope, c, c_ctx, w_mod, b_mod, g_attn_pre,
           g_attn_post, w_in, g_q, w_uq, g_kv, w_ukv, w_sgu, b_sgu, g_sgu, beta_sgu, w_o,
           g_ffn_pre, g_ffn_post, w_ff1, w_ff2):
    batch, seq, _ = x_prompt.shape
    dec_batch, dec_seq, _ = x_sample.shape
    past_len = cache_ckv.shape[2]
    depth = w_mod.shape[0]
    assert depth == 1

    wts = _prepare_weights(g_attn_pre[0], g_attn_post[0], g_q[0], w_uq[0], g_kv[0],
                           w_ukv[0], w_sgu[0], b_sgu[0], g_sgu[0], beta_sgu[0], w_o[0],
                           g_ffn_pre[0], g_ffn_post[0], w_ff1[0], w_ff2[0])

    cond8 = jnp.concatenate(
        [c_ctx[None, :], c, jnp.zeros((MOD_ROWS - 1 - dec_batch, D_MODEL), F32)], axis=0)
    mod3, w_attn, w_gate = _modulation(cond8, w_mod[0], b_mod[0].reshape(1, -1), w_in[0].T)
    wts = dict(wts, w_attn=w_attn, w_gate=w_gate)

    ktab_ctx = jnp.asarray(np.arange(LANES)[None, :] < QK_ROPE, F32)
    qtab_ctx = jnp.asarray(QUERY_SCALE * (np.arange(LANES)[None, :] < QK_NOPE + QK_ROPE), F32)
    xp2d = x_prompt.reshape(batch * seq, D_MODEL)
    q_c, k_c, v_c, sgu_c, ckv_c, kr_c, w_o16, w_ff1_16, w_ff2_16 = _phase1(
        xp2d, mod3, lambda i: 0, ktab_ctx, qtab_ctx, lambda i: 0, wts, True)
    wts = dict(wts, w_o=w_o16, w_ff1=w_ff1_16, w_ff2=w_ff2_16)
    assert seq == SUB_ROWS and dec_seq % PHASE2_ROWS_LAT == 0
    y_prompt = _phase2(xp2d, mod3, lambda b: 0, q_c, sgu_c, [k_c, v_c], [PHASE2_ROWS_CTX], False,
                       PHASE2_ROWS_CTX, batch * seq // PHASE2_ROWS_CTX, 1, wts, "phase2_ctx")

    ktab_lat, qtab_lat = _rope_tables(dec_seq)
    tiles_per_seq = dec_seq // TOKEN_TILE
    xs2d = x_sample.reshape(dec_batch * dec_seq, D_MODEL)
    q_l, k_l, v_l, sgu_l = _phase1(
        xs2d, mod3, lambda i: 1 + i // tiles_per_seq, ktab_lat, qtab_lat,
        lambda i: i % tiles_per_seq, wts, False)
    kr_p = cache_krope[:, 0].reshape(dec_batch * past_len, QK_ROPE)
    pad = jnp.zeros_like(kr_p)
    kin_p = jnp.concatenate(
        [cache_ckv[:, 0].reshape(dec_batch * past_len, KV_LORA), kr_p, pad, kr_p, pad], axis=1)
    k_p, v_p = _cache_kv(kin_p, wts)
    y_sample = _phase2(xs2d, mod3, lambda b: 1 + b, q_l, sgu_l, [k_p, v_p, k_l, v_l],
                       [past_len, dec_seq], True, PHASE2_ROWS_LAT, dec_batch,
                       dec_seq // PHASE2_ROWS_LAT, wts, "phase2_lat")

    return (y_prompt.reshape(batch, seq, D_MODEL),
            y_sample.reshape(dec_batch, dec_seq, D_MODEL),
            ckv_c.reshape(batch, 1, seq, KV_LORA),
            kr_c.reshape(batch, 1, QK_ROPE, seq).transpose(0, 1, 3, 2))
```

```python
import functools
import math

import jax
import jax.numpy as jnp
import numpy as np
from jax import lax
from jax.experimental import pallas as pl
from jax.experimental.pallas import tpu as pltpu

D_MODEL = 1024
GRID_W = 64
MLA_HEADS = 8
QK_NOPE = 64
QK_ROPE = 32
V_HEAD = 64
Q_LORA = 256
KV_LORA = 128
MLA_WIDTH = MLA_HEADS * V_HEAD
SGU_HEADS = 8
SGU_WIDTH = D_MODEL - MLA_WIDTH
SGU_HEAD_DIM = SGU_WIDTH // SGU_HEADS
CHUNK = 128
D_FF = 4 * D_MODEL
AXIS_PAIRS = QK_ROPE // 4
ROPE_BASE = 10000.0
EPS = 1e-6
N_MOD = 6
ATTN_SCALE = (QK_NOPE + QK_ROPE) ** -0.5
QUERY_SCALE = ATTN_SCALE * math.log2(math.e)

LANES = 128
HEAD_SLOT = LANES
QK_WIDTH = MLA_HEADS * HEAD_SLOT
ATTN_PROJ = Q_LORA + KV_LORA + LANES
MOD_ROWS = 8
MOD_K_ROWS = 128
MOD_STREAMS = 2
TOKEN_TILE = 1024
PHASE1_CHAIN = 1024
SUB_ROWS = 256
PHASE2_ROWS_CTX = 512
PHASE2_ROWS_LAT = 512
QK_AHEAD = 1
FF_CHUNK = 1024
VMEM_LIMIT = 56 * 1024 * 1024

BF16 = jnp.bfloat16
F32 = jnp.float32


def _dot(a, b):
    return jnp.dot(a, b, preferred_element_type=F32)


def _dot_nt(a, b):
    return lax.dot_general(a, b, (((1,), (1,)), ((), ())), preferred_element_type=F32)


def _rms(x, g):
    return x * lax.rsqrt(jnp.mean(x * x, axis=-1, keepdims=True) + EPS) * g


def _gelu(x):
    inner = math.sqrt(2.0 / math.pi) * (x + 0.044715 * (x * x * x))
    return x * (0.5 * (1.0 + jnp.tanh(inner)))


def _mod_kernel(*refs):
    cond_refs = refs[:MOD_STREAMS]
    w_refs = refs[MOD_STREAMS:2 * MOD_STREAMS]
    (b_ref, w_in_attn_ref, w_in_gate_ref, o_ref, w_attn_ref, w_gate_ref) = refs[2 * MOD_STREAMS:]

    @pl.when(pl.program_id(0) == 0)
    def _():
        o_ref[...] = jnp.broadcast_to(b_ref[...], o_ref.shape)

    def part(cond_ref, w_ref):
        cnd = cond_ref[...]
        act = cnd * (1.0 / (1.0 + jnp.exp(-cnd)))
        return _dot(act.astype(BF16), w_ref[...].astype(BF16))

    acc = part(cond_refs[0], w_refs[0])
    for cond_ref, w_ref in zip(cond_refs[1:], w_refs[1:]):
        acc = acc + part(cond_ref, w_ref)
    o_ref[...] += acc

    w_gate_ref[...] = w_in_gate_ref[...].astype(BF16)

    @pl.when(pl.program_id(0) == 0)
    def _():
        rope0 = Q_LORA + KV_LORA
        kr = w_in_attn_ref[rope0:rope0 + QK_ROPE, :]
        row = lax.broadcasted_iota(jnp.int32, kr.shape, 0)
        kr_sw = jnp.where(row % 2 == 0, pltpu.roll(kr, QK_ROPE - 1, 0), pltpu.roll(kr, 1, 0))
        w_attn_ref[0:rope0 + QK_ROPE, :] = w_in_attn_ref[...].astype(BF16)
        w_attn_ref[rope0 + QK_ROPE:ATTN_PROJ, :] = jnp.concatenate(
            [kr_sw, kr, kr_sw], axis=0).astype(BF16)


def _modulation(cond8, w_mod, b_mod, w_in_t):
    n = w_mod.shape[1]
    tk = MOD_K_ROWS
    nk = D_MODEL // tk // MOD_STREAMS
    attn_rows = Q_LORA + KV_LORA + QK_ROPE
    gate_rows = 2 * SGU_WIDTH // nk
    cond_specs = [pl.BlockSpec((MOD_ROWS, tk), lambda k, s=s: (0, k + s * nk))
                  for s in range(MOD_STREAMS)]
    w_specs = [pl.BlockSpec((tk, n), lambda k, s=s: (k + s * nk, 0)) for s in range(MOD_STREAMS)]
    return pl.pallas_call(
        _mod_kernel,
        out_shape=[jax.ShapeDtypeStruct((MOD_ROWS, n), F32),
                   jax.ShapeDtypeStruct((ATTN_PROJ, D_MODEL), BF16),
                   jax.ShapeDtypeStruct((2 * SGU_WIDTH, D_MODEL), BF16)],
        grid=(nk,),
        in_specs=cond_specs + w_specs + [
            pl.BlockSpec((1, n), lambda k: (0, 0)),
            pl.BlockSpec((attn_rows, D_MODEL), lambda k: (0, 0)),
            pl.BlockSpec((pl.Element(gate_rows), pl.Element(D_MODEL)),
                         lambda k: (pl.multiple_of(attn_rows + k * gate_rows, QK_ROPE), 0)),
        ],
        out_specs=[pl.BlockSpec((MOD_ROWS, n), lambda k: (0, 0)),
                   pl.BlockSpec((ATTN_PROJ, D_MODEL), lambda k: (0, 0)),
                   pl.BlockSpec((gate_rows, D_MODEL), lambda k: (k, 0))],
        compiler_params=pltpu.CompilerParams(
            dimension_semantics=("arbitrary",), vmem_limit_bytes=VMEM_LIMIT),
        name="modulation",
    )(*([cond8] * MOD_STREAMS), *([w_mod] * MOD_STREAMS), b_mod, w_in_t, w_in_t)


def _mod_rows(mod_ref, row):
    vec = mod_ref[pl.ds(row, 1), :]
    return [vec[:, j * D_MODEL:(j + 1) * D_MODEL] for j in range(N_MOD)]


def _phase1_kernel(emit_cache, mod_row_fn, x_ref, mod_ref, ktab_ref, qtab_ref, g_pre_ref, w_attn_ref,
                   w_gate_ref,
                   g_q_ref, w_uq_ref, g_kv_ref, w_k_ref, w_uv_ref, g_sgu_ref, beta_sgu_ref,
                   w_sgu_ref, bias_ref, *rest):
    if emit_cache:
        (w_o32_ref, w_ff1_32_ref, w_ff2_32_ref, q_ref, k_ref, v_ref, sgu_ref, ckv_ref, kr_ref,
         w_o16_ref, w_ff1_16_ref, w_ff2_16_ref) = rest
        w_o16_ref[...] = w_o32_ref[...].astype(BF16)
        w_ff1_16_ref[...] = w_ff1_32_ref[...].astype(BF16)
        w_ff2_16_ref[...] = w_ff2_32_ref[...].astype(BF16)
    else:
        kin_past_ref, q_ref, k_ref, v_ref, sgu_ref = rest
        kin_past = kin_past_ref[...].astype(BF16)
        past = kin_past.shape[0]
        k_ref[:, 0:past] = _dot_nt(w_k_ref[...], kin_past).astype(BF16)
        v_ref[0:past, :] = _dot(kin_past[:, 0:KV_LORA], w_uv_ref[...]).astype(BF16)
    kv0 = 0 if emit_cache else past
    tm = x_ref.shape[0]
    mod = _mod_rows(mod_ref, mod_row_fn(pl.program_id(0)))
    shift_a, scale_a = mod[0], mod[1]
    low_half = lax.broadcasted_iota(jnp.int32, (CHUNK, LANES), 1) < SGU_HEAD_DIM
    per_pos_tables = ktab_ref.shape[0] > 1

    def project(r0):
        rows = slice(r0, r0 + PHASE1_CHAIN)
        h = (_rms(x_ref[rows, :], g_pre_ref[...]) * (1.0 + scale_a) + shift_a).astype(BF16)
        proj = _dot_nt(h, w_attn_ref[...])
        gate = _dot_nt(h, w_gate_ref[...])
        return proj, gate

    def expand_qkv(r0, proj):
        rows = slice(r0, r0 + PHASE1_CHAIN)
        tab_rows = rows if per_pos_tables else slice(None)
        cq = _rms(proj[:, 0:Q_LORA], g_q_ref[...])
        q = _dot(cq.astype(BF16), w_uq_ref[...])
        qtab = qtab_ref[tab_rows, :]
        q_ref[rows, :] = jnp.concatenate(
            [(q[:, s * HEAD_SLOT:(s + 1) * HEAD_SLOT] * qtab).astype(BF16)
             for s in range(MLA_HEADS)], axis=1)
        ckv_n = _rms(proj[:, Q_LORA:Q_LORA + KV_LORA], g_kv_ref[...])
        rope_slab = proj[:, Q_LORA + KV_LORA:Q_LORA + KV_LORA + LANES]
        if emit_cache:
            ckv_ref[rows, :] = ckv_n
            kr_t = rope_slab.T
            for e in range(PHASE1_CHAIN // SUB_ROWS):
                elem = r0 // SUB_ROWS + e
                kr_ref[elem * QK_ROPE:(elem + 1) * QK_ROPE, :] = (
                    kr_t[0:QK_ROPE, e * SUB_ROWS:(e + 1) * SUB_ROWS])
        ckv_b = ckv_n.astype(BF16)
        kin = jnp.concatenate([ckv_b, (rope_slab * ktab_ref[tab_rows, :]).astype(BF16)], axis=1)
        kv_rows = slice(kv0 + r0, kv0 + r0 + PHASE1_CHAIN)
        k_ref[:, kv_rows] = _dot_nt(w_k_ref[...], kin).astype(BF16)
        v_ref[kv_rows, :] = _dot(ckv_b, w_uv_ref[...]).astype(BF16)

    def gating_unit(r0, gate):
        vv = _gelu(gate[:, SGU_WIDTH:2 * SGU_WIDTH])
        mu = jnp.mean(vv, axis=-1, keepdims=True)
        vc = vv - mu
        var = jnp.mean(vc * vc, axis=-1, keepdims=True)
        vn = (vc * lax.rsqrt(var + EPS) * g_sgu_ref[...] + beta_sgu_ref[...]).astype(BF16)
        n_chunks = PHASE1_CHAIN // CHUNK
        for j in range(SGU_WIDTH // LANES):
            lanes = slice(j * LANES, (j + 1) * LANES)
            rhs = jnp.concatenate(
                [vn[n * CHUNK:(n + 1) * CHUNK, lanes] for n in range(n_chunks)], axis=1)
            o = _dot(w_sgu_ref[j], rhs)
            u = _gelu(gate[:, lanes])
            bias = bias_ref[:, lanes]
            for n in range(n_chunks):
                even = o[0:CHUNK, n * LANES:(n + 1) * LANES]
                odd = o[CHUNK:2 * CHUNK, n * LANES:(n + 1) * LANES]
                mixed = jnp.where(low_half, even, odd) + bias
                sgu_ref[r0 + n * CHUNK:r0 + (n + 1) * CHUNK, lanes] = (
                    u[n * CHUNK:(n + 1) * CHUNK, :] * mixed).astype(BF16)

    starts = list(range(0, tm, PHASE1_CHAIN))
    ahead = project(starts[0])
    for i, r0 in enumerate(starts):
        proj, gate = ahead
        if i + 1 < len(starts):
            ahead = project(starts[i + 1])
        expand_qkv(r0, proj)
        gating_unit(r0, gate)


def _const_spec(shape):
    nd = len(shape)
    return pl.BlockSpec(shape, lambda *_: (0,) * nd, pipeline_mode=pl.Buffered(1))


def _phase1(x2d, mod3, mod_row_fn, ktab, qtab, tab_fn, wts, emit_cache, kin_past=None):
    n_tok = x2d.shape[0]
    tm = TOKEN_TILE
    grid = (n_tok // tm,)
    tab_block = (ktab.shape[0] if ktab.shape[0] == 1 else tm, LANES)
    in_specs = [
        pl.BlockSpec((tm, D_MODEL), lambda i: (i, 0)),
        _const_spec((MOD_ROWS, N_MOD * D_MODEL)),
        pl.BlockSpec(tab_block, lambda i: (tab_fn(i), 0)),
        pl.BlockSpec(tab_block, lambda i: (tab_fn(i), 0)),
        _const_spec((1, D_MODEL)),
        _const_spec((ATTN_PROJ, D_MODEL)),
        _const_spec((2 * SGU_WIDTH, D_MODEL)),
        _const_spec((1, Q_LORA)),
        _const_spec((Q_LORA, QK_WIDTH)),
        _const_spec((1, KV_LORA)),
        _const_spec((QK_WIDTH, 2 * LANES)),
        _const_spec((KV_LORA, MLA_WIDTH)),
        _const_spec((1, SGU_WIDTH)),
        _const_spec((1, SGU_WIDTH)),
        _const_spec((SGU_WIDTH // LANES, 2 * CHUNK, CHUNK)),
        _const_spec((CHUNK, SGU_WIDTH)),
    ]
    out_shape = [
        jax.ShapeDtypeStruct((n_tok, QK_WIDTH), BF16),
        jax.ShapeDtypeStruct((QK_WIDTH, n_tok), BF16),
        jax.ShapeDtypeStruct((n_tok, MLA_WIDTH), BF16),
        jax.ShapeDtypeStruct((n_tok, SGU_WIDTH), BF16),
    ]
    out_specs = [
        pl.BlockSpec((tm, QK_WIDTH), lambda i: (i, 0)),
        pl.BlockSpec((QK_WIDTH, tm), lambda i: (0, i)),
        pl.BlockSpec((tm, MLA_WIDTH), lambda i: (i, 0)),
        pl.BlockSpec((tm, SGU_WIDTH), lambda i: (i, 0)),
    ]
    extra_inputs = []
    if kin_past is not None:
        past = kin_past.shape[0] // grid[0]
        extra_inputs.append(kin_past)
        in_specs.append(pl.BlockSpec((past, 2 * LANES), lambda i: (i, 0)))
        kv_len = past + tm
        out_shape[1] = jax.ShapeDtypeStruct((QK_WIDTH, grid[0] * kv_len), BF16)
        out_shape[2] = jax.ShapeDtypeStruct((grid[0] * kv_len, MLA_WIDTH), BF16)
        out_specs[1] = pl.BlockSpec((QK_WIDTH, kv_len), lambda i: (0, i))
        out_specs[2] = pl.BlockSpec((kv_len, MLA_WIDTH), lambda i: (i, 0))
    if emit_cache:
        kr_rows = tm // SUB_ROWS * QK_ROPE
        out_shape += [jax.ShapeDtypeStruct((n_tok, KV_LORA), F32),
                      jax.ShapeDtypeStruct((n_tok // SUB_ROWS * QK_ROPE, SUB_ROWS), F32)]
        out_specs += [pl.BlockSpec((tm, KV_LORA), lambda i: (i, 0)),
                      pl.BlockSpec((kr_rows, SUB_ROWS), lambda i: (i, 0))]
        for w in (wts["w_o32"], wts["w_ff1_32"], wts["w_ff2_32"]):
            rows, cols = w.shape
            blk = (rows // grid[0], cols)
            extra_inputs.append(w)
            in_specs.append(pl.BlockSpec(blk, lambda i: (i, 0)))
            out_shape.append(jax.ShapeDtypeStruct(w.shape, BF16))
            out_specs.append(pl.BlockSpec(blk, lambda i: (i, 0)))
    return pl.pallas_call(
        functools.partial(_phase1_kernel, emit_cache, mod_row_fn),
        out_shape=out_shape,
        grid=grid,
        in_specs=in_specs,
        out_specs=out_specs,
        compiler_params=pltpu.CompilerParams(
            dimension_semantics=("arbitrary",), vmem_limit_bytes=VMEM_LIMIT),
        name="phase1_ctx" if emit_cache else "phase1_lat",
    )(x2d, mod3, ktab, qtab, wts["g_attn_pre"], wts["w_attn"], wts["w_gate"], wts["g_q"], wts["w_uq"],
      wts["g_kv"], wts["w_k"], wts["w_uv"], wts["g_sgu"], wts["beta_sgu"], wts["w_sgu"],
      wts["bias_sgu"], *extra_inputs)


def _interleave(*gens):
    results = [None] * len(gens)
    live = list(range(len(gens)))
    while live:
        for i in list(live):
            try:
                next(gens[i])
            except StopIteration as stop:
                results[i] = stop.value
                live.remove(i)
    return results


def _run(gen):
    return _interleave(gen)[0]


def _attend(q_ref, kv_views, r0, low_half):
    def head_scores(hd):
        qh = q_ref[r0:r0 + SUB_ROWS, hd * HEAD_SLOT:(hd + 1) * HEAD_SLOT]
        return [_dot(qh, k_view(hd * HEAD_SLOT, HEAD_SLOT)) for k_view, _ in kv_views]

    def head_probs(scores):
        m = scores[0].max(axis=-1, keepdims=True)
        for s in scores[1:]:
            m = jnp.maximum(m, s.max(axis=-1, keepdims=True))
        probs = []
        denom = None
        for s in scores:
            p = jnp.exp2(s - m)
            ps = p.sum(axis=-1, keepdims=True)
            denom = ps if denom is None else denom + ps
            probs.append(p.astype(BF16))
        return probs, denom

    def head_values(hd, probs, denom):
        slab = hd // 2
        acc = None
        for (_, v_view), p in zip(kv_views, probs):
            pv = _dot(p, v_view(slab * LANES, LANES))
            acc = pv if acc is None else acc + pv
        return acc / denom

    pair_out = []
    head_out = None
    ahead = [head_scores(hd) for hd in range(QK_AHEAD)]
    for hd in range(MLA_HEADS):
        if hd + QK_AHEAD < MLA_HEADS:
            ahead.append(head_scores(hd + QK_AHEAD))
        o = head_values(hd, *head_probs(ahead.pop(0)))
        if hd % 2 == 0:
            head_out = o
        else:
            pair_out.append(jnp.where(low_half, head_out, o).astype(BF16))
            yield
    return jnp.concatenate(pair_out, axis=1)


def _phase2_kernel(n_kv, shared_kv, mod_row_fn, x_ref, mod_ref, q_ref, sgu_ref, *refs):
    kv_refs = refs[:2 * n_kv]
    (w_o_ref, g_post_ref, g_fpre_ref, g_fpost_ref, w_ff1_ref, w_ff2_ref, o_ref) = refs[2 * n_kv:]
    tq = x_ref.shape[0]
    low_half = lax.broadcasted_iota(jnp.int32, (SUB_ROWS, LANES), 1) < V_HEAD
    _, _, gate_a, shift_f, scale_f, gate_f = _mod_rows(mod_ref, mod_row_fn(pl.program_id(0)))
    def attention(r0):
        views = []
        for t in range(n_kv):
            k_ref, v_ref = kv_refs[2 * t], kv_refs[2 * t + 1]
            if shared_kv:
                views.append((lambda c, w, k_ref=k_ref: k_ref[c:c + w, :],
                              lambda c, w, v_ref=v_ref: v_ref[:, c:c + w]))
            else:
                views.append((lambda c, w, k_ref=k_ref: k_ref[c:c + w, r0:r0 + SUB_ROWS],
                              lambda c, w, v_ref=v_ref: v_ref[r0:r0 + SUB_ROWS, c:c + w]))
        return _attend(q_ref, views, r0, low_half)

    def mixer_proj(r0, attn):
        mix_in = jnp.concatenate([attn, sgu_ref[r0:r0 + SUB_ROWS, :]], axis=1)
        return _dot(mix_in, w_o_ref[...])

    def mixer_norm(r0, mix):
        x1 = x_ref[r0:r0 + SUB_ROWS, :] + gate_a * _rms(mix, g_post_ref[...])
        h = (_rms(x1, g_fpre_ref[...]) * (1.0 + scale_f) + shift_f).astype(BF16)
        return x1, h

    def ffn(h):
        f = None
        for c in range(D_FF // FF_CHUNK):
            hid = jnp.maximum(_dot(h, w_ff1_ref[:, c * FF_CHUNK:(c + 1) * FF_CHUNK]), 0.0)
            part = _dot((hid * hid).astype(BF16), w_ff2_ref[c * FF_CHUNK:(c + 1) * FF_CHUNK, :])
            f = part if f is None else f + part
            yield
        return f

    def finish(r0, x1, f):
        o_ref[r0:r0 + SUB_ROWS, :] = x1 + gate_f * _rms(f, g_fpost_ref[...])

    starts = [a * SUB_ROWS for a in range(tq // SUB_ROWS)]
    x1h = {r0: mixer_norm(r0, mixer_proj(r0, _run(attention(r0)))) for r0 in starts}
    for r0 in starts:
        finish(r0, x1h[r0][0], _run(ffn(x1h[r0][1])))


def _phase2(x2d, mod3, mod_row_fn, q, sgu, kv_list, kv_rows, shared_kv, tq, n_outer, n_inner, wts,
            name):
    n_kv = len(kv_list) // 2
    in_specs = [
        pl.BlockSpec((tq, D_MODEL), lambda b, i: (b * n_inner + i, 0)),
        _const_spec((MOD_ROWS, N_MOD * D_MODEL)),
        pl.BlockSpec((tq, QK_WIDTH), lambda b, i: (b * n_inner + i, 0)),
        pl.BlockSpec((tq, SGU_WIDTH), lambda b, i: (b * n_inner + i, 0)),
    ]
    for t in range(n_kv):
        in_specs.append(pl.BlockSpec((QK_WIDTH, kv_rows[t]), lambda b, i: (0, b)))
        in_specs.append(pl.BlockSpec((kv_rows[t], MLA_WIDTH), lambda b, i: (b, 0)))
    in_specs += [
        _const_spec((D_MODEL, D_MODEL)),
        _const_spec((1, D_MODEL)),
        _const_spec((1, D_MODEL)),
        _const_spec((1, D_MODEL)),
        _const_spec((D_MODEL, D_FF)),
        _const_spec((D_FF, D_MODEL)),
    ]
    return pl.pallas_call(
        functools.partial(_phase2_kernel, n_kv, shared_kv, mod_row_fn),
        out_shape=jax.ShapeDtypeStruct(x2d.shape, F32),
        grid=(n_outer, n_inner),
        in_specs=in_specs,
        out_specs=pl.BlockSpec((tq, D_MODEL), lambda b, i: (b * n_inner + i, 0)),
        compiler_params=pltpu.CompilerParams(
            dimension_semantics=("arbitrary", "arbitrary"), vmem_limit_bytes=VMEM_LIMIT),
        name=name,
    )(x2d, mod3, q, sgu, *kv_list, wts["w_o"], wts["g_attn_post"], wts["g_ffn_pre"],
      wts["g_ffn_post"], wts["w_ff1"], wts["w_ff2"])


def _pair_swap(w):
    shp = w.shape
    return w.reshape(shp[:-1] + (shp[-1] // 2, 2))[..., ::-1].reshape(shp)


def _prepare_weights(g_attn_pre, g_attn_post, g_q, w_uq, g_kv, w_ukv, w_sgu, b_sgu, g_sgu,
                     beta_sgu, w_o, g_ffn_pre, g_ffn_post, w_ff1, w_ff2):
    w_uq_h = w_uq.reshape(Q_LORA, MLA_HEADS, QK_NOPE + QK_ROPE)
    w_uq_ext = jnp.concatenate(
        [w_uq_h, _pair_swap(w_uq_h[..., QK_NOPE:])], axis=-1).reshape(Q_LORA, QK_WIDTH).astype(BF16)

    w_ukv_h = w_ukv.reshape(KV_LORA, MLA_HEADS, QK_NOPE + V_HEAD)
    w_uk_slots = jnp.concatenate(
        [w_ukv_h[..., :QK_NOPE], jnp.zeros((KV_LORA, MLA_HEADS, HEAD_SLOT - QK_NOPE), F32)],
        axis=-1).reshape(KV_LORA, QK_WIDTH)
    eye = jnp.eye(QK_ROPE, dtype=F32)
    zeros_rope = jnp.zeros((QK_ROPE, QK_ROPE), F32)
    zeros_nope = jnp.zeros((QK_ROPE, QK_NOPE), F32)
    to_lo = jnp.tile(jnp.concatenate([zeros_nope, eye, zeros_rope], axis=1), (1, MLA_HEADS))
    to_hi = jnp.tile(jnp.concatenate([zeros_nope, zeros_rope, eye], axis=1), (1, MLA_HEADS))
    w_k = jnp.concatenate([w_uk_slots, to_lo, to_lo, to_hi, to_hi], axis=0).T.astype(BF16)
    w_uv = w_ukv_h[..., QK_NOPE:].reshape(KV_LORA, MLA_WIDTH).astype(BF16)

    w_sgu_pair = w_sgu.reshape(SGU_HEADS // 2, 2 * CHUNK, CHUNK).astype(BF16)
    bias_sgu = jnp.repeat(b_sgu.T, SGU_HEAD_DIM, axis=1)
    row = lambda a: a.reshape(1, -1)
    return {
        "g_attn_pre": row(g_attn_pre), "g_attn_post": row(g_attn_post),
        "g_q": row(g_q), "w_uq": w_uq_ext, "g_kv": row(g_kv), "w_k": w_k, "w_uv": w_uv,
        "g_sgu": row(g_sgu), "beta_sgu": row(beta_sgu), "w_sgu": w_sgu_pair, "bias_sgu": bias_sgu,
        "w_o32": w_o, "g_ffn_pre": row(g_ffn_pre), "g_ffn_post": row(g_ffn_post),
        "w_ff1_32": w_ff1, "w_ff2_32": w_ff2,
    }


def _rope_tables(n_tok):
    rows = n_tok // GRID_W
    row = np.repeat(np.arange(rows), GRID_W).astype(np.float32)
    col = np.tile(np.arange(GRID_W), rows).astype(np.float32)
    freqs = (1.0 / (ROPE_BASE ** (np.arange(AXIS_PAIRS, dtype=np.float32) / AXIS_PAIRS))).astype(
        np.float32)
    ang = np.concatenate([row[:, None] * freqs, col[:, None] * freqs], axis=-1)
    cos = np.repeat(np.cos(ang), 2, axis=1)
    sin = np.repeat(np.sin(ang), 2, axis=1) * np.tile(np.array([-1.0, 1.0], np.float32), QK_ROPE // 2)
    ktab = np.concatenate([cos, sin, cos, sin], axis=1)
    qtab = QUERY_SCALE * np.concatenate([np.ones((n_tok, QK_NOPE), np.float32), cos, sin], axis=1)
    return jnp.asarray(ktab, F32), jnp.asarray(qtab, F32)


def kernel(x_prompt, x_sample, cache_ckv, cache_krope, c, c_ctx, w_mod, b_mod, g_attn_pre,
           g_attn_post, w_in, g_q, w_uq, g_kv, w_ukv, w_sgu, b_sgu, g_sgu, beta_sgu, w_o,
           g_ffn_pre, g_ffn_post, w_ff1, w_ff2):
    batch, seq, _ = x_prompt.shape
    dec_batch, dec_seq, _ = x_sample.shape
    past_len = cache_ckv.shape[2]
    depth = w_mod.shape[0]
    assert depth == 1

    wts = _prepare_weights(g_attn_pre[0], g_attn_post[0], g_q[0], w_uq[0], g_kv[0],
                           w_ukv[0], w_sgu[0], b_sgu[0], g_sgu[0], beta_sgu[0], w_o[0],
                           g_ffn_pre[0], g_ffn_post[0], w_ff1[0], w_ff2[0])

    cond8 = jnp.concatenate(
        [c_ctx[None, :], c, jnp.zeros((MOD_ROWS - 1 - dec_batch, D_MODEL), F32)], axis=0)
    mod3, w_attn, w_gate = _modulation(cond8, w_mod[0], b_mod[0].reshape(1, -1), w_in[0].T)
    wts = dict(wts, w_attn=w_attn, w_gate=w_gate)

    ktab_ctx = jnp.asarray(np.arange(LANES)[None, :] < QK_ROPE, F32)
    qtab_ctx = jnp.asarray(QUERY_SCALE * (np.arange(LANES)[None, :] < QK_NOPE + QK_ROPE), F32)
    xp2d = x_prompt.reshape(batch * seq, D_MODEL)
    q_c, k_c, v_c, sgu_c, ckv_c, kr_c, w_o16, w_ff1_16, w_ff2_16 = _phase1(
        xp2d, mod3, lambda i: 0, ktab_ctx, qtab_ctx, lambda i: 0, wts, True)
    wts = dict(wts, w_o=w_o16, w_ff1=w_ff1_16, w_ff2=w_ff2_16)
    assert seq == SUB_ROWS and dec_seq % PHASE2_ROWS_LAT == 0
    y_prompt = _phase2(xp2d, mod3, lambda b: 0, q_c, sgu_c, [k_c, v_c], [PHASE2_ROWS_CTX], False,
                       PHASE2_ROWS_CTX, batch * seq // PHASE2_ROWS_CTX, 1, wts, "phase2_ctx")

    ktab_lat, qtab_lat = _rope_tables(dec_seq)
    tiles_per_seq = dec_seq // TOKEN_TILE
    xs2d = x_sample.reshape(dec_batch * dec_seq, D_MODEL)
    assert tiles_per_seq == 1
    kr_p = cache_krope[:, 0].reshape(dec_batch * past_len, QK_ROPE)
    pad = jnp.zeros_like(kr_p)
    kin_p = jnp.concatenate(
        [cache_ckv[:, 0].reshape(dec_batch * past_len, KV_LORA), kr_p, pad, kr_p, pad], axis=1)
    q_l, k_l, v_l, sgu_l = _phase1(
        xs2d, mod3, lambda i: 1 + i, ktab_lat, qtab_lat, lambda i: 0, wts, False, kin_p)
    y_sample = _phase2(xs2d, mod3, lambda b: 1 + b, q_l, sgu_l, [k_l, v_l],
                       [past_len + dec_seq], True, PHASE2_ROWS_LAT, dec_batch,
                       dec_seq // PHASE2_ROWS_LAT, wts, "phase2_lat")

    return (y_prompt.reshape(batch, seq, D_MODEL),
            y_sample.reshape(dec_batch, dec_seq, D_MODEL),
            ckv_c.reshape(batch, 1, seq, KV_LORA),
            kr_c.reshape(batch, 1, QK_ROPE, seq).transpose(0, 1, 3, 2))
```

```python
import functools
import math

import jax
import jax.numpy as jnp
import numpy as np
from jax import lax
from jax.experimental import pallas as pl
from jax.experimental.pallas import tpu as pltpu

D_MODEL = 1024
GRID_W = 64
MLA_HEADS = 8
QK_NOPE = 64
QK_ROPE = 32
V_HEAD = 64
Q_LORA = 256
KV_LORA = 128
MLA_WIDTH = MLA_HEADS * V_HEAD
SGU_HEADS = 8
SGU_WIDTH = D_MODEL - MLA_WIDTH
SGU_HEAD_DIM = SGU_WIDTH // SGU_HEADS
CHUNK = 128
D_FF = 4 * D_MODEL
AXIS_PAIRS = QK_ROPE // 4
ROPE_BASE = 10000.0
EPS = 1e-6
N_MOD = 6
ATTN_SCALE = (QK_NOPE + QK_ROPE) ** -0.5
QUERY_SCALE = ATTN_SCALE * math.log2(math.e)

LANES = 128
HEAD_SLOT = LANES
QK_WIDTH = MLA_HEADS * HEAD_SLOT
ATTN_PROJ = Q_LORA + KV_LORA + LANES
MOD_ROWS = 8
MOD_K_ROWS = 128
MOD_STREAMS = 4
TOKEN_TILE = 1024
PHASE1_CHAIN = 1024
SUB_ROWS = 256
PHASE2_ROWS_CTX = 512
PHASE2_ROWS_LAT = 512
QK_AHEAD = 1
FF_CHUNK = 1024
VMEM_LIMIT = 56 * 1024 * 1024

BF16 = jnp.bfloat16
F32 = jnp.float32


def _dot(a, b):
    return jnp.dot(a, b, preferred_element_type=F32)


def _dot_nt(a, b):
    return lax.dot_general(a, b, (((1,), (1,)), ((), ())), preferred_element_type=F32)


def _rms(x, g):
    return x * lax.rsqrt(jnp.mean(x * x, axis=-1, keepdims=True) + EPS) * g


def _gelu(x):
    inner = math.sqrt(2.0 / math.pi) * (x + 0.044715 * (x * x * x))
    return x * (0.5 * (1.0 + jnp.tanh(inner)))


def _mod_kernel(*refs):
    cond_refs = refs[:MOD_STREAMS]
    w_refs = refs[MOD_STREAMS:2 * MOD_STREAMS]
    (b_ref, w_in_attn_ref, w_in_gate_ref, o_ref, w_attn_ref, w_gate_ref) = refs[2 * MOD_STREAMS:]

    @pl.when(pl.program_id(0) == 0)
    def _():
        o_ref[...] = jnp.broadcast_to(b_ref[...], o_ref.shape)

    def part(cond_ref, w_ref):
        cnd = cond_ref[...]
        act = cnd * (1.0 / (1.0 + jnp.exp(-cnd)))
        return _dot(act.astype(BF16), w_ref[...].astype(BF16))

    acc = part(cond_refs[0], w_refs[0])
    for cond_ref, w_ref in zip(cond_refs[1:], w_refs[1:]):
        acc = acc + part(cond_ref, w_ref)
    o_ref[...] += acc

    w_gate_ref[...] = w_in_gate_ref[...].astype(BF16)

    @pl.when(pl.program_id(0) == 0)
    def _():
        rope0 = Q_LORA + KV_LORA
        kr = w_in_attn_ref[rope0:rope0 + QK_ROPE, :]
        row = lax.broadcasted_iota(jnp.int32, kr.shape, 0)
        kr_sw = jnp.where(row % 2 == 0, pltpu.roll(kr, QK_ROPE - 1, 0), pltpu.roll(kr, 1, 0))
        w_attn_ref[0:rope0 + QK_ROPE, :] = w_in_attn_ref[...].astype(BF16)
        w_attn_ref[rope0 + QK_ROPE:ATTN_PROJ, :] = jnp.concatenate(
            [kr_sw, kr, kr_sw], axis=0).astype(BF16)


def _modulation(cond8, w_mod, b_mod, w_in_t):
    n = w_mod.shape[1]
    tk = MOD_K_ROWS
    nk = D_MODEL // tk // MOD_STREAMS
    attn_rows = Q_LORA + KV_LORA + QK_ROPE
    gate_rows = 2 * SGU_WIDTH // nk
    cond_specs = [pl.BlockSpec((MOD_ROWS, tk), lambda k, s=s: (0, k + s * nk))
                  for s in range(MOD_STREAMS)]
    w_specs = [pl.BlockSpec((tk, n), lambda k, s=s: (k + s * nk, 0)) for s in range(MOD_STREAMS)]
    return pl.pallas_call(
        _mod_kernel,
        out_shape=[jax.ShapeDtypeStruct((MOD_ROWS, n), F32),
                   jax.ShapeDtypeStruct((ATTN_PROJ, D_MODEL), BF16),
                   jax.ShapeDtypeStruct((2 * SGU_WIDTH, D_MODEL), BF16)],
        grid=(nk,),
        in_specs=cond_specs + w_specs + [
            pl.BlockSpec((1, n), lambda k: (0, 0)),
            pl.BlockSpec((attn_rows, D_MODEL), lambda k: (0, 0)),
            pl.BlockSpec((pl.Element(gate_rows), pl.Element(D_MODEL)),
                         lambda k: (pl.multiple_of(attn_rows + k * gate_rows, QK_ROPE), 0)),
        ],
        out_specs=[pl.BlockSpec((MOD_ROWS, n), lambda k: (0, 0)),
                   pl.BlockSpec((ATTN_PROJ, D_MODEL), lambda k: (0, 0)),
                   pl.BlockSpec((gate_rows, D_MODEL), lambda k: (k, 0))],
        compiler_params=pltpu.CompilerParams(
            dimension_semantics=("arbitrary",), vmem_limit_bytes=VMEM_LIMIT),
        name="modulation",
    )(*([cond8] * MOD_STREAMS), *([w_mod] * MOD_STREAMS), b_mod, w_in_t, w_in_t)


def _mod_rows(mod_ref, row):
    vec = mod_ref[pl.ds(row, 1), :]
    return [vec[:, j * D_MODEL:(j + 1) * D_MODEL] for j in range(N_MOD)]


def _phase1_kernel(emit_cache, mod_row_fn, x_ref, mod_ref, ktab_ref, qtab_ref, g_pre_ref, w_attn_ref,
                   w_gate_ref,
                   g_q_ref, w_uq_ref, g_kv_ref, w_k_ref, w_uv_ref, g_sgu_ref, beta_sgu_ref,
                   w_sgu_ref, bias_ref, *rest):
    if emit_cache:
        (w_o32_ref, w_ff1_32_ref, w_ff2_32_ref, q_ref, k_ref, v_ref, sgu_ref, ckv_ref, kr_ref,
         w_o16_ref, w_ff1_16_ref, w_ff2_16_ref) = rest
        w_o16_ref[...] = w_o32_ref[...].astype(BF16)
        w_ff1_16_ref[...] = w_ff1_32_ref[...].astype(BF16)
        w_ff2_16_ref[...] = w_ff2_32_ref[...].astype(BF16)
    else:
        kin_past_ref, q_ref, k_ref, v_ref, sgu_ref = rest
        kin_past = kin_past_ref[...].astype(BF16)
        past = kin_past.shape[0]
        k_ref[:, 0:past] = _dot_nt(w_k_ref[...], kin_past).astype(BF16)
        v_ref[0:past, :] = _dot(kin_past[:, 0:KV_LORA], w_uv_ref[...]).astype(BF16)
    kv0 = 0 if emit_cache else past
    tm = x_ref.shape[0]
    mod = _mod_rows(mod_ref, mod_row_fn(pl.program_id(0)))
    shift_a, scale_a = mod[0], mod[1]
    low_half = lax.broadcasted_iota(jnp.int32, (CHUNK, LANES), 1) < SGU_HEAD_DIM
    per_pos_tables = ktab_ref.shape[0] > 1

    def project(r0):
        rows = slice(r0, r0 + PHASE1_CHAIN)
        h = (_rms(x_ref[rows, :], g_pre_ref[...]) * (1.0 + scale_a) + shift_a).astype(BF16)
        proj = _dot_nt(h, w_attn_ref[...])
        gate = _dot_nt(h, w_gate_ref[...])
        return proj, gate

    def expand_qkv(r0, proj):
        rows = slice(r0, r0 + PHASE1_CHAIN)
        tab_rows = rows if per_pos_tables else slice(None)
        cq = _rms(proj[:, 0:Q_LORA], g_q_ref[...])
        q = _dot(cq.astype(BF16), w_uq_ref[...])
        qtab = qtab_ref[tab_rows, :]
        q_ref[rows, :] = jnp.concatenate(
            [(q[:, s * HEAD_SLOT:(s + 1) * HEAD_SLOT] * qtab).astype(BF16)
             for s in range(MLA_HEADS)], axis=1)
        ckv_n = _rms(proj[:, Q_LORA:Q_LORA + KV_LORA], g_kv_ref[...])
        rope_slab = proj[:, Q_LORA + KV_LORA:Q_LORA + KV_LORA + LANES]
        if emit_cache:
            ckv_ref[rows, :] = ckv_n
            kr_t = rope_slab.T
            for e in range(PHASE1_CHAIN // SUB_ROWS):
                elem = r0 // SUB_ROWS + e
                kr_ref[elem * QK_ROPE:(elem + 1) * QK_ROPE, :] = (
                    kr_t[0:QK_ROPE, e * SUB_ROWS:(e + 1) * SUB_ROWS])
        ckv_b = ckv_n.astype(BF16)
        kin = jnp.concatenate([ckv_b, (rope_slab * ktab_ref[tab_rows, :]).astype(BF16)], axis=1)
        kv_rows = slice(kv0 + r0, kv0 + r0 + PHASE1_CHAIN)
        k_ref[:, kv_rows] = _dot_nt(w_k_ref[...], kin).astype(BF16)
        v_ref[kv_rows, :] = _dot(ckv_b, w_uv_ref[...]).astype(BF16)

    def gating_unit(r0, gate):
        vv = _gelu(gate[:, SGU_WIDTH:2 * SGU_WIDTH])
        mu = jnp.mean(vv, axis=-1, keepdims=True)
        vc = vv - mu
        var = jnp.mean(vc * vc, axis=-1, keepdims=True)
        vn = (vc * lax.rsqrt(var + EPS) * g_sgu_ref[...] + beta_sgu_ref[...]).astype(BF16)
        n_chunks = PHASE1_CHAIN // CHUNK
        for j in range(SGU_WIDTH // LANES):
            lanes = slice(j * LANES, (j + 1) * LANES)
            rhs = jnp.concatenate(
                [vn[n * CHUNK:(n + 1) * CHUNK, lanes] for n in range(n_chunks)], axis=1)
            o = _dot(w_sgu_ref[j], rhs)
            u = _gelu(gate[:, lanes])
            bias = bias_ref[:, lanes]
            for n in range(n_chunks):
                even = o[0:CHUNK, n * LANES:(n + 1) * LANES]
                odd = o[CHUNK:2 * CHUNK, n * LANES:(n + 1) * LANES]
                mixed = jnp.where(low_half, even, odd) + bias
                sgu_ref[r0 + n * CHUNK:r0 + (n + 1) * CHUNK, lanes] = (
                    u[n * CHUNK:(n + 1) * CHUNK, :] * mixed).astype(BF16)

    starts = list(range(0, tm, PHASE1_CHAIN))
    ahead = project(starts[0])
    for i, r0 in enumerate(starts):
        proj, gate = ahead
        if i + 1 < len(starts):
            ahead = project(starts[i + 1])
        expand_qkv(r0, proj)
        gating_unit(r0, gate)


def _const_spec(shape):
    nd = len(shape)
    return pl.BlockSpec(shape, lambda *_: (0,) * nd, pipeline_mode=pl.Buffered(1))


def _phase1(x2d, mod3, mod_row_fn, ktab, qtab, tab_fn, wts, emit_cache, kin_past=None):
    n_tok = x2d.shape[0]
    tm = TOKEN_TILE
    grid = (n_tok // tm,)
    tab_block = (ktab.shape[0] if ktab.shape[0] == 1 else tm, LANES)
    in_specs = [
        pl.BlockSpec((tm, D_MODEL), lambda i: (i, 0)),
        _const_spec((MOD_ROWS, N_MOD * D_MODEL)),
        pl.BlockSpec(tab_block, lambda i: (tab_fn(i), 0)),
        pl.BlockSpec(tab_block, lambda i: (tab_fn(i), 0)),
        _const_spec((1, D_MODEL)),
        _const_spec((ATTN_PROJ, D_MODEL)),
        _const_spec((2 * SGU_WIDTH, D_MODEL)),
        _const_spec((1, Q_LORA)),
        _const_spec((Q_LORA, QK_WIDTH)),
        _const_spec((1, KV_LORA)),
        _const_spec((QK_WIDTH, 2 * LANES)),
        _const_spec((KV_LORA, MLA_WIDTH)),
        _const_spec((1, SGU_WIDTH)),
        _const_spec((1, SGU_WIDTH)),
        _const_spec((SGU_WIDTH // LANES, 2 * CHUNK, CHUNK)),
        _const_spec((CHUNK, SGU_WIDTH)),
    ]
    out_shape = [
        jax.ShapeDtypeStruct((n_tok, QK_WIDTH), BF16),
        jax.ShapeDtypeStruct((QK_WIDTH, n_tok), BF16),
        jax.ShapeDtypeStruct((n_tok, MLA_WIDTH), BF16),
        jax.ShapeDtypeStruct((n_tok, SGU_WIDTH), BF16),
    ]
    out_specs = [
        pl.BlockSpec((tm, QK_WIDTH), lambda i: (i, 0)),
        pl.BlockSpec((QK_WIDTH, tm), lambda i: (0, i)),
        pl.BlockSpec((tm, MLA_WIDTH), lambda i: (i, 0)),
        pl.BlockSpec((tm, SGU_WIDTH), lambda i: (i, 0)),
    ]
    extra_inputs = []
    if kin_past is not None:
        past = kin_past.shape[0] // grid[0]
        extra_inputs.append(kin_past)
        in_specs.append(pl.BlockSpec((past, 2 * LANES), lambda i: (i, 0)))
        kv_len = past + tm
        out_shape[1] = jax.ShapeDtypeStruct((QK_WIDTH, grid[0] * kv_len), BF16)
        out_shape[2] = jax.ShapeDtypeStruct((grid[0] * kv_len, MLA_WIDTH), BF16)
        out_specs[1] = pl.BlockSpec((QK_WIDTH, kv_len), lambda i: (0, i))
        out_specs[2] = pl.BlockSpec((kv_len, MLA_WIDTH), lambda i: (i, 0))
    if emit_cache:
        kr_rows = tm // SUB_ROWS * QK_ROPE
        out_shape += [jax.ShapeDtypeStruct((n_tok, KV_LORA), F32),
                      jax.ShapeDtypeStruct((n_tok // SUB_ROWS * QK_ROPE, SUB_ROWS), F32)]
        out_specs += [pl.BlockSpec((tm, KV_LORA), lambda i: (i, 0)),
                      pl.BlockSpec((kr_rows, SUB_ROWS), lambda i: (i, 0))]
        for w in (wts["w_o32"], wts["w_ff1_32"], wts["w_ff2_32"]):
            rows, cols = w.shape
            blk = (rows // grid[0], cols)
            extra_inputs.append(w)
            in_specs.append(pl.BlockSpec(blk, lambda i: (i, 0)))
            out_shape.append(jax.ShapeDtypeStruct(w.shape, BF16))
            out_specs.append(pl.BlockSpec(blk, lambda i: (i, 0)))
    return pl.pallas_call(
        functools.partial(_phase1_kernel, emit_cache, mod_row_fn),
        out_shape=out_shape,
        grid=grid,
        in_specs=in_specs,
        out_specs=out_specs,
        compiler_params=pltpu.CompilerParams(
            dimension_semantics=("arbitrary",), vmem_limit_bytes=VMEM_LIMIT),
        name="phase1_ctx" if emit_cache else "phase1_lat",
    )(x2d, mod3, ktab, qtab, wts["g_attn_pre"], wts["w_attn"], wts["w_gate"], wts["g_q"], wts["w_uq"],
      wts["g_kv"], wts["w_k"], wts["w_uv"], wts["g_sgu"], wts["beta_sgu"], wts["w_sgu"],
      wts["bias_sgu"], *extra_inputs)


def _interleave(*gens):
    results = [None] * len(gens)
    live = list(range(len(gens)))
    while live:
        for i in list(live):
            try:
                next(gens[i])
            except StopIteration as stop:
                results[i] = stop.value
                live.remove(i)
    return results


def _run(gen):
    return _interleave(gen)[0]


def _attend(q_ref, kv_views, r0, low_half):
    def head_scores(hd):
        qh = q_ref[r0:r0 + SUB_ROWS, hd * HEAD_SLOT:(hd + 1) * HEAD_SLOT]
        return [_dot(qh, k_view(hd * HEAD_SLOT, HEAD_SLOT)) for k_view, _ in kv_views]

    def head_probs(scores):
        m = scores[0].max(axis=-1, keepdims=True)
        for s in scores[1:]:
            m = jnp.maximum(m, s.max(axis=-1, keepdims=True))
        probs = []
        denom = None
        for s in scores:
            p = jnp.exp2(s - m)
            ps = p.sum(axis=-1, keepdims=True)
            denom = ps if denom is None else denom + ps
            probs.append(p.astype(BF16))
        return probs, denom

    def head_values(hd, probs, denom):
        slab = hd // 2
        acc = None
        for (_, v_view), p in zip(kv_views, probs):
            pv = _dot(p, v_view(slab * LANES, LANES))
            acc = pv if acc is None else acc + pv
        return acc / denom

    pair_out = []
    head_out = None
    ahead = [head_scores(hd) for hd in range(QK_AHEAD)]
    for hd in range(MLA_HEADS):
        if hd + QK_AHEAD < MLA_HEADS:
            ahead.append(head_scores(hd + QK_AHEAD))
        o = head_values(hd, *head_probs(ahead.pop(0)))
        if hd % 2 == 0:
            head_out = o
        else:
            pair_out.append(jnp.where(low_half, head_out, o).astype(BF16))
            yield
    return jnp.concatenate(pair_out, axis=1)


def _phase2_kernel(n_kv, shared_kv, mod_row_fn, x_ref, mod_ref, q_ref, sgu_ref, *refs):
    kv_refs = refs[:2 * n_kv]
    (w_o_ref, g_post_ref, g_fpre_ref, g_fpost_ref, w_ff1_ref, w_ff2_ref, o_ref) = refs[2 * n_kv:]
    tq = x_ref.shape[0]
    low_half = lax.broadcasted_iota(jnp.int32, (SUB_ROWS, LANES), 1) < V_HEAD
    _, _, gate_a, shift_f, scale_f, gate_f = _mod_rows(mod_ref, mod_row_fn(pl.program_id(0)))
    def attention(r0):
        views = []
        for t in range(n_kv):
            k_ref, v_ref = kv_refs[2 * t], kv_refs[2 * t + 1]
            if shared_kv:
                views.append((lambda c, w, k_ref=k_ref: k_ref[c:c + w, :],
                              lambda c, w, v_ref=v_ref: v_ref[:, c:c + w]))
            else:
                views.append((lambda c, w, k_ref=k_ref: k_ref[c:c + w, r0:r0 + SUB_ROWS],
                              lambda c, w, v_ref=v_ref: v_ref[r0:r0 + SUB_ROWS, c:c + w]))
        return _attend(q_ref, views, r0, low_half)

    def mixer_proj(r0, attn):
        mix_in = jnp.concatenate([attn, sgu_ref[r0:r0 + SUB_ROWS, :]], axis=1)
        return _dot(mix_in, w_o_ref[...])

    def mixer_norm(r0, mix):
        x1 = x_ref[r0:r0 + SUB_ROWS, :] + gate_a * _rms(mix, g_post_ref[...])
        h = (_rms(x1, g_fpre_ref[...]) * (1.0 + scale_f) + shift_f).astype(BF16)
        return x1, h

    def ffn(h):
        f = None
        for c in range(D_FF // FF_CHUNK):
            hid = jnp.maximum(_dot(h, w_ff1_ref[:, c * FF_CHUNK:(c + 1) * FF_CHUNK]), 0.0)
            part = _dot((hid * hid).astype(BF16), w_ff2_ref[c * FF_CHUNK:(c + 1) * FF_CHUNK, :])
            f = part if f is None else f + part
            yield
        return f

    def finish(r0, x1, f):
        o_ref[r0:r0 + SUB_ROWS, :] = x1 + gate_f * _rms(f, g_fpost_ref[...])

    starts = [a * SUB_ROWS for a in range(tq // SUB_ROWS)]
    x1h = {r0: mixer_norm(r0, mixer_proj(r0, _run(attention(r0)))) for r0 in starts}
    for r0 in starts:
        finish(r0, x1h[r0][0], _run(ffn(x1h[r0][1])))


def _phase2(x2d, mod3, mod_row_fn, q, sgu, kv_list, kv_rows, shared_kv, tq, n_outer, n_inner, wts,
            name):
    n_kv = len(kv_list) // 2
    in_specs = [
        pl.BlockSpec((tq, D_MODEL), lambda b, i: (b * n_inner + i, 0)),
        _const_spec((MOD_ROWS, N_MOD * D_MODEL)),
        pl.BlockSpec((tq, QK_WIDTH), lambda b, i: (b * n_inner + i, 0)),
        pl.BlockSpec((tq, SGU_WIDTH), lambda b, i: (b * n_inner + i, 0)),
    ]
    for t in range(n_kv):
        in_specs.append(pl.BlockSpec((QK_WIDTH, kv_rows[t]), lambda b, i: (0, b)))
        in_specs.append(pl.BlockSpec((kv_rows[t], MLA_WIDTH), lambda b, i: (b, 0)))
    in_specs += [
        _const_spec((D_MODEL, D_MODEL)),
        _const_spec((1, D_MODEL)),
        _const_spec((1, D_MODEL)),
        _const_spec((1, D_MODEL)),
        _const_spec((D_MODEL, D_FF)),
        _const_spec((D_FF, D_MODEL)),
    ]
    return pl.pallas_call(
        functools.partial(_phase2_kernel, n_kv, shared_kv, mod_row_fn),
        out_shape=jax.ShapeDtypeStruct(x2d.shape, F32),
        grid=(n_outer, n_inner),
        in_specs=in_specs,
        out_specs=pl.BlockSpec((tq, D_MODEL), lambda b, i: (b * n_inner + i, 0)),
        compiler_params=pltpu.CompilerParams(
            dimension_semantics=("arbitrary", "arbitrary"), vmem_limit_bytes=VMEM_LIMIT),
        name=name,
    )(x2d, mod3, q, sgu, *kv_list, wts["w_o"], wts["g_attn_post"], wts["g_ffn_pre"],
      wts["g_ffn_post"], wts["w_ff1"], wts["w_ff2"])


def _pair_swap(w):
    shp = w.shape
    return w.reshape(shp[:-1] + (shp[-1] // 2, 2))[..., ::-1].reshape(shp)


def _prepare_weights(g_attn_pre, g_attn_post, g_q, w_uq, g_kv, w_ukv, w_sgu, b_sgu, g_sgu,
                     beta_sgu, w_o, g_ffn_pre, g_ffn_post, w_ff1, w_ff2):
    w_uq_h = w_uq.reshape(Q_LORA, MLA_HEADS, QK_NOPE + QK_ROPE)
    w_uq_ext = jnp.concatenate(
        [w_uq_h, _pair_swap(w_uq_h[..., QK_NOPE:])], axis=-1).reshape(Q_LORA, QK_WIDTH).astype(BF16)

    w_ukv_h = w_ukv.reshape(KV_LORA, MLA_HEADS, QK_NOPE + V_HEAD)
    w_uk_slots = jnp.concatenate(
        [w_ukv_h[..., :QK_NOPE], jnp.zeros((KV_LORA, MLA_HEADS, HEAD_SLOT - QK_NOPE), F32)],
        axis=-1).reshape(KV_LORA, QK_WIDTH)
    eye = jnp.eye(QK_ROPE, dtype=F32)
    zeros_rope = jnp.zeros((QK_ROPE, QK_ROPE), F32)
    zeros_nope = jnp.zeros((QK_ROPE, QK_NOPE), F32)
    to_lo = jnp.tile(jnp.concatenate([zeros_nope, eye, zeros_rope], axis=1), (1, MLA_HEADS))
    to_hi = jnp.tile(jnp.concatenate([zeros_nope, zeros_rope, eye], axis=1), (1, MLA_HEADS))
    w_k = jnp.concatenate([w_uk_slots, to_lo, to_lo, to_hi, to_hi], axis=0).T.astype(BF16)
    w_uv = w_ukv_h[..., QK_NOPE:].reshape(KV_LORA, MLA_WIDTH).astype(BF16)

    w_sgu_pair = w_sgu.reshape(SGU_HEADS // 2, 2 * CHUNK, CHUNK).astype(BF16)
    bias_sgu = jnp.repeat(b_sgu.T, SGU_HEAD_DIM, axis=1)
    row = lambda a: a.reshape(1, -1)
    return {
        "g_attn_pre": row(g_attn_pre), "g_attn_post": row(g_attn_post),
        "g_q": row(g_q), "w_uq": w_uq_ext, "g_kv": row(g_kv), "w_k": w_k, "w_uv": w_uv,
        "g_sgu": row(g_sgu), "beta_sgu": row(beta_sgu), "w_sgu": w_sgu_pair, "bias_sgu": bias_sgu,
        "w_o32": w_o, "g_ffn_pre": row(g_ffn_pre), "g_ffn_post": row(g_ffn_post),
        "w_ff1_32": w_ff1, "w_ff2_32": w_ff2,
    }


def _rope_tables(n_tok):
    rows = n_tok // GRID_W
    row = np.repeat(np.arange(rows), GRID_W).astype(np.float32)
    col = np.tile(np.arange(GRID_W), rows).astype(np.float32)
    freqs = (1.0 / (ROPE_BASE ** (np.arange(AXIS_PAIRS, dtype=np.float32) / AXIS_PAIRS))).astype(
        np.float32)
    ang = np.concatenate([row[:, None] * freqs, col[:, None] * freqs], axis=-1)
    cos = np.repeat(np.cos(ang), 2, axis=1)
    sin = np.repeat(np.sin(ang), 2, axis=1) * np.tile(np.array([-1.0, 1.0], np.float32), QK_ROPE // 2)
    ktab = np.concatenate([cos, sin, cos, sin], axis=1)
    qtab = QUERY_SCALE * np.concatenate([np.ones((n_tok, QK_NOPE), np.float32), cos, sin], axis=1)
    return jnp.asarray(ktab, F32), jnp.asarray(qtab, F32)


def kernel(x_prompt, x_sample, cache_ckv, cache_krope, c, c_ctx, w_mod, b_mod, g_attn_pre,
           g_attn_post, w_in, g_q, w_uq, g_kv, w_ukv, w_sgu, b_sgu, g_sgu, beta_sgu, w_o,
           g_ffn_pre, g_ffn_post, w_ff1, w_ff2):
    batch, seq, _ = x_prompt.shape
    dec_batch, dec_seq, _ = x_sample.shape
    past_len = cache_ckv.shape[2]
    depth = w_mod.shape[0]
    assert depth == 1

    wts = _prepare_weights(g_attn_pre[0], g_attn_post[0], g_q[0], w_uq[0], g_kv[0],
                           w_ukv[0], w_sgu[0], b_sgu[0], g_sgu[0], beta_sgu[0], w_o[0],
                           g_ffn_pre[0], g_ffn_post[0], w_ff1[0], w_ff2[0])

    cond8 = jnp.concatenate(
        [c_ctx[None, :], c, jnp.zeros((MOD_ROWS - 1 - dec_batch, D_MODEL), F32)], axis=0)
    mod3, w_attn, w_gate = _modulation(cond8, w_mod[0], b_mod[0].reshape(1, -1), w_in[0].T)
    wts = dict(wts, w_attn=w_attn, w_gate=w_gate)

    ktab_ctx = jnp.asarray(np.arange(LANES)[None, :] < QK_ROPE, F32)
    qtab_ctx = jnp.asarray(QUERY_SCALE * (np.arange(LANES)[None, :] < QK_NOPE + QK_ROPE), F32)
    xp2d = x_prompt.reshape(batch * seq, D_MODEL)
    q_c, k_c, v_c, sgu_c, ckv_c, kr_c, w_o16, w_ff1_16, w_ff2_16 = _phase1(
        xp2d, mod3, lambda i: 0, ktab_ctx, qtab_ctx, lambda i: 0, wts, True)
    wts = dict(wts, w_o=w_o16, w_ff1=w_ff1_16, w_ff2=w_ff2_16)
    assert seq == SUB_ROWS and dec_seq % PHASE2_ROWS_LAT == 0
    y_prompt = _phase2(xp2d, mod3, lambda b: 0, q_c, sgu_c, [k_c, v_c], [PHASE2_ROWS_CTX], False,
                       PHASE2_ROWS_CTX, batch * seq // PHASE2_ROWS_CTX, 1, wts, "phase2_ctx")

    ktab_lat, qtab_lat = _rope_tables(dec_seq)
    tiles_per_seq = dec_seq // TOKEN_TILE
    xs2d = x_sample.reshape(dec_batch * dec_seq, D_MODEL)
    assert tiles_per_seq == 1
    kr_p = cache_krope[:, 0].reshape(dec_batch * past_len, QK_ROPE)
    pad = jnp.zeros_like(kr_p)
    kin_p = jnp.concatenate(
        [cache_ckv[:, 0].reshape(dec_batch * past_len, KV_LORA), kr_p, pad, kr_p, pad], axis=1)
    q_l, k_l, v_l, sgu_l = _phase1(
        xs2d, mod3, lambda i: 1 + i, ktab_lat, qtab_lat, lambda i: 0, wts, False, kin_p)
    y_sample = _phase2(xs2d, mod3, lambda b: 1 + b, q_l, sgu_l, [k_l, v_l],
                       [past_len + dec_seq], True, PHASE2_ROWS_LAT, dec_batch,
                       dec_seq // PHASE2_ROWS_LAT, wts, "phase2_lat")

    return (y_prompt.reshape(batch, seq, D_MODEL),
            y_sample.reshape(dec_batch, dec_seq, D_MODEL),
            ckv_c.reshape(batch, 1, seq, KV_LORA),
            kr_c.reshape(batch, 1, QK_ROPE, seq).transpose(0, 1, 3, 2))
```

```python
import functools
import math

import jax
import jax.numpy as jnp
import numpy as np
from jax import lax
from jax.experimental import pallas as pl
from jax.experimental.pallas import tpu as pltpu

D_MODEL = 1024
GRID_W = 64
MLA_HEADS = 8
QK_NOPE = 64
QK_ROPE = 32
V_HEAD = 64
Q_LORA = 256
KV_LORA = 128
MLA_WIDTH = MLA_HEADS * V_HEAD
SGU_HEADS = 8
SGU_WIDTH = D_MODEL - MLA_WIDTH
SGU_HEAD_DIM = SGU_WIDTH // SGU_HEADS
CHUNK = 128
D_FF = 4 * D_MODEL
AXIS_PAIRS = QK_ROPE // 4
ROPE_BASE = 10000.0
EPS = 1e-6
N_MOD = 6
ATTN_SCALE = (QK_NOPE + QK_ROPE) ** -0.5
QUERY_SCALE = ATTN_SCALE * math.log2(math.e)

LANES = 128
HEAD_SLOT = LANES
QK_WIDTH = MLA_HEADS * HEAD_SLOT
ATTN_PROJ = Q_LORA + KV_LORA + LANES
MOD_ROWS = 8
MOD_K_ROWS = 128
MOD_STREAMS = 2
TOKEN_TILE = 1024
PHASE1_CHAIN = 1024
SUB_ROWS = 256
PHASE2_ROWS_CTX = 512
PHASE2_ROWS_LAT = 512
KEY_SPLIT = 512
QK_AHEAD = 1
FF_CHUNK = 1024
VMEM_LIMIT = 56 * 1024 * 1024

BF16 = jnp.bfloat16
F32 = jnp.float32


def _dot(a, b):
    return jnp.dot(a, b, preferred_element_type=F32)


def _dot_nt(a, b):
    return lax.dot_general(a, b, (((1,), (1,)), ((), ())), preferred_element_type=F32)


def _rms(x, g):
    return x * lax.rsqrt(jnp.mean(x * x, axis=-1, keepdims=True) + EPS) * g


def _gelu(x):
    inner = math.sqrt(2.0 / math.pi) * (x + 0.044715 * (x * x * x))
    return x * (0.5 * (1.0 + jnp.tanh(inner)))


def _mod_kernel(*refs):
    cond_refs = refs[:MOD_STREAMS]
    w_refs = refs[MOD_STREAMS:2 * MOD_STREAMS]
    (b_ref, w_in_attn_ref, w_in_gate_ref, o_ref, w_attn_ref, w_gate_ref) = refs[2 * MOD_STREAMS:]

    @pl.when(pl.program_id(0) == 0)
    def _():
        o_ref[...] = jnp.broadcast_to(b_ref[...], o_ref.shape)

    def part(cond_ref, w_ref):
        cnd = cond_ref[...]
        act = cnd * (1.0 / (1.0 + jnp.exp(-cnd)))
        return _dot(act.astype(BF16), w_ref[...].astype(BF16))

    acc = part(cond_refs[0], w_refs[0])
    for cond_ref, w_ref in zip(cond_refs[1:], w_refs[1:]):
        acc = acc + part(cond_ref, w_ref)
    o_ref[...] += acc

    w_gate_ref[...] = w_in_gate_ref[...].astype(BF16)

    @pl.when(pl.program_id(0) == 0)
    def _():
        rope0 = Q_LORA + KV_LORA
        kr = w_in_attn_ref[rope0:rope0 + QK_ROPE, :]
        row = lax.broadcasted_iota(jnp.int32, kr.shape, 0)
        kr_sw = jnp.where(row % 2 == 0, pltpu.roll(kr, QK_ROPE - 1, 0), pltpu.roll(kr, 1, 0))
        w_attn_ref[0:rope0 + QK_ROPE, :] = w_in_attn_ref[...].astype(BF16)
        w_attn_ref[rope0 + QK_ROPE:ATTN_PROJ, :] = jnp.concatenate(
            [kr_sw, kr, kr_sw], axis=0).astype(BF16)


def _modulation(cond8, w_mod, b_mod, w_in_t):
    n = w_mod.shape[1]
    tk = MOD_K_ROWS
    nk = D_MODEL // tk // MOD_STREAMS
    attn_rows = Q_LORA + KV_LORA + QK_ROPE
    gate_rows = 2 * SGU_WIDTH // nk
    cond_specs = [pl.BlockSpec((MOD_ROWS, tk), lambda k, s=s: (0, k + s * nk))
                  for s in range(MOD_STREAMS)]
    w_specs = [pl.BlockSpec((tk, n), lambda k, s=s: (k + s * nk, 0)) for s in range(MOD_STREAMS)]
    return pl.pallas_call(
        _mod_kernel,
        out_shape=[jax.ShapeDtypeStruct((MOD_ROWS, n), F32),
                   jax.ShapeDtypeStruct((ATTN_PROJ, D_MODEL), BF16),
                   jax.ShapeDtypeStruct((2 * SGU_WIDTH, D_MODEL), BF16)],
        grid=(nk,),
        in_specs=cond_specs + w_specs + [
            pl.BlockSpec((1, n), lambda k: (0, 0)),
            pl.BlockSpec((attn_rows, D_MODEL), lambda k: (0, 0)),
            pl.BlockSpec((pl.Element(gate_rows), pl.Element(D_MODEL)),
                         lambda k: (pl.multiple_of(attn_rows + k * gate_rows, QK_ROPE), 0)),
        ],
        out_specs=[pl.BlockSpec((MOD_ROWS, n), lambda k: (0, 0)),
                   pl.BlockSpec((ATTN_PROJ, D_MODEL), lambda k: (0, 0)),
                   pl.BlockSpec((gate_rows, D_MODEL), lambda k: (k, 0))],
        compiler_params=pltpu.CompilerParams(
            dimension_semantics=("arbitrary",), vmem_limit_bytes=VMEM_LIMIT),
        name="modulation",
    )(*([cond8] * MOD_STREAMS), *([w_mod] * MOD_STREAMS), b_mod, w_in_t, w_in_t)


def _mod_rows(mod_ref, row):
    vec = mod_ref[pl.ds(row, 1), :]
    return [vec[:, j * D_MODEL:(j + 1) * D_MODEL] for j in range(N_MOD)]


def _phase1_kernel(emit_cache, mod_row_fn, x_ref, mod_ref, ktab_ref, qtab_ref, g_pre_ref, w_attn_ref,
                   w_gate_ref,
                   g_q_ref, w_uq_ref, g_kv_ref, w_k_ref, w_uv_ref, g_sgu_ref, beta_sgu_ref,
                   w_sgu_ref, bias_ref, *rest):
    if emit_cache:
        (w_o32_ref, w_ff1_32_ref, w_ff2_32_ref, q_ref, k_ref, v_ref, sgu_ref, ckv_ref, kr_ref,
         w_o16_ref, w_ff1_16_ref, w_ff2_16_ref) = rest
        w_o16_ref[...] = w_o32_ref[...].astype(BF16)
        w_ff1_16_ref[...] = w_ff1_32_ref[...].astype(BF16)
        w_ff2_16_ref[...] = w_ff2_32_ref[...].astype(BF16)
    else:
        q_ref, k_ref, v_ref, sgu_ref = rest
    tm = x_ref.shape[0]
    mod = _mod_rows(mod_ref, mod_row_fn(pl.program_id(0)))
    shift_a, scale_a = mod[0], mod[1]
    low_half = lax.broadcasted_iota(jnp.int32, (CHUNK, LANES), 1) < SGU_HEAD_DIM
    per_pos_tables = ktab_ref.shape[0] > 1

    def project(r0):
        rows = slice(r0, r0 + PHASE1_CHAIN)
        h = (_rms(x_ref[rows, :], g_pre_ref[...]) * (1.0 + scale_a) + shift_a).astype(BF16)
        proj = _dot_nt(h, w_attn_ref[...])
        gate = _dot_nt(h, w_gate_ref[...])
        return proj, gate

    def expand_qkv(r0, proj):
        rows = slice(r0, r0 + PHASE1_CHAIN)
        tab_rows = rows if per_pos_tables else slice(None)
        cq = _rms(proj[:, 0:Q_LORA], g_q_ref[...])
        q = _dot(cq.astype(BF16), w_uq_ref[...])
        qtab = qtab_ref[tab_rows, :]
        q_ref[rows, :] = jnp.concatenate(
            [(q[:, s * HEAD_SLOT:(s + 1) * HEAD_SLOT] * qtab).astype(BF16)
             for s in range(MLA_HEADS)], axis=1)
        ckv_n = _rms(proj[:, Q_LORA:Q_LORA + KV_LORA], g_kv_ref[...])
        rope_slab = proj[:, Q_LORA + KV_LORA:Q_LORA + KV_LORA + LANES]
        if emit_cache:
            ckv_ref[rows, :] = ckv_n
            kr_t = rope_slab.T
            for e in range(PHASE1_CHAIN // SUB_ROWS):
                elem = r0 // SUB_ROWS + e
                kr_ref[elem * QK_ROPE:(elem + 1) * QK_ROPE, :] = (
                    kr_t[0:QK_ROPE, e * SUB_ROWS:(e + 1) * SUB_ROWS])
        ckv_b = ckv_n.astype(BF16)
        kin = jnp.concatenate([ckv_b, (rope_slab * ktab_ref[tab_rows, :]).astype(BF16)], axis=1)
        k_ref[:, rows] = _dot_nt(w_k_ref[...], kin).astype(BF16)
        v_ref[rows, :] = _dot(ckv_b, w_uv_ref[...]).astype(BF16)

    def gating_unit(r0, gate):
        vv = _gelu(gate[:, SGU_WIDTH:2 * SGU_WIDTH])
        mu = jnp.mean(vv, axis=-1, keepdims=True)
        vc = vv - mu
        var = jnp.mean(vc * vc, axis=-1, keepdims=True)
        vn = (vc * lax.rsqrt(var + EPS) * g_sgu_ref[...] + beta_sgu_ref[...]).astype(BF16)
        n_chunks = PHASE1_CHAIN // CHUNK
        for j in range(SGU_WIDTH // LANES):
            lanes = slice(j * LANES, (j + 1) * LANES)
            rhs = jnp.concatenate(
                [vn[n * CHUNK:(n + 1) * CHUNK, lanes] for n in range(n_chunks)], axis=1)
            o = _dot(w_sgu_ref[j], rhs)
            u = _gelu(gate[:, lanes])
            bias = bias_ref[:, lanes]
            for n in range(n_chunks):
                even = o[0:CHUNK, n * LANES:(n + 1) * LANES]
                odd = o[CHUNK:2 * CHUNK, n * LANES:(n + 1) * LANES]
                mixed = jnp.where(low_half, even, odd) + bias
                sgu_ref[r0 + n * CHUNK:r0 + (n + 1) * CHUNK, lanes] = (
                    u[n * CHUNK:(n + 1) * CHUNK, :] * mixed).astype(BF16)

    starts = list(range(0, tm, PHASE1_CHAIN))
    ahead = project(starts[0])
    for i, r0 in enumerate(starts):
        proj, gate = ahead
        if i + 1 < len(starts):
            ahead = project(starts[i + 1])
        expand_qkv(r0, proj)
        gating_unit(r0, gate)


def _const_spec(shape):
    nd = len(shape)
    return pl.BlockSpec(shape, lambda *_: (0,) * nd, pipeline_mode=pl.Buffered(1))


def _phase1(x2d, mod3, mod_row_fn, ktab, qtab, tab_fn, wts, emit_cache):
    n_tok = x2d.shape[0]
    tm = TOKEN_TILE
    grid = (n_tok // tm,)
    tab_block = (ktab.shape[0] if ktab.shape[0] == 1 else tm, LANES)
    in_specs = [
        pl.BlockSpec((tm, D_MODEL), lambda i: (i, 0)),
        _const_spec((MOD_ROWS, N_MOD * D_MODEL)),
        pl.BlockSpec(tab_block, lambda i: (tab_fn(i), 0)),
        pl.BlockSpec(tab_block, lambda i: (tab_fn(i), 0)),
        _const_spec((1, D_MODEL)),
        _const_spec((ATTN_PROJ, D_MODEL)),
        _const_spec((2 * SGU_WIDTH, D_MODEL)),
        _const_spec((1, Q_LORA)),
        _const_spec((Q_LORA, QK_WIDTH)),
        _const_spec((1, KV_LORA)),
        _const_spec((QK_WIDTH, 2 * LANES)),
        _const_spec((KV_LORA, MLA_WIDTH)),
        _const_spec((1, SGU_WIDTH)),
        _const_spec((1, SGU_WIDTH)),
        _const_spec((SGU_WIDTH // LANES, 2 * CHUNK, CHUNK)),
        _const_spec((CHUNK, SGU_WIDTH)),
    ]
    out_shape = [
        jax.ShapeDtypeStruct((n_tok, QK_WIDTH), BF16),
        jax.ShapeDtypeStruct((QK_WIDTH, n_tok), BF16),
        jax.ShapeDtypeStruct((n_tok, MLA_WIDTH), BF16),
        jax.ShapeDtypeStruct((n_tok, SGU_WIDTH), BF16),
    ]
    out_specs = [
        pl.BlockSpec((tm, QK_WIDTH), lambda i: (i, 0)),
        pl.BlockSpec((QK_WIDTH, tm), lambda i: (0, i)),
        pl.BlockSpec((tm, MLA_WIDTH), lambda i: (i, 0)),
        pl.BlockSpec((tm, SGU_WIDTH), lambda i: (i, 0)),
    ]
    extra_inputs = []
    if emit_cache:
        kr_rows = tm // SUB_ROWS * QK_ROPE
        out_shape += [jax.ShapeDtypeStruct((n_tok, KV_LORA), F32),
                      jax.ShapeDtypeStruct((n_tok // SUB_ROWS * QK_ROPE, SUB_ROWS), F32)]
        out_specs += [pl.BlockSpec((tm, KV_LORA), lambda i: (i, 0)),
                      pl.BlockSpec((kr_rows, SUB_ROWS), lambda i: (i, 0))]
        for w in (wts["w_o32"], wts["w_ff1_32"], wts["w_ff2_32"]):
            rows, cols = w.shape
            blk = (rows // grid[0], cols)
            extra_inputs.append(w)
            in_specs.append(pl.BlockSpec(blk, lambda i: (i, 0)))
            out_shape.append(jax.ShapeDtypeStruct(w.shape, BF16))
            out_specs.append(pl.BlockSpec(blk, lambda i: (i, 0)))
    return pl.pallas_call(
        functools.partial(_phase1_kernel, emit_cache, mod_row_fn),
        out_shape=out_shape,
        grid=grid,
        in_specs=in_specs,
        out_specs=out_specs,
        compiler_params=pltpu.CompilerParams(
            dimension_semantics=("arbitrary",), vmem_limit_bytes=VMEM_LIMIT),
        name="phase1_ctx" if emit_cache else "phase1_lat",
    )(x2d, mod3, ktab, qtab, wts["g_attn_pre"], wts["w_attn"], wts["w_gate"], wts["g_q"], wts["w_uq"],
      wts["g_kv"], wts["w_k"], wts["w_uv"], wts["g_sgu"], wts["beta_sgu"], wts["w_sgu"],
      wts["bias_sgu"], *extra_inputs)


def _cache_kv_kernel(kin_ref, w_k_ref, w_uv_ref, k_ref, v_ref):
    kin = kin_ref[...].astype(BF16)
    k_ref[...] = _dot_nt(w_k_ref[...], kin).astype(BF16)
    v_ref[...] = _dot(kin[:, 0:KV_LORA], w_uv_ref[...]).astype(BF16)


def _cache_kv(kin2d, wts):
    n_tok = kin2d.shape[0]
    tm = n_tok
    return pl.pallas_call(
        _cache_kv_kernel,
        out_shape=[jax.ShapeDtypeStruct((QK_WIDTH, n_tok), BF16),
                   jax.ShapeDtypeStruct((n_tok, MLA_WIDTH), BF16)],
        grid=(n_tok // tm,),
        in_specs=[
            pl.BlockSpec((tm, 2 * LANES), lambda i: (i, 0)),
            _const_spec((QK_WIDTH, 2 * LANES)),
            _const_spec((KV_LORA, MLA_WIDTH)),
        ],
        out_specs=[pl.BlockSpec((QK_WIDTH, tm), lambda i: (0, i)),
                   pl.BlockSpec((tm, MLA_WIDTH), lambda i: (i, 0))],
        compiler_params=pltpu.CompilerParams(
            dimension_semantics=("arbitrary",), vmem_limit_bytes=VMEM_LIMIT),
        name="cache_kv",
    )(kin2d, wts["w_k"], wts["w_uv"])


def _interleave(*gens):
    results = [None] * len(gens)
    live = list(range(len(gens)))
    while live:
        for i in list(live):
            try:
                next(gens[i])
            except StopIteration as stop:
                results[i] = stop.value
                live.remove(i)
    return results


def _run(gen):
    return _interleave(gen)[0]


def _attend(q_ref, kv_views, r0, low_half):
    def head_scores(hd):
        qh = q_ref[r0:r0 + SUB_ROWS, hd * HEAD_SLOT:(hd + 1) * HEAD_SLOT]
        return [_dot(qh, k_view(hd * HEAD_SLOT, HEAD_SLOT)) for k_view, _ in kv_views]

    def head_probs(scores):
        m = scores[0].max(axis=-1, keepdims=True)
        for s in scores[1:]:
            m = jnp.maximum(m, s.max(axis=-1, keepdims=True))
        probs = []
        denom = None
        for s in scores:
            p = jnp.exp2(s - m)
            ps = p.sum(axis=-1, keepdims=True)
            denom = ps if denom is None else denom + ps
            probs.append(p.astype(BF16))
        return probs, denom

    def head_values(hd, probs, denom):
        slab = hd // 2
        acc = None
        for (_, v_view), p in zip(kv_views, probs):
            pv = _dot(p, v_view(slab * LANES, LANES))
            acc = pv if acc is None else acc + pv
        return acc / denom

    pair_out = []
    head_out = None
    ahead = [head_scores(hd) for hd in range(QK_AHEAD)]
    for hd in range(MLA_HEADS):
        if hd + QK_AHEAD < MLA_HEADS:
            ahead.append(head_scores(hd + QK_AHEAD))
        o = head_values(hd, *head_probs(ahead.pop(0)))
        if hd % 2 == 0:
            head_out = o
        else:
            pair_out.append(jnp.where(low_half, head_out, o).astype(BF16))
            yield
    return jnp.concatenate(pair_out, axis=1)


def _phase2_kernel(n_kv, shared_kv, mod_row_fn, x_ref, mod_ref, q_ref, sgu_ref, *refs):
    kv_refs = refs[:2 * n_kv]
    (w_o_ref, g_post_ref, g_fpre_ref, g_fpost_ref, w_ff1_ref, w_ff2_ref, o_ref) = refs[2 * n_kv:]
    tq = x_ref.shape[0]
    low_half = lax.broadcasted_iota(jnp.int32, (SUB_ROWS, LANES), 1) < V_HEAD
    _, _, gate_a, shift_f, scale_f, gate_f = _mod_rows(mod_ref, mod_row_fn(pl.program_id(0)))
    def attention(r0):
        views = []
        for t in range(n_kv):
            k_ref, v_ref = kv_refs[2 * t], kv_refs[2 * t + 1]
            if shared_kv:
                n_keys = v_ref.shape[0]
                step = min(n_keys, KEY_SPLIT)
                for j in range(0, n_keys, step):
                    views.append(
                        (lambda c, w, k_ref=k_ref, j=j, n=step: k_ref[c:c + w, j:j + n],
                         lambda c, w, v_ref=v_ref, j=j, n=step: v_ref[j:j + n, c:c + w]))
            else:
                views.append((lambda c, w, k_ref=k_ref: k_ref[c:c + w, r0:r0 + SUB_ROWS],
                              lambda c, w, v_ref=v_ref: v_ref[r0:r0 + SUB_ROWS, c:c + w]))
        return _attend(q_ref, views, r0, low_half)

    def mixer_proj(r0, attn):
        mix_in = jnp.concatenate([attn, sgu_ref[r0:r0 + SUB_ROWS, :]], axis=1)
        return _dot(mix_in, w_o_ref[...])

    def mixer_norm(r0, mix):
        x1 = x_ref[r0:r0 + SUB_ROWS, :] + gate_a * _rms(mix, g_post_ref[...])
        h = (_rms(x1, g_fpre_ref[...]) * (1.0 + scale_f) + shift_f).astype(BF16)
        return x1, h

    def ffn(h):
        f = None
        for c in range(D_FF // FF_CHUNK):
            hid = jnp.maximum(_dot(h, w_ff1_ref[:, c * FF_CHUNK:(c + 1) * FF_CHUNK]), 0.0)
            part = _dot((hid * hid).astype(BF16), w_ff2_ref[c * FF_CHUNK:(c + 1) * FF_CHUNK, :])
            f = part if f is None else f + part
            yield
        return f

    def finish(r0, x1, f):
        o_ref[r0:r0 + SUB_ROWS, :] = x1 + gate_f * _rms(f, g_fpost_ref[...])

    starts = [a * SUB_ROWS for a in range(tq // SUB_ROWS)]
    x1h = {r0: mixer_norm(r0, mixer_proj(r0, _run(attention(r0)))) for r0 in starts}
    for r0 in starts:
        finish(r0, x1h[r0][0], _run(ffn(x1h[r0][1])))


def _phase2(x2d, mod3, mod_row_fn, q, sgu, kv_list, kv_rows, shared_kv, tq, n_outer, n_inner, wts,
            name):
    n_kv = len(kv_list) // 2
    in_specs = [
        pl.BlockSpec((tq, D_MODEL), lambda b, i: (b * n_inner + i, 0)),
        _const_spec((MOD_ROWS, N_MOD * D_MODEL)),
        pl.BlockSpec((tq, QK_WIDTH), lambda b, i: (b * n_inner + i, 0)),
        pl.BlockSpec((tq, SGU_WIDTH), lambda b, i: (b * n_inner + i, 0)),
    ]
    for t in range(n_kv):
        in_specs.append(pl.BlockSpec((QK_WIDTH, kv_rows[t]), lambda b, i: (0, b)))
        in_specs.append(pl.BlockSpec((kv_rows[t], MLA_WIDTH), lambda b, i: (b, 0)))
    in_specs += [
        _const_spec((D_MODEL, D_MODEL)),
        _const_spec((1, D_MODEL)),
        _const_spec((1, D_MODEL)),
        _const_spec((1, D_MODEL)),
        _const_spec((D_MODEL, D_FF)),
        _const_spec((D_FF, D_MODEL)),
    ]
    return pl.pallas_call(
        functools.partial(_phase2_kernel, n_kv, shared_kv, mod_row_fn),
        out_shape=jax.ShapeDtypeStruct(x2d.shape, F32),
        grid=(n_outer, n_inner),
        in_specs=in_specs,
        out_specs=pl.BlockSpec((tq, D_MODEL), lambda b, i: (b * n_inner + i, 0)),
        compiler_params=pltpu.CompilerParams(
            dimension_semantics=("arbitrary", "arbitrary"), vmem_limit_bytes=VMEM_LIMIT),
        name=name,
    )(x2d, mod3, q, sgu, *kv_list, wts["w_o"], wts["g_attn_post"], wts["g_ffn_pre"],
      wts["g_ffn_post"], wts["w_ff1"], wts["w_ff2"])


def _pair_swap(w):
    shp = w.shape
    return w.reshape(shp[:-1] + (shp[-1] // 2, 2))[..., ::-1].reshape(shp)


def _prepare_weights(g_attn_pre, g_attn_post, g_q, w_uq, g_kv, w_ukv, w_sgu, b_sgu, g_sgu,
                     beta_sgu, w_o, g_ffn_pre, g_ffn_post, w_ff1, w_ff2):
    w_uq_h = w_uq.reshape(Q_LORA, MLA_HEADS, QK_NOPE + QK_ROPE)
    w_uq_ext = jnp.concatenate(
        [w_uq_h, _pair_swap(w_uq_h[..., QK_NOPE:])], axis=-1).reshape(Q_LORA, QK_WIDTH).astype(BF16)

    w_ukv_h = w_ukv.reshape(KV_LORA, MLA_HEADS, QK_NOPE + V_HEAD)
    w_uk_slots = jnp.concatenate(
        [w_ukv_h[..., :QK_NOPE], jnp.zeros((KV_LORA, MLA_HEADS, HEAD_SLOT - QK_NOPE), F32)],
        axis=-1).reshape(KV_LORA, QK_WIDTH)
    eye = jnp.eye(QK_ROPE, dtype=F32)
    zeros_rope = jnp.zeros((QK_ROPE, QK_ROPE), F32)
    zeros_nope = jnp.zeros((QK_ROPE, QK_NOPE), F32)
    to_lo = jnp.tile(jnp.concatenate([zeros_nope, eye, zeros_rope], axis=1), (1, MLA_HEADS))
    to_hi = jnp.tile(jnp.concatenate([zeros_nope, zeros_rope, eye], axis=1), (1, MLA_HEADS))
    w_k = jnp.concatenate([w_uk_slots, to_lo, to_lo, to_hi, to_hi], axis=0).T.astype(BF16)
    w_uv = w_ukv_h[..., QK_NOPE:].reshape(KV_LORA, MLA_WIDTH).astype(BF16)

    w_sgu_pair = w_sgu.reshape(SGU_HEADS // 2, 2 * CHUNK, CHUNK).astype(BF16)
    bias_sgu = jnp.repeat(b_sgu.T, SGU_HEAD_DIM, axis=1)
    row = lambda a: a.reshape(1, -1)
    return {
        "g_attn_pre": row(g_attn_pre), "g_attn_post": row(g_attn_post),
        "g_q": row(g_q), "w_uq": w_uq_ext, "g_kv": row(g_kv), "w_k": w_k, "w_uv": w_uv,
        "g_sgu": row(g_sgu), "beta_sgu": row(beta_sgu), "w_sgu": w_sgu_pair, "bias_sgu": bias_sgu,
        "w_o32": w_o, "g_ffn_pre": row(g_ffn_pre), "g_ffn_post": row(g_ffn_post),
        "w_ff1_32": w_ff1, "w_ff2_32": w_ff2,
    }


def _rope_tables(n_tok):
    rows = n_tok // GRID_W
    row = np.repeat(np.arange(rows), GRID_W).astype(np.float32)
    col = np.tile(np.arange(GRID_W), rows).astype(np.float32)
    freqs = (1.0 / (ROPE_BASE ** (np.arange(AXIS_PAIRS, dtype=np.float32) / AXIS_PAIRS))).astype(
        np.float32)
    ang = np.concatenate([row[:, None] * freqs, col[:, None] * freqs], axis=-1)
    cos = np.repeat(np.cos(ang), 2, axis=1)
    sin = np.repeat(np.sin(ang), 2, axis=1) * np.tile(np.array([-1.0, 1.0], np.float32), QK_ROPE // 2)
    ktab = np.concatenate([cos, sin, cos, sin], axis=1)
    qtab = QUERY_SCALE * np.concatenate([np.ones((n_tok, QK_NOPE), np.float32), cos, sin], axis=1)
    return jnp.asarray(ktab, F32), jnp.asarray(qtab, F32)


def kernel(x_prompt, x_sample, cache_ckv, cache_krope, c, c_ctx, w_mod, b_mod, g_attn_pre,
           g_attn_post, w_in, g_q, w_uq, g_kv, w_ukv, w_sgu, b_sgu, g_sgu, beta_sgu, w_o,
           g_ffn_pre, g_ffn_post, w_ff1, w_ff2):
    batch, seq, _ = x_prompt.shape
    dec_batch, dec_seq, _ = x_sample.shape
    past_len = cache_ckv.shape[2]
    depth = w_mod.shape[0]
    assert depth == 1

    wts = _prepare_weights(g_attn_pre[0], g_attn_post[0], g_q[0], w_uq[0], g_kv[0],
                           w_ukv[0], w_sgu[0], b_sgu[0], g_sgu[0], beta_sgu[0], w_o[0],
                           g_ffn_pre[0], g_ffn_post[0], w_ff1[0], w_ff2[0])

    cond8 = jnp.concatenate(
        [c_ctx[None, :], c, jnp.zeros((MOD_ROWS - 1 - dec_batch, D_MODEL), F32)], axis=0)
    mod3, w_attn, w_gate = _modulation(cond8, w_mod[0], b_mod[0].reshape(1, -1), w_in[0].T)
    wts = dict(wts, w_attn=w_attn, w_gate=w_gate)

    ktab_ctx = jnp.asarray(np.arange(LANES)[None, :] < QK_ROPE, F32)
    qtab_ctx = jnp.asarray(QUERY_SCALE * (np.arange(LANES)[None, :] < QK_NOPE + QK_ROPE), F32)
    xp2d = x_prompt.reshape(batch * seq, D_MODEL)
    q_c, k_c, v_c, sgu_c, ckv_c, kr_c, w_o16, w_ff1_16, w_ff2_16 = _phase1(
        xp2d, mod3, lambda i: 0, ktab_ctx, qtab_ctx, lambda i: 0, wts, True)
    wts = dict(wts, w_o=w_o16, w_ff1=w_ff1_16, w_ff2=w_ff2_16)
    assert seq == SUB_ROWS and dec_seq % PHASE2_ROWS_LAT == 0
    y_prompt = _phase2(xp2d, mod3, lambda b: 0, q_c, sgu_c, [k_c, v_c], [PHASE2_ROWS_CTX], False,
                       PHASE2_ROWS_CTX, batch * seq // PHASE2_ROWS_CTX, 1, wts, "phase2_ctx")

    ktab_lat, qtab_lat = _rope_tables(dec_seq)
    tiles_per_seq = dec_seq // TOKEN_TILE
    xs2d = x_sample.reshape(dec_batch * dec_seq, D_MODEL)
    q_l, k_l, v_l, sgu_l = _phase1(
        xs2d, mod3, lambda i: 1 + i // tiles_per_seq, ktab_lat, qtab_lat,
        lambda i: i % tiles_per_seq, wts, False)
    kr_p = cache_krope[:, 0].reshape(dec_batch * past_len, QK_ROPE)
    pad = jnp.zeros_like(kr_p)
    kin_p = jnp.concatenate(
        [cache_ckv[:, 0].reshape(dec_batch * past_len, KV_LORA), kr_p, pad, kr_p, pad], axis=1)
    k_p, v_p = _cache_kv(kin_p, wts)
    y_sample = _phase2(xs2d, mod3, lambda b: 1 + b, q_l, sgu_l, [k_p, v_p, k_l, v_l],
                       [past_len, dec_seq], True, PHASE2_ROWS_LAT, dec_batch,
                       dec_seq // PHASE2_ROWS_LAT, wts, "phase2_lat")

    return (y_prompt.reshape(batch, seq, D_MODEL),
            y_sample.reshape(dec_batch, dec_seq, D_MODEL),
            ckv_c.reshape(batch, 1, seq, KV_LORA),
            kr_c.reshape(batch, 1, QK_ROPE, seq).transpose(0, 1, 3, 2))
```

```python
import functools
import math

import jax
import jax.numpy as jnp
import numpy as np
from jax import lax
from jax.experimental import pallas as pl
from jax.experimental.pallas import tpu as pltpu

D_MODEL = 1024
GRID_W = 64
MLA_HEADS = 8
QK_NOPE = 64
QK_ROPE = 32
V_HEAD = 64
Q_LORA = 256
KV_LORA = 128
MLA_WIDTH = MLA_HEADS * V_HEAD
SGU_HEADS = 8
SGU_WIDTH = D_MODEL - MLA_WIDTH
SGU_HEAD_DIM = SGU_WIDTH // SGU_HEADS
CHUNK = 128
D_FF = 4 * D_MODEL
AXIS_PAIRS = QK_ROPE // 4
ROPE_BASE = 10000.0
EPS = 1e-6
N_MOD = 6
ATTN_SCALE = (QK_NOPE + QK_ROPE) ** -0.5
QUERY_SCALE = ATTN_SCALE * math.log2(math.e)

LANES = 128
HEAD_SLOT = LANES
QK_WIDTH = MLA_HEADS * HEAD_SLOT
ATTN_PROJ = Q_LORA + KV_LORA + LANES
MOD_ROWS = 8
MOD_K_ROWS = 128
MOD_STREAMS = 2
TOKEN_TILE = 1024
PHASE1_CHAIN = 1024
SUB_ROWS = 256
PHASE2_ROWS_CTX = 512
PHASE2_ROWS_LAT = 512
KEY_SPLIT = 256
QK_AHEAD = 1
FF_CHUNK = 1024
VMEM_LIMIT = 56 * 1024 * 1024

BF16 = jnp.bfloat16
F32 = jnp.float32


def _dot(a, b):
    return jnp.dot(a, b, preferred_element_type=F32)


def _dot_nt(a, b):
    return lax.dot_general(a, b, (((1,), (1,)), ((), ())), preferred_element_type=F32)


def _rms(x, g):
    return x * lax.rsqrt(jnp.mean(x * x, axis=-1, keepdims=True) + EPS) * g


def _gelu(x):
    inner = math.sqrt(2.0 / math.pi) * (x + 0.044715 * (x * x * x))
    return x * (0.5 * (1.0 + jnp.tanh(inner)))


def _mod_kernel(*refs):
    cond_refs = refs[:MOD_STREAMS]
    w_refs = refs[MOD_STREAMS:2 * MOD_STREAMS]
    (b_ref, w_in_attn_ref, w_in_gate_ref, o_ref, w_attn_ref, w_gate_ref) = refs[2 * MOD_STREAMS:]

    @pl.when(pl.program_id(0) == 0)
    def _():
        o_ref[...] = jnp.broadcast_to(b_ref[...], o_ref.shape)

    def part(cond_ref, w_ref):
        cnd = cond_ref[...]
        act = cnd * (1.0 / (1.0 + jnp.exp(-cnd)))
        return _dot(act.astype(BF16), w_ref[...].astype(BF16))

    acc = part(cond_refs[0], w_refs[0])
    for cond_ref, w_ref in zip(cond_refs[1:], w_refs[1:]):
        acc = acc + part(cond_ref, w_ref)
    o_ref[...] += acc

    w_gate_ref[...] = w_in_gate_ref[...].astype(BF16)

    @pl.when(pl.program_id(0) == 0)
    def _():
        rope0 = Q_LORA + KV_LORA
        kr = w_in_attn_ref[rope0:rope0 + QK_ROPE, :]
        row = lax.broadcasted_iota(jnp.int32, kr.shape, 0)
        kr_sw = jnp.where(row % 2 == 0, pltpu.roll(kr, QK_ROPE - 1, 0), pltpu.roll(kr, 1, 0))
        w_attn_ref[0:rope0 + QK_ROPE, :] = w_in_attn_ref[...].astype(BF16)
        w_attn_ref[rope0 + QK_ROPE:ATTN_PROJ, :] = jnp.concatenate(
            [kr_sw, kr, kr_sw], axis=0).astype(BF16)


def _modulation(cond8, w_mod, b_mod, w_in_t):
    n = w_mod.shape[1]
    tk = MOD_K_ROWS
    nk = D_MODEL // tk // MOD_STREAMS
    attn_rows = Q_LORA + KV_LORA + QK_ROPE
    gate_rows = 2 * SGU_WIDTH // nk
    cond_specs = [pl.BlockSpec((MOD_ROWS, tk), lambda k, s=s: (0, k + s * nk))
                  for s in range(MOD_STREAMS)]
    w_specs = [pl.BlockSpec((tk, n), lambda k, s=s: (k + s * nk, 0)) for s in range(MOD_STREAMS)]
    return pl.pallas_call(
        _mod_kernel,
        out_shape=[jax.ShapeDtypeStruct((MOD_ROWS, n), F32),
                   jax.ShapeDtypeStruct((ATTN_PROJ, D_MODEL), BF16),
                   jax.ShapeDtypeStruct((2 * SGU_WIDTH, D_MODEL), BF16)],
        grid=(nk,),
        in_specs=cond_specs + w_specs + [
            pl.BlockSpec((1, n), lambda k: (0, 0)),
            pl.BlockSpec((attn_rows, D_MODEL), lambda k: (0, 0)),
            pl.BlockSpec((pl.Element(gate_rows), pl.Element(D_MODEL)),
                         lambda k: (pl.multiple_of(attn_rows + k * gate_rows, QK_ROPE), 0)),
        ],
        out_specs=[pl.BlockSpec((MOD_ROWS, n), lambda k: (0, 0)),
                   pl.BlockSpec((ATTN_PROJ, D_MODEL), lambda k: (0, 0)),
                   pl.BlockSpec((gate_rows, D_MODEL), lambda k: (k, 0))],
        compiler_params=pltpu.CompilerParams(
            dimension_semantics=("arbitrary",), vmem_limit_bytes=VMEM_LIMIT),
        name="modulation",
    )(*([cond8] * MOD_STREAMS), *([w_mod] * MOD_STREAMS), b_mod, w_in_t, w_in_t)


def _mod_rows(mod_ref, row):
    vec = mod_ref[pl.ds(row, 1), :]
    return [vec[:, j * D_MODEL:(j + 1) * D_MODEL] for j in range(N_MOD)]


def _phase1_kernel(emit_cache, mod_row_fn, x_ref, mod_ref, ktab_ref, qtab_ref, g_pre_ref, w_attn_ref,
                   w_gate_ref,
                   g_q_ref, w_uq_ref, g_kv_ref, w_k_ref, w_uv_ref, g_sgu_ref, beta_sgu_ref,
                   w_sgu_ref, bias_ref, *rest):
    if emit_cache:
        (w_o32_ref, w_ff1_32_ref, w_ff2_32_ref, q_ref, k_ref, v_ref, sgu_ref, ckv_ref, kr_ref,
         w_o16_ref, w_ff1_16_ref, w_ff2_16_ref) = rest
        w_o16_ref[...] = w_o32_ref[...].astype(BF16)
        w_ff1_16_ref[...] = w_ff1_32_ref[...].astype(BF16)
        w_ff2_16_ref[...] = w_ff2_32_ref[...].astype(BF16)
    else:
        q_ref, k_ref, v_ref, sgu_ref = rest
    tm = x_ref.shape[0]
    mod = _mod_rows(mod_ref, mod_row_fn(pl.program_id(0)))
    shift_a, scale_a = mod[0], mod[1]
    low_half = lax.broadcasted_iota(jnp.int32, (CHUNK, LANES), 1) < SGU_HEAD_DIM
    per_pos_tables = ktab_ref.shape[0] > 1

    def project(r0):
        rows = slice(r0, r0 + PHASE1_CHAIN)
        h = (_rms(x_ref[rows, :], g_pre_ref[...]) * (1.0 + scale_a) + shift_a).astype(BF16)
        proj = _dot_nt(h, w_attn_ref[...])
        gate = _dot_nt(h, w_gate_ref[...])
        return proj, gate

    def expand_qkv(r0, proj):
        rows = slice(r0, r0 + PHASE1_CHAIN)
        tab_rows = rows if per_pos_tables else slice(None)
        cq = _rms(proj[:, 0:Q_LORA], g_q_ref[...])
        q = _dot(cq.astype(BF16), w_uq_ref[...])
        qtab = qtab_ref[tab_rows, :]
        q_ref[rows, :] = jnp.concatenate(
            [(q[:, s * HEAD_SLOT:(s + 1) * HEAD_SLOT] * qtab).astype(BF16)
             for s in range(MLA_HEADS)], axis=1)
        ckv_n = _rms(proj[:, Q_LORA:Q_LORA + KV_LORA], g_kv_ref[...])
        rope_slab = proj[:, Q_LORA + KV_LORA:Q_LORA + KV_LORA + LANES]
        if emit_cache:
            ckv_ref[rows, :] = ckv_n
            kr_t = rope_slab.T
            for e in range(PHASE1_CHAIN // SUB_ROWS):
                elem = r0 // SUB_ROWS + e
                kr_ref[elem * QK_ROPE:(elem + 1) * QK_ROPE, :] = (
                    kr_t[0:QK_ROPE, e * SUB_ROWS:(e + 1) * SUB_ROWS])
        ckv_b = ckv_n.astype(BF16)
        kin = jnp.concatenate([ckv_b, (rope_slab * ktab_ref[tab_rows, :]).astype(BF16)], axis=1)
        k_ref[:, rows] = _dot_nt(w_k_ref[...], kin).astype(BF16)
        v_ref[rows, :] = _dot(ckv_b, w_uv_ref[...]).astype(BF16)

    def gating_unit(r0, gate):
        vv = _gelu(gate[:, SGU_WIDTH:2 * SGU_WIDTH])
        mu = jnp.mean(vv, axis=-1, keepdims=True)
        vc = vv - mu
        var = jnp.mean(vc * vc, axis=-1, keepdims=True)
        vn = (vc * lax.rsqrt(var + EPS) * g_sgu_ref[...] + beta_sgu_ref[...]).astype(BF16)
        n_chunks = PHASE1_CHAIN // CHUNK
        for j in range(SGU_WIDTH // LANES):
            lanes = slice(j * LANES, (j + 1) * LANES)
            rhs = jnp.concatenate(
                [vn[n * CHUNK:(n + 1) * CHUNK, lanes] for n in range(n_chunks)], axis=1)
            o = _dot(w_sgu_ref[j], rhs)
            u = _gelu(gate[:, lanes])
            bias = bias_ref[:, lanes]
            for n in range(n_chunks):
                even = o[0:CHUNK, n * LANES:(n + 1) * LANES]
                odd = o[CHUNK:2 * CHUNK, n * LANES:(n + 1) * LANES]
                mixed = jnp.where(low_half, even, odd) + bias
                sgu_ref[r0 + n * CHUNK:r0 + (n + 1) * CHUNK, lanes] = (
                    u[n * CHUNK:(n + 1) * CHUNK, :] * mixed).astype(BF16)

    starts = list(range(0, tm, PHASE1_CHAIN))
    ahead = project(starts[0])
    for i, r0 in enumerate(starts):
        proj, gate = ahead
        if i + 1 < len(starts):
            ahead = project(starts[i + 1])
        expand_qkv(r0, proj)
        gating_unit(r0, gate)


def _const_spec(shape):
    nd = len(shape)
    return pl.BlockSpec(shape, lambda *_: (0,) * nd, pipeline_mode=pl.Buffered(1))


def _phase1(x2d, mod3, mod_row_fn, ktab, qtab, tab_fn, wts, emit_cache):
    n_tok = x2d.shape[0]
    tm = TOKEN_TILE
    grid = (n_tok // tm,)
    tab_block = (ktab.shape[0] if ktab.shape[0] == 1 else tm, LANES)
    in_specs = [
        pl.BlockSpec((tm, D_MODEL), lambda i: (i, 0)),
        _const_spec((MOD_ROWS, N_MOD * D_MODEL)),
        pl.BlockSpec(tab_block, lambda i: (tab_fn(i), 0)),
        pl.BlockSpec(tab_block, lambda i: (tab_fn(i), 0)),
        _const_spec((1, D_MODEL)),
        _const_spec((ATTN_PROJ, D_MODEL)),
        _const_spec((2 * SGU_WIDTH, D_MODEL)),
        _const_spec((1, Q_LORA)),
        _const_spec((Q_LORA, QK_WIDTH)),
        _const_spec((1, KV_LORA)),
        _const_spec((QK_WIDTH, 2 * LANES)),
        _const_spec((KV_LORA, MLA_WIDTH)),
        _const_spec((1, SGU_WIDTH)),
        _const_spec((1, SGU_WIDTH)),
        _const_spec((SGU_WIDTH // LANES, 2 * CHUNK, CHUNK)),
        _const_spec((CHUNK, SGU_WIDTH)),
    ]
    out_shape = [
        jax.ShapeDtypeStruct((n_tok, QK_WIDTH), BF16),
        jax.ShapeDtypeStruct((QK_WIDTH, n_tok), BF16),
        jax.ShapeDtypeStruct((n_tok, MLA_WIDTH), BF16),
        jax.ShapeDtypeStruct((n_tok, SGU_WIDTH), BF16),
    ]
    out_specs = [
        pl.BlockSpec((tm, QK_WIDTH), lambda i: (i, 0)),
        pl.BlockSpec((QK_WIDTH, tm), lambda i: (0, i)),
        pl.BlockSpec((tm, MLA_WIDTH), lambda i: (i, 0)),
        pl.BlockSpec((tm, SGU_WIDTH), lambda i: (i, 0)),
    ]
    extra_inputs = []
    if emit_cache:
        kr_rows = tm // SUB_ROWS * QK_ROPE
        out_shape += [jax.ShapeDtypeStruct((n_tok, KV_LORA), F32),
                      jax.ShapeDtypeStruct((n_tok // SUB_ROWS * QK_ROPE, SUB_ROWS), F32)]
        out_specs += [pl.BlockSpec((tm, KV_LORA), lambda i: (i, 0)),
                      pl.BlockSpec((kr_rows, SUB_ROWS), lambda i: (i, 0))]
        for w in (wts["w_o32"], wts["w_ff1_32"], wts["w_ff2_32"]):
            rows, cols = w.shape
            blk = (rows // grid[0], cols)
            extra_inputs.append(w)
            in_specs.append(pl.BlockSpec(blk, lambda i: (i, 0)))
            out_shape.append(jax.ShapeDtypeStruct(w.shape, BF16))
            out_specs.append(pl.BlockSpec(blk, lambda i: (i, 0)))
    return pl.pallas_call(
        functools.partial(_phase1_kernel, emit_cache, mod_row_fn),
        out_shape=out_shape,
        grid=grid,
        in_specs=in_specs,
        out_specs=out_specs,
        compiler_params=pltpu.CompilerParams(
            dimension_semantics=("arbitrary",), vmem_limit_bytes=VMEM_LIMIT),
        name="phase1_ctx" if emit_cache else "phase1_lat",
    )(x2d, mod3, ktab, qtab, wts["g_attn_pre"], wts["w_attn"], wts["w_gate"], wts["g_q"], wts["w_uq"],
      wts["g_kv"], wts["w_k"], wts["w_uv"], wts["g_sgu"], wts["beta_sgu"], wts["w_sgu"],
      wts["bias_sgu"], *extra_inputs)


def _cache_kv_kernel(kin_ref, w_k_ref, w_uv_ref, k_ref, v_ref):
    kin = kin_ref[...].astype(BF16)
    k_ref[...] = _dot_nt(w_k_ref[...], kin).astype(BF16)
    v_ref[...] = _dot(kin[:, 0:KV_LORA], w_uv_ref[...]).astype(BF16)


def _cache_kv(kin2d, wts):
    n_tok = kin2d.shape[0]
    tm = n_tok
    return pl.pallas_call(
        _cache_kv_kernel,
        out_shape=[jax.ShapeDtypeStruct((QK_WIDTH, n_tok), BF16),
                   jax.ShapeDtypeStruct((n_tok, MLA_WIDTH), BF16)],
        grid=(n_tok // tm,),
        in_specs=[
            pl.BlockSpec((tm, 2 * LANES), lambda i: (i, 0)),
            _const_spec((QK_WIDTH, 2 * LANES)),
            _const_spec((KV_LORA, MLA_WIDTH)),
        ],
        out_specs=[pl.BlockSpec((QK_WIDTH, tm), lambda i: (0, i)),
                   pl.BlockSpec((tm, MLA_WIDTH), lambda i: (i, 0))],
        compiler_params=pltpu.CompilerParams(
            dimension_semantics=("arbitrary",), vmem_limit_bytes=VMEM_LIMIT),
        name="cache_kv",
    )(kin2d, wts["w_k"], wts["w_uv"])


def _interleave(*gens):
    results = [None] * len(gens)
    live = list(range(len(gens)))
    while live:
        for i in list(live):
            try:
                next(gens[i])
            except StopIteration as stop:
                results[i] = stop.value
                live.remove(i)
    return results


def _run(gen):
    return _interleave(gen)[0]


def _attend(q_ref, kv_views, r0, low_half):
    def head_scores(hd):
        qh = q_ref[r0:r0 + SUB_ROWS, hd * HEAD_SLOT:(hd + 1) * HEAD_SLOT]
        return [_dot(qh, k_view(hd * HEAD_SLOT, HEAD_SLOT)) for k_view, _ in kv_views]

    def head_probs(scores):
        m = scores[0].max(axis=-1, keepdims=True)
        for s in scores[1:]:
            m = jnp.maximum(m, s.max(axis=-1, keepdims=True))
        probs = []
        denom = None
        for s in scores:
            p = jnp.exp2(s - m)
            ps = p.sum(axis=-1, keepdims=True)
            denom = ps if denom is None else denom + ps
            probs.append(p.astype(BF16))
        return probs, denom

    def head_values(hd, probs, denom):
        slab = hd // 2
        acc = None
        for (_, v_view), p in zip(kv_views, probs):
            pv = _dot(p, v_view(slab * LANES, LANES))
            acc = pv if acc is None else acc + pv
        return acc / denom

    pair_out = []
    head_out = None
    ahead = [head_scores(hd) for hd in range(QK_AHEAD)]
    for hd in range(MLA_HEADS):
        if hd + QK_AHEAD < MLA_HEADS:
            ahead.append(head_scores(hd + QK_AHEAD))
        o = head_values(hd, *head_probs(ahead.pop(0)))
        if hd % 2 == 0:
            head_out = o
        else:
            pair_out.append(jnp.where(low_half, head_out, o).astype(BF16))
            yield
    return jnp.concatenate(pair_out, axis=1)


def _phase2_kernel(n_kv, shared_kv, mod_row_fn, x_ref, mod_ref, q_ref, sgu_ref, *refs):
    kv_refs = refs[:2 * n_kv]
    (w_o_ref, g_post_ref, g_fpre_ref, g_fpost_ref, w_ff1_ref, w_ff2_ref, o_ref) = refs[2 * n_kv:]
    tq = x_ref.shape[0]
    low_half = lax.broadcasted_iota(jnp.int32, (SUB_ROWS, LANES), 1) < V_HEAD
    _, _, gate_a, shift_f, scale_f, gate_f = _mod_rows(mod_ref, mod_row_fn(pl.program_id(0)))
    def attention(r0):
        views = []
        for t in range(n_kv):
            k_ref, v_ref = kv_refs[2 * t], kv_refs[2 * t + 1]
            if shared_kv:
                n_keys = v_ref.shape[0]
                step = min(n_keys, KEY_SPLIT)
                for j in range(0, n_keys, step):
                    views.append(
                        (lambda c, w, k_ref=k_ref, j=j, n=step: k_ref[c:c + w, j:j + n],
                         lambda c, w, v_ref=v_ref, j=j, n=step: v_ref[j:j + n, c:c + w]))
            else:
                views.append((lambda c, w, k_ref=k_ref: k_ref[c:c + w, r0:r0 + SUB_ROWS],
                              lambda c, w, v_ref=v_ref: v_ref[r0:r0 + SUB_ROWS, c:c + w]))
        return _attend(q_ref, views, r0, low_half)

    def mixer_proj(r0, attn):
        mix_in = jnp.concatenate([attn, sgu_ref[r0:r0 + SUB_ROWS, :]], axis=1)
        return _dot(mix_in, w_o_ref[...])

    def mixer_norm(r0, mix):
        x1 = x_ref[r0:r0 + SUB_ROWS, :] + gate_a * _rms(mix, g_post_ref[...])
        h = (_rms(x1, g_fpre_ref[...]) * (1.0 + scale_f) + shift_f).astype(BF16)
        return x1, h

    def ffn(h):
        f = None
        for c in range(D_FF // FF_CHUNK):
            hid = jnp.maximum(_dot(h, w_ff1_ref[:, c * FF_CHUNK:(c + 1) * FF_CHUNK]), 0.0)
            part = _dot((hid * hid).astype(BF16), w_ff2_ref[c * FF_CHUNK:(c + 1) * FF_CHUNK, :])
            f = part if f is None else f + part
            yield
        return f

    def finish(r0, x1, f):
        o_ref[r0:r0 + SUB_ROWS, :] = x1 + gate_f * _rms(f, g_fpost_ref[...])

    starts = [a * SUB_ROWS for a in range(tq // SUB_ROWS)]
    x1h = {r0: mixer_norm(r0, mixer_proj(r0, _run(attention(r0)))) for r0 in starts}
    for r0 in starts:
        finish(r0, x1h[r0][0], _run(ffn(x1h[r0][1])))


def _phase2(x2d, mod3, mod_row_fn, q, sgu, kv_list, kv_rows, shared_kv, tq, n_outer, n_inner, wts,
            name):
    n_kv = len(kv_list) // 2
    in_specs = [
        pl.BlockSpec((tq, D_MODEL), lambda b, i: (b * n_inner + i, 0)),
        _const_spec((MOD_ROWS, N_MOD * D_MODEL)),
        pl.BlockSpec((tq, QK_WIDTH), lambda b, i: (b * n_inner + i, 0)),
        pl.BlockSpec((tq, SGU_WIDTH), lambda b, i: (b * n_inner + i, 0)),
    ]
    for t in range(n_kv):
        in_specs.append(pl.BlockSpec((QK_WIDTH, kv_rows[t]), lambda b, i: (0, b)))
        in_specs.append(pl.BlockSpec((kv_rows[t], MLA_WIDTH), lambda b, i: (b, 0)))
    in_specs += [
        _const_spec((D_MODEL, D_MODEL)),
        _const_spec((1, D_MODEL)),
        _const_spec((1, D_MODEL)),
        _const_spec((1, D_MODEL)),
        _const_spec((D_MODEL, D_FF)),
        _const_spec((D_FF, D_MODEL)),
    ]
    return pl.pallas_call(
        functools.partial(_phase2_kernel, n_kv, shared_kv, mod_row_fn),
        out_shape=jax.ShapeDtypeStruct(x2d.shape, F32),
        grid=(n_outer, n_inner),
        in_specs=in_specs,
        out_specs=pl.BlockSpec((tq, D_MODEL), lambda b, i: (b * n_inner + i, 0)),
        compiler_params=pltpu.CompilerParams(
            dimension_semantics=("arbitrary", "arbitrary"), vmem_limit_bytes=VMEM_LIMIT),
        name=name,
    )(x2d, mod3, q, sgu, *kv_list, wts["w_o"], wts["g_attn_post"], wts["g_ffn_pre"],
      wts["g_ffn_post"], wts["w_ff1"], wts["w_ff2"])


def _pair_swap(w):
    shp = w.shape
    return w.reshape(shp[:-1] + (shp[-1] // 2, 2))[..., ::-1].reshape(shp)


def _prepare_weights(g_attn_pre, g_attn_post, g_q, w_uq, g_kv, w_ukv, w_sgu, b_sgu, g_sgu,
                     beta_sgu, w_o, g_ffn_pre, g_ffn_post, w_ff1, w_ff2):
    w_uq_h = w_uq.reshape(Q_LORA, MLA_HEADS, QK_NOPE + QK_ROPE)
    w_uq_ext = jnp.concatenate(
        [w_uq_h, _pair_swap(w_uq_h[..., QK_NOPE:])], axis=-1).reshape(Q_LORA, QK_WIDTH).astype(BF16)

    w_ukv_h = w_ukv.reshape(KV_LORA, MLA_HEADS, QK_NOPE + V_HEAD)
    w_uk_slots = jnp.concatenate(
        [w_ukv_h[..., :QK_NOPE], jnp.zeros((KV_LORA, MLA_HEADS, HEAD_SLOT - QK_NOPE), F32)],
        axis=-1).reshape(KV_LORA, QK_WIDTH)
    eye = jnp.eye(QK_ROPE, dtype=F32)
    zeros_rope = jnp.zeros((QK_ROPE, QK_ROPE), F32)
    zeros_nope = jnp.zeros((QK_ROPE, QK_NOPE), F32)
    to_lo = jnp.tile(jnp.concatenate([zeros_nope, eye, zeros_rope], axis=1), (1, MLA_HEADS))
    to_hi = jnp.tile(jnp.concatenate([zeros_nope, zeros_rope, eye], axis=1), (1, MLA_HEADS))
    w_k = jnp.concatenate([w_uk_slots, to_lo, to_lo, to_hi, to_hi], axis=0).T.astype(BF16)
    w_uv = w_ukv_h[..., QK_NOPE:].reshape(KV_LORA, MLA_WIDTH).astype(BF16)

    w_sgu_pair = w_sgu.reshape(SGU_HEADS // 2, 2 * CHUNK, CHUNK).astype(BF16)
    bias_sgu = jnp.repeat(b_sgu.T, SGU_HEAD_DIM, axis=1)
    row = lambda a: a.reshape(1, -1)
    return {
        "g_attn_pre": row(g_attn_pre), "g_attn_post": row(g_attn_post),
        "g_q": row(g_q), "w_uq": w_uq_ext, "g_kv": row(g_kv), "w_k": w_k, "w_uv": w_uv,
        "g_sgu": row(g_sgu), "beta_sgu": row(beta_sgu), "w_sgu": w_sgu_pair, "bias_sgu": bias_sgu,
        "w_o32": w_o, "g_ffn_pre": row(g_ffn_pre), "g_ffn_post": row(g_ffn_post),
        "w_ff1_32": w_ff1, "w_ff2_32": w_ff2,
    }


def _rope_tables(n_tok):
    rows = n_tok // GRID_W
    row = np.repeat(np.arange(rows), GRID_W).astype(np.float32)
    col = np.tile(np.arange(GRID_W), rows).astype(np.float32)
    freqs = (1.0 / (ROPE_BASE ** (np.arange(AXIS_PAIRS, dtype=np.float32) / AXIS_PAIRS))).astype(
        np.float32)
    ang = np.concatenate([row[:, None] * freqs, col[:, None] * freqs], axis=-1)
    cos = np.repeat(np.cos(ang), 2, axis=1)
    sin = np.repeat(np.sin(ang), 2, axis=1) * np.tile(np.array([-1.0, 1.0], np.float32), QK_ROPE // 2)
    ktab = np.concatenate([cos, sin, cos, sin], axis=1)
    qtab = QUERY_SCALE * np.concatenate([np.ones((n_tok, QK_NOPE), np.float32), cos, sin], axis=1)
    return jnp.asarray(ktab, F32), jnp.asarray(qtab, F32)


def kernel(x_prompt, x_sample, cache_ckv, cache_krope, c, c_ctx, w_mod, b_mod, g_attn_pre,
           g_attn_post, w_in, g_q, w_uq, g_kv, w_ukv, w_sgu, b_sgu, g_sgu, beta_sgu, w_o,
           g_ffn_pre, g_ffn_post, w_ff1, w_ff2):
    batch, seq, _ = x_prompt.shape
    dec_batch, dec_seq, _ = x_sample.shape
    past_len = cache_ckv.shape[2]
    depth = w_mod.shape[0]
    assert depth == 1

    wts = _prepare_weights(g_attn_pre[0], g_attn_post[0], g_q[0], w_uq[0], g_kv[0],
                           w_ukv[0], w_sgu[0], b_sgu[0], g_sgu[0], beta_sgu[0], w_o[0],
                           g_ffn_pre[0], g_ffn_post[0], w_ff1[0], w_ff2[0])

    cond8 = jnp.concatenate(
        [c_ctx[None, :], c, jnp.zeros((MOD_ROWS - 1 - dec_batch, D_MODEL), F32)], axis=0)
    mod3, w_attn, w_gate = _modulation(cond8, w_mod[0], b_mod[0].reshape(1, -1), w_in[0].T)
    wts = dict(wts, w_attn=w_attn, w_gate=w_gate)

    ktab_ctx = jnp.asarray(np.arange(LANES)[None, :] < QK_ROPE, F32)
    qtab_ctx = jnp.asarray(QUERY_SCALE * (np.arange(LANES)[None, :] < QK_NOPE + QK_ROPE), F32)
    xp2d = x_prompt.reshape(batch * seq, D_MODEL)
    q_c, k_c, v_c, sgu_c, ckv_c, kr_c, w_o16, w_ff1_16, w_ff2_16 = _phase1(
        xp2d, mod3, lambda i: 0, ktab_ctx, qtab_ctx, lambda i: 0, wts, True)
    wts = dict(wts, w_o=w_o16, w_ff1=w_ff1_16, w_ff2=w_ff2_16)
    assert seq == SUB_ROWS and dec_seq % PHASE2_ROWS_LAT == 0
    y_prompt = _phase2(xp2d, mod3, lambda b: 0, q_c, sgu_c, [k_c, v_c], [PHASE2_ROWS_CTX], False,
                       PHASE2_ROWS_CTX, batch * seq // PHASE2_ROWS_CTX, 1, wts, "phase2_ctx")

    ktab_lat, qtab_lat = _rope_tables(dec_seq)
    tiles_per_seq = dec_seq // TOKEN_TILE
    xs2d = x_sample.reshape(dec_batch * dec_seq, D_MODEL)
    q_l, k_l, v_l, sgu_l = _phase1(
        xs2d, mod3, lambda i: 1 + i // tiles_per_seq, ktab_lat, qtab_lat,
        lambda i: i % tiles_per_seq, wts, False)
    kr_p = cache_krope[:, 0].reshape(dec_batch * past_len, QK_ROPE)
    pad = jnp.zeros_like(kr_p)
    kin_p = jnp.concatenate(
        [cache_ckv[:, 0].reshape(dec_batch * past_len, KV_LORA), kr_p, pad, kr_p, pad], axis=1)
    k_p, v_p = _cache_kv(kin_p, wts)
    y_sample = _phase2(xs2d, mod3, lambda b: 1 + b, q_l, sgu_l, [k_p, v_p, k_l, v_l],
                       [past_len, dec_seq], True, PHASE2_ROWS_LAT, dec_batch,
                       dec_seq // PHASE2_ROWS_LAT, wts, "phase2_lat")

    return (y_prompt.reshape(batch, seq, D_MODEL),
            y_sample.reshape(dec_batch, dec_seq, D_MODEL),
            ckv_c.reshape(batch, 1, seq, KV_LORA),
            kr_c.reshape(batch, 1, QK_ROPE, seq).transpose(0, 1, 3, 2))
```

```python
import functools
import math

import jax
import jax.numpy as jnp
import numpy as np
from jax import lax
from jax.experimental import pallas as pl
from jax.experimental.pallas import tpu as pltpu

D_MODEL = 1024
GRID_W = 64
MLA_HEADS = 8
QK_NOPE = 64
QK_ROPE = 32
V_HEAD = 64
Q_LORA = 256
KV_LORA = 128
MLA_WIDTH = MLA_HEADS * V_HEAD
SGU_HEADS = 8
SGU_WIDTH = D_MODEL - MLA_WIDTH
SGU_HEAD_DIM = SGU_WIDTH // SGU_HEADS
CHUNK = 128
D_FF = 4 * D_MODEL
AXIS_PAIRS = QK_ROPE // 4
ROPE_BASE = 10000.0
EPS = 1e-6
N_MOD = 6
ATTN_SCALE = (QK_NOPE + QK_ROPE) ** -0.5
QUERY_SCALE = ATTN_SCALE * math.log2(math.e)

LANES = 128
HEAD_SLOT = LANES
QK_WIDTH = MLA_HEADS * HEAD_SLOT
ATTN_PROJ = Q_LORA + KV_LORA + LANES
MOD_ROWS = 8
MOD_K_ROWS = 128
MOD_STREAMS = 2
TOKEN_TILE = 1024
PHASE1_CHAIN = 1024
SUB_ROWS = 256
PHASE2_ROWS_CTX = 512
PHASE2_ROWS_LAT = 512
KEY_SPLIT = 512
QK_AHEAD = 1
FF_CHUNK = 1024
VMEM_LIMIT = 56 * 1024 * 1024

BF16 = jnp.bfloat16
F32 = jnp.float32


def _dot(a, b):
    return jnp.dot(a, b, preferred_element_type=F32)


def _dot_nt(a, b):
    return lax.dot_general(a, b, (((1,), (1,)), ((), ())), preferred_element_type=F32)


def _rms(x, g):
    return x * lax.rsqrt(jnp.mean(x * x, axis=-1, keepdims=True) + EPS) * g


def _gelu(x):
    inner = math.sqrt(2.0 / math.pi) * (x + 0.044715 * (x * x * x))
    return x * (0.5 * (1.0 + jnp.tanh(inner)))


def _mod_kernel(*refs):
    cond_refs = refs[:MOD_STREAMS]
    w_refs = refs[MOD_STREAMS:2 * MOD_STREAMS]
    (b_ref, w_in_attn_ref, w_in_gate_ref, o_ref, w_attn_ref, w_gate_ref) = refs[2 * MOD_STREAMS:]

    @pl.when(pl.program_id(0) == 0)
    def _():
        o_ref[...] = jnp.broadcast_to(b_ref[...], o_ref.shape)

    def part(cond_ref, w_ref):
        cnd = cond_ref[...]
        act = cnd * (1.0 / (1.0 + jnp.exp(-cnd)))
        return _dot(act.astype(BF16), w_ref[...].astype(BF16))

    acc = part(cond_refs[0], w_refs[0])
    for cond_ref, w_ref in zip(cond_refs[1:], w_refs[1:]):
        acc = acc + part(cond_ref, w_ref)
    o_ref[...] += acc

    w_gate_ref[...] = w_in_gate_ref[...].astype(BF16)

    @pl.when(pl.program_id(0) == 0)
    def _():
        rope0 = Q_LORA + KV_LORA
        kr = w_in_attn_ref[rope0:rope0 + QK_ROPE, :]
        row = lax.broadcasted_iota(jnp.int32, kr.shape, 0)
        kr_sw = jnp.where(row % 2 == 0, pltpu.roll(kr, QK_ROPE - 1, 0), pltpu.roll(kr, 1, 0))
        w_attn_ref[0:rope0 + QK_ROPE, :] = w_in_attn_ref[...].astype(BF16)
        w_attn_ref[rope0 + QK_ROPE:ATTN_PROJ, :] = jnp.concatenate(
            [kr_sw, kr, kr_sw], axis=0).astype(BF16)


def _modulation(cond8, w_mod, b_mod, w_in_t):
    n = w_mod.shape[1]
    tk = MOD_K_ROWS
    nk = D_MODEL // tk // MOD_STREAMS
    attn_rows = Q_LORA + KV_LORA + QK_ROPE
    gate_rows = 2 * SGU_WIDTH // nk
    cond_specs = [pl.BlockSpec((MOD_ROWS, tk), lambda k, s=s: (0, k + s * nk))
                  for s in range(MOD_STREAMS)]
    w_specs = [pl.BlockSpec((tk, n), lambda k, s=s: (k + s * nk, 0)) for s in range(MOD_STREAMS)]
    return pl.pallas_call(
        _mod_kernel,
        out_shape=[jax.ShapeDtypeStruct((MOD_ROWS, n), F32),
                   jax.ShapeDtypeStruct((ATTN_PROJ, D_MODEL), BF16),
                   jax.ShapeDtypeStruct((2 * SGU_WIDTH, D_MODEL), BF16)],
        grid=(nk,),
        in_specs=cond_specs + w_specs + [
            pl.BlockSpec((1, n), lambda k: (0, 0)),
            pl.BlockSpec((attn_rows, D_MODEL), lambda k: (0, 0)),
            pl.BlockSpec((pl.Element(gate_rows), pl.Element(D_MODEL)),
                         lambda k: (pl.multiple_of(attn_rows + k * gate_rows, QK_ROPE), 0)),
        ],
        out_specs=[pl.BlockSpec((MOD_ROWS, n), lambda k: (0, 0)),
                   pl.BlockSpec((ATTN_PROJ, D_MODEL), lambda k: (0, 0)),
                   pl.BlockSpec((gate_rows, D_MODEL), lambda k: (k, 0))],
        compiler_params=pltpu.CompilerParams(
            dimension_semantics=("arbitrary",), vmem_limit_bytes=VMEM_LIMIT),
        name="modulation",
    )(*([cond8] * MOD_STREAMS), *([w_mod] * MOD_STREAMS), b_mod, w_in_t, w_in_t)


def _mod_rows(mod_ref, row):
    vec = mod_ref[pl.ds(row, 1), :]
    return [vec[:, j * D_MODEL:(j + 1) * D_MODEL] for j in range(N_MOD)]


def _phase1_kernel(emit_cache, mod_row_fn, x_ref, mod_ref, ktab_ref, qtab_ref, g_pre_ref, w_attn_ref,
                   w_gate_ref,
                   g_q_ref, w_uq_ref, g_kv_ref, w_k_ref, w_uv_ref, g_sgu_ref, beta_sgu_ref,
                   w_sgu_ref, bias_ref, *rest):
    if emit_cache:
        (w_o32_ref, w_ff1_32_ref, w_ff2_32_ref, q_ref, k_ref, v_ref, sgu_ref, ckv_ref, kr_ref,
         w_o16_ref, w_ff1_16_ref, w_ff2_16_ref) = rest
        w_o16_ref[...] = w_o32_ref[...].astype(BF16)
        w_ff1_16_ref[...] = w_ff1_32_ref[...].astype(BF16)
        w_ff2_16_ref[...] = w_ff2_32_ref[...].astype(BF16)
    else:
        q_ref, k_ref, v_ref, sgu_ref = rest
    tm = x_ref.shape[0]
    mod = _mod_rows(mod_ref, mod_row_fn(pl.program_id(0)))
    shift_a, scale_a = mod[0], mod[1]
    low_half = lax.broadcasted_iota(jnp.int32, (CHUNK, LANES), 1) < SGU_HEAD_DIM
    per_pos_tables = ktab_ref.shape[0] > 1

    def project(r0):
        rows = slice(r0, r0 + PHASE1_CHAIN)
        h = (_rms(x_ref[rows, :], g_pre_ref[...]) * (1.0 + scale_a) + shift_a).astype(BF16)
        proj = _dot_nt(h, w_attn_ref[...])
        gate = _dot_nt(h, w_gate_ref[...])
        return proj, gate

    def expand_qkv(r0, proj):
        rows = slice(r0, r0 + PHASE1_CHAIN)
        tab_rows = rows if per_pos_tables else slice(None)
        cq = _rms(proj[:, 0:Q_LORA], g_q_ref[...])
        q = _dot(cq.astype(BF16), w_uq_ref[...])
        qtab = qtab_ref[tab_rows, :]
        q_ref[rows, :] = jnp.concatenate(
            [(q[:, s * HEAD_SLOT:(s + 1) * HEAD_SLOT] * qtab).astype(BF16)
             for s in range(MLA_HEADS)], axis=1)
        ckv_n = _rms(proj[:, Q_LORA:Q_LORA + KV_LORA], g_kv_ref[...])
        rope_slab = proj[:, Q_LORA + KV_LORA:Q_LORA + KV_LORA + LANES]
        if emit_cache:
            ckv_ref[rows, :] = ckv_n
            kr_t = rope_slab.T
            for e in range(PHASE1_CHAIN // SUB_ROWS):
                elem = r0 // SUB_ROWS + e
                kr_ref[elem * QK_ROPE:(elem + 1) * QK_ROPE, :] = (
                    kr_t[0:QK_ROPE, e * SUB_ROWS:(e + 1) * SUB_ROWS])
        ckv_b = ckv_n.astype(BF16)
        kin = jnp.concatenate([ckv_b, (rope_slab * ktab_ref[tab_rows, :]).astype(BF16)], axis=1)
        k_ref[:, rows] = _dot_nt(w_k_ref[...], kin).astype(BF16)
        v_ref[rows, :] = _dot(ckv_b, w_uv_ref[...]).astype(BF16)

    def gating_unit(r0, gate):
        vv = _gelu(gate[:, SGU_WIDTH:2 * SGU_WIDTH])
        mu = jnp.mean(vv, axis=-1, keepdims=True)
        vc = vv - mu
        var = jnp.mean(vc * vc, axis=-1, keepdims=True)
        vn = (vc * lax.rsqrt(var + EPS) * g_sgu_ref[...] + beta_sgu_ref[...]).astype(BF16)
        n_chunks = PHASE1_CHAIN // CHUNK
        for j in range(SGU_WIDTH // LANES):
            lanes = slice(j * LANES, (j + 1) * LANES)
            rhs = jnp.concatenate(
                [vn[n * CHUNK:(n + 1) * CHUNK, lanes] for n in range(n_chunks)], axis=1)
            o = _dot(w_sgu_ref[j], rhs)
            u = _gelu(gate[:, lanes])
            bias = bias_ref[:, lanes]
            for n in range(n_chunks):
                even = o[0:CHUNK, n * LANES:(n + 1) * LANES]
                odd = o[CHUNK:2 * CHUNK, n * LANES:(n + 1) * LANES]
                mixed = jnp.where(low_half, even, odd) + bias
                sgu_ref[r0 + n * CHUNK:r0 + (n + 1) * CHUNK, lanes] = (
                    u[n * CHUNK:(n + 1) * CHUNK, :] * mixed).astype(BF16)

    starts = list(range(0, tm, PHASE1_CHAIN))
    ahead = project(starts[0])
    for i, r0 in enumerate(starts):
        proj, gate = ahead
        if i + 1 < len(starts):
            ahead = project(starts[i + 1])
        expand_qkv(r0, proj)
        gating_unit(r0, gate)


def _const_spec(shape):
    nd = len(shape)
    return pl.BlockSpec(shape, lambda *_: (0,) * nd, pipeline_mode=pl.Buffered(1))


def _phase1(x2d, mod3, mod_row_fn, ktab, qtab, tab_fn, wts, emit_cache):
    n_tok = x2d.shape[0]
    tm = TOKEN_TILE
    grid = (n_tok // tm,)
    tab_block = (ktab.shape[0] if ktab.shape[0] == 1 else tm, LANES)
    in_specs = [
        pl.BlockSpec((tm, D_MODEL), lambda i: (i, 0)),
        _const_spec((MOD_ROWS, N_MOD * D_MODEL)),
        pl.BlockSpec(tab_block, lambda i: (tab_fn(i), 0)),
        pl.BlockSpec(tab_block, lambda i: (tab_fn(i), 0)),
        _const_spec((1, D_MODEL)),
        _const_spec((ATTN_PROJ, D_MODEL)),
        _const_spec((2 * SGU_WIDTH, D_MODEL)),
        _const_spec((1, Q_LORA)),
        _const_spec((Q_LORA, QK_WIDTH)),
        _const_spec((1, KV_LORA)),
        _const_spec((QK_WIDTH, 2 * LANES)),
        _const_spec((KV_LORA, MLA_WIDTH)),
        _const_spec((1, SGU_WIDTH)),
        _const_spec((1, SGU_WIDTH)),
        _const_spec((SGU_WIDTH // LANES, 2 * CHUNK, CHUNK)),
        _const_spec((CHUNK, SGU_WIDTH)),
    ]
    out_shape = [
        jax.ShapeDtypeStruct((n_tok, QK_WIDTH), BF16),
        jax.ShapeDtypeStruct((QK_WIDTH, n_tok), BF16),
        jax.ShapeDtypeStruct((n_tok, MLA_WIDTH), BF16),
        jax.ShapeDtypeStruct((n_tok, SGU_WIDTH), BF16),
    ]
    out_specs = [
        pl.BlockSpec((tm, QK_WIDTH), lambda i: (i, 0)),
        pl.BlockSpec((QK_WIDTH, tm), lambda i: (0, i)),
        pl.BlockSpec((tm, MLA_WIDTH), lambda i: (i, 0)),
        pl.BlockSpec((tm, SGU_WIDTH), lambda i: (i, 0)),
    ]
    extra_inputs = []
    if emit_cache:
        kr_rows = tm // SUB_ROWS * QK_ROPE
        out_shape += [jax.ShapeDtypeStruct((n_tok, KV_LORA), F32),
                      jax.ShapeDtypeStruct((n_tok // SUB_ROWS * QK_ROPE, SUB_ROWS), F32)]
        out_specs += [pl.BlockSpec((tm, KV_LORA), lambda i: (i, 0)),
                      pl.BlockSpec((kr_rows, SUB_ROWS), lambda i: (i, 0))]
        for w in (wts["w_o32"], wts["w_ff1_32"], wts["w_ff2_32"]):
            rows, cols = w.shape
            blk = (rows // grid[0], cols)
            extra_inputs.append(w)
            in_specs.append(pl.BlockSpec(blk, lambda i: (i, 0)))
            out_shape.append(jax.ShapeDtypeStruct(w.shape, BF16))
            out_specs.append(pl.BlockSpec(blk, lambda i: (i, 0)))
    return pl.pallas_call(
        functools.partial(_phase1_kernel, emit_cache, mod_row_fn),
        out_shape=out_shape,
        grid=grid,
        in_specs=in_specs,
        out_specs=out_specs,
        compiler_params=pltpu.CompilerParams(
            dimension_semantics=("arbitrary",), vmem_limit_bytes=VMEM_LIMIT),
        name="phase1_ctx" if emit_cache else "phase1_lat",
    )(x2d, mod3, ktab, qtab, wts["g_attn_pre"], wts["w_attn"], wts["w_gate"], wts["g_q"], wts["w_uq"],
      wts["g_kv"], wts["w_k"], wts["w_uv"], wts["g_sgu"], wts["beta_sgu"], wts["w_sgu"],
      wts["bias_sgu"], *extra_inputs)


def _cache_kv_kernel(kin_ref, w_k_ref, w_uv_ref, k_ref, v_ref):
    kin = kin_ref[...].astype(BF16)
    k_ref[...] = _dot_nt(w_k_ref[...], kin).astype(BF16)
    v_ref[...] = _dot(kin[:, 0:KV_LORA], w_uv_ref[...]).astype(BF16)


def _cache_kv(kin2d, wts):
    n_tok = kin2d.shape[0]
    tm = n_tok
    return pl.pallas_call(
        _cache_kv_kernel,
        out_shape=[jax.ShapeDtypeStruct((QK_WIDTH, n_tok), BF16),
                   jax.ShapeDtypeStruct((n_tok, MLA_WIDTH), BF16)],
        grid=(n_tok // tm,),
        in_specs=[
            pl.BlockSpec((tm, 2 * LANES), lambda i: (i, 0)),
            _const_spec((QK_WIDTH, 2 * LANES)),
            _const_spec((KV_LORA, MLA_WIDTH)),
        ],
        out_specs=[pl.BlockSpec((QK_WIDTH, tm), lambda i: (0, i)),
                   pl.BlockSpec((tm, MLA_WIDTH), lambda i: (i, 0))],
        compiler_params=pltpu.CompilerParams(
            dimension_semantics=("arbitrary",), vmem_limit_bytes=VMEM_LIMIT),
        name="cache_kv",
    )(kin2d, wts["w_k"], wts["w_uv"])


def _interleave(*gens):
    results = [None] * len(gens)
    live = list(range(len(gens)))
    while live:
        for i in list(live):
            try:
                next(gens[i])
            except StopIteration as stop:
                results[i] = stop.value
                live.remove(i)
    return results


def _run(gen):
    return _interleave(gen)[0]


def _attend(q_ref, kv_views, r0, low_half):
    def head_scores(hd):
        qh = q_ref[r0:r0 + SUB_ROWS, hd * HEAD_SLOT:(hd + 1) * HEAD_SLOT]
        return [_dot(qh, k_view(hd * HEAD_SLOT, HEAD_SLOT)) for k_view, _ in kv_views]

    def head_probs(scores):
        m = scores[0].max(axis=-1, keepdims=True)
        for s in scores[1:]:
            m = jnp.maximum(m, s.max(axis=-1, keepdims=True))
        probs = []
        denom = None
        for s in scores:
            p = jnp.exp2(s - m)
            ps = p.sum(axis=-1, keepdims=True)
            denom = ps if denom is None else denom + ps
            probs.append(p.astype(BF16))
        return probs, denom

    def head_values(hd, probs, denom):
        slab = hd // 2
        acc = None
        for (_, v_view), p in zip(kv_views, probs):
            pv = _dot(p, v_view(slab * LANES, LANES))
            acc = pv if acc is None else acc + pv
        return acc / denom

    pair_out = []
    head_out = None
    ahead = [head_scores(hd) for hd in range(QK_AHEAD)]
    for hd in range(MLA_HEADS):
        if hd + QK_AHEAD < MLA_HEADS:
            ahead.append(head_scores(hd + QK_AHEAD))
        o = head_values(hd, *head_probs(ahead.pop(0)))
        if hd % 2 == 0:
            head_out = o
        else:
            pair_out.append(jnp.where(low_half, head_out, o).astype(BF16))
            yield
    return jnp.concatenate(pair_out, axis=1)


def _phase2_kernel(n_kv, shared_kv, mod_row_fn, x_ref, mod_ref, q_ref, sgu_ref, *refs):
    kv_refs = refs[:2 * n_kv]
    (w_o_ref, g_post_ref, g_fpre_ref, g_fpost_ref, w_ff1_ref, w_ff2_ref, o_ref) = refs[2 * n_kv:]
    tq = x_ref.shape[0]
    low_half = lax.broadcasted_iota(jnp.int32, (SUB_ROWS, LANES), 1) < V_HEAD
    _, _, gate_a, shift_f, scale_f, gate_f = _mod_rows(mod_ref, mod_row_fn(pl.program_id(0)))
    def attention(r0):
        views = []
        for t in range(n_kv):
            k_ref, v_ref = kv_refs[2 * t], kv_refs[2 * t + 1]
            if shared_kv:
                n_keys = v_ref.shape[0]
                step = min(n_keys, KEY_SPLIT)
                for j in range(0, n_keys, step):
                    views.append(
                        (lambda c, w, k_ref=k_ref, j=j, n=step: k_ref[c:c + w, j:j + n],
                         lambda c, w, v_ref=v_ref, j=j, n=step: v_ref[j:j + n, c:c + w]))
            else:
                views.append((lambda c, w, k_ref=k_ref: k_ref[c:c + w, r0:r0 + SUB_ROWS],
                              lambda c, w, v_ref=v_ref: v_ref[r0:r0 + SUB_ROWS, c:c + w]))
        return _attend(q_ref, views, r0, low_half)

    def mixer_proj(r0, attn):
        mix_in = jnp.concatenate([attn, sgu_ref[r0:r0 + SUB_ROWS, :]], axis=1)
        return _dot(mix_in, w_o_ref[...])

    def mixer_norm(r0, mix):
        x1 = x_ref[r0:r0 + SUB_ROWS, :] + gate_a * _rms(mix, g_post_ref[...])
        h = (_rms(x1, g_fpre_ref[...]) * (1.0 + scale_f) + shift_f).astype(BF16)
        return x1, h

    def ffn(h):
        f = None
        for c in range(D_FF // FF_CHUNK):
            hid = jnp.maximum(_dot(h, w_ff1_ref[:, c * FF_CHUNK:(c + 1) * FF_CHUNK]), 0.0)
            part = _dot((hid * hid).astype(BF16), w_ff2_ref[c * FF_CHUNK:(c + 1) * FF_CHUNK, :])
            f = part if f is None else f + part
            yield
        return f

    def finish(r0, x1, f):
        o_ref[r0:r0 + SUB_ROWS, :] = x1 + gate_f * _rms(f, g_fpost_ref[...])

    starts = [a * SUB_ROWS for a in range(tq // SUB_ROWS)]
    x1h = {r0: mixer_norm(r0, mixer_proj(r0, _run(attention(r0)))) for r0 in starts}
    for r0 in starts:
        finish(r0, x1h[r0][0], _run(ffn(x1h[r0][1])))


def _phase2(x2d, mod3, mod_row_fn, q, sgu, kv_list, kv_rows, shared_kv, tq, n_outer, n_inner, wts,
            name):
    n_kv = len(kv_list) // 2
    in_specs = [
        pl.BlockSpec((tq, D_MODEL), lambda b, i: (b * n_inner + i, 0)),
        _const_spec((MOD_ROWS, N_MOD * D_MODEL)),
        pl.BlockSpec((tq, QK_WIDTH), lambda b, i: (b * n_inner + i, 0)),
        pl.BlockSpec((tq, SGU_WIDTH), lambda b, i: (b * n_inner + i, 0)),
    ]
    for t in range(n_kv):
        mode = pl.Buffered(1) if shared_kv else None
        in_specs.append(pl.BlockSpec((QK_WIDTH, kv_rows[t]), lambda b, i: (0, b),
                                     pipeline_mode=mode))
        in_specs.append(pl.BlockSpec((kv_rows[t], MLA_WIDTH), lambda b, i: (b, 0),
                                     pipeline_mode=mode))
    in_specs += [
        _const_spec((D_MODEL, D_MODEL)),
        _const_spec((1, D_MODEL)),
        _const_spec((1, D_MODEL)),
        _const_spec((1, D_MODEL)),
        _const_spec((D_MODEL, D_FF)),
        _const_spec((D_FF, D_MODEL)),
    ]
    return pl.pallas_call(
        functools.partial(_phase2_kernel, n_kv, shared_kv, mod_row_fn),
        out_shape=jax.ShapeDtypeStruct(x2d.shape, F32),
        grid=(n_outer, n_inner),
        in_specs=in_specs,
        out_specs=pl.BlockSpec((tq, D_MODEL), lambda b, i: (b * n_inner + i, 0)),
        compiler_params=pltpu.CompilerParams(
            dimension_semantics=("arbitrary", "arbitrary"), vmem_limit_bytes=VMEM_LIMIT),
        name=name,
    )(x2d, mod3, q, sgu, *kv_list, wts["w_o"], wts["g_attn_post"], wts["g_ffn_pre"],
      wts["g_ffn_post"], wts["w_ff1"], wts["w_ff2"])


def _pair_swap(w):
    shp = w.shape
    return w.reshape(shp[:-1] + (shp[-1] // 2, 2))[..., ::-1].reshape(shp)


def _prepare_weights(g_attn_pre, g_attn_post, g_q, w_uq, g_kv, w_ukv, w_sgu, b_sgu, g_sgu,
                     beta_sgu, w_o, g_ffn_pre, g_ffn_post, w_ff1, w_ff2):
    w_uq_h = w_uq.reshape(Q_LORA, MLA_HEADS, QK_NOPE + QK_ROPE)
    w_uq_ext = jnp.concatenate(
        [w_uq_h, _pair_swap(w_uq_h[..., QK_NOPE:])], axis=-1).reshape(Q_LORA, QK_WIDTH).astype(BF16)

    w_ukv_h = w_ukv.reshape(KV_LORA, MLA_HEADS, QK_NOPE + V_HEAD)
    w_uk_slots = jnp.concatenate(
        [w_ukv_h[..., :QK_NOPE], jnp.zeros((KV_LORA, MLA_HEADS, HEAD_SLOT - QK_NOPE), F32)],
        axis=-1).reshape(KV_LORA, QK_WIDTH)
    eye = jnp.eye(QK_ROPE, dtype=F32)
    zeros_rope = jnp.zeros((QK_ROPE, QK_ROPE), F32)
    zeros_nope = jnp.zeros((QK_ROPE, QK_NOPE), F32)
    to_lo = jnp.tile(jnp.concatenate([zeros_nope, eye, zeros_rope], axis=1), (1, MLA_HEADS))
    to_hi = jnp.tile(jnp.concatenate([zeros_nope, zeros_rope, eye], axis=1), (1, MLA_HEADS))
    w_k = jnp.concatenate([w_uk_slots, to_lo, to_lo, to_hi, to_hi], axis=0).T.astype(BF16)
    w_uv = w_ukv_h[..., QK_NOPE:].reshape(KV_LORA, MLA_WIDTH).astype(BF16)

    w_sgu_pair = w_sgu.reshape(SGU_HEADS // 2, 2 * CHUNK, CHUNK).astype(BF16)
    bias_sgu = jnp.repeat(b_sgu.T, SGU_HEAD_DIM, axis=1)
    row = lambda a: a.reshape(1, -1)
    return {
        "g_attn_pre": row(g_attn_pre), "g_attn_post": row(g_attn_post),
        "g_q": row(g_q), "w_uq": w_uq_ext, "g_kv": row(g_kv), "w_k": w_k, "w_uv": w_uv,
        "g_sgu": row(g_sgu), "beta_sgu": row(beta_sgu), "w_sgu": w_sgu_pair, "bias_sgu": bias_sgu,
        "w_o32": w_o, "g_ffn_pre": row(g_ffn_pre), "g_ffn_post": row(g_ffn_post),
        "w_ff1_32": w_ff1, "w_ff2_32": w_ff2,
    }


def _rope_tables(n_tok):
    rows = n_tok // GRID_W
    row = np.repeat(np.arange(rows), GRID_W).astype(np.float32)
    col = np.tile(np.arange(GRID_W), rows).astype(np.float32)
    freqs = (1.0 / (ROPE_BASE ** (np.arange(AXIS_PAIRS, dtype=np.float32) / AXIS_PAIRS))).astype(
        np.float32)
    ang = np.concatenate([row[:, None] * freqs, col[:, None] * freqs], axis=-1)
    cos = np.repeat(np.cos(ang), 2, axis=1)
    sin = np.repeat(np.sin(ang), 2, axis=1) * np.tile(np.array([-1.0, 1.0], np.float32), QK_ROPE // 2)
    ktab = np.concatenate([cos, sin, cos, sin], axis=1)
    qtab = QUERY_SCALE * np.concatenate([np.ones((n_tok, QK_NOPE), np.float32), cos, sin], axis=1)
    return jnp.asarray(ktab, F32), jnp.asarray(qtab, F32)


def kernel(x_prompt, x_sample, cache_ckv, cache_krope, c, c_ctx, w_mod, b_mod, g_attn_pre,
           g_attn_post, w_in, g_q, w_uq, g_kv, w_ukv, w_sgu, b_sgu, g_sgu, beta_sgu, w_o,
           g_ffn_pre, g_ffn_post, w_ff1, w_ff2):
    batch, seq, _ = x_prompt.shape
    dec_batch, dec_seq, _ = x_sample.shape
    past_len = cache_ckv.shape[2]
    depth = w_mod.shape[0]
    assert depth == 1

    wts = _prepare_weights(g_attn_pre[0], g_attn_post[0], g_q[0], w_uq[0], g_kv[0],
                           w_ukv[0], w_sgu[0], b_sgu[0], g_sgu[0], beta_sgu[0], w_o[0],
                           g_ffn_pre[0], g_ffn_post[0], w_ff1[0], w_ff2[0])

    cond8 = jnp.concatenate(
        [c_ctx[None, :], c, jnp.zeros((MOD_ROWS - 1 - dec_batch, D_MODEL), F32)], axis=0)
    mod3, w_attn, w_gate = _modulation(cond8, w_mod[0], b_mod[0].reshape(1, -1), w_in[0].T)
    wts = dict(wts, w_attn=w_attn, w_gate=w_gate)

    ktab_ctx = jnp.asarray(np.arange(LANES)[None, :] < QK_ROPE, F32)
    qtab_ctx = jnp.asarray(QUERY_SCALE * (np.arange(LANES)[None, :] < QK_NOPE + QK_ROPE), F32)
    xp2d = x_prompt.reshape(batch * seq, D_MODEL)
    q_c, k_c, v_c, sgu_c, ckv_c, kr_c, w_o16, w_ff1_16, w_ff2_16 = _phase1(
        xp2d, mod3, lambda i: 0, ktab_ctx, qtab_ctx, lambda i: 0, wts, True)
    wts = dict(wts, w_o=w_o16, w_ff1=w_ff1_16, w_ff2=w_ff2_16)
    assert seq == SUB_ROWS and dec_seq % PHASE2_ROWS_LAT == 0
    y_prompt = _phase2(xp2d, mod3, lambda b: 0, q_c, sgu_c, [k_c, v_c], [PHASE2_ROWS_CTX], False,
                       PHASE2_ROWS_CTX, batch * seq // PHASE2_ROWS_CTX, 1, wts, "phase2_ctx")

    ktab_lat, qtab_lat = _rope_tables(dec_seq)
    tiles_per_seq = dec_seq // TOKEN_TILE
    xs2d = x_sample.reshape(dec_batch * dec_seq, D_MODEL)
    q_l, k_l, v_l, sgu_l = _phase1(
        xs2d, mod3, lambda i: 1 + i // tiles_per_seq, ktab_lat, qtab_lat,
        lambda i: i % tiles_per_seq, wts, False)
    kr_p = cache_krope[:, 0].reshape(dec_batch * past_len, QK_ROPE)
    pad = jnp.zeros_like(kr_p)
    kin_p = jnp.concatenate(
        [cache_ckv[:, 0].reshape(dec_batch * past_len, KV_LORA), kr_p, pad, kr_p, pad], axis=1)
    k_p, v_p = _cache_kv(kin_p, wts)
    y_sample = _phase2(xs2d, mod3, lambda b: 1 + b, q_l, sgu_l, [k_p, v_p, k_l, v_l],
                       [past_len, dec_seq], True, PHASE2_ROWS_LAT, dec_batch,
                       dec_seq // PHASE2_ROWS_LAT, wts, "phase2_lat")

    return (y_prompt.reshape(batch, seq, D_MODEL),
            y_sample.reshape(dec_batch, dec_seq, D_MODEL),
            ckv_c.reshape(batch, 1, seq, KV_LORA),
            kr_c.reshape(batch, 1, QK_ROPE, seq).transpose(0, 1, 3, 2))
```
